```python
import math
import jax
import jax.numpy as jnp
from jax import lax
import numpy as np

D_MODEL = 1024
BATCH = 8
SEQ = 2048
DEPTH = 2

GRID_W = 64
CTX_LEN = 256
N_MIXERS = 2
N_MLA_LAYERS = (DEPTH + N_MIXERS - 1) // N_MIXERS
N_S5_LAYERS = DEPTH // N_MIXERS
EPS = 1e-6

MLA_HEADS = 16
QK_NOPE_DIM = 64
QK_ROPE_DIM = 32
V_HEAD_DIM = 64
Q_LORA_RANK = 256
KV_LORA_RANK = 128
MLA_WIDTH = MLA_HEADS * V_HEAD_DIM
QK_DIM = QK_NOPE_DIM + QK_ROPE_DIM
SOFTMAX_SCALE = QK_DIM ** -0.5
ROPE_THETA = 10000.0
Q_BLOCK = 128
MLA_IN_WIDTH = Q_LORA_RANK + KV_LORA_RANK + QK_ROPE_DIM + MLA_WIDTH

S5_WIDTH = D_MODEL
S5_GROUP = 16
S5_GROUPS = S5_WIDTH // S5_GROUP
S5_STATE = 64
DT_MIN = 0.001
DT_MAX = 0.1

kernel_name = "hybrid_mla_s5_context_prefix_dit"


def rmsnorm(x, g):
    xf = x.astype(jnp.float32)
    y = xf * lax.rsqrt(jnp.mean(xf * xf, axis=-1, keepdims=True) + EPS)
    return (y * g.astype(jnp.float32)).astype(x.dtype)


def grid_positions(L):
    rows = L // GRID_W
    row = jnp.repeat(jnp.arange(rows, dtype=jnp.int32), GRID_W)
    col = jnp.tile(jnp.arange(GRID_W, dtype=jnp.int32), rows)
    return row, col


def rope_1d(x, pos):
    d = x.shape[-1]
    inv = 1.0 / (ROPE_THETA ** (jnp.arange(0, d, 2, dtype=jnp.float32) / d))
    ang = pos.astype(jnp.float32)[:, None] * inv[None, :]
    cos = jnp.cos(ang)[:, None, :].astype(x.dtype)
    sin = jnp.sin(ang)[:, None, :].astype(x.dtype)
    x1, x2 = x[..., : d // 2], x[..., d // 2:]
    return jnp.concatenate([x1 * cos - x2 * sin, x1 * sin + x2 * cos], axis=-1)


def axial_rope(x, row, col):
    h = x.shape[-1] // 2
    return jnp.concatenate([rope_1d(x[..., :h], row), rope_1d(x[..., h:], col)], axis=-1)


def mla_project(h, w_in, q_norm, w_uq, kv_norm, w_ukv):
    B_, L, _ = h.shape
    p = h @ w_in
    o1 = Q_LORA_RANK
    o2 = o1 + KV_LORA_RANK
    o3 = o2 + QK_ROPE_DIM
    cq, ckv, kr, z = p[..., :o1], p[..., o1:o2], p[..., o2:o3], p[..., o3:]
    q = (rmsnorm(cq, q_norm) @ w_uq).reshape(B_, L, MLA_HEADS, QK_DIM)
    kv = (rmsnorm(ckv, kv_norm) @ w_ukv).reshape(B_, L, MLA_HEADS, QK_NOPE_DIM + V_HEAD_DIM)
    q_nope, q_rope = q[..., :QK_NOPE_DIM], q[..., QK_NOPE_DIM:]
    k_nope, v = kv[..., :QK_NOPE_DIM], kv[..., QK_NOPE_DIM:]
    return q_nope, q_rope, k_nope, kr[:, :, None, :], v, z


def mla_keys(k_nope, kr):
    kr_b = jnp.broadcast_to(kr, k_nope.shape[:-1] + (QK_ROPE_DIM,))
    return jnp.concatenate([k_nope, kr_b], axis=-1)


def attend(q, k, v):
    s = jnp.einsum('bqhd,bkhd->bhqk', q.astype(jnp.float32), k.astype(jnp.float32)) * SOFTMAX_SCALE
    p = jax.nn.softmax(s, axis=-1)
    return jnp.einsum('bhqk,bkhd->bqhd', p, v.astype(jnp.float32)).astype(v.dtype)


def mla_mixer(h_lat, h_ctx, need_ctx, w_in, q_norm, w_uq, kv_norm, w_ukv, w_out):
    B_, L, _ = h_lat.shape
    Lc = h_ctx.shape[1]
    row, col = grid_positions(L)
    qn_l, qr_l, kn_l, kr_l, v_l, z_l = mla_project(h_lat, w_in, q_norm, w_uq, kv_norm, w_ukv)
    qn_c, qr_c, kn_c, kr_c, v_c, z_c = mla_project(h_ctx, w_in, q_norm, w_uq, kv_norm, w_ukv)
    q_lat = jnp.concatenate([qn_l, axial_rope(qr_l, row, col)], axis=-1)
    k_lat = mla_keys(kn_l, axial_rope(kr_l, row, col))
    k_ctx = mla_keys(kn_c, kr_c)
    k_all = jnp.concatenate([k_ctx, k_lat], axis=1)
    v_all = jnp.concatenate([v_c, v_l], axis=1)
    nb = L // Q_BLOCK
    qb = jnp.transpose(q_lat.reshape(B_, nb, Q_BLOCK, MLA_HEADS, QK_DIM), (1, 0, 2, 3, 4))
    ob = lax.map(lambda qq: attend(qq, k_all, v_all), qb)
    o_lat = jnp.transpose(ob, (1, 0, 2, 3, 4)).reshape(B_, L, MLA_WIDTH)
    out_lat = (o_lat * jax.nn.silu(z_l)) @ w_out
    out_ctx = None
    if need_ctx:
        q_ctx = jnp.concatenate([qn_c, qr_c], axis=-1)
        o_ctx = attend(q_ctx, k_ctx, v_c).reshape(B_, Lc, MLA_WIDTH)
        out_ctx = (o_ctx * jax.nn.silu(z_c)) @ w_out
    return out_lat, out_ctx


def s5_discretise(a_re, a_im, log_step, b_re, b_im):
    dt = jnp.exp(log_step.astype(jnp.float32))[:, None]
    ar = a_re.astype(jnp.float32)
    ai = a_im.astype(jnp.float32)
    mag = jnp.exp(ar * dt)
    lb_re = mag * jnp.cos(ai * dt)
    lb_im = mag * jnp.sin(ai * dt)
    den = ar * ar + ai * ai
    nr = lb_re - 1.0
    f_re = ((nr * ar + lb_im * ai) / den)[..., None]
    f_im = ((lb_im * ar - nr * ai) / den)[..., None]
    br = b_re.astype(jnp.float32)
    bi = b_im.astype(jnp.float32)
    bb_re = f_re * br - f_im * bi
    bb_im = f_re * bi + f_im * br
    return lb_re, lb_im, bb_re, bb_im


def linear_recurrence_combine(e1, e2):
    a1r, a1i, b1r, b1i = e1
    a2r, a2i, b2r, b2i = e2
    return (a2r * a1r - a2i * a1i,
            a2r * a1i + a2i * a1r,
            a2r * b1r - a2i * b1i + b2r,
            a2r * b1i + a2i * b1r + b2i)


def s5_scan(u, disc, c_re, c_im, s0, reverse):
    lb_re, lb_im, bb_re, bb_im = disc
    L = u.shape[1]
    bu_re = jnp.einsum('blgc,gpc->blgp', u, bb_re)
    bu_im = jnp.einsum('blgc,gpc->blgp', u, bb_im)
    if s0 is not None:
        s0r, s0i = s0
        idx = L - 1 if reverse else 0
        bu_re = bu_re.at[:, idx].add(lb_re * s0r - lb_im * s0i)
        bu_im = bu_im.at[:, idx].add(lb_re * s0i + lb_im * s0r)
    a_re = jnp.broadcast_to(lb_re, (1, L) + lb_re.shape)
    a_im = jnp.broadcast_to(lb_im, (1, L) + lb_im.shape)
    _, _, s_re, s_im = lax.associative_scan(
        linear_recurrence_combine, (a_re, a_im, bu_re, bu_im), reverse=reverse, axis=1)
    y = (jnp.einsum('blgp,gcp->blgc', s_re, c_re.astype(jnp.float32))
         - jnp.einsum('blgp,gcp->blgc', s_im, c_im.astype(jnp.float32)))
    fin = (s_re[:, 0], s_im[:, 0]) if reverse else (s_re[:, -1], s_im[:, -1])
    return y, fin


def s5_finish(y_ssm, u, z, d, w_glu, b_glu, w_out):
    B_, L, _ = u.shape
    y = y_ssm.reshape(B_, L, S5_WIDTH) + d.astype(jnp.float32) * u.astype(jnp.float32)
    y = jax.nn.gelu(y).astype(u.dtype)
    y = y * jax.nn.sigmoid(y @ w_glu + b_glu)
    return (y * jax.nn.silu(z)) @ w_out


def s5_mixer(h_lat, h_ctx, need_ctx, w_in, a_re, a_im, log_step, b_re, b_im, c_re, c_im,
             d, w_glu, b_glu, w_out):
    B_, L, _ = h_lat.shape
    Lc = h_ctx.shape[1]
    p_l = h_lat @ w_in
    p_c = h_ctx @ w_in
    u_l, z_l = p_l[..., :S5_WIDTH], p_l[..., S5_WIDTH:]
    u_c, z_c = p_c[..., :S5_WIDTH], p_c[..., S5_WIDTH:]
    g_l = u_l.astype(jnp.float32).reshape(B_, L, S5_GROUPS, S5_GROUP)
    g_c = u_c.astype(jnp.float32).reshape(B_, Lc, S5_GROUPS, S5_GROUP)
    y_l = jnp.zeros_like(g_l)
    y_c = jnp.zeros_like(g_c)
    for k, rev in enumerate((False, True)):
        disc = s5_discretise(a_re[k], a_im[k], log_step[k], b_re[k], b_im[k])
        yc_k, s_fin = s5_scan(g_c, disc, c_re[k], c_im[k], None, rev)
        yl_k, _ = s5_scan(g_l, disc, c_re[k], c_im[k], s_fin, rev)
        y_l = y_l + yl_k
        y_c = y_c + yc_k
    out_lat = s5_finish(y_l, u_l, z_l, d, w_glu, b_glu, w_out)
    out_ctx = s5_finish(y_c, u_c, z_c, d, w_glu, b_glu, w_out) if need_ctx else None
    return out_lat, out_ctx


def setup_inputs(seed: int = 0) -> dict:
    key = jax.random.key(seed)
    ks = jax.random.split(key, 32)

    def nrm(k, shape, scale):
        return jax.random.normal(k, shape, jnp.float32) * scale

    D, E = D_MODEL, MLA_WIDTH
    G, P, CH = S5_GROUPS, S5_STATE, S5_GROUP
    a_im_base = jnp.pi * jnp.arange(P, dtype=jnp.float32)
    return {
        'x': nrm(ks[0], (BATCH, SEQ, D), 1.0),
        'c': nrm(ks[1], (BATCH, D), 1.0),
        'ctx': nrm(ks[2], (BATCH, CTX_LEN, D), 1.0),
        'c_ctx': nrm(ks[3], (D,), 1.0),
        'ada_w': nrm(ks[4], (DEPTH, D, 3 * D), 0.5 * D ** -0.5),
        'ada_b': nrm(ks[5], (DEPTH, 3 * D), 0.01),
        'norm_g': 1.0 + nrm(ks[6], (DEPTH, D), 0.01),
        'mla_w_in': nrm(ks[7], (N_MLA_LAYERS, D, MLA_IN_WIDTH), D ** -0.5),
        'mla_q_norm': 1.0 + nrm(ks[8], (N_MLA_LAYERS, Q_LORA_RANK), 0.01),
        'mla_w_uq': nrm(ks[9], (N_MLA_LAYERS, Q_LORA_RANK, MLA_HEADS * QK_DIM), Q_LORA_RANK ** -0.5),
        'mla_kv_norm': 1.0 + nrm(ks[10], (N_MLA_LAYERS, KV_LORA_RANK), 0.01),
        'mla_w_ukv': nrm(ks[11], (N_MLA_LAYERS, KV_LORA_RANK, MLA_HEADS * (QK_NOPE_DIM + V_HEAD_DIM)),
                         KV_LORA_RANK ** -0.5),
        'mla_w_out': nrm(ks[12], (N_MLA_LAYERS, E, D), E ** -0.5),
        's5_w_in': nrm(ks[13], (N_S5_LAYERS, D, 2 * S5_WIDTH), D ** -0.5),
        's5_a_re': -0.5 + nrm(ks[14], (N_S5_LAYERS, 2, G, P), 0.01),
        's5_a_im': a_im_base + nrm(ks[15], (N_S5_LAYERS, 2, G, P), 0.01),
        's5_log_step': jax.random.uniform(ks[16], (N_S5_LAYERS, 2, G), jnp.float32,
                                          math.log(DT_MIN), math.log(DT_MAX)),
        's5_b_re': nrm(ks[17], (N_S5_LAYERS, 2, G, P, CH), (2 * CH) ** -0.5),
        's5_b_im': nrm(ks[18], (N_S5_LAYERS, 2, G, P, CH), (2 * CH) ** -0.5),
        's5_c_re': nrm(ks[19], (N_S5_LAYERS, 2, G, CH, P), P ** -0.5),
        's5_c_im': nrm(ks[20], (N_S5_LAYERS, 2, G, CH, P), P ** -0.5),
        's5_d': nrm(ks[21], (N_S5_LAYERS, S5_WIDTH), 1.0),
        's5_w_glu': nrm(ks[22], (N_S5_LAYERS, S5_WIDTH, S5_WIDTH), S5_WIDTH ** -0.5),
        's5_b_glu': nrm(ks[23], (N_S5_LAYERS, S5_WIDTH), 0.01),
        's5_w_out': nrm(ks[24], (N_S5_LAYERS, S5_WIDTH, D), S5_WIDTH ** -0.5),
        'final_g': 1.0 + nrm(ks[25], (D,), 0.01),
    }


def reference(x, c, ctx, c_ctx, ada_w, ada_b, norm_g,
              mla_w_in, mla_q_norm, mla_w_uq, mla_kv_norm, mla_w_ukv, mla_w_out,
              s5_w_in, s5_a_re, s5_a_im, s5_log_step, s5_b_re, s5_b_im, s5_c_re, s5_c_im,
              s5_d, s5_w_glu, s5_b_glu, s5_w_out, final_g):
    silu_c = jax.nn.silu(c)
    silu_cc = jax.nn.silu(c_ctx)
    for i in range(DEPTH):
        need_ctx = i < DEPTH - 1
        mod_l = silu_c @ ada_w[i] + ada_b[i]
        mod_c = silu_cc @ ada_w[i] + ada_b[i]
        sh_l, sc_l, gt_l = jnp.split(mod_l, 3, axis=-1)
        sh_c, sc_c, gt_c = jnp.split(mod_c, 3, axis=-1)
        h_l = rmsnorm(x, norm_g[i]) * (1.0 + sc_l[:, None, :]) + sh_l[:, None, :]
        h_c = rmsnorm(ctx, norm_g[i]) * (1.0 + sc_c) + sh_c
        j = i // N_MIXERS
        if i % N_MIXERS == 0:
            o_l, o_c = mla_mixer(h_l, h_c, need_ctx, mla_w_in[j], mla_q_norm[j], mla_w_uq[j],
                                 mla_kv_norm[j], mla_w_ukv[j], mla_w_out[j])
        else:
            o_l, o_c = s5_mixer(h_l, h_c, need_ctx, s5_w_in[j], s5_a_re[j], s5_a_im[j],
                                s5_log_step[j], s5_b_re[j], s5_b_im[j], s5_c_re[j], s5_c_im[j],
                                s5_d[j], s5_w_glu[j], s5_b_glu[j], s5_w_out[j])
        x = x + gt_l[:, None, :] * o_l
        if need_ctx:
            ctx = ctx + gt_c * o_c
    return rmsnorm(x, final_g)
```

```python
import functools
import math

import jax
import jax.numpy as jnp
from jax import lax
from jax.experimental import pallas as pl
from jax.experimental.pallas import tpu as pltpu

D_MODEL = 1024
BATCH = 8
SEQ = 2048
GRID_W = 64
CTX_LEN = 256
TOK = CTX_LEN + SEQ
EPS = 1e-6

MLA_HEADS = 16
QK_NOPE_DIM = 64
QK_ROPE_DIM = 32
V_HEAD_DIM = 64
Q_LORA_RANK = 256
KV_LORA_RANK = 128
MLA_WIDTH = MLA_HEADS * V_HEAD_DIM
QK_DIM = QK_NOPE_DIM + QK_ROPE_DIM
SOFTMAX_SCALE = QK_DIM ** -0.5
ROPE_THETA = 10000.0
HEAD_PAD = 128
QK_PAD = MLA_HEADS * HEAD_PAD
PROJ_W = 1536

S5_WIDTH = D_MODEL
S5_GROUP = 16
S5_GROUPS = 64
S5_STATE = 64
SLABS = 4
SLAB_CH = 256
SLAB_ST = 1024

TM = 256
TQ = 256
TB = 32
VMEM_LIMIT = 56 * 1024 * 1024

F32 = jnp.float32
BF16 = jnp.bfloat16


def _params(sem):
    return pltpu.CompilerParams(dimension_semantics=sem, vmem_limit_bytes=VMEM_LIMIT)


def _silu(v):
    return v * jax.nn.sigmoid(v)


def _rms(v, g):
    return v * lax.rsqrt(jnp.mean(v * v, axis=-1, keepdims=True) + EPS) * g


def _dot(a, b):
    return jnp.dot(a, b, preferred_element_type=F32)


def _mod_kernel(cc_ref, w_ref, b_ref, o_ref):
    a = _silu(cc_ref[...]).astype(BF16)
    o_ref[0] = _dot(a, w_ref[0].astype(BF16)) + b_ref[0]


def _modulation(cc, ada_w, ada_b):
    depth = ada_w.shape[0]
    tn = 768
    return pl.pallas_call(
        _mod_kernel,
        grid=(depth, 3 * D_MODEL // tn),
        in_specs=[
            pl.BlockSpec((16, D_MODEL), lambda i, j: (0, 0)),
            pl.BlockSpec((1, D_MODEL, tn), lambda i, j: (i, 0, j)),
            pl.BlockSpec((1, 1, tn), lambda i, j: (i, 0, j)),
        ],
        out_specs=pl.BlockSpec((1, 16, tn), lambda i, j: (i, 0, j)),
        out_shape=jax.ShapeDtypeStruct((depth, 16, 3 * D_MODEL), F32),
        compiler_params=_params(("arbitrary", "arbitrary")),
        name="modulation",
    )(cc, ada_w, ada_b.reshape(depth, 1, 3 * D_MODEL))


def _mla_proj_kernel(x_ref, mod_ref, g_ref, win_ref, qg_ref, kvg_ref, wqa_ref, wqb_ref, wk_ref, wv_ref,
                     cos_ref, sin_ref, kt_ref, q_ref, k_ref, v_ref, sz_ref):
    x = x_ref[0]
    mod = mod_ref[0, 0]
    sh = mod[:, :D_MODEL]
    sc = mod[:, D_MODEL:2 * D_MODEL]
    h = _rms(x, g_ref[...]) * (1.0 + sc) + sh
    p = _dot(h.astype(BF16), win_ref[...])
    cqn = _rms(p[:, :Q_LORA_RANK], qg_ref[...]).astype(BF16)
    ckvn = _rms(p[:, Q_LORA_RANK:Q_LORA_RANK + KV_LORA_RANK], kvg_ref[...]).astype(BF16)
    kr = p[:, 384:512]
    z = p[:, 512:]
    qa = _dot(cqn, wqa_ref[...])
    qb = _dot(cqn, wqb_ref[...])
    cos = cos_ref[...]
    sin = sin_ref[...]
    for hd in range(MLA_HEADS):
        sl = slice(hd * HEAD_PAD, (hd + 1) * HEAD_PAD)
        q_ref[0, :, sl] = (qa[:, sl] * cos + qb[:, sl] * sin).astype(BF16)
    kin = jnp.concatenate([ckvn, (kr * kt_ref[...]).astype(BF16)], axis=-1)
    k_ref[0] = _dot(kin, wk_ref[...]).astype(BF16)
    v_ref[0] = _dot(ckvn, wv_ref[...]).astype(BF16)
    sz_ref[0] = _silu(z).astype(BF16)


def _mla_proj(xc, mod, g, win, qg, kvg, wqa, wqb, wk, wv, cos, sin, kt):
    nct = CTX_LEN // TM
    full = lambda shape: pl.BlockSpec(shape, lambda b, i: (0,) * len(shape))
    tok = lambda w: pl.BlockSpec((1, TM, w), lambda b, i: (b, i, 0))
    pos = pl.BlockSpec((TM, HEAD_PAD), lambda b, i: (i, 0))
    return pl.pallas_call(
        _mla_proj_kernel,
        grid=(BATCH, TOK // TM),
        in_specs=[
            tok(D_MODEL),
            pl.BlockSpec((1, 1, 1, 3 * D_MODEL), lambda b, i: (b, jnp.where(i < nct, 0, 1), 0, 0)),
            full((1, D_MODEL)), full((D_MODEL, PROJ_W)), full((1, Q_LORA_RANK)), full((1, KV_LORA_RANK)),
            full((Q_LORA_RANK, QK_PAD)), full((Q_LORA_RANK, QK_PAD)), full((256, QK_PAD)),
            full((KV_LORA_RANK, MLA_WIDTH)), pos, pos, pos,
        ],
        out_specs=[tok(QK_PAD), tok(QK_PAD), tok(MLA_WIDTH), tok(MLA_WIDTH)],
        out_shape=[
            jax.ShapeDtypeStruct((BATCH, TOK, QK_PAD), BF16),
            jax.ShapeDtypeStruct((BATCH, TOK, QK_PAD), BF16),
            jax.ShapeDtypeStruct((BATCH, TOK, MLA_WIDTH), BF16),
            jax.ShapeDtypeStruct((BATCH, TOK, MLA_WIDTH), BF16),
        ],
        compiler_params=_params(("arbitrary", "arbitrary")),
        name="mla_proj",
    )(xc, mod, g, win, qg, kvg, wqa, wqb, wk, wv, cos, sin, kt)


def _attn_kernel(q_ref, k_ref, v_ref, o_ref):
    qi = pl.program_id(2)

    def body(nk):
        outs = []
        for hh in range(2):
            sl = slice(hh * HEAD_PAD, (hh + 1) * HEAD_PAD)
            q = q_ref[0, :, sl]
            k = k_ref[0, :nk, sl]
            s = lax.dot_general(q, k, (((1,), (1,)), ((), ())), preferred_element_type=F32)
            m = jnp.max(s, axis=-1, keepdims=True)
            p = jnp.exp(s - m)
            l = jnp.sum(p, axis=-1, keepdims=True)
            outs.append(_dot(p.astype(BF16), v_ref[0, :nk, :]) / l)
        lane = lax.broadcasted_iota(jnp.int32, outs[0].shape, 1)
        o_ref[0] = jnp.where(lane < V_HEAD_DIM, outs[0], outs[1]).astype(BF16)

    @pl.when(qi < CTX_LEN // TQ)
    def _():
        body(CTX_LEN)

    @pl.when(qi >= CTX_LEN // TQ)
    def _():
        body(TOK)


def _attention(q, k, v):
    return pl.pallas_call(
        _attn_kernel,
        grid=(BATCH, MLA_HEADS // 2, TOK // TQ),
        in_specs=[
            pl.BlockSpec((1, TQ, 2 * HEAD_PAD), lambda b, h, i: (b, i, h)),
            pl.BlockSpec((1, TOK, 2 * HEAD_PAD), lambda b, h, i: (b, 0, h)),
            pl.BlockSpec((1, TOK, 2 * V_HEAD_DIM), lambda b, h, i: (b, 0, h)),
        ],
        out_specs=pl.BlockSpec((1, TQ, 2 * V_HEAD_DIM), lambda b, h, i: (b, i, h)),
        out_shape=jax.ShapeDtypeStruct((BATCH, TOK, MLA_WIDTH), BF16),
        compiler_params=_params(("arbitrary", "arbitrary", "arbitrary")),
        name="attention",
    )(q, k, v)


def _mla_out_kernel(o_ref, sz_ref, x_ref, mod0_ref, mod1_ref, g1_ref, wout_ref, win_ref,
                    x1_ref, u_ref, sz1_ref):
    a = (o_ref[0].astype(F32) * sz_ref[0].astype(F32)).astype(BF16)
    gt = mod0_ref[0, 0][:, 2 * D_MODEL:]
    x1 = x_ref[0] + gt * _dot(a, wout_ref[...])
    x1_ref[0] = x1
    mod1 = mod1_ref[0, 0]
    h = _rms(x1, g1_ref[...]) * (1.0 + mod1[:, D_MODEL:2 * D_MODEL]) + mod1[:, :D_MODEL]
    p = _dot(h.astype(BF16), win_ref[...])
    u_ref[...] = p[:, :S5_WIDTH].astype(BF16)
    sz1_ref[0] = _silu(p[:, S5_WIDTH:]).astype(BF16)


def _mla_out(o, sz, xc, mod0, mod1, g1, wout, win):
    nct = CTX_LEN // TM
    full = lambda shape: pl.BlockSpec(shape, lambda b, i: (0,) * len(shape))
    tok = lambda w: pl.BlockSpec((1, TM, w), lambda b, i: (b, i, 0))
    modspec = pl.BlockSpec((1, 1, 1, 3 * D_MODEL), lambda b, i: (b, jnp.where(i < nct, 0, 1), 0, 0))
    return pl.pallas_call(
        _mla_out_kernel,
        grid=(BATCH, TOK // TM),
        in_specs=[tok(MLA_WIDTH), tok(MLA_WIDTH), tok(D_MODEL), modspec, modspec,
                  full((1, D_MODEL)), full((MLA_WIDTH, D_MODEL)), full((D_MODEL, 2 * S5_WIDTH))],
        out_specs=[tok(D_MODEL),
                   pl.BlockSpec((TM, S5_WIDTH), lambda b, i: (i, b)),
                   tok(S5_WIDTH)],
        out_shape=[
            jax.ShapeDtypeStruct((BATCH, TOK, D_MODEL), F32),
            jax.ShapeDtypeStruct((TOK, BATCH * S5_WIDTH), BF16),
            jax.ShapeDtypeStruct((BATCH, TOK, S5_WIDTH), BF16),
        ],
        compiler_params=_params(("arbitrary", "arbitrary")),
        name="mla_out_s5_in",
    )(o, sz, xc, mod0, mod1, g1, wout, win)


def _disc_kernel(are_ref, aim_ref, ls_ref, bre_ref, bim_ref, lbre_ref, lbim_ref, bbre_ref, bbim_ref):
    ar = are_ref[...]
    ai = aim_ref[...]
    dt = jnp.exp(ls_ref[...])
    mag = jnp.exp(ar * dt)
    lb_re = mag * jnp.cos(ai * dt)
    lb_im = mag * jnp.sin(ai * dt)
    den = ar * ar + ai * ai
    nr = lb_re - 1.0
    f_re = ((nr * ar + lb_im * ai) / den)[:, None, :]
    f_im = ((lb_im * ar - nr * ai) / den)[:, None, :]
    br = bre_ref[...]
    bi = bim_ref[...]
    lbre_ref[...] = lb_re
    lbim_ref[...] = lb_im
    bbre_ref[...] = f_re * br - f_im * bi
    bbim_ref[...] = f_re * bi + f_im * br


def _discretise(a_re, a_im, log_step, b_re_t, b_im_t):
    n = a_re.shape[0]
    return pl.pallas_call(
        _disc_kernel,
        out_shape=[
            jax.ShapeDtypeStruct((n, S5_STATE), F32),
            jax.ShapeDtypeStruct((n, S5_STATE), F32),
            jax.ShapeDtypeStruct((n, S5_GROUP, S5_STATE), F32),
            jax.ShapeDtypeStruct((n, S5_GROUP, S5_STATE), F32),
        ],
        name="s5_discretise",
    )(a_re, a_im, log_step, b_re_t, b_im_t)


def _scan_kernel(u_ref, bd_ref, cd_ref, are_ref, aim_ref, y_ref, state_ref, bu_ref, s_ref, *, reverse):
    @pl.when(pl.program_id(0) == 0)
    def _():
        state_ref[...] = jnp.zeros_like(state_ref)

    npair = TB // 2
    for slab in range(SLABS):
        bu_ref[...] = _dot(u_ref[:, slab * SLAB_CH:(slab + 1) * SLAB_CH], bd_ref[slab])
        a_re = are_ref[slab]
        a_im = aim_ref[slab]

        def step(kk, carry):
            s_re, s_im = carry
            kp = (npair - 1 - kk) if reverse else kk
            r0 = pl.multiple_of(kp * 16, 16)
            bre = bu_ref[pl.ds(r0, 16), :SLAB_ST]
            bim = bu_ref[pl.ds(r0, 16), SLAB_ST:]
            first, second = (slice(8, 16), slice(0, 8)) if reverse else (slice(0, 8), slice(8, 16))
            re1 = a_re * s_re - a_im * s_im + bre[first]
            im1 = a_re * s_im + a_im * s_re + bim[first]
            re2 = a_re * re1 - a_im * im1 + bre[second]
            im2 = a_re * im1 + a_im * re1 + bim[second]
            lo_re, hi_re = (re2, re1) if reverse else (re1, re2)
            lo_im, hi_im = (im2, im1) if reverse else (im1, im2)
            s_ref[pl.ds(r0, 16), :SLAB_ST] = jnp.concatenate([lo_re, hi_re], axis=0).astype(BF16)
            s_ref[pl.ds(r0, 16), SLAB_ST:] = jnp.concatenate([lo_im, hi_im], axis=0).astype(BF16)
            return re2, im2

        s_re, s_im = lax.fori_loop(0, npair, step, (state_ref[slab, 0], state_ref[slab, 1]), unroll=2)
        state_ref[slab, 0] = s_re
        state_ref[slab, 1] = s_im
        y_ref[:, slab * SLAB_CH:(slab + 1) * SLAB_CH] = _dot(s_ref[...], cd_ref[slab])


def _scan(u_rows, bd, cd, a_re, a_im, reverse):
    nb = TOK // TB
    nbc = CTX_LEN // TB
    if reverse:
        blk = lambda i: (jnp.where(i < nbc, nbc - 1 - i, nb - 1 - (i - nbc)), 0)
    else:
        blk = lambda i: (i, 0)
    full = lambda shape: pl.BlockSpec(shape, lambda i: (0,) * len(shape))
    rows = TB * BATCH
    return pl.pallas_call(
        functools.partial(_scan_kernel, reverse=reverse),
        grid=(nb,),
        in_specs=[
            pl.BlockSpec((rows, S5_WIDTH), blk),
            full((SLABS, SLAB_CH, 2 * SLAB_ST)), full((SLABS, 2 * SLAB_ST, SLAB_CH)),
            full((SLABS, BATCH, SLAB_ST)), full((SLABS, BATCH, SLAB_ST)),
        ],
        out_specs=pl.BlockSpec((rows, S5_WIDTH), blk),
        out_shape=jax.ShapeDtypeStruct((TOK * BATCH, S5_WIDTH), F32),
        scratch_shapes=[
            pltpu.VMEM((SLABS, 2, BATCH, SLAB_ST), F32),
            pltpu.VMEM((rows, 2 * SLAB_ST), F32),
            pltpu.VMEM((rows, 2 * SLAB_ST), BF16),
        ],
        compiler_params=_params(("arbitrary",)),
        name="s5_scan_bwd" if reverse else "s5_scan_fwd",
    )(u_rows, bd, cd, a_re, a_im)


def _fin_kernel(yf_ref, yb_ref, u_ref, sz_ref, x_ref, mod_ref, d_ref, wglu_ref, bglu_ref, wout_ref, fg_ref, o_ref):
    y = yf_ref[...] + yb_ref[...] + d_ref[...] * u_ref[...].astype(F32)
    y = jax.nn.gelu(y)
    y = y * jax.nn.sigmoid(_dot(y.astype(BF16), wglu_ref[...]) + bglu_ref[...])
    a = (y * sz_ref[0].astype(F32)).astype(BF16)
    gt = mod_ref[0, 0][:, 2 * D_MODEL:]
    x2 = x_ref[0] + gt * _dot(a, wout_ref[...])
    o_ref[0] = _rms(x2, fg_ref[...])


def _finish(yf, yb, u_tb, sz1, x1, mod1, d, wglu, bglu, wout, fg):
    nct = CTX_LEN // TM
    full = lambda shape: pl.BlockSpec(shape, lambda b, i: (0,) * len(shape))
    tmaj = pl.BlockSpec((TM, S5_WIDTH), lambda b, i: (i + nct, b))
    tok = pl.BlockSpec((1, TM, D_MODEL), lambda b, i: (b, i + nct, 0))
    return pl.pallas_call(
        _fin_kernel,
        grid=(BATCH, SEQ // TM),
        in_specs=[tmaj, tmaj, tmaj, tok, tok,
                  pl.BlockSpec((1, 1, 1, 3 * D_MODEL), lambda b, i: (b, 1, 0, 0)),
                  full((1, S5_WIDTH)), full((S5_WIDTH, S5_WIDTH)), full((1, S5_WIDTH)),
                  full((S5_WIDTH, D_MODEL)), full((1, D_MODEL))],
        out_specs=pl.BlockSpec((1, TM, D_MODEL), lambda b, i: (b, i, 0)),
        out_shape=jax.ShapeDtypeStruct((BATCH, SEQ, D_MODEL), F32),
        compiler_params=_params(("arbitrary", "arbitrary")),
        name="s5_finish",
    )(yf, yb, u_tb, sz1, x1, mod1, d, wglu, bglu, wout, fg)


def _rot_partner(w):
    return jnp.concatenate([-w[..., 8:16], w[..., 0:8], -w[..., 24:32], w[..., 16:24]], axis=-1)


def _rope_tables():
    h = QK_ROPE_DIM // 2
    inv = 1.0 / (ROPE_THETA ** (jnp.arange(0, h, 2, dtype=F32) / h))
    pos = jnp.arange(SEQ, dtype=jnp.int32)
    ang_r = (pos // GRID_W).astype(F32)[:, None] * inv[None, :]
    ang_c = (pos % GRID_W).astype(F32)[:, None] * inv[None, :]
    cos32 = jnp.concatenate([jnp.cos(ang_r)] * 2 + [jnp.cos(ang_c)] * 2, axis=-1)
    sin32 = jnp.concatenate([jnp.sin(ang_r)] * 2 + [jnp.sin(ang_c)] * 2, axis=-1)
    ones = jnp.ones((SEQ, QK_NOPE_DIM), F32)
    zeros = jnp.zeros((SEQ, QK_NOPE_DIM), F32)
    pad = jnp.zeros((SEQ, HEAD_PAD - QK_DIM), F32)
    cos_l = jnp.concatenate([ones, cos32, pad], axis=-1)
    sin_l = jnp.concatenate([zeros, sin32, pad], axis=-1)
    kt_l = jnp.concatenate([cos32, sin32, zeros], axis=-1)
    cos_c = jnp.concatenate([jnp.ones((CTX_LEN, QK_DIM), F32), jnp.zeros((CTX_LEN, HEAD_PAD - QK_DIM), F32)], -1)
    sin_c = jnp.zeros((CTX_LEN, HEAD_PAD), F32)
    kt_c = jnp.concatenate([jnp.ones((CTX_LEN, 32), F32), jnp.zeros((CTX_LEN, HEAD_PAD - 32), F32)], -1)
    return (jnp.concatenate([cos_c, cos_l], 0), jnp.concatenate([sin_c, sin_l], 0),
            jnp.concatenate([kt_c, kt_l], 0))


def _mla_weights(w_in, w_uq, w_ukv):
    o1, o2, o3 = Q_LORA_RANK, Q_LORA_RANK + KV_LORA_RANK, Q_LORA_RANK + KV_LORA_RANK + QK_ROPE_DIM
    w_kr = w_in[:, o2:o3]
    win = jnp.concatenate([w_in[:, :o2], w_kr, _rot_partner(w_kr), jnp.zeros((D_MODEL, 64), F32), w_in[:, o3:]],
                          axis=-1).astype(BF16)
    wq = (w_uq * SOFTMAX_SCALE).reshape(Q_LORA_RANK, MLA_HEADS, QK_DIM)
    zq = jnp.zeros((Q_LORA_RANK, MLA_HEADS, HEAD_PAD - QK_DIM), F32)
    wqa = jnp.concatenate([wq, zq], axis=-1).reshape(Q_LORA_RANK, QK_PAD).astype(BF16)
    wqb = jnp.concatenate([jnp.zeros_like(wq[..., :QK_NOPE_DIM]), _rot_partner(wq[..., QK_NOPE_DIM:]), zq],
                          axis=-1).reshape(Q_LORA_RANK, QK_PAD).astype(BF16)
    wkv = w_ukv.reshape(KV_LORA_RANK, MLA_HEADS, QK_NOPE_DIM + V_HEAD_DIM)
    wk_top = jnp.concatenate([wkv[..., :QK_NOPE_DIM], jnp.zeros((KV_LORA_RANK, MLA_HEADS, 64), F32)], axis=-1)
    eye = jnp.eye(QK_ROPE_DIM, dtype=F32)[:, None, :]
    place = jnp.concatenate([jnp.zeros((QK_ROPE_DIM, MLA_HEADS, QK_NOPE_DIM), F32),
                             jnp.broadcast_to(eye, (QK_ROPE_DIM, MLA_HEADS, QK_ROPE_DIM)),
                             jnp.zeros((QK_ROPE_DIM, MLA_HEADS, HEAD_PAD - QK_DIM), F32)], axis=-1)
    wk = jnp.concatenate([wk_top, place, place, jnp.zeros((64, MLA_HEADS, HEAD_PAD), F32)], axis=0)
    wk = wk.reshape(256, QK_PAD).astype(BF16)
    wv = wkv[..., QK_NOPE_DIM:].reshape(KV_LORA_RANK, MLA_WIDTH).astype(BF16)
    return win, wqa, wqb, wk, wv


def _s5_weights(lb_re, lb_im, bb_re, bb_im, c_re, c_im):
    eye = jnp.eye(16, dtype=F32)

    def place_b(bb):
        bb = bb.reshape(2, SLABS, 16, S5_GROUP, S5_STATE)
        return jnp.einsum('dsgcp,gh->dsgchp', bb, eye).reshape(2, SLABS, SLAB_CH, SLAB_ST)

    def place_c(cc):
        cc = cc.reshape(2, SLABS, 16, S5_GROUP, S5_STATE)
        return jnp.einsum('dsgcp,gh->dsgphc', cc, eye).reshape(2, SLABS, SLAB_ST, SLAB_CH)

    bd = jnp.concatenate([place_b(bb_re), place_b(bb_im)], axis=-1).astype(BF16)
    cd = jnp.concatenate([place_c(c_re), -place_c(c_im)], axis=2).astype(BF16)

    def lanes(lb):
        lb = lb.reshape(2, SLABS, 1, SLAB_ST)
        return jnp.broadcast_to(lb, (2, SLABS, BATCH, SLAB_ST))

    return bd, cd, lanes(lb_re), lanes(lb_im)


def kernel(x, c, ctx, c_ctx, ada_w, ada_b, norm_g, mla_w_in, mla_q_norm, mla_w_uq, mla_kv_norm, mla_w_ukv, mla_w_out, s5_w_in, s5_a_re, s5_a_im, s5_log_step, s5_b_re, s5_b_im, s5_c_re, s5_c_im, s5_d, s5_w_glu, s5_b_glu, s5_w_out, final_g):
    cc = jnp.concatenate([c, c_ctx[None, :], jnp.zeros((7, D_MODEL), F32)], axis=0)
    mods = _modulation(cc, ada_w, ada_b)

    def mod_rows(i):
        ctx_row = jnp.broadcast_to(mods[i, 8][None, :], (BATCH, 3 * D_MODEL))
        return jnp.stack([ctx_row, mods[i, :BATCH]], axis=1)[:, :, None, :]

    mod0, mod1 = mod_rows(0), mod_rows(1)
    xc = jnp.concatenate([ctx, x], axis=1)

    win, wqa, wqb, wk, wv = _mla_weights(mla_w_in[0], mla_w_uq[0], mla_w_ukv[0])
    cos, sin, kt = _rope_tables()
    q, k, v, sz = _mla_proj(xc, mod0, norm_g[0][None, :], win, mla_q_norm[0][None, :], mla_kv_norm[0][None, :],
                            wqa, wqb, wk, wv, cos, sin, kt)
    o = _attention(q, k, v)
    x1, u_tb, sz1 = _mla_out(o, sz, xc, mod0, mod1, norm_g[1][None, :], mla_w_out[0].astype(BF16),
                             s5_w_in[0].astype(BF16))

    n = 2 * S5_GROUPS
    lb_re, lb_im, bb_re, bb_im = _discretise(
        s5_a_re[0].reshape(n, S5_STATE), s5_a_im[0].reshape(n, S5_STATE), s5_log_step[0].reshape(n, 1),
        jnp.swapaxes(s5_b_re[0], -1, -2).reshape(n, S5_GROUP, S5_STATE),
        jnp.swapaxes(s5_b_im[0], -1, -2).reshape(n, S5_GROUP, S5_STATE))
    bd, cd, a_re, a_im = _s5_weights(lb_re, lb_im, bb_re, bb_im, s5_c_re[0], s5_c_im[0])
    u_rows = u_tb.reshape(TOK * BATCH, S5_WIDTH)
    yf = _scan(u_rows, bd[0], cd[0], a_re[0], a_im[0], reverse=False)
    yb = _scan(u_rows, bd[1], cd[1], a_re[1], a_im[1], reverse=True)
    return _finish(yf.reshape(TOK, BATCH * S5_WIDTH), yb.reshape(TOK, BATCH * S5_WIDTH), u_tb, sz1, x1, mod1,
                   s5_d[0][None, :], s5_w_glu[0].astype(BF16), s5_b_glu[0][None, :], s5_w_out[0].astype(BF16),
                   final_g[None, :])
```

```python
import functools
import math

import jax
import jax.numpy as jnp
from jax import lax
from jax.experimental import pallas as pl
from jax.experimental.pallas import tpu as pltpu

D_MODEL = 1024
BATCH = 8
SEQ = 2048
GRID_W = 64
CTX_LEN = 256
TOK = CTX_LEN + SEQ
EPS = 1e-6

MLA_HEADS = 16
QK_NOPE_DIM = 64
QK_ROPE_DIM = 32
V_HEAD_DIM = 64
Q_LORA_RANK = 256
KV_LORA_RANK = 128
MLA_WIDTH = MLA_HEADS * V_HEAD_DIM
QK_DIM = QK_NOPE_DIM + QK_ROPE_DIM
SOFTMAX_SCALE = QK_DIM ** -0.5
ROPE_THETA = 10000.0
HEAD_PAD = 128
QK_PAD = MLA_HEADS * HEAD_PAD
PROJ_W = 1536

S5_WIDTH = D_MODEL
S5_GROUP = 16
S5_GROUPS = 64
S5_STATE = 64
SLABS = 4
SLAB_CH = 256
SLAB_ST = 1024

TM = 256
TQ = 256
KCH = 256
TB = 32
VMEM_LIMIT = 56 * 1024 * 1024

F32 = jnp.float32
BF16 = jnp.bfloat16


def _params(sem, flags=None):
    return pltpu.CompilerParams(dimension_semantics=sem, vmem_limit_bytes=VMEM_LIMIT, flags=flags)


def _silu(v):
    return v * jax.nn.sigmoid(v)


def _rms(v, g):
    return v * lax.rsqrt(jnp.mean(v * v, axis=-1, keepdims=True) + EPS) * g


def _dot(a, b):
    return jnp.dot(a, b, preferred_element_type=F32)


def _mod_kernel(cc_ref, w_ref, b_ref, o_ref):
    a = _silu(cc_ref[...]).astype(BF16)
    o_ref[0] = _dot(a, w_ref[0].astype(BF16)) + b_ref[0]


def _modulation(cc, ada_w, ada_b):
    depth = ada_w.shape[0]
    tn = 768
    return pl.pallas_call(
        _mod_kernel,
        grid=(depth, 3 * D_MODEL // tn),
        in_specs=[
            pl.BlockSpec((16, D_MODEL), lambda i, j: (0, 0)),
            pl.BlockSpec((1, D_MODEL, tn), lambda i, j: (i, 0, j)),
            pl.BlockSpec((1, 1, tn), lambda i, j: (i, 0, j)),
        ],
        out_specs=pl.BlockSpec((1, 16, tn), lambda i, j: (i, 0, j)),
        out_shape=jax.ShapeDtypeStruct((depth, 16, 3 * D_MODEL), F32),
        compiler_params=_params(("arbitrary", "arbitrary")),
        name="modulation",
    )(cc, ada_w, ada_b.reshape(depth, 1, 3 * D_MODEL))


def _mla_proj_kernel(x_ref, mod_ref, g_ref, win_ref, qg_ref, kvg_ref, wqa_ref, wqb_ref, wk_ref, wv_ref,
                     cos_ref, sin_ref, kt_ref, q_ref, k_ref, v_ref, sz_ref):
    x = x_ref[0]
    mod = mod_ref[0, 0]
    sh = mod[:, :D_MODEL]
    sc = mod[:, D_MODEL:2 * D_MODEL]
    h = _rms(x, g_ref[...]) * (1.0 + sc) + sh
    p = _dot(h.astype(BF16), win_ref[...])
    cqn = _rms(p[:, :Q_LORA_RANK], qg_ref[...]).astype(BF16)
    ckvn = _rms(p[:, Q_LORA_RANK:Q_LORA_RANK + KV_LORA_RANK], kvg_ref[...]).astype(BF16)
    kr = p[:, 384:512]
    z = p[:, 512:]
    qa = _dot(cqn, wqa_ref[...])
    qb = _dot(cqn, wqb_ref[...])
    cos = cos_ref[...]
    sin = sin_ref[...]
    for hd in range(MLA_HEADS):
        sl = slice(hd * HEAD_PAD, (hd + 1) * HEAD_PAD)
        q_ref[0, :, sl] = (qa[:, sl] * cos + qb[:, sl] * sin).astype(BF16)
    kin = jnp.concatenate([ckvn, (kr * kt_ref[...]).astype(BF16)], axis=-1)
    k_ref[0] = _dot(kin, wk_ref[...]).astype(BF16)
    v_ref[0] = _dot(ckvn, wv_ref[...]).astype(BF16)
    sz_ref[0] = _silu(z).astype(BF16)


def _mla_proj(xc, mod, g, win, qg, kvg, wqa, wqb, wk, wv, cos, sin, kt):
    nct = CTX_LEN // TM
    full = lambda shape: pl.BlockSpec(shape, lambda b, i: (0,) * len(shape))
    tok = lambda w: pl.BlockSpec((1, TM, w), lambda b, i: (b, i, 0))
    pos = pl.BlockSpec((TM, HEAD_PAD), lambda b, i: (i, 0))
    return pl.pallas_call(
        _mla_proj_kernel,
        grid=(BATCH, TOK // TM),
        in_specs=[
            tok(D_MODEL),
            pl.BlockSpec((1, 1, 1, 3 * D_MODEL), lambda b, i: (b, jnp.where(i < nct, 0, 1), 0, 0)),
            full((1, D_MODEL)), full((D_MODEL, PROJ_W)), full((1, Q_LORA_RANK)), full((1, KV_LORA_RANK)),
            full((Q_LORA_RANK, QK_PAD)), full((Q_LORA_RANK, QK_PAD)), full((256, QK_PAD)),
            full((KV_LORA_RANK, MLA_WIDTH)), pos, pos, pos,
        ],
        out_specs=[tok(QK_PAD), tok(QK_PAD), tok(MLA_WIDTH), tok(MLA_WIDTH)],
        out_shape=[
            jax.ShapeDtypeStruct((BATCH, TOK, QK_PAD), BF16),
            jax.ShapeDtypeStruct((BATCH, TOK, QK_PAD), BF16),
            jax.ShapeDtypeStruct((BATCH, TOK, MLA_WIDTH), BF16),
            jax.ShapeDtypeStruct((BATCH, TOK, MLA_WIDTH), BF16),
        ],
        compiler_params=_params(("arbitrary", "arbitrary")),
        name="mla_proj",
    )(xc, mod, g, win, qg, kvg, wqa, wqb, wk, wv, cos, sin, kt)


def _attn_kernel(q_ref, k_ref, v_ref, o_ref, s_buf, m_buf):
    nt = SEQ // TQ

    def scores(row, nk, slot):
        for hh in range(2):
            sl = slice(hh * HEAD_PAD, (hh + 1) * HEAD_PAD)
            s = lax.dot_general(q_ref[0, pl.ds(row, TQ), sl], k_ref[0, :nk, sl], (((1,), (1,)), ((), ())),
                                preferred_element_type=F32)
            s_buf[slot, hh, :, :nk] = s
            m_buf[slot, hh] = jnp.broadcast_to(jnp.max(s, axis=-1, keepdims=True), (TQ, KCH))

    def values(row, nk, slot):
        outs = []
        for hh in range(2):
            m = m_buf[slot, hh]
            lsum = jnp.zeros((TQ, KCH), F32)
            ps = []
            for n in range(nk // KCH):
                pn = jnp.exp2(s_buf[slot, hh, :, n * KCH:(n + 1) * KCH] - m)
                lsum = lsum + pn
                ps.append(pn.astype(BF16))
            l = jnp.sum(lsum, axis=-1, keepdims=True)
            outs.append(_dot(jnp.concatenate(ps, axis=-1), v_ref[0, :nk, :]) / l)
        lane = lax.broadcasted_iota(jnp.int32, outs[0].shape, 1)
        o_ref[0, pl.ds(row, TQ), :] = jnp.where(lane < V_HEAD_DIM, outs[0], outs[1]).astype(BF16)

    def lat_row(t):
        return pl.multiple_of(CTX_LEN + t * TQ, TQ)

    scores(0, CTX_LEN, 0)
    values(0, CTX_LEN, 0)

    scores(lat_row(0), TOK, 0)

    def pair(pp, carry):
        t = 1 + 2 * pp
        scores(lat_row(t), TOK, 1)
        values(lat_row(t - 1), TOK, 0)
        scores(lat_row(t + 1), TOK, 0)
        values(lat_row(t), TOK, 1)
        return carry

    lax.fori_loop(0, (nt - 2) // 2, pair, 0)
    scores(lat_row(nt - 1), TOK, 1)
    values(lat_row(nt - 2), TOK, 0)
    values(lat_row(nt - 1), TOK, 1)


def _attention(q, k, v):
    return pl.pallas_call(
        _attn_kernel,
        grid=(BATCH, MLA_HEADS // 2),
        in_specs=[
            pl.BlockSpec((1, TOK, 2 * HEAD_PAD), lambda b, h: (b, 0, h)),
            pl.BlockSpec((1, TOK, 2 * HEAD_PAD), lambda b, h: (b, 0, h)),
            pl.BlockSpec((1, TOK, 2 * V_HEAD_DIM), lambda b, h: (b, 0, h)),
        ],
        out_specs=pl.BlockSpec((1, TOK, 2 * V_HEAD_DIM), lambda b, h: (b, 0, h)),
        out_shape=jax.ShapeDtypeStruct((BATCH, TOK, MLA_WIDTH), BF16),
        scratch_shapes=[
            pltpu.VMEM((2, 2, TQ, TOK), F32),
            pltpu.VMEM((2, 2, TQ, KCH), F32),
        ],
        compiler_params=_params(("arbitrary", "arbitrary")),
        name="attention",
    )(q, k, v)


def _mla_out_kernel(o_ref, sz_ref, x_ref, mod0_ref, mod1_ref, g1_ref, wout_ref, win_ref,
                    x1_ref, u_ref, sz1_ref):
    a = (o_ref[0].astype(F32) * sz_ref[0].astype(F32)).astype(BF16)
    gt = mod0_ref[0, 0][:, 2 * D_MODEL:]
    x1 = x_ref[0] + gt * _dot(a, wout_ref[...])
    x1_ref[0] = x1
    mod1 = mod1_ref[0, 0]
    h = _rms(x1, g1_ref[...]) * (1.0 + mod1[:, D_MODEL:2 * D_MODEL]) + mod1[:, :D_MODEL]
    p = _dot(h.astype(BF16), win_ref[...])
    u_ref[...] = p[:, :S5_WIDTH].astype(BF16)
    sz1_ref[0] = _silu(p[:, S5_WIDTH:]).astype(BF16)


def _mla_out(o, sz, xc, mod0, mod1, g1, wout, win):
    nct = CTX_LEN // TM
    full = lambda shape: pl.BlockSpec(shape, lambda b, i: (0,) * len(shape))
    tok = lambda w: pl.BlockSpec((1, TM, w), lambda b, i: (b, i, 0))
    modspec = pl.BlockSpec((1, 1, 1, 3 * D_MODEL), lambda b, i: (b, jnp.where(i < nct, 0, 1), 0, 0))
    return pl.pallas_call(
        _mla_out_kernel,
        grid=(BATCH, TOK // TM),
        in_specs=[tok(MLA_WIDTH), tok(MLA_WIDTH), tok(D_MODEL), modspec, modspec,
                  full((1, D_MODEL)), full((MLA_WIDTH, D_MODEL)), full((D_MODEL, 2 * S5_WIDTH))],
        out_specs=[tok(D_MODEL),
                   pl.BlockSpec((TM, S5_WIDTH), lambda b, i: (i, b)),
                   tok(S5_WIDTH)],
        out_shape=[
            jax.ShapeDtypeStruct((BATCH, TOK, D_MODEL), F32),
            jax.ShapeDtypeStruct((TOK, BATCH * S5_WIDTH), BF16),
            jax.ShapeDtypeStruct((BATCH, TOK, S5_WIDTH), BF16),
        ],
        compiler_params=_params(("arbitrary", "arbitrary")),
        name="mla_out_s5_in",
    )(o, sz, xc, mod0, mod1, g1, wout, win)


def _disc_kernel(are_ref, aim_ref, ls_ref, bre_ref, bim_ref, lbre_ref, lbim_ref, bbre_ref, bbim_ref):
    ar = are_ref[...]
    ai = aim_ref[...]
    dt = jnp.exp(ls_ref[...])
    mag = jnp.exp(ar * dt)
    lb_re = mag * jnp.cos(ai * dt)
    lb_im = mag * jnp.sin(ai * dt)
    den = ar * ar + ai * ai
    nr = lb_re - 1.0
    f_re = ((nr * ar + lb_im * ai) / den)[:, None, :]
    f_im = ((lb_im * ar - nr * ai) / den)[:, None, :]
    br = bre_ref[...]
    bi = bim_ref[...]
    lbre_ref[...] = lb_re
    lbim_ref[...] = lb_im
    bbre_ref[...] = f_re * br - f_im * bi
    bbim_ref[...] = f_re * bi + f_im * br


def _discretise(a_re, a_im, log_step, b_re_t, b_im_t):
    n = a_re.shape[0]
    return pl.pallas_call(
        _disc_kernel,
        out_shape=[
            jax.ShapeDtypeStruct((n, S5_STATE), F32),
            jax.ShapeDtypeStruct((n, S5_STATE), F32),
            jax.ShapeDtypeStruct((n, S5_GROUP, S5_STATE), F32),
            jax.ShapeDtypeStruct((n, S5_GROUP, S5_STATE), F32),
        ],
        name="s5_discretise",
    )(a_re, a_im, log_step, b_re_t, b_im_t)


def _scan_kernel(u_ref, bd_ref, cd_ref, are_ref, aim_ref, y_ref, state_ref, bu_ref, s_ref, *, reverse):
    @pl.when(pl.program_id(0) == 0)
    def _():
        state_ref[...] = jnp.zeros_like(state_ref)

    npair = TB // 2
    for slab in range(SLABS):
        bu_ref[...] = _dot(u_ref[:, slab * SLAB_CH:(slab + 1) * SLAB_CH], bd_ref[slab])
        a_re = are_ref[slab]
        a_im = aim_ref[slab]

        def step(kk, carry):
            s_re, s_im = carry
            kp = (npair - 1 - kk) if reverse else kk
            r0 = pl.multiple_of(kp * 16, 16)
            bre = bu_ref[pl.ds(r0, 16), :SLAB_ST]
            bim = bu_ref[pl.ds(r0, 16), SLAB_ST:]
            first, second = (slice(8, 16), slice(0, 8)) if reverse else (slice(0, 8), slice(8, 16))
            re1 = a_re * s_re - a_im * s_im + bre[first]
            im1 = a_re * s_im + a_im * s_re + bim[first]
            re2 = a_re * re1 - a_im * im1 + bre[second]
            im2 = a_re * im1 + a_im * re1 + bim[second]
            lo_re, hi_re = (re2, re1) if reverse else (re1, re2)
            lo_im, hi_im = (im2, im1) if reverse else (im1, im2)
            s_ref[pl.ds(r0, 16), :SLAB_ST] = jnp.concatenate([lo_re, hi_re], axis=0).astype(BF16)
            s_ref[pl.ds(r0, 16), SLAB_ST:] = jnp.concatenate([lo_im, hi_im], axis=0).astype(BF16)
            return re2, im2

        s_re, s_im = lax.fori_loop(0, npair, step, (state_ref[slab, 0], state_ref[slab, 1]), unroll=2)
        state_ref[slab, 0] = s_re
        state_ref[slab, 1] = s_im
        y_ref[:, slab * SLAB_CH:(slab + 1) * SLAB_CH] = _dot(s_ref[...], cd_ref[slab])


def _scan(u_rows, bd, cd, a_re, a_im, reverse):
    nb = TOK // TB
    nbc = CTX_LEN // TB
    if reverse:
        blk = lambda i: (jnp.where(i < nbc, nbc - 1 - i, nb - 1 - (i - nbc)), 0)
    else:
        blk = lambda i: (i, 0)
    full = lambda shape: pl.BlockSpec(shape, lambda i: (0,) * len(shape))
    rows = TB * BATCH
    return pl.pallas_call(
        functools.partial(_scan_kernel, reverse=reverse),
        grid=(nb,),
        in_specs=[
            pl.BlockSpec((rows, S5_WIDTH), blk),
            full((SLABS, SLAB_CH, 2 * SLAB_ST)), full((SLABS, 2 * SLAB_ST, SLAB_CH)),
            full((SLABS, BATCH, SLAB_ST)), full((SLABS, BATCH, SLAB_ST)),
        ],
        out_specs=pl.BlockSpec((rows, S5_WIDTH), blk),
        out_shape=jax.ShapeDtypeStruct((TOK * BATCH, S5_WIDTH), F32),
        scratch_shapes=[
            pltpu.VMEM((SLABS, 2, BATCH, SLAB_ST), F32),
            pltpu.VMEM((rows, 2 * SLAB_ST), F32),
            pltpu.VMEM((rows, 2 * SLAB_ST), BF16),
        ],
        compiler_params=_params(("arbitrary",)),
        name="s5_scan_bwd" if reverse else "s5_scan_fwd",
    )(u_rows, bd, cd, a_re, a_im)


def _fin_kernel(yf_ref, yb_ref, u_ref, sz_ref, x_ref, mod_ref, d_ref, wglu_ref, bglu_ref, wout_ref, fg_ref, o_ref):
    y = yf_ref[...] + yb_ref[...] + d_ref[...] * u_ref[...].astype(F32)
    y = jax.nn.gelu(y)
    y = y * jax.nn.sigmoid(_dot(y.astype(BF16), wglu_ref[...]) + bglu_ref[...])
    a = (y * sz_ref[0].astype(F32)).astype(BF16)
    gt = mod_ref[0, 0][:, 2 * D_MODEL:]
    x2 = x_ref[0] + gt * _dot(a, wout_ref[...])
    o_ref[0] = _rms(x2, fg_ref[...])


def _finish(yf, yb, u_tb, sz1, x1, mod1, d, wglu, bglu, wout, fg):
    nct = CTX_LEN // TM
    full = lambda shape: pl.BlockSpec(shape, lambda b, i: (0,) * len(shape))
    tmaj = pl.BlockSpec((TM, S5_WIDTH), lambda b, i: (i + nct, b))
    tok = pl.BlockSpec((1, TM, D_MODEL), lambda b, i: (b, i + nct, 0))
    return pl.pallas_call(
        _fin_kernel,
        grid=(BATCH, SEQ // TM),
        in_specs=[tmaj, tmaj, tmaj, tok, tok,
                  pl.BlockSpec((1, 1, 1, 3 * D_MODEL), lambda b, i: (b, 1, 0, 0)),
                  full((1, S5_WIDTH)), full((S5_WIDTH, S5_WIDTH)), full((1, S5_WIDTH)),
                  full((S5_WIDTH, D_MODEL)), full((1, D_MODEL))],
        out_specs=pl.BlockSpec((1, TM, D_MODEL), lambda b, i: (b, i, 0)),
        out_shape=jax.ShapeDtypeStruct((BATCH, SEQ, D_MODEL), F32),
        compiler_params=_params(("arbitrary", "arbitrary")),
        name="s5_finish",
    )(yf, yb, u_tb, sz1, x1, mod1, d, wglu, bglu, wout, fg)


def _rot_partner(w):
    return jnp.concatenate([-w[..., 8:16], w[..., 0:8], -w[..., 24:32], w[..., 16:24]], axis=-1)


def _rope_tables():
    h = QK_ROPE_DIM // 2
    inv = 1.0 / (ROPE_THETA ** (jnp.arange(0, h, 2, dtype=F32) / h))
    pos = jnp.arange(SEQ, dtype=jnp.int32)
    ang_r = (pos // GRID_W).astype(F32)[:, None] * inv[None, :]
    ang_c = (pos % GRID_W).astype(F32)[:, None] * inv[None, :]
    cos32 = jnp.concatenate([jnp.cos(ang_r)] * 2 + [jnp.cos(ang_c)] * 2, axis=-1)
    sin32 = jnp.concatenate([jnp.sin(ang_r)] * 2 + [jnp.sin(ang_c)] * 2, axis=-1)
    ones = jnp.ones((SEQ, QK_NOPE_DIM), F32)
    zeros = jnp.zeros((SEQ, QK_NOPE_DIM), F32)
    pad = jnp.zeros((SEQ, HEAD_PAD - QK_DIM), F32)
    cos_l = jnp.concatenate([ones, cos32, pad], axis=-1)
    sin_l = jnp.concatenate([zeros, sin32, pad], axis=-1)
    kt_l = jnp.concatenate([cos32, sin32, zeros], axis=-1)
    cos_c = jnp.concatenate([jnp.ones((CTX_LEN, QK_DIM), F32), jnp.zeros((CTX_LEN, HEAD_PAD - QK_DIM), F32)], -1)
    sin_c = jnp.zeros((CTX_LEN, HEAD_PAD), F32)
    kt_c = jnp.concatenate([jnp.ones((CTX_LEN, 32), F32), jnp.zeros((CTX_LEN, HEAD_PAD - 32), F32)], -1)
    return (jnp.concatenate([cos_c, cos_l], 0), jnp.concatenate([sin_c, sin_l], 0),
            jnp.concatenate([kt_c, kt_l], 0))


def _mla_weights(w_in, w_uq, w_ukv):
    o1, o2, o3 = Q_LORA_RANK, Q_LORA_RANK + KV_LORA_RANK, Q_LORA_RANK + KV_LORA_RANK + QK_ROPE_DIM
    w_kr = w_in[:, o2:o3]
    win = jnp.concatenate([w_in[:, :o2], w_kr, _rot_partner(w_kr), jnp.zeros((D_MODEL, 64), F32), w_in[:, o3:]],
                          axis=-1).astype(BF16)
    wq = (w_uq * (SOFTMAX_SCALE * math.log2(math.e))).reshape(Q_LORA_RANK, MLA_HEADS, QK_DIM)
    zq = jnp.zeros((Q_LORA_RANK, MLA_HEADS, HEAD_PAD - QK_DIM), F32)
    wqa = jnp.concatenate([wq, zq], axis=-1).reshape(Q_LORA_RANK, QK_PAD).astype(BF16)
    wqb = jnp.concatenate([jnp.zeros_like(wq[..., :QK_NOPE_DIM]), _rot_partner(wq[..., QK_NOPE_DIM:]), zq],
                          axis=-1).reshape(Q_LORA_RANK, QK_PAD).astype(BF16)
    wkv = w_ukv.reshape(KV_LORA_RANK, MLA_HEADS, QK_NOPE_DIM + V_HEAD_DIM)
    wk_top = jnp.concatenate([wkv[..., :QK_NOPE_DIM], jnp.zeros((KV_LORA_RANK, MLA_HEADS, 64), F32)], axis=-1)
    eye = jnp.eye(QK_ROPE_DIM, dtype=F32)[:, None, :]
    place = jnp.concatenate([jnp.zeros((QK_ROPE_DIM, MLA_HEADS, QK_NOPE_DIM), F32),
                             jnp.broadcast_to(eye, (QK_ROPE_DIM, MLA_HEADS, QK_ROPE_DIM)),
                             jnp.zeros((QK_ROPE_DIM, MLA_HEADS, HEAD_PAD - QK_DIM), F32)], axis=-1)
    wk = jnp.concatenate([wk_top, place, place, jnp.zeros((64, MLA_HEADS, HEAD_PAD), F32)], axis=0)
    wk = wk.reshape(256, QK_PAD).astype(BF16)
    wv = wkv[..., QK_NOPE_DIM:].reshape(KV_LORA_RANK, MLA_WIDTH).astype(BF16)
    return win, wqa, wqb, wk, wv


def _s5_weights(lb_re, lb_im, bb_re, bb_im, c_re, c_im):
    eye = jnp.eye(16, dtype=F32)

    def place_b(bb):
        bb = bb.reshape(2, SLABS, 16, S5_GROUP, S5_STATE)
        return jnp.einsum('dsgcp,gh->dsgchp', bb, eye).reshape(2, SLABS, SLAB_CH, SLAB_ST)

    def place_c(cc):
        cc = cc.reshape(2, SLABS, 16, S5_GROUP, S5_STATE)
        return jnp.einsum('dsgcp,gh->dsgphc', cc, eye).reshape(2, SLABS, SLAB_ST, SLAB_CH)

    bd = jnp.concatenate([place_b(bb_re), place_b(bb_im)], axis=-1).astype(BF16)
    cd = jnp.concatenate([place_c(c_re), -place_c(c_im)], axis=2).astype(BF16)

    def lanes(lb):
        lb = lb.reshape(2, SLABS, 1, SLAB_ST)
        return jnp.broadcast_to(lb, (2, SLABS, BATCH, SLAB_ST))

    return bd, cd, lanes(lb_re), lanes(lb_im)


def kernel(x, c, ctx, c_ctx, ada_w, ada_b, norm_g, mla_w_in, mla_q_norm, mla_w_uq, mla_kv_norm, mla_w_ukv, mla_w_out, s5_w_in, s5_a_re, s5_a_im, s5_log_step, s5_b_re, s5_b_im, s5_c_re, s5_c_im, s5_d, s5_w_glu, s5_b_glu, s5_w_out, final_g):
    cc = jnp.concatenate([c, c_ctx[None, :], jnp.zeros((7, D_MODEL), F32)], axis=0)
    mods = _modulation(cc, ada_w, ada_b)

    def mod_rows(i):
        ctx_row = jnp.broadcast_to(mods[i, 8][None, :], (BATCH, 3 * D_MODEL))
        return jnp.stack([ctx_row, mods[i, :BATCH]], axis=1)[:, :, None, :]

    mod0, mod1 = mod_rows(0), mod_rows(1)
    xc = jnp.concatenate([ctx, x], axis=1)

    win, wqa, wqb, wk, wv = _mla_weights(mla_w_in[0], mla_w_uq[0], mla_w_ukv[0])
    cos, sin, kt = _rope_tables()
    q, k, v, sz = _mla_proj(xc, mod0, norm_g[0][None, :], win, mla_q_norm[0][None, :], mla_kv_norm[0][None, :],
                            wqa, wqb, wk, wv, cos, sin, kt)
    o = _attention(q, k, v)
    x1, u_tb, sz1 = _mla_out(o, sz, xc, mod0, mod1, norm_g[1][None, :], mla_w_out[0].astype(BF16),
                             s5_w_in[0].astype(BF16))

    n = 2 * S5_GROUPS
    lb_re, lb_im, bb_re, bb_im = _discretise(
        s5_a_re[0].reshape(n, S5_STATE), s5_a_im[0].reshape(n, S5_STATE), s5_log_step[0].reshape(n, 1),
        jnp.swapaxes(s5_b_re[0], -1, -2).reshape(n, S5_GROUP, S5_STATE),
        jnp.swapaxes(s5_b_im[0], -1, -2).reshape(n, S5_GROUP, S5_STATE))
    bd, cd, a_re, a_im = _s5_weights(lb_re, lb_im, bb_re, bb_im, s5_c_re[0], s5_c_im[0])
    u_rows = u_tb.reshape(TOK * BATCH, S5_WIDTH)
    yf = _scan(u_rows, bd[0], cd[0], a_re[0], a_im[0], reverse=False)
    yb = _scan(u_rows, bd[1], cd[1], a_re[1], a_im[1], reverse=True)
    return _finish(yf.reshape(TOK, BATCH * S5_WIDTH), yb.reshape(TOK, BATCH * S5_WIDTH), u_tb, sz1, x1, mod1,
                   s5_d[0][None, :], s5_w_glu[0].astype(BF16), s5_b_glu[0][None, :], s5_w_out[0].astype(BF16),
                   final_g[None, :])
```

```python
import math

import jax
import jax.numpy as jnp
from jax import lax
from jax.experimental import pallas as pl
from jax.experimental.pallas import tpu as pltpu

D_MODEL = 1024
BATCH = 8
SEQ = 2048
GRID_W = 64
CTX_LEN = 256
TOK = CTX_LEN + SEQ
EPS = 1e-6

MLA_HEADS = 16
QK_NOPE_DIM = 64
QK_ROPE_DIM = 32
V_HEAD_DIM = 64
Q_LORA_RANK = 256
KV_LORA_RANK = 128
MLA_WIDTH = MLA_HEADS * V_HEAD_DIM
QK_DIM = QK_NOPE_DIM + QK_ROPE_DIM
SOFTMAX_SCALE = QK_DIM ** -0.5
ROPE_THETA = 10000.0
HEAD_PAD = 128
QK_PAD = MLA_HEADS * HEAD_PAD
PROJ_W = 1536

S5_WIDTH = D_MODEL
S5_GROUP = 16
S5_GROUPS = 64
S5_STATE = 64
CH_T = 4
UNIT_G = 4
UNIT_CH = UNIT_G * S5_GROUP
UNITS = S5_GROUPS // UNIT_G
UNIT_K = CH_T * UNIT_CH
UNIT_ST = UNIT_G * S5_STATE
NCH = TOK // CH_T
NCH_CTX = CTX_LEN // CH_T
LANE = 128
SUB = 8
ROW_PITCH = 584

TM = 256
TQ = 256
KCH = 256
VMEM_LIMIT = 56 * 1024 * 1024

F32 = jnp.float32
BF16 = jnp.bfloat16


def _params(sem, flags=None):
    return pltpu.CompilerParams(dimension_semantics=sem, vmem_limit_bytes=VMEM_LIMIT, flags=flags)


def _silu(v):
    return v * jax.nn.sigmoid(v)


def _rms(v, g):
    return v * lax.rsqrt(jnp.mean(v * v, axis=-1, keepdims=True) + EPS) * g


def _dot(a, b):
    return jnp.dot(a, b, preferred_element_type=F32)


def _mod_kernel(cc_ref, w_ref, b_ref, o_ref):
    a = _silu(cc_ref[...]).astype(BF16)
    o_ref[0] = _dot(a, w_ref[0].astype(BF16)) + b_ref[0]


def _modulation(cc, ada_w, ada_b):
    depth = ada_w.shape[0]
    tn = 768
    return pl.pallas_call(
        _mod_kernel,
        grid=(depth, 3 * D_MODEL // tn),
        in_specs=[
            pl.BlockSpec((16, D_MODEL), lambda i, j: (0, 0)),
            pl.BlockSpec((1, D_MODEL, tn), lambda i, j: (i, 0, j)),
            pl.BlockSpec((1, 1, tn), lambda i, j: (i, 0, j)),
        ],
        out_specs=pl.BlockSpec((1, 16, tn), lambda i, j: (i, 0, j)),
        out_shape=jax.ShapeDtypeStruct((depth, 16, 3 * D_MODEL), F32),
        compiler_params=_params(("arbitrary", "arbitrary")),
        name="modulation",
    )(cc, ada_w, ada_b.reshape(depth, 1, 3 * D_MODEL))


def _mla_proj_kernel(x_ref, mod_ref, g_ref, win_ref, qg_ref, kvg_ref, wqa_ref, wqb_ref, wk_ref, wv_ref,
                     cos_ref, sin_ref, kt_ref, q_ref, k_ref, v_ref, sz_ref):
    x = x_ref[0]
    mod = mod_ref[0, 0]
    sh = mod[:, :D_MODEL]
    sc = mod[:, D_MODEL:2 * D_MODEL]
    h = _rms(x, g_ref[...]) * (1.0 + sc) + sh
    p = _dot(h.astype(BF16), win_ref[...])
    cqn = _rms(p[:, :Q_LORA_RANK], qg_ref[...]).astype(BF16)
    ckvn = _rms(p[:, Q_LORA_RANK:Q_LORA_RANK + KV_LORA_RANK], kvg_ref[...]).astype(BF16)
    kr = p[:, 384:512]
    z = p[:, 512:]
    qa = _dot(cqn, wqa_ref[...])
    qb = _dot(cqn, wqb_ref[...])
    cos = cos_ref[...]
    sin = sin_ref[...]
    for hd in range(MLA_HEADS):
        sl = slice(hd * HEAD_PAD, (hd + 1) * HEAD_PAD)
        q_ref[0, :, sl] = (qa[:, sl] * cos + qb[:, sl] * sin).astype(BF16)
    kin = jnp.concatenate([ckvn, (kr * kt_ref[...]).astype(BF16)], axis=-1)
    k_ref[0] = _dot(kin, wk_ref[...]).astype(BF16)
    v_ref[0] = _dot(ckvn, wv_ref[...]).astype(BF16)
    sz_ref[0] = _silu(z).astype(BF16)


def _mla_proj(xc, mod, g, win, qg, kvg, wqa, wqb, wk, wv, cos, sin, kt):
    nct = CTX_LEN // TM
    full = lambda shape: pl.BlockSpec(shape, lambda b, i: (0,) * len(shape))
    tok = lambda w: pl.BlockSpec((1, TM, w), lambda b, i: (b, i, 0))
    pos = pl.BlockSpec((TM, HEAD_PAD), lambda b, i: (i, 0))
    return pl.pallas_call(
        _mla_proj_kernel,
        grid=(BATCH, TOK // TM),
        in_specs=[
            tok(D_MODEL),
            pl.BlockSpec((1, 1, 1, 3 * D_MODEL), lambda b, i: (b, jnp.where(i < nct, 0, 1), 0, 0)),
            full((1, D_MODEL)), full((D_MODEL, PROJ_W)), full((1, Q_LORA_RANK)), full((1, KV_LORA_RANK)),
            full((Q_LORA_RANK, QK_PAD)), full((Q_LORA_RANK, QK_PAD)), full((256, QK_PAD)),
            full((KV_LORA_RANK, MLA_WIDTH)), pos, pos, pos,
        ],
        out_specs=[tok(QK_PAD), tok(QK_PAD), tok(MLA_WIDTH), tok(MLA_WIDTH)],
        out_shape=[
            jax.ShapeDtypeStruct((BATCH, TOK, QK_PAD), BF16),
            jax.ShapeDtypeStruct((BATCH, TOK, QK_PAD), BF16),
            jax.ShapeDtypeStruct((BATCH, TOK, MLA_WIDTH), BF16),
            jax.ShapeDtypeStruct((BATCH, TOK, MLA_WIDTH), BF16),
        ],
        compiler_params=_params(("arbitrary", "arbitrary")),
        name="mla_proj",
    )(xc, mod, g, win, qg, kvg, wqa, wqb, wk, wv, cos, sin, kt)


def _attn_kernel(q_ref, k_ref, v_ref, o_ref, s_buf, m_buf):
    nt = SEQ // TQ

    def scores(row, nk, slot):
        for hh in range(2):
            sl = slice(hh * HEAD_PAD, (hh + 1) * HEAD_PAD)
            s = lax.dot_general(q_ref[0, pl.ds(row, TQ), sl], k_ref[0, :nk, sl], (((1,), (1,)), ((), ())),
                                preferred_element_type=F32)
            s_buf[slot, hh, :, :nk] = s
            m_buf[slot, hh] = jnp.broadcast_to(jnp.max(s, axis=-1, keepdims=True), (TQ, KCH))

    def values(row, nk, slot):
        outs = []
        for hh in range(2):
            m = m_buf[slot, hh]
            lsum = jnp.zeros((TQ, KCH), F32)
            ps = []
            for n in range(nk // KCH):
                pn = jnp.exp2(s_buf[slot, hh, :, n * KCH:(n + 1) * KCH] - m)
                lsum = lsum + pn
                ps.append(pn.astype(BF16))
            l = jnp.sum(lsum, axis=-1, keepdims=True)
            outs.append(_dot(jnp.concatenate(ps, axis=-1), v_ref[0, :nk, :]) / l)
        lane = lax.broadcasted_iota(jnp.int32, outs[0].shape, 1)
        o_ref[0, pl.ds(row, TQ), :] = jnp.where(lane < V_HEAD_DIM, outs[0], outs[1]).astype(BF16)

    def lat_row(t):
        return pl.multiple_of(CTX_LEN + t * TQ, TQ)

    scores(0, CTX_LEN, 0)
    values(0, CTX_LEN, 0)

    scores(lat_row(0), TOK, 0)

    def pair(pp, carry):
        t = 1 + 2 * pp
        scores(lat_row(t), TOK, 1)
        values(lat_row(t - 1), TOK, 0)
        scores(lat_row(t + 1), TOK, 0)
        values(lat_row(t), TOK, 1)
        return carry

    lax.fori_loop(0, (nt - 2) // 2, pair, 0)
    scores(lat_row(nt - 1), TOK, 1)
    values(lat_row(nt - 2), TOK, 0)
    values(lat_row(nt - 1), TOK, 1)


def _attention(q, k, v):
    return pl.pallas_call(
        _attn_kernel,
        grid=(BATCH, MLA_HEADS // 2),
        in_specs=[
            pl.BlockSpec((1, TOK, 2 * HEAD_PAD), lambda b, h: (b, 0, h)),
            pl.BlockSpec((1, TOK, 2 * HEAD_PAD), lambda b, h: (b, 0, h)),
            pl.BlockSpec((1, TOK, 2 * V_HEAD_DIM), lambda b, h: (b, 0, h)),
        ],
        out_specs=pl.BlockSpec((1, TOK, 2 * V_HEAD_DIM), lambda b, h: (b, 0, h)),
        out_shape=jax.ShapeDtypeStruct((BATCH, TOK, MLA_WIDTH), BF16),
        scratch_shapes=[
            pltpu.VMEM((2, 2, TQ, TOK), F32),
            pltpu.VMEM((2, 2, TQ, KCH), F32),
        ],
        compiler_params=_params(("arbitrary", "arbitrary")),
        name="attention",
    )(q, k, v)


def _swap_halves(va, vb):
    lo = lax.broadcasted_iota(jnp.int32, va.shape, 1) < UNIT_CH
    return (jnp.where(lo, va, pltpu.roll(vb, UNIT_CH, 1)),
            jnp.where(lo, pltpu.roll(va, UNIT_CH, 1), vb))


def _mla_out_kernel(o_ref, sz_ref, x_ref, mod0_ref, mod1_ref, g1_ref, wout_ref, win_ref,
                    x1_ref, xu_ref, sz1_ref, tok_scr):
    a = (o_ref[0].astype(F32) * sz_ref[0].astype(F32)).astype(BF16)
    gt = mod0_ref[0, 0][:, 2 * D_MODEL:]
    x1 = x_ref[0] + gt * _dot(a, wout_ref[...])
    x1_ref[0] = x1
    mod1 = mod1_ref[0, 0]
    h = _rms(x1, g1_ref[...]) * (1.0 + mod1[:, D_MODEL:2 * D_MODEL]) + mod1[:, :D_MODEL]
    p = _dot(h.astype(BF16), win_ref[...])
    sz1_ref[0] = _silu(p[:, S5_WIDTH:]).astype(BF16)
    for m in range(S5_WIDTH // LANE):
        tok_scr[m] = p[:, m * LANE:(m + 1) * LANE]
    for m in range(S5_WIDTH // LANE):
        v = [tok_scr[m, pl.ds(t, TM // CH_T, stride=CH_T), :] for t in range(CH_T)]
        for hf in range(CH_T // 2):
            even, odd = _swap_halves(v[2 * hf], v[2 * hf + 1])
            c0 = 2 * m * UNIT_K + hf * LANE
            xu_ref[0, :, c0:c0 + LANE] = even.astype(BF16)
            xu_ref[0, :, c0 + UNIT_K:c0 + UNIT_K + LANE] = odd.astype(BF16)


def _mla_out(o, sz, xc, mod0, mod1, g1, wout, win):
    nct = CTX_LEN // TM
    full = lambda shape: pl.BlockSpec(shape, lambda b, i: (0,) * len(shape))
    tok = lambda w: pl.BlockSpec((1, TM, w), lambda b, i: (b, i, 0))
    modspec = pl.BlockSpec((1, 1, 1, 3 * D_MODEL), lambda b, i: (b, jnp.where(i < nct, 0, 1), 0, 0))
    return pl.pallas_call(
        _mla_out_kernel,
        grid=(BATCH, TOK // TM),
        in_specs=[tok(MLA_WIDTH), tok(MLA_WIDTH), tok(D_MODEL), modspec, modspec,
                  full((1, D_MODEL)), full((MLA_WIDTH, D_MODEL)), full((D_MODEL, 2 * S5_WIDTH))],
        out_specs=[tok(D_MODEL),
                   pl.BlockSpec((1, TM // CH_T, UNITS * UNIT_K), lambda b, i: (b, i, 0)),
                   tok(S5_WIDTH)],
        out_shape=[
            jax.ShapeDtypeStruct((BATCH, TOK, D_MODEL), F32),
            jax.ShapeDtypeStruct((BATCH, NCH, UNITS * UNIT_K), BF16),
            jax.ShapeDtypeStruct((BATCH, TOK, S5_WIDTH), BF16),
        ],
        scratch_shapes=[pltpu.VMEM((S5_WIDTH // LANE, TM, LANE), F32)],
        compiler_params=_params(("arbitrary", "arbitrary")),
        name="mla_out_s5_in",
    )(o, sz, xc, mod0, mod1, g1, wout, win)


def _cmul(ar, ai, br, bi):
    return ar * br - ai * bi, ar * bi + ai * br


def _group_dot(a, b):
    return lax.dot_general(a, b, (((2,), (2,)), ((0,), (0,))), precision=lax.Precision.HIGHEST,
                           preferred_element_type=F32)


def _s5_prep_kernel(are_ref, aim_ref, ls_ref, bre_ref, bim_ref, cre_ref, cim_ref, lam_ref, pb_ref, cp_ref, kk_ref):
    ar = are_ref[...]
    ai = aim_ref[...]
    dt = jnp.exp(ls_ref[...])
    mag = jnp.exp(ar * dt)
    lb_re = mag * jnp.cos(ai * dt)
    lb_im = mag * jnp.sin(ai * dt)
    den = ar * ar + ai * ai
    nr = lb_re - 1.0
    f_re = ((nr * ar + lb_im * ai) / den)[:, None, :]
    f_im = ((lb_im * ar - nr * ai) / den)[:, None, :]
    bb_re, bb_im = _cmul(f_re, f_im, bre_ref[...], bim_ref[...])
    c_re = cre_ref[...]
    c_im = cim_ref[...]
    pw_re = jnp.ones_like(lb_re)
    pw_im = jnp.zeros_like(lb_re)
    for r in range(CH_T + 1):
        pr = pw_re[:, None, :]
        pi = pw_im[:, None, :]
        cl_re, cl_im = _cmul(c_re, c_im, pr, pi)
        if r < CH_T:
            q_re, q_im = _cmul(pr, pi, bb_re, bb_im)
            pb_ref[0, r] = q_re
            pb_ref[1, r] = q_im
            kk_ref[r] = _group_dot(cl_re, bb_re) - _group_dot(cl_im, bb_im)
        if r > 0:
            cp_ref[0, r - 1] = cl_re
            cp_ref[1, r - 1] = cl_im
        if r == CH_T:
            lam_ref[0] = pw_re
            lam_ref[1] = pw_im
        else:
            pw_re, pw_im = _cmul(pw_re, pw_im, lb_re, lb_im)


def _s5_prep(a_re, a_im, log_step, b_re_t, b_im_t, c_re, c_im):
    n = a_re.shape[0]
    return pl.pallas_call(
        _s5_prep_kernel,
        out_shape=[
            jax.ShapeDtypeStruct((2, n, S5_STATE), F32),
            jax.ShapeDtypeStruct((2, CH_T, n, S5_GROUP, S5_STATE), F32),
            jax.ShapeDtypeStruct((2, CH_T, n, S5_GROUP, S5_STATE), F32),
            jax.ShapeDtypeStruct((CH_T, n, S5_GROUP, S5_GROUP), F32),
        ],
        name="s5_prep",
    )(a_re, a_im, log_step, b_re_t, b_im_t, c_re, c_im)


STATE_TILES = 2 * 2 * UNIT_ST // LANE


def _s5_core_kernel(x_ref, w1_ref, co_ref, lam_ref, y_ref, st_scr):
    w1 = w1_ref[0]
    for b in range(BATCH):
        r = _dot(x_ref[b], w1)
        y_ref[b] = r[:, :UNIT_K] + r[:, UNIT_K:2 * UNIT_K]
        for lt in range(STATE_TILES):
            c0 = 2 * UNIT_K + lt * LANE
            st_scr[lt, b * ROW_PITCH:b * ROW_PITCH + NCH, :] = r[:, c0:c0 + LANE]
    lam = [lam_ref[0, lt] for lt in range(STATE_TILES)]

    def chunk_step(state, row, base):
        idx = pl.ds(row, BATCH, stride=ROW_PITCH)
        z = [st_scr[base + k, idx, :] for k in range(4)]
        for k in range(4):
            st_scr[base + k, idx, :] = state[k]
        ar0, ar1, ai0, ai1 = lam[base:base + 4]
        return [ar0 * state[0] - ai0 * state[2] + z[0], ar1 * state[1] - ai1 * state[3] + z[1],
                ar0 * state[2] + ai0 * state[0] + z[2], ar1 * state[3] + ai1 * state[1] + z[3]]

    def step(i, carry):
        s_f, s_b = carry
        row_b = jnp.where(i < NCH_CTX, NCH_CTX - 1 - i, NCH + NCH_CTX - 1 - i)
        return chunk_step(s_f, i, 0), chunk_step(s_b, row_b, 4)

    zero = [jnp.zeros((BATCH, LANE), F32)] * 4
    lax.fori_loop(0, NCH, step, (zero, zero), unroll=2)
    co = co_ref[0]
    for b in range(BATCH):
        lhs = jnp.concatenate([st_scr[lt, b * ROW_PITCH:b * ROW_PITCH + NCH, :] for lt in range(STATE_TILES)],
                              axis=-1)
        y_ref[b] = y_ref[b] + _dot(lhs.astype(BF16), co)


def _s5_core(xu, w1, co, lam):
    return pl.pallas_call(
        _s5_core_kernel,
        grid=(UNITS,),
        in_specs=[
            pl.BlockSpec((BATCH, NCH, UNIT_K), lambda q: (0, 0, q)),
            pl.BlockSpec((1, UNIT_K, 2 * UNIT_K + 4 * UNIT_ST), lambda q: (q, 0, 0)),
            pl.BlockSpec((1, 4 * UNIT_ST, UNIT_K), lambda q: (q, 0, 0)),
            pl.BlockSpec((1, STATE_TILES, SUB, LANE), lambda q: (q, 0, 0, 0)),
        ],
        out_specs=pl.BlockSpec((BATCH, NCH, UNIT_K), lambda q: (0, 0, q)),
        out_shape=jax.ShapeDtypeStruct((BATCH, NCH, UNITS * UNIT_K), F32),
        scratch_shapes=[pltpu.VMEM((STATE_TILES, BATCH * ROW_PITCH, LANE), F32)],
        compiler_params=_params(("arbitrary",)),
        name="s5_core",
    )(xu, w1, co, lam)


def _fin_kernel(y_ref, sz_ref, x_ref, mod_ref, wglu_ref, bglu_ref, wout_ref, fg_ref, o_ref, tok_scr):
    for m in range(S5_WIDTH // LANE):
        for hf in range(CH_T // 2):
            c0 = 2 * m * UNIT_K + hf * LANE
            va, vb = _swap_halves(y_ref[0, :, c0:c0 + LANE], y_ref[0, :, c0 + UNIT_K:c0 + UNIT_K + LANE])
            tok_scr[m, pl.ds(2 * hf, TM // CH_T, stride=CH_T), :] = va
            tok_scr[m, pl.ds(2 * hf + 1, TM // CH_T, stride=CH_T), :] = vb
    y = jnp.concatenate([tok_scr[m] for m in range(S5_WIDTH // LANE)], axis=-1)
    y = jax.nn.gelu(y)
    y = y * jax.nn.sigmoid(_dot(y.astype(BF16), wglu_ref[...]) + bglu_ref[...])
    a = (y * sz_ref[0].astype(F32)).astype(BF16)
    gt = mod_ref[0, 0][:, 2 * D_MODEL:]
    x2 = x_ref[0] + gt * _dot(a, wout_ref[...])
    o_ref[0] = _rms(x2, fg_ref[...])


def _finish(y, sz1, x1, mod1, wglu, bglu, wout, fg):
    nct = CTX_LEN // TM
    full = lambda shape: pl.BlockSpec(shape, lambda b, i: (0,) * len(shape))
    tok = pl.BlockSpec((1, TM, D_MODEL), lambda b, i: (b, i + nct, 0))
    return pl.pallas_call(
        _fin_kernel,
        grid=(BATCH, SEQ // TM),
        in_specs=[pl.BlockSpec((1, TM // CH_T, UNITS * UNIT_K), lambda b, i: (b, i + nct, 0)), tok, tok,
                  pl.BlockSpec((1, 1, 1, 3 * D_MODEL), lambda b, i: (b, 1, 0, 0)),
                  full((S5_WIDTH, S5_WIDTH)), full((1, S5_WIDTH)),
                  full((S5_WIDTH, D_MODEL)), full((1, D_MODEL))],
        out_specs=pl.BlockSpec((1, TM, D_MODEL), lambda b, i: (b, i, 0)),
        out_shape=jax.ShapeDtypeStruct((BATCH, SEQ, D_MODEL), F32),
        scratch_shapes=[pltpu.VMEM((S5_WIDTH // LANE, TM, LANE), F32)],
        compiler_params=_params(("arbitrary", "arbitrary")),
        name="s5_finish",
    )(y, sz1, x1, mod1, wglu, bglu, wout, fg)


def _rot_partner(w):
    return jnp.concatenate([-w[..., 8:16], w[..., 0:8], -w[..., 24:32], w[..., 16:24]], axis=-1)


def _rope_tables():
    h = QK_ROPE_DIM // 2
    inv = 1.0 / (ROPE_THETA ** (jnp.arange(0, h, 2, dtype=F32) / h))
    pos = jnp.arange(SEQ, dtype=jnp.int32)
    ang_r = (pos // GRID_W).astype(F32)[:, None] * inv[None, :]
    ang_c = (pos % GRID_W).astype(F32)[:, None] * inv[None, :]
    cos32 = jnp.concatenate([jnp.cos(ang_r)] * 2 + [jnp.cos(ang_c)] * 2, axis=-1)
    sin32 = jnp.concatenate([jnp.sin(ang_r)] * 2 + [jnp.sin(ang_c)] * 2, axis=-1)
    ones = jnp.ones((SEQ, QK_NOPE_DIM), F32)
    zeros = jnp.zeros((SEQ, QK_NOPE_DIM), F32)
    pad = jnp.zeros((SEQ, HEAD_PAD - QK_DIM), F32)
    cos_l = jnp.concatenate([ones, cos32, pad], axis=-1)
    sin_l = jnp.concatenate([zeros, sin32, pad], axis=-1)
    kt_l = jnp.concatenate([cos32, sin32, zeros], axis=-1)
    cos_c = jnp.concatenate([jnp.ones((CTX_LEN, QK_DIM), F32), jnp.zeros((CTX_LEN, HEAD_PAD - QK_DIM), F32)], -1)
    sin_c = jnp.zeros((CTX_LEN, HEAD_PAD), F32)
    kt_c = jnp.concatenate([jnp.ones((CTX_LEN, 32), F32), jnp.zeros((CTX_LEN, HEAD_PAD - 32), F32)], -1)
    return (jnp.concatenate([cos_c, cos_l], 0), jnp.concatenate([sin_c, sin_l], 0),
            jnp.concatenate([kt_c, kt_l], 0))


def _mla_weights(w_in, w_uq, w_ukv):
    o1, o2, o3 = Q_LORA_RANK, Q_LORA_RANK + KV_LORA_RANK, Q_LORA_RANK + KV_LORA_RANK + QK_ROPE_DIM
    w_kr = w_in[:, o2:o3]
    win = jnp.concatenate([w_in[:, :o2], w_kr, _rot_partner(w_kr), jnp.zeros((D_MODEL, 64), F32), w_in[:, o3:]],
                          axis=-1).astype(BF16)
    wq = (w_uq * (SOFTMAX_SCALE * math.log2(math.e))).reshape(Q_LORA_RANK, MLA_HEADS, QK_DIM)
    zq = jnp.zeros((Q_LORA_RANK, MLA_HEADS, HEAD_PAD - QK_DIM), F32)
    wqa = jnp.concatenate([wq, zq], axis=-1).reshape(Q_LORA_RANK, QK_PAD).astype(BF16)
    wqb = jnp.concatenate([jnp.zeros_like(wq[..., :QK_NOPE_DIM]), _rot_partner(wq[..., QK_NOPE_DIM:]), zq],
                          axis=-1).reshape(Q_LORA_RANK, QK_PAD).astype(BF16)
    wkv = w_ukv.reshape(KV_LORA_RANK, MLA_HEADS, QK_NOPE_DIM + V_HEAD_DIM)
    wk_top = jnp.concatenate([wkv[..., :QK_NOPE_DIM], jnp.zeros((KV_LORA_RANK, MLA_HEADS, 64), F32)], axis=-1)
    eye = jnp.eye(QK_ROPE_DIM, dtype=F32)[:, None, :]
    place = jnp.concatenate([jnp.zeros((QK_ROPE_DIM, MLA_HEADS, QK_NOPE_DIM), F32),
                             jnp.broadcast_to(eye, (QK_ROPE_DIM, MLA_HEADS, QK_ROPE_DIM)),
                             jnp.zeros((QK_ROPE_DIM, MLA_HEADS, HEAD_PAD - QK_DIM), F32)], axis=-1)
    wk = jnp.concatenate([wk_top, place, place, jnp.zeros((64, MLA_HEADS, HEAD_PAD), F32)], axis=0)
    wk = wk.reshape(256, QK_PAD).astype(BF16)
    wv = wkv[..., QK_NOPE_DIM:].reshape(KV_LORA_RANK, MLA_WIDTH).astype(BF16)
    return win, wqa, wqb, wk, wv


def _s5_weights(lam, pb, cp, kk, d):
    eg = jnp.eye(UNIT_G, dtype=F32)
    lag = jnp.arange(CH_T)[None, :] - jnp.arange(CH_T)[:, None]
    ks = jnp.arange(CH_T)[None, None, :]
    sel = jnp.stack([lag[:, :, None] == ks, -lag[:, :, None] == ks]).astype(F32)
    k5 = kk.reshape(CH_T, 2, UNITS, UNIT_G, S5_GROUP, S5_GROUP)
    m = jnp.einsum('djtk,kdqgcx,gh->dqjgxthc', sel, k5, eg)
    skip = jnp.einsum('jt,qgc,gh,xc->qjgxthc', jnp.eye(CH_T, dtype=F32), d.reshape(UNITS, UNIT_G, S5_GROUP), eg,
                      jnp.eye(S5_GROUP, dtype=F32))
    m = m.at[0].add(skip).reshape(2, UNITS, UNIT_K, UNIT_K)

    pb = pb.reshape(2, CH_T, 2, UNITS, UNIT_G, S5_GROUP, S5_STATE)
    bz = [jnp.einsum('ijqgxp,gh->qjgxihp', pbj, eg).reshape(UNITS, UNIT_K, 2 * UNIT_ST)
          for pbj in (pb[:, ::-1, 0], pb[:, :, 1])]

    cp = cp.reshape(2, CH_T, 2, UNITS, UNIT_G, S5_GROUP, S5_STATE)
    cp = cp * jnp.array([1.0, -1.0], F32).reshape(2, 1, 1, 1, 1, 1, 1)
    co = [jnp.einsum('itqgcp,gh->qigpthc', cpt, eg).reshape(UNITS, 2 * UNIT_ST, UNIT_K)
          for cpt in (cp[:, :, 0], cp[:, ::-1, 1])]

    w1 = jnp.concatenate([m[0], m[1], bz[0], bz[1]], axis=-1).astype(BF16)
    co = jnp.concatenate(co, axis=1).astype(BF16)
    lam = lam.reshape(2, 2, UNITS, UNIT_ST // LANE, LANE)
    lam = jnp.concatenate([lam[0, 0], lam[1, 0], lam[0, 1], lam[1, 1]], axis=1)
    lam = jnp.broadcast_to(lam[:, :, None, :], (UNITS, STATE_TILES, SUB, LANE))
    return w1, co, lam


def kernel(x, c, ctx, c_ctx, ada_w, ada_b, norm_g, mla_w_in, mla_q_norm, mla_w_uq, mla_kv_norm, mla_w_ukv, mla_w_out, s5_w_in, s5_a_re, s5_a_im, s5_log_step, s5_b_re, s5_b_im, s5_c_re, s5_c_im, s5_d, s5_w_glu, s5_b_glu, s5_w_out, final_g):
    cc = jnp.concatenate([c, c_ctx[None, :], jnp.zeros((7, D_MODEL), F32)], axis=0)
    mods = _modulation(cc, ada_w, ada_b)

    def mod_rows(i):
        ctx_row = jnp.broadcast_to(mods[i, 8][None, :], (BATCH, 3 * D_MODEL))
        return jnp.stack([ctx_row, mods[i, :BATCH]], axis=1)[:, :, None, :]

    mod0, mod1 = mod_rows(0), mod_rows(1)
    xc = jnp.concatenate([ctx, x], axis=1)

    win, wqa, wqb, wk, wv = _mla_weights(mla_w_in[0], mla_w_uq[0], mla_w_ukv[0])
    cos, sin, kt = _rope_tables()
    q, k, v, sz = _mla_proj(xc, mod0, norm_g[0][None, :], win, mla_q_norm[0][None, :], mla_kv_norm[0][None, :],
                            wqa, wqb, wk, wv, cos, sin, kt)
    o = _attention(q, k, v)
    x1, xu, sz1 = _mla_out(o, sz, xc, mod0, mod1, norm_g[1][None, :], mla_w_out[0].astype(BF16),
                           s5_w_in[0].astype(BF16))

    n = 2 * S5_GROUPS
    lam, pb, cp, kk = _s5_prep(
        s5_a_re[0].reshape(n, S5_STATE), s5_a_im[0].reshape(n, S5_STATE), s5_log_step[0].reshape(n, 1),
        jnp.swapaxes(s5_b_re[0], -1, -2).reshape(n, S5_GROUP, S5_STATE),
        jnp.swapaxes(s5_b_im[0], -1, -2).reshape(n, S5_GROUP, S5_STATE),
        s5_c_re[0].reshape(n, S5_GROUP, S5_STATE), s5_c_im[0].reshape(n, S5_GROUP, S5_STATE))
    w1, co, lam = _s5_weights(lam, pb, cp, kk, s5_d[0])
    y = _s5_core(xu, w1, co, lam)
    return _finish(y, sz1, x1, mod1, s5_w_glu[0].astype(BF16), s5_b_glu[0][None, :], s5_w_out[0].astype(BF16),
                   final_g[None, :])
```

```python
import math

import jax
import jax.numpy as jnp
from jax import lax
from jax.experimental import pallas as pl
from jax.experimental.pallas import tpu as pltpu

D_MODEL = 1024
BATCH = 8
SEQ = 2048
GRID_W = 64
CTX_LEN = 256
TOK = CTX_LEN + SEQ
EPS = 1e-6

MLA_HEADS = 16
QK_NOPE_DIM = 64
QK_ROPE_DIM = 32
V_HEAD_DIM = 64
Q_LORA_RANK = 256
KV_LORA_RANK = 128
MLA_WIDTH = MLA_HEADS * V_HEAD_DIM
QK_DIM = QK_NOPE_DIM + QK_ROPE_DIM
SOFTMAX_SCALE = QK_DIM ** -0.5
ROPE_THETA = 10000.0
HEAD_PAD = 128
QK_PAD = MLA_HEADS * HEAD_PAD
PROJ_W = 1536

S5_WIDTH = D_MODEL
S5_GROUP = 16
S5_GROUPS = 64
S5_STATE = 64
CH_T = 4
UNIT_G = 4
UNIT_CH = UNIT_G * S5_GROUP
UNITS = S5_GROUPS // UNIT_G
UNIT_K = CH_T * UNIT_CH
UNIT_ST = UNIT_G * S5_STATE
NCH = TOK // CH_T
NCH_CTX = CTX_LEN // CH_T
LANE = 128
SUB = 8
ROW_PITCH = 584

TM = 256
TQ = 256
KCH = 256
VMEM_LIMIT = 56 * 1024 * 1024

F32 = jnp.float32
BF16 = jnp.bfloat16


def _params(sem, flags=None):
    return pltpu.CompilerParams(dimension_semantics=sem, vmem_limit_bytes=VMEM_LIMIT, flags=flags)


def _silu(v):
    return v * jax.nn.sigmoid(v)


def _rms(v, g):
    return v * lax.rsqrt(jnp.mean(v * v, axis=-1, keepdims=True) + EPS) * g


def _dot(a, b):
    return jnp.dot(a, b, preferred_element_type=F32)


def _mod_kernel(cc_ref, w_ref, b_ref, o_ref):
    a = _silu(cc_ref[...]).astype(BF16)
    o_ref[0] = _dot(a, w_ref[0].astype(BF16)) + b_ref[0]


def _modulation(cc, ada_w, ada_b):
    depth = ada_w.shape[0]
    tn = 768
    return pl.pallas_call(
        _mod_kernel,
        grid=(depth, 3 * D_MODEL // tn),
        in_specs=[
            pl.BlockSpec((16, D_MODEL), lambda i, j: (0, 0)),
            pl.BlockSpec((1, D_MODEL, tn), lambda i, j: (i, 0, j)),
            pl.BlockSpec((1, 1, tn), lambda i, j: (i, 0, j)),
        ],
        out_specs=pl.BlockSpec((1, 16, tn), lambda i, j: (i, 0, j)),
        out_shape=jax.ShapeDtypeStruct((depth, 16, 3 * D_MODEL), F32),
        compiler_params=_params(("arbitrary", "arbitrary")),
        name="modulation",
    )(cc, ada_w, ada_b.reshape(depth, 1, 3 * D_MODEL))


def _tok_specs():
    nct = CTX_LEN // TM
    return (pl.BlockSpec((1, TM, D_MODEL), lambda b, i: (b, jnp.minimum(i, nct - 1), 0)),
            pl.BlockSpec((1, TM, D_MODEL), lambda b, i: (b, jnp.maximum(i - nct, 0), 0)))


def _tok_tile(ctx_ref, x_ref):
    return jnp.where(pl.program_id(1) < CTX_LEN // TM, ctx_ref[0], x_ref[0])


def _mla_proj_kernel(ctx_ref, x_ref, mod_ref, g_ref, win_ref, qg_ref, kvg_ref, wqa_ref, wqb_ref, wk_ref, wv_ref,
                     cos_ref, sin_ref, kt_ref, q_ref, k_ref, v_ref, sz_ref):
    x = _tok_tile(ctx_ref, x_ref)
    mod = mod_ref[0, 0]
    sh = mod[:, :D_MODEL]
    sc = mod[:, D_MODEL:2 * D_MODEL]
    h = _rms(x, g_ref[...]) * (1.0 + sc) + sh
    p = _dot(h.astype(BF16), win_ref[...])
    cqn = _rms(p[:, :Q_LORA_RANK], qg_ref[...]).astype(BF16)
    ckvn = _rms(p[:, Q_LORA_RANK:Q_LORA_RANK + KV_LORA_RANK], kvg_ref[...]).astype(BF16)
    kr = p[:, 384:512]
    z = p[:, 512:]
    qa = _dot(cqn, wqa_ref[...])
    qb = _dot(cqn, wqb_ref[...])
    cos = cos_ref[...]
    sin = sin_ref[...]
    for hd in range(MLA_HEADS):
        sl = slice(hd * HEAD_PAD, (hd + 1) * HEAD_PAD)
        q_ref[0, :, sl] = (qa[:, sl] * cos + qb[:, sl] * sin).astype(BF16)
    kin = jnp.concatenate([ckvn, (kr * kt_ref[...]).astype(BF16)], axis=-1)
    k_ref[0] = _dot(kin, wk_ref[...]).astype(BF16)
    v_ref[0] = _dot(ckvn, wv_ref[...]).astype(BF16)
    sz_ref[0] = _silu(z).astype(BF16)


def _mla_proj(ctx, x, mod, g, win, qg, kvg, wqa, wqb, wk, wv, cos, sin, kt):
    nct = CTX_LEN // TM
    full = lambda shape: pl.BlockSpec(shape, lambda b, i: (0,) * len(shape))
    tok = lambda w: pl.BlockSpec((1, TM, w), lambda b, i: (b, i, 0))
    pos = pl.BlockSpec((TM, HEAD_PAD), lambda b, i: (i, 0))
    return pl.pallas_call(
        _mla_proj_kernel,
        grid=(BATCH, TOK // TM),
        in_specs=[
            *_tok_specs(),
            pl.BlockSpec((1, 1, 1, 3 * D_MODEL), lambda b, i: (b, jnp.where(i < nct, 0, 1), 0, 0)),
            full((1, D_MODEL)), full((D_MODEL, PROJ_W)), full((1, Q_LORA_RANK)), full((1, KV_LORA_RANK)),
            full((Q_LORA_RANK, QK_PAD)), full((Q_LORA_RANK, QK_PAD)), full((256, QK_PAD)),
            full((KV_LORA_RANK, MLA_WIDTH)), pos, pos, pos,
        ],
        out_specs=[tok(QK_PAD), tok(QK_PAD), tok(MLA_WIDTH), tok(MLA_WIDTH)],
        out_shape=[
            jax.ShapeDtypeStruct((BATCH, TOK, QK_PAD), BF16),
            jax.ShapeDtypeStruct((BATCH, TOK, QK_PAD), BF16),
            jax.ShapeDtypeStruct((BATCH, TOK, MLA_WIDTH), BF16),
            jax.ShapeDtypeStruct((BATCH, TOK, MLA_WIDTH), BF16),
        ],
        compiler_params=_params(("arbitrary", "arbitrary")),
        name="mla_proj",
    )(ctx, x, mod, g, win, qg, kvg, wqa, wqb, wk, wv, cos, sin, kt)


def _attn_kernel(q_ref, k_ref, v_ref, o_ref, s_buf, m_buf):
    nt = SEQ // TQ

    def scores(row, nk, slot):
        for hh in range(2):
            sl = slice(hh * HEAD_PAD, (hh + 1) * HEAD_PAD)
            s = lax.dot_general(q_ref[0, pl.ds(row, TQ), sl], k_ref[0, :nk, sl], (((1,), (1,)), ((), ())),
                                preferred_element_type=F32)
            s_buf[slot, hh, :, :nk] = s
            m_buf[slot, hh] = jnp.broadcast_to(jnp.max(s, axis=-1, keepdims=True), (TQ, KCH))

    def values(row, nk, slot):
        outs = []
        for hh in range(2):
            m = m_buf[slot, hh]
            lsum = jnp.zeros((TQ, KCH), F32)
            ps = []
            for n in range(nk // KCH):
                pn = jnp.exp2(s_buf[slot, hh, :, n * KCH:(n + 1) * KCH] - m)
                lsum = lsum + pn
                ps.append(pn.astype(BF16))
            l = jnp.sum(lsum, axis=-1, keepdims=True)
            outs.append(_dot(jnp.concatenate(ps, axis=-1), v_ref[0, :nk, :]) / l)
        lane = lax.broadcasted_iota(jnp.int32, outs[0].shape, 1)
        o_ref[0, pl.ds(row, TQ), :] = jnp.where(lane < V_HEAD_DIM, outs[0], outs[1]).astype(BF16)

    def lat_row(t):
        return pl.multiple_of(CTX_LEN + t * TQ, TQ)

    scores(0, CTX_LEN, 0)
    values(0, CTX_LEN, 0)

    scores(lat_row(0), TOK, 0)

    def pair(pp, carry):
        t = 1 + 2 * pp
        scores(lat_row(t), TOK, 1)
        values(lat_row(t - 1), TOK, 0)
        scores(lat_row(t + 1), TOK, 0)
        values(lat_row(t), TOK, 1)
        return carry

    lax.fori_loop(0, (nt - 2) // 2, pair, 0)
    scores(lat_row(nt - 1), TOK, 1)
    values(lat_row(nt - 2), TOK, 0)
    values(lat_row(nt - 1), TOK, 1)


def _attention(q, k, v):
    return pl.pallas_call(
        _attn_kernel,
        grid=(BATCH, MLA_HEADS // 2),
        in_specs=[
            pl.BlockSpec((1, TOK, 2 * HEAD_PAD), lambda b, h: (b, 0, h)),
            pl.BlockSpec((1, TOK, 2 * HEAD_PAD), lambda b, h: (b, 0, h)),
            pl.BlockSpec((1, TOK, 2 * V_HEAD_DIM), lambda b, h: (b, 0, h)),
        ],
        out_specs=pl.BlockSpec((1, TOK, 2 * V_HEAD_DIM), lambda b, h: (b, 0, h)),
        out_shape=jax.ShapeDtypeStruct((BATCH, TOK, MLA_WIDTH), BF16),
        scratch_shapes=[
            pltpu.VMEM((2, 2, TQ, TOK), F32),
            pltpu.VMEM((2, 2, TQ, KCH), F32),
        ],
        compiler_params=_params(("arbitrary", "arbitrary")),
        name="attention",
    )(q, k, v)


def _swap_halves(va, vb):
    lo = lax.broadcasted_iota(jnp.int32, va.shape, 1) < UNIT_CH
    return (jnp.where(lo, va, pltpu.roll(vb, UNIT_CH, 1)),
            jnp.where(lo, pltpu.roll(va, UNIT_CH, 1), vb))


def _mla_out_kernel(o_ref, sz_ref, ctx_ref, x_ref, mod0_ref, mod1_ref, g1_ref, wout_ref, win_ref,
                    x1_ref, xu_ref, sz1_ref, tok_scr):
    a = (o_ref[0].astype(F32) * sz_ref[0].astype(F32)).astype(BF16)
    gt = mod0_ref[0, 0][:, 2 * D_MODEL:]
    x1 = _tok_tile(ctx_ref, x_ref) + gt * _dot(a, wout_ref[...])
    x1_ref[0] = x1
    mod1 = mod1_ref[0, 0]
    h = _rms(x1, g1_ref[...]) * (1.0 + mod1[:, D_MODEL:2 * D_MODEL]) + mod1[:, :D_MODEL]
    p = _dot(h.astype(BF16), win_ref[...])
    sz1_ref[0] = _silu(p[:, S5_WIDTH:]).astype(BF16)
    for m in range(S5_WIDTH // LANE):
        tok_scr[m] = p[:, m * LANE:(m + 1) * LANE]
    for m in range(S5_WIDTH // LANE):
        v = [tok_scr[m, pl.ds(t, TM // CH_T, stride=CH_T), :] for t in range(CH_T)]
        for hf in range(CH_T // 2):
            even, odd = _swap_halves(v[2 * hf], v[2 * hf + 1])
            c0 = 2 * m * UNIT_K + hf * LANE
            xu_ref[0, :, c0:c0 + LANE] = even.astype(BF16)
            xu_ref[0, :, c0 + UNIT_K:c0 + UNIT_K + LANE] = odd.astype(BF16)


def _mla_out(o, sz, ctx, x, mod0, mod1, g1, wout, win):
    nct = CTX_LEN // TM
    full = lambda shape: pl.BlockSpec(shape, lambda b, i: (0,) * len(shape))
    tok = lambda w: pl.BlockSpec((1, TM, w), lambda b, i: (b, i, 0))
    lat = lambda w: pl.BlockSpec((1, TM, w), lambda b, i: (b, jnp.maximum(i - nct, 0), 0))
    modspec = pl.BlockSpec((1, 1, 1, 3 * D_MODEL), lambda b, i: (b, jnp.where(i < nct, 0, 1), 0, 0))
    return pl.pallas_call(
        _mla_out_kernel,
        grid=(BATCH, TOK // TM),
        in_specs=[tok(MLA_WIDTH), tok(MLA_WIDTH), *_tok_specs(), modspec, modspec,
                  full((1, D_MODEL)), full((MLA_WIDTH, D_MODEL)), full((D_MODEL, 2 * S5_WIDTH))],
        out_specs=[lat(D_MODEL),
                   pl.BlockSpec((1, TM // CH_T, UNITS * UNIT_K), lambda b, i: (b, i, 0)),
                   lat(S5_WIDTH)],
        out_shape=[
            jax.ShapeDtypeStruct((BATCH, SEQ, D_MODEL), F32),
            jax.ShapeDtypeStruct((BATCH, NCH, UNITS * UNIT_K), BF16),
            jax.ShapeDtypeStruct((BATCH, SEQ, S5_WIDTH), BF16),
        ],
        scratch_shapes=[pltpu.VMEM((S5_WIDTH // LANE, TM, LANE), F32)],
        compiler_params=_params(("arbitrary", "arbitrary")),
        name="mla_out_s5_in",
    )(o, sz, ctx, x, mod0, mod1, g1, wout, win)


def _cmul(ar, ai, br, bi):
    return ar * br - ai * bi, ar * bi + ai * br


def _group_dot(a, b):
    return lax.dot_general(a, b, (((2,), (2,)), ((0,), (0,))), precision=lax.Precision.HIGHEST,
                           preferred_element_type=F32)


def _s5_prep_kernel(are_ref, aim_ref, ls_ref, bre_ref, bim_ref, cre_ref, cim_ref, lam_ref, pb_ref, cp_ref, kk_ref):
    n = are_ref.shape[0]
    eye = (lax.broadcasted_iota(jnp.int32, (n, S5_STATE, S5_STATE), 1)
           == lax.broadcasted_iota(jnp.int32, (n, S5_STATE, S5_STATE), 2)).astype(F32)
    ar = are_ref[...]
    ai = aim_ref[...]
    dt = jnp.exp(ls_ref[...])
    mag = jnp.exp(ar * dt)
    lb_re = mag * jnp.cos(ai * dt)
    lb_im = mag * jnp.sin(ai * dt)
    den = ar * ar + ai * ai
    nr = lb_re - 1.0
    f_re = ((nr * ar + lb_im * ai) / den)[:, None, :]
    f_im = ((lb_im * ar - nr * ai) / den)[:, None, :]
    bb_re, bb_im = _cmul(f_re, f_im, bre_ref[...], bim_ref[...])
    c_re = cre_ref[...]
    c_im = cim_ref[...]
    pw_re = jnp.ones_like(lb_re)
    pw_im = jnp.zeros_like(lb_re)
    for r in range(CH_T + 1):
        pr = pw_re[:, None, :]
        pi = pw_im[:, None, :]
        cl_re, cl_im = _cmul(c_re, c_im, pr, pi)
        if r < CH_T:
            q_re, q_im = _cmul(pr, pi, bb_re, bb_im)
            pb_ref[0, r] = q_re
            pb_ref[1, r] = q_im
            kk_ref[r] = _group_dot(bb_re, cl_re) - _group_dot(bb_im, cl_im)
        if r > 0:
            cp_ref[0, r - 1] = _group_dot(eye, cl_re)
            cp_ref[1, r - 1] = _group_dot(eye, -cl_im)
        if r == CH_T:
            lam_ref[0] = pw_re
            lam_ref[1] = pw_im
        else:
            pw_re, pw_im = _cmul(pw_re, pw_im, lb_re, lb_im)


def _s5_prep(a_re, a_im, log_step, b_re_t, b_im_t, c_re, c_im):
    n = a_re.shape[0]
    nb = 16
    row2 = pl.BlockSpec((nb, S5_STATE), lambda i: (i, 0))
    row3 = pl.BlockSpec((nb, S5_GROUP, S5_STATE), lambda i: (i, 0, 0))
    return pl.pallas_call(
        _s5_prep_kernel,
        grid=(n // nb,),
        in_specs=[row2, row2, pl.BlockSpec((nb, 1), lambda i: (i, 0)), row3, row3, row3, row3],
        out_specs=[
            pl.BlockSpec((2, nb, S5_STATE), lambda i: (0, i, 0)),
            pl.BlockSpec((2, CH_T, nb, S5_GROUP, S5_STATE), lambda i: (0, 0, i, 0, 0)),
            pl.BlockSpec((2, CH_T, nb, S5_STATE, S5_GROUP), lambda i: (0, 0, i, 0, 0)),
            pl.BlockSpec((CH_T, nb, S5_GROUP, S5_GROUP), lambda i: (0, i, 0, 0)),
        ],
        out_shape=[
            jax.ShapeDtypeStruct((2, n, S5_STATE), F32),
            jax.ShapeDtypeStruct((2, CH_T, n, S5_GROUP, S5_STATE), F32),
            jax.ShapeDtypeStruct((2, CH_T, n, S5_STATE, S5_GROUP), F32),
            jax.ShapeDtypeStruct((CH_T, n, S5_GROUP, S5_GROUP), F32),
        ],
        compiler_params=_params(("arbitrary",)),
        name="s5_prep",
    )(a_re, a_im, log_step, b_re_t, b_im_t, c_re, c_im)


STATE_TILES = 2 * 2 * UNIT_ST // LANE


def _hdot(a, b):
    return jnp.dot(a, b, precision=lax.Precision.HIGHEST, preferred_element_type=F32)


def _unit_operators(kk_ref, pb_ref, cp_ref, d_ref):
    def iota(shape, dim):
        return lax.broadcasted_iota(jnp.int32, shape, dim)

    rep16 = (iota((S5_GROUP, UNIT_K), 1) % S5_GROUP == iota((S5_GROUP, UNIT_K), 0)).astype(F32)
    rep64 = (iota((S5_STATE, UNIT_ST), 1) % S5_STATE == iota((S5_STATE, UNIT_ST), 0)).astype(F32)
    row = iota((UNIT_CH, UNIT_K), 0)
    col = iota((UNIT_CH, UNIT_K), 1)
    same_group_out = row // S5_GROUP == (col // S5_GROUP) % UNIT_G
    same_group_st = row // S5_GROUP == col // S5_STATE
    on_diag = row == col % UNIT_CH
    col_t = col // UNIT_CH
    srow = iota((UNIT_ST, UNIT_K), 0)
    scol = iota((UNIT_ST, UNIT_K), 1)
    st_same_group = srow // S5_STATE == (scol // S5_GROUP) % UNIT_G
    st_col_t = scol // UNIT_CH
    ms, bzs, cos = [], [], []
    for d in range(2):
        kexp = [_hdot(kk_ref[k, d, 0].reshape(UNIT_CH, S5_GROUP), rep16) for k in range(CH_T)]
        rows = []
        for j in range(CH_T):
            acc = jnp.zeros((UNIT_CH, UNIT_K), F32)
            for k in range(CH_T):
                lag_ok = (col_t - j == k) if d == 0 else (j - col_t == k)
                acc = acc + jnp.where(lag_ok & same_group_out, kexp[k], 0.0)
            if d == 0:
                acc = acc + jnp.where((col_t == j) & on_diag, d_ref[0], 0.0)
            rows.append(acc)
        ms.append(jnp.concatenate(rows, axis=0))
        rows = []
        for j in range(CH_T):
            r = CH_T - 1 - j if d == 0 else j
            rows.append(jnp.concatenate(
                [jnp.where(same_group_st, _hdot(pb_ref[ri, r, d, 0].reshape(UNIT_CH, S5_STATE), rep64), 0.0)
                 for ri in range(2)], axis=-1))
        bzs.append(jnp.concatenate(rows, axis=0))
        rows = []
        for ri in range(2):
            acc = jnp.zeros((UNIT_ST, UNIT_K), F32)
            for rr in range(CH_T):
                t = rr if d == 0 else CH_T - 1 - rr
                acc = acc + jnp.where((st_col_t == t) & st_same_group,
                                      _hdot(cp_ref[ri, rr, d, 0].reshape(UNIT_ST, S5_GROUP), rep16), 0.0)
            rows.append(acc)
        cos.append(jnp.concatenate(rows, axis=0))
    return jnp.concatenate(ms + bzs, axis=-1).astype(BF16), jnp.concatenate(cos, axis=0).astype(BF16)


def _s5_core_kernel(x_ref, kk_ref, pb_ref, cp_ref, d_ref, lam_ref, y_ref, st_scr):
    w1, co = _unit_operators(kk_ref, pb_ref, cp_ref, d_ref)
    for b in range(BATCH):
        r = _dot(x_ref[b], w1)
        y_ref[b] = r[:, :UNIT_K] + r[:, UNIT_K:2 * UNIT_K]
        for lt in range(STATE_TILES):
            c0 = 2 * UNIT_K + lt * LANE
            st_scr[lt, b * ROW_PITCH:b * ROW_PITCH + NCH, :] = r[:, c0:c0 + LANE]
    lam = [lam_ref[0, lt] for lt in range(STATE_TILES)]

    def chunk_step(state, row, base):
        idx = pl.ds(row, BATCH, stride=ROW_PITCH)
        z = [st_scr[base + k, idx, :] for k in range(4)]
        for k in range(4):
            st_scr[base + k, idx, :] = state[k]
        ar0, ar1, ai0, ai1 = lam[base:base + 4]
        return [ar0 * state[0] - ai0 * state[2] + z[0], ar1 * state[1] - ai1 * state[3] + z[1],
                ar0 * state[2] + ai0 * state[0] + z[2], ar1 * state[3] + ai1 * state[1] + z[3]]

    def step(i, carry):
        s_f, s_b = carry
        row_b = jnp.where(i < NCH_CTX, NCH_CTX - 1 - i, NCH + NCH_CTX - 1 - i)
        return chunk_step(s_f, i, 0), chunk_step(s_b, row_b, 4)

    zero = [jnp.zeros((BATCH, LANE), F32)] * 4
    lax.fori_loop(0, NCH, step, (zero, zero), unroll=2)
    for b in range(BATCH):
        lhs = jnp.concatenate([st_scr[lt, b * ROW_PITCH:b * ROW_PITCH + NCH, :] for lt in range(STATE_TILES)],
                              axis=-1)
        y_ref[b] = y_ref[b] + _dot(lhs.astype(BF16), co)


def _s5_core(xu, kk, pb, cp, d, lam):
    return pl.pallas_call(
        _s5_core_kernel,
        grid=(UNITS,),
        in_specs=[
            pl.BlockSpec((BATCH, NCH, UNIT_K), lambda q: (0, 0, q)),
            pl.BlockSpec((CH_T, 2, 1, UNIT_G, S5_GROUP, S5_GROUP), lambda q: (0, 0, q, 0, 0, 0)),
            pl.BlockSpec((2, CH_T, 2, 1, UNIT_G, S5_GROUP, S5_STATE), lambda q: (0, 0, 0, q, 0, 0, 0)),
            pl.BlockSpec((2, CH_T, 2, 1, UNIT_G, S5_STATE, S5_GROUP), lambda q: (0, 0, 0, q, 0, 0, 0)),
            pl.BlockSpec((1, UNIT_CH, 1), lambda q: (q, 0, 0)),
            pl.BlockSpec((1, STATE_TILES, SUB, LANE), lambda q: (q, 0, 0, 0)),
        ],
        out_specs=pl.BlockSpec((BATCH, NCH, UNIT_K), lambda q: (0, 0, q)),
        out_shape=jax.ShapeDtypeStruct((BATCH, NCH, UNITS * UNIT_K), F32),
        scratch_shapes=[pltpu.VMEM((STATE_TILES, BATCH * ROW_PITCH, LANE), F32)],
        compiler_params=_params(("arbitrary",)),
        name="s5_core",
    )(xu, kk, pb, cp, d, lam)


def _fin_kernel(y_ref, sz_ref, x_ref, mod_ref, wglu_ref, bglu_ref, wout_ref, fg_ref, o_ref, tok_scr):
    for m in range(S5_WIDTH // LANE):
        for hf in range(CH_T // 2):
            c0 = 2 * m * UNIT_K + hf * LANE
            va, vb = _swap_halves(y_ref[0, :, c0:c0 + LANE], y_ref[0, :, c0 + UNIT_K:c0 + UNIT_K + LANE])
            tok_scr[m, pl.ds(2 * hf, TM // CH_T, stride=CH_T), :] = va
            tok_scr[m, pl.ds(2 * hf + 1, TM // CH_T, stride=CH_T), :] = vb
    y = jnp.concatenate([tok_scr[m] for m in range(S5_WIDTH // LANE)], axis=-1)
    y = jax.nn.gelu(y)
    y = y * jax.nn.sigmoid(_dot(y.astype(BF16), wglu_ref[...]) + bglu_ref[...])
    a = (y * sz_ref[0].astype(F32)).astype(BF16)
    gt = mod_ref[0, 0][:, 2 * D_MODEL:]
    x2 = x_ref[0] + gt * _dot(a, wout_ref[...])
    o_ref[0] = _rms(x2, fg_ref[...])


def _finish(y, sz1, x1, mod1, wglu, bglu, wout, fg):
    nct = CTX_LEN // TM
    full = lambda shape: pl.BlockSpec(shape, lambda b, i: (0,) * len(shape))
    tok = pl.BlockSpec((1, TM, D_MODEL), lambda b, i: (b, i, 0))
    return pl.pallas_call(
        _fin_kernel,
        grid=(BATCH, SEQ // TM),
        in_specs=[pl.BlockSpec((1, TM // CH_T, UNITS * UNIT_K), lambda b, i: (b, i + nct, 0)), tok, tok,
                  pl.BlockSpec((1, 1, 1, 3 * D_MODEL), lambda b, i: (b, 1, 0, 0)),
                  full((S5_WIDTH, S5_WIDTH)), full((1, S5_WIDTH)),
                  full((S5_WIDTH, D_MODEL)), full((1, D_MODEL))],
        out_specs=pl.BlockSpec((1, TM, D_MODEL), lambda b, i: (b, i, 0)),
        out_shape=jax.ShapeDtypeStruct((BATCH, SEQ, D_MODEL), F32),
        scratch_shapes=[pltpu.VMEM((S5_WIDTH // LANE, TM, LANE), F32)],
        compiler_params=_params(("arbitrary", "arbitrary")),
        name="s5_finish",
    )(y, sz1, x1, mod1, wglu, bglu, wout, fg)


def _rot_partner(w):
    return jnp.concatenate([-w[..., 8:16], w[..., 0:8], -w[..., 24:32], w[..., 16:24]], axis=-1)


def _rope_tables():
    h = QK_ROPE_DIM // 2
    inv = 1.0 / (ROPE_THETA ** (jnp.arange(0, h, 2, dtype=F32) / h))
    pos = jnp.arange(SEQ, dtype=jnp.int32)
    ang_r = (pos // GRID_W).astype(F32)[:, None] * inv[None, :]
    ang_c = (pos % GRID_W).astype(F32)[:, None] * inv[None, :]
    cos32 = jnp.concatenate([jnp.cos(ang_r)] * 2 + [jnp.cos(ang_c)] * 2, axis=-1)
    sin32 = jnp.concatenate([jnp.sin(ang_r)] * 2 + [jnp.sin(ang_c)] * 2, axis=-1)
    ones = jnp.ones((SEQ, QK_NOPE_DIM), F32)
    zeros = jnp.zeros((SEQ, QK_NOPE_DIM), F32)
    pad = jnp.zeros((SEQ, HEAD_PAD - QK_DIM), F32)
    cos_l = jnp.concatenate([ones, cos32, pad], axis=-1)
    sin_l = jnp.concatenate([zeros, sin32, pad], axis=-1)
    kt_l = jnp.concatenate([cos32, sin32, zeros], axis=-1)
    cos_c = jnp.concatenate([jnp.ones((CTX_LEN, QK_DIM), F32), jnp.zeros((CTX_LEN, HEAD_PAD - QK_DIM), F32)], -1)
    sin_c = jnp.zeros((CTX_LEN, HEAD_PAD), F32)
    kt_c = jnp.concatenate([jnp.ones((CTX_LEN, 32), F32), jnp.zeros((CTX_LEN, HEAD_PAD - 32), F32)], -1)
    return (jnp.concatenate([cos_c, cos_l], 0), jnp.concatenate([sin_c, sin_l], 0),
            jnp.concatenate([kt_c, kt_l], 0))


def _mla_weights(w_in, w_uq, w_ukv):
    o1, o2, o3 = Q_LORA_RANK, Q_LORA_RANK + KV_LORA_RANK, Q_LORA_RANK + KV_LORA_RANK + QK_ROPE_DIM
    w_kr = w_in[:, o2:o3]
    win = jnp.concatenate([w_in[:, :o2], w_kr, _rot_partner(w_kr), jnp.zeros((D_MODEL, 64), F32), w_in[:, o3:]],
                          axis=-1).astype(BF16)
    wq = (w_uq * (SOFTMAX_SCALE * math.log2(math.e))).reshape(Q_LORA_RANK, MLA_HEADS, QK_DIM)
    zq = jnp.zeros((Q_LORA_RANK, MLA_HEADS, HEAD_PAD - QK_DIM), F32)
    wqa = jnp.concatenate([wq, zq], axis=-1).reshape(Q_LORA_RANK, QK_PAD).astype(BF16)
    wqb = jnp.concatenate([jnp.zeros_like(wq[..., :QK_NOPE_DIM]), _rot_partner(wq[..., QK_NOPE_DIM:]), zq],
                          axis=-1).reshape(Q_LORA_RANK, QK_PAD).astype(BF16)
    wkv = w_ukv.reshape(KV_LORA_RANK, MLA_HEADS, QK_NOPE_DIM + V_HEAD_DIM)
    wk_top = jnp.concatenate([wkv[..., :QK_NOPE_DIM], jnp.zeros((KV_LORA_RANK, MLA_HEADS, 64), F32)], axis=-1)
    eye = jnp.eye(QK_ROPE_DIM, dtype=F32)[:, None, :]
    place = jnp.concatenate([jnp.zeros((QK_ROPE_DIM, MLA_HEADS, QK_NOPE_DIM), F32),
                             jnp.broadcast_to(eye, (QK_ROPE_DIM, MLA_HEADS, QK_ROPE_DIM)),
                             jnp.zeros((QK_ROPE_DIM, MLA_HEADS, HEAD_PAD - QK_DIM), F32)], axis=-1)
    wk = jnp.concatenate([wk_top, place, place, jnp.zeros((64, MLA_HEADS, HEAD_PAD), F32)], axis=0)
    wk = wk.reshape(256, QK_PAD).astype(BF16)
    wv = wkv[..., QK_NOPE_DIM:].reshape(KV_LORA_RANK, MLA_WIDTH).astype(BF16)
    return win, wqa, wqb, wk, wv


def _lam_tiles(lam):
    lam = lam.reshape(2, 2, UNITS, UNIT_ST // LANE, LANE)
    lam = jnp.concatenate([lam[0, 0], lam[1, 0], lam[0, 1], lam[1, 1]], axis=1)
    return jnp.broadcast_to(lam[:, :, None, :], (UNITS, STATE_TILES, SUB, LANE))


def kernel(x, c, ctx, c_ctx, ada_w, ada_b, norm_g, mla_w_in, mla_q_norm, mla_w_uq, mla_kv_norm, mla_w_ukv, mla_w_out, s5_w_in, s5_a_re, s5_a_im, s5_log_step, s5_b_re, s5_b_im, s5_c_re, s5_c_im, s5_d, s5_w_glu, s5_b_glu, s5_w_out, final_g):
    cc = jnp.concatenate([c, c_ctx[None, :], jnp.zeros((7, D_MODEL), F32)], axis=0)
    mods = _modulation(cc, ada_w, ada_b)

    def mod_rows(i):
        ctx_row = jnp.broadcast_to(mods[i, 8][None, :], (BATCH, 3 * D_MODEL))
        return jnp.stack([ctx_row, mods[i, :BATCH]], axis=1)[:, :, None, :]

    mod0, mod1 = mod_rows(0), mod_rows(1)

    win, wqa, wqb, wk, wv = _mla_weights(mla_w_in[0], mla_w_uq[0], mla_w_ukv[0])
    cos, sin, kt = _rope_tables()
    q, k, v, sz = _mla_proj(ctx, x, mod0, norm_g[0][None, :], win, mla_q_norm[0][None, :],
                            mla_kv_norm[0][None, :], wqa, wqb, wk, wv, cos, sin, kt)
    o = _attention(q, k, v)
    x1, xu, sz1 = _mla_out(o, sz, ctx, x, mod0, mod1, norm_g[1][None, :], mla_w_out[0].astype(BF16),
                           s5_w_in[0].astype(BF16))

    n = 2 * S5_GROUPS
    lam, pb, cp, kk = _s5_prep(
        s5_a_re[0].reshape(n, S5_STATE), s5_a_im[0].reshape(n, S5_STATE), s5_log_step[0].reshape(n, 1),
        jnp.swapaxes(s5_b_re[0], -1, -2).reshape(n, S5_GROUP, S5_STATE),
        jnp.swapaxes(s5_b_im[0], -1, -2).reshape(n, S5_GROUP, S5_STATE),
        s5_c_re[0].reshape(n, S5_GROUP, S5_STATE), s5_c_im[0].reshape(n, S5_GROUP, S5_STATE))
    y = _s5_core(xu, kk.reshape(CH_T, 2, UNITS, UNIT_G, S5_GROUP, S5_GROUP),
                 pb.reshape(2, CH_T, 2, UNITS, UNIT_G, S5_GROUP, S5_STATE),
                 cp.reshape(2, CH_T, 2, UNITS, UNIT_G, S5_STATE, S5_GROUP),
                 s5_d[0].reshape(UNITS, UNIT_CH, 1), _lam_tiles(lam))
    return _finish(y, sz1, x1, mod1, s5_w_glu[0].astype(BF16), s5_b_glu[0][None, :], s5_w_out[0].astype(BF16),
                   final_g[None, :])
```

```python
import math

import jax
import jax.numpy as jnp
from jax import lax
from jax.experimental import pallas as pl
from jax.experimental.pallas import tpu as pltpu

D_MODEL = 1024
BATCH = 8
SEQ = 2048
GRID_W = 64
CTX_LEN = 256
TOK = CTX_LEN + SEQ
EPS = 1e-6

MLA_HEADS = 16
QK_NOPE_DIM = 64
QK_ROPE_DIM = 32
V_HEAD_DIM = 64
Q_LORA_RANK = 256
KV_LORA_RANK = 128
MLA_WIDTH = MLA_HEADS * V_HEAD_DIM
QK_DIM = QK_NOPE_DIM + QK_ROPE_DIM
SOFTMAX_SCALE = QK_DIM ** -0.5
ROPE_THETA = 10000.0
HEAD_PAD = 128
QK_PAD = MLA_HEADS * HEAD_PAD
PROJ_W = 1536

S5_WIDTH = D_MODEL
S5_GROUP = 16
S5_GROUPS = 64
S5_STATE = 64
CH_T = 4
UNIT_G = 4
UNIT_CH = UNIT_G * S5_GROUP
UNITS = S5_GROUPS // UNIT_G
UNIT_K = CH_T * UNIT_CH
UNIT_ST = UNIT_G * S5_STATE
NCH = TOK // CH_T
NCH_CTX = CTX_LEN // CH_T
LANE = 128
SUB = 8

TM = 256
TQ = 256
KCH = 256
VMEM_LIMIT = 56 * 1024 * 1024

F32 = jnp.float32
BF16 = jnp.bfloat16


def _params(sem, flags=None):
    return pltpu.CompilerParams(dimension_semantics=sem, vmem_limit_bytes=VMEM_LIMIT, flags=flags)


def _silu(v):
    return v * jax.nn.sigmoid(v)


def _rms(v, g):
    return v * lax.rsqrt(jnp.mean(v * v, axis=-1, keepdims=True) + EPS) * g


def _dot(a, b):
    return jnp.dot(a, b, preferred_element_type=F32)


def _mod_kernel(cc_ref, w_ref, b_ref, o_ref):
    a = _silu(cc_ref[...]).astype(BF16)
    o_ref[0] = _dot(a, w_ref[0].astype(BF16)) + b_ref[0]


def _modulation(cc, ada_w, ada_b):
    depth = ada_w.shape[0]
    tn = 768
    return pl.pallas_call(
        _mod_kernel,
        grid=(depth, 3 * D_MODEL // tn),
        in_specs=[
            pl.BlockSpec((16, D_MODEL), lambda i, j: (0, 0)),
            pl.BlockSpec((1, D_MODEL, tn), lambda i, j: (i, 0, j)),
            pl.BlockSpec((1, 1, tn), lambda i, j: (i, 0, j)),
        ],
        out_specs=pl.BlockSpec((1, 16, tn), lambda i, j: (i, 0, j)),
        out_shape=jax.ShapeDtypeStruct((depth, 16, 3 * D_MODEL), F32),
        compiler_params=_params(("arbitrary", "arbitrary")),
        name="modulation",
    )(cc, ada_w, ada_b.reshape(depth, 1, 3 * D_MODEL))


def _tok_specs():
    nct = CTX_LEN // TM
    return (pl.BlockSpec((1, TM, D_MODEL), lambda b, i: (b, jnp.minimum(i, nct - 1), 0)),
            pl.BlockSpec((1, TM, D_MODEL), lambda b, i: (b, jnp.maximum(i - nct, 0), 0)))


def _tok_tile(ctx_ref, x_ref):
    return jnp.where(pl.program_id(1) < CTX_LEN // TM, ctx_ref[0], x_ref[0])


def _mla_proj_kernel(ctx_ref, x_ref, mod_ref, g_ref, win_ref, qg_ref, kvg_ref, wqa_ref, wqb_ref, wk_ref, wv_ref,
                     cos_ref, sin_ref, kt_ref, q_ref, k_ref, v_ref, sz_ref):
    x = _tok_tile(ctx_ref, x_ref)
    mod = mod_ref[0, 0]
    sh = mod[:, :D_MODEL]
    sc = mod[:, D_MODEL:2 * D_MODEL]
    h = _rms(x, g_ref[...]) * (1.0 + sc) + sh
    p = _dot(h.astype(BF16), win_ref[...])
    cqn = _rms(p[:, :Q_LORA_RANK], qg_ref[...]).astype(BF16)
    ckvn = _rms(p[:, Q_LORA_RANK:Q_LORA_RANK + KV_LORA_RANK], kvg_ref[...]).astype(BF16)
    kr = p[:, 384:512]
    z = p[:, 512:]
    qa = _dot(cqn, wqa_ref[...])
    qb = _dot(cqn, wqb_ref[...])
    cos = cos_ref[...]
    sin = sin_ref[...]
    for hd in range(MLA_HEADS):
        sl = slice(hd * HEAD_PAD, (hd + 1) * HEAD_PAD)
        q_ref[0, :, sl] = (qa[:, sl] * cos + qb[:, sl] * sin).astype(BF16)
    kin = jnp.concatenate([ckvn, (kr * kt_ref[...]).astype(BF16)], axis=-1)
    k_ref[0] = _dot(kin, wk_ref[...]).astype(BF16)
    v_ref[0] = _dot(ckvn, wv_ref[...]).astype(BF16)
    sz_ref[0] = _silu(z).astype(BF16)


def _mla_proj(ctx, x, mod, g, win, qg, kvg, wqa, wqb, wk, wv, cos, sin, kt):
    nct = CTX_LEN // TM
    full = lambda shape: pl.BlockSpec(shape, lambda b, i: (0,) * len(shape))
    tok = lambda w: pl.BlockSpec((1, TM, w), lambda b, i: (b, i, 0))
    pos = pl.BlockSpec((TM, HEAD_PAD), lambda b, i: (i, 0))
    return pl.pallas_call(
        _mla_proj_kernel,
        grid=(BATCH, TOK // TM),
        in_specs=[
            *_tok_specs(),
            pl.BlockSpec((1, 1, 1, 3 * D_MODEL), lambda b, i: (b, jnp.where(i < nct, 0, 1), 0, 0)),
            full((1, D_MODEL)), full((D_MODEL, PROJ_W)), full((1, Q_LORA_RANK)), full((1, KV_LORA_RANK)),
            full((Q_LORA_RANK, QK_PAD)), full((Q_LORA_RANK, QK_PAD)), full((256, QK_PAD)),
            full((KV_LORA_RANK, MLA_WIDTH)), pos, pos, pos,
        ],
        out_specs=[tok(QK_PAD), tok(QK_PAD), tok(MLA_WIDTH), tok(MLA_WIDTH)],
        out_shape=[
            jax.ShapeDtypeStruct((BATCH, TOK, QK_PAD), BF16),
            jax.ShapeDtypeStruct((BATCH, TOK, QK_PAD), BF16),
            jax.ShapeDtypeStruct((BATCH, TOK, MLA_WIDTH), BF16),
            jax.ShapeDtypeStruct((BATCH, TOK, MLA_WIDTH), BF16),
        ],
        compiler_params=_params(("arbitrary", "arbitrary")),
        name="mla_proj",
    )(ctx, x, mod, g, win, qg, kvg, wqa, wqb, wk, wv, cos, sin, kt)


def _attn_kernel(q_ref, k_ref, v_ref, o_ref, s_buf, m_buf):
    nt = SEQ // TQ

    def scores(row, nk, slot):
        for hh in range(2):
            sl = slice(hh * HEAD_PAD, (hh + 1) * HEAD_PAD)
            s = lax.dot_general(q_ref[0, pl.ds(row, TQ), sl], k_ref[0, :nk, sl], (((1,), (1,)), ((), ())),
                                preferred_element_type=F32)
            s_buf[slot, hh, :, :nk] = s
            m_buf[slot, hh] = jnp.broadcast_to(jnp.max(s, axis=-1, keepdims=True), (TQ, KCH))

    def values(row, nk, slot):
        outs = []
        for hh in range(2):
            m = m_buf[slot, hh]
            lsum = jnp.zeros((TQ, KCH), F32)
            ps = []
            for n in range(nk // KCH):
                pn = jnp.exp2(s_buf[slot, hh, :, n * KCH:(n + 1) * KCH] - m)
                lsum = lsum + pn
                ps.append(pn.astype(BF16))
            l = jnp.sum(lsum, axis=-1, keepdims=True)
            outs.append(_dot(jnp.concatenate(ps, axis=-1), v_ref[0, :nk, :]) / l)
        lane = lax.broadcasted_iota(jnp.int32, outs[0].shape, 1)
        o_ref[0, pl.ds(row, TQ), :] = jnp.where(lane < V_HEAD_DIM, outs[0], outs[1]).astype(BF16)

    def lat_row(t):
        return pl.multiple_of(CTX_LEN + t * TQ, TQ)

    scores(0, CTX_LEN, 0)
    values(0, CTX_LEN, 0)

    scores(lat_row(0), TOK, 0)

    def pair(pp, carry):
        t = 1 + 2 * pp
        scores(lat_row(t), TOK, 1)
        values(lat_row(t - 1), TOK, 0)
        scores(lat_row(t + 1), TOK, 0)
        values(lat_row(t), TOK, 1)
        return carry

    lax.fori_loop(0, (nt - 2) // 2, pair, 0)
    scores(lat_row(nt - 1), TOK, 1)
    values(lat_row(nt - 2), TOK, 0)
    values(lat_row(nt - 1), TOK, 1)


def _attention(q, k, v):
    return pl.pallas_call(
        _attn_kernel,
        grid=(BATCH, MLA_HEADS // 2),
        in_specs=[
            pl.BlockSpec((1, TOK, 2 * HEAD_PAD), lambda b, h: (b, 0, h)),
            pl.BlockSpec((1, TOK, 2 * HEAD_PAD), lambda b, h: (b, 0, h)),
            pl.BlockSpec((1, TOK, 2 * V_HEAD_DIM), lambda b, h: (b, 0, h)),
        ],
        out_specs=pl.BlockSpec((1, TOK, 2 * V_HEAD_DIM), lambda b, h: (b, 0, h)),
        out_shape=jax.ShapeDtypeStruct((BATCH, TOK, MLA_WIDTH), BF16),
        scratch_shapes=[
            pltpu.VMEM((2, 2, TQ, TOK), F32),
            pltpu.VMEM((2, 2, TQ, KCH), F32),
        ],
        compiler_params=_params(("arbitrary", "arbitrary")),
        name="attention",
    )(q, k, v)


def _swap_halves(va, vb):
    lo = lax.broadcasted_iota(jnp.int32, va.shape, 1) < UNIT_CH
    return (jnp.where(lo, va, pltpu.roll(vb, UNIT_CH, 1)),
            jnp.where(lo, pltpu.roll(va, UNIT_CH, 1), vb))


def _mla_out_kernel(o_ref, sz_ref, ctx_ref, x_ref, mod0_ref, mod1_ref, g1_ref, wout_ref, win_ref,
                    x1_ref, xu_ref, sz1_ref, tok_scr):
    a = (o_ref[0].astype(F32) * sz_ref[0].astype(F32)).astype(BF16)
    gt = mod0_ref[0, 0][:, 2 * D_MODEL:]
    x1 = _tok_tile(ctx_ref, x_ref) + gt * _dot(a, wout_ref[...])
    x1_ref[0] = x1
    mod1 = mod1_ref[0, 0]
    h = _rms(x1, g1_ref[...]) * (1.0 + mod1[:, D_MODEL:2 * D_MODEL]) + mod1[:, :D_MODEL]
    p = _dot(h.astype(BF16), win_ref[...])
    sz1_ref[0] = _silu(p[:, S5_WIDTH:]).astype(BF16)
    for m in range(S5_WIDTH // LANE):
        tok_scr[m] = p[:, m * LANE:(m + 1) * LANE]
    for m in range(S5_WIDTH // LANE):
        v = [tok_scr[m, pl.ds(t, TM // CH_T, stride=CH_T), :] for t in range(CH_T)]
        for hf in range(CH_T // 2):
            even, odd = _swap_halves(v[2 * hf], v[2 * hf + 1])
            c0 = 2 * m * UNIT_K + hf * LANE
            xu_ref[0, :, c0:c0 + LANE] = even.astype(BF16)
            xu_ref[0, :, c0 + UNIT_K:c0 + UNIT_K + LANE] = odd.astype(BF16)


def _mla_out(o, sz, ctx, x, mod0, mod1, g1, wout, win):
    nct = CTX_LEN // TM
    full = lambda shape: pl.BlockSpec(shape, lambda b, i: (0,) * len(shape))
    tok = lambda w: pl.BlockSpec((1, TM, w), lambda b, i: (b, i, 0))
    lat = lambda w: pl.BlockSpec((1, TM, w), lambda b, i: (b, jnp.maximum(i - nct, 0), 0))
    modspec = pl.BlockSpec((1, 1, 1, 3 * D_MODEL), lambda b, i: (b, jnp.where(i < nct, 0, 1), 0, 0))
    return pl.pallas_call(
        _mla_out_kernel,
        grid=(BATCH, TOK // TM),
        in_specs=[tok(MLA_WIDTH), tok(MLA_WIDTH), *_tok_specs(), modspec, modspec,
                  full((1, D_MODEL)), full((MLA_WIDTH, D_MODEL)), full((D_MODEL, 2 * S5_WIDTH))],
        out_specs=[lat(D_MODEL),
                   pl.BlockSpec((1, TM // CH_T, UNITS * UNIT_K), lambda b, i: (b, i, 0)),
                   lat(S5_WIDTH)],
        out_shape=[
            jax.ShapeDtypeStruct((BATCH, SEQ, D_MODEL), F32),
            jax.ShapeDtypeStruct((BATCH, NCH, UNITS * UNIT_K), BF16),
            jax.ShapeDtypeStruct((BATCH, SEQ, S5_WIDTH), BF16),
        ],
        scratch_shapes=[pltpu.VMEM((S5_WIDTH // LANE, TM, LANE), F32)],
        compiler_params=_params(("arbitrary", "arbitrary")),
        name="mla_out_s5_in",
    )(o, sz, ctx, x, mod0, mod1, g1, wout, win)


def _cmul(ar, ai, br, bi):
    return ar * br - ai * bi, ar * bi + ai * br


def _group_dot(a, b, precision=lax.Precision.HIGHEST):
    return lax.dot_general(a, b, (((2,), (2,)), ((0,), (0,))), precision=precision, preferred_element_type=F32)


def _group_transpose(eye, a):
    return _group_dot(eye, a.astype(BF16), precision=None)


def _s5_prep_kernel(are_ref, aim_ref, ls_ref, bre_ref, bim_ref, cre_ref, cim_ref, lam_ref, pb_ref, cp_ref, kk_ref):
    n = are_ref.shape[0]
    eye = (lax.broadcasted_iota(jnp.int32, (n, S5_STATE, S5_STATE), 1)
           == lax.broadcasted_iota(jnp.int32, (n, S5_STATE, S5_STATE), 2)).astype(BF16)
    ar = are_ref[...]
    ai = aim_ref[...]
    dt = jnp.exp(ls_ref[...])
    mag = jnp.exp(ar * dt)
    lb_re = mag * jnp.cos(ai * dt)
    lb_im = mag * jnp.sin(ai * dt)
    den = ar * ar + ai * ai
    nr = lb_re - 1.0
    f_re = ((nr * ar + lb_im * ai) / den)[:, None, :]
    f_im = ((lb_im * ar - nr * ai) / den)[:, None, :]
    bb_re, bb_im = _cmul(f_re, f_im, bre_ref[...], bim_ref[...])
    c_re = cre_ref[...]
    c_im = cim_ref[...]
    pw_re = jnp.ones_like(lb_re)
    pw_im = jnp.zeros_like(lb_re)
    for r in range(CH_T + 1):
        pr = pw_re[:, None, :]
        pi = pw_im[:, None, :]
        cl_re, cl_im = _cmul(c_re, c_im, pr, pi)
        if r < CH_T:
            q_re, q_im = _cmul(pr, pi, bb_re, bb_im)
            pb_ref[0, r] = q_re
            pb_ref[1, r] = q_im
            kk_ref[r] = _group_dot(bb_re, cl_re) - _group_dot(bb_im, cl_im)
        if r > 0:
            cp_ref[0, r - 1] = _group_transpose(eye, cl_re)
            cp_ref[1, r - 1] = _group_transpose(eye, -cl_im)
        if r == CH_T:
            lam_ref[0] = pw_re
            lam_ref[1] = pw_im
        else:
            pw_re, pw_im = _cmul(pw_re, pw_im, lb_re, lb_im)


def _s5_prep(a_re, a_im, log_step, b_re_t, b_im_t, c_re, c_im):
    n = a_re.shape[0]
    nb = 16
    row2 = pl.BlockSpec((nb, S5_STATE), lambda i: (i, 0))
    row3 = pl.BlockSpec((nb, S5_GROUP, S5_STATE), lambda i: (i, 0, 0))
    return pl.pallas_call(
        _s5_prep_kernel,
        grid=(n // nb,),
        in_specs=[row2, row2, pl.BlockSpec((nb, 1), lambda i: (i, 0)), row3, row3, row3, row3],
        out_specs=[
            pl.BlockSpec((2, nb, S5_STATE), lambda i: (0, i, 0)),
            pl.BlockSpec((2, CH_T, nb, S5_GROUP, S5_STATE), lambda i: (0, 0, i, 0, 0)),
            pl.BlockSpec((2, CH_T, nb, S5_STATE, S5_GROUP), lambda i: (0, 0, i, 0, 0)),
            pl.BlockSpec((CH_T, nb, S5_GROUP, S5_GROUP), lambda i: (0, i, 0, 0)),
        ],
        out_shape=[
            jax.ShapeDtypeStruct((2, n, S5_STATE), F32),
            jax.ShapeDtypeStruct((2, CH_T, n, S5_GROUP, S5_STATE), F32),
            jax.ShapeDtypeStruct((2, CH_T, n, S5_STATE, S5_GROUP), F32),
            jax.ShapeDtypeStruct((CH_T, n, S5_GROUP, S5_GROUP), F32),
        ],
        compiler_params=_params(("arbitrary",)),
        name="s5_prep",
    )(a_re, a_im, log_step, b_re_t, b_im_t, c_re, c_im)


STATE_TILES = 2 * 2 * UNIT_ST // LANE


def _hdot(a, rep):
    return _dot(a.astype(BF16), rep)


def _unit_operators(kk_ref, pb_ref, cp_ref, d_ref):
    def iota(shape, dim):
        return lax.broadcasted_iota(jnp.int32, shape, dim)

    rep16 = (iota((S5_GROUP, UNIT_K), 1) % S5_GROUP == iota((S5_GROUP, UNIT_K), 0)).astype(BF16)
    rep64 = (iota((S5_STATE, UNIT_ST), 1) % S5_STATE == iota((S5_STATE, UNIT_ST), 0)).astype(BF16)
    row = iota((UNIT_CH, UNIT_K), 0)
    col = iota((UNIT_CH, UNIT_K), 1)
    same_group_out = row // S5_GROUP == (col // S5_GROUP) % UNIT_G
    same_group_st = row // S5_GROUP == col // S5_STATE
    on_diag = row == col % UNIT_CH
    col_t = col // UNIT_CH
    srow = iota((UNIT_ST, UNIT_K), 0)
    scol = iota((UNIT_ST, UNIT_K), 1)
    st_same_group = srow // S5_STATE == (scol // S5_GROUP) % UNIT_G
    st_col_t = scol // UNIT_CH
    ms, bzs, cos = [], [], []
    for d in range(2):
        kexp = [_hdot(kk_ref[k, d, 0].reshape(UNIT_CH, S5_GROUP), rep16) for k in range(CH_T)]
        rows = []
        for j in range(CH_T):
            acc = jnp.zeros((UNIT_CH, UNIT_K), F32)
            for k in range(CH_T):
                lag_ok = (col_t - j == k) if d == 0 else (j - col_t == k)
                acc = acc + jnp.where(lag_ok & same_group_out, kexp[k], 0.0)
            if d == 0:
                acc = acc + jnp.where((col_t == j) & on_diag, d_ref[0], 0.0)
            rows.append(acc)
        ms.append(jnp.concatenate(rows, axis=0))
        rows = []
        for j in range(CH_T):
            r = CH_T - 1 - j if d == 0 else j
            rows.append(jnp.concatenate(
                [jnp.where(same_group_st, _hdot(pb_ref[ri, r, d, 0].reshape(UNIT_CH, S5_STATE), rep64), 0.0)
                 for ri in range(2)], axis=-1))
        bzs.append(jnp.concatenate(rows, axis=0))
        rows = []
        for ri in range(2):
            acc = jnp.zeros((UNIT_ST, UNIT_K), F32)
            for rr in range(CH_T):
                t = rr if d == 0 else CH_T - 1 - rr
                acc = acc + jnp.where((st_col_t == t) & st_same_group,
                                      _hdot(cp_ref[ri, rr, d, 0].reshape(UNIT_ST, S5_GROUP), rep16), 0.0)
            rows.append(acc)
        cos.append(jnp.concatenate(rows, axis=0))
    return jnp.concatenate(ms + bzs, axis=-1).astype(BF16), jnp.concatenate(cos, axis=0).astype(BF16)


def _s5_core_kernel(x_ref, kk_ref, pb_ref, cp_ref, d_ref, lam_ref, y_ref, st_scr, yi_scr):
    w1, co = _unit_operators(kk_ref, pb_ref, cp_ref, d_ref)
    for b in range(BATCH):
        r = _dot(x_ref[b], w1)
        y_ref[b] = r[:, :UNIT_K] + r[:, UNIT_K:2 * UNIT_K]
        for lt in range(STATE_TILES):
            c0 = 2 * UNIT_K + lt * LANE
            st_scr[lt, pl.ds(b, NCH, stride=BATCH), :] = r[:, c0:c0 + LANE]
    lam = [lam_ref[0, lt] for lt in range(STATE_TILES)]

    def rows(chunk):
        return pl.ds(pl.multiple_of(chunk * BATCH, BATCH), BATCH)

    def load_z(row, base):
        return [st_scr[base + k, rows(row), :] for k in range(4)]

    def advance(state, z, row, base):
        for k in range(4):
            st_scr[base + k, rows(row), :] = state[k]
        ar0, ar1, ai0, ai1 = lam[base:base + 4]
        return [ar0 * state[0] - ai0 * state[2] + z[0], ar1 * state[1] - ai1 * state[3] + z[1],
                ar0 * state[2] + ai0 * state[0] + z[2], ar1 * state[3] + ai1 * state[1] + z[3]]

    def bwd_row(i):
        return jnp.where(i < NCH_CTX, NCH_CTX - 1 - i, NCH + NCH_CTX - 1 - i)

    def step(i, carry):
        s_f, z_f, s_b, z_b = carry
        nxt = jnp.minimum(i + 1, NCH - 1)
        z_f_next = load_z(nxt, 0)
        z_b_next = load_z(bwd_row(nxt), 4)
        return advance(s_f, z_f, i, 0), z_f_next, advance(s_b, z_b, bwd_row(i), 4), z_b_next

    zero = [jnp.zeros((BATCH, LANE), F32)] * 4
    lax.fori_loop(0, NCH, step, (zero, load_z(0, 0), zero, load_z(NCH_CTX - 1, 4)), unroll=2)
    for rb in range(BATCH):
        sl = slice(rb * NCH, (rb + 1) * NCH)
        lhs = jnp.concatenate([st_scr[lt, sl, :] for lt in range(STATE_TILES)], axis=-1)
        yi = _dot(lhs.astype(BF16), co)
        for t in range(UNIT_K // LANE):
            yi_scr[t, sl, :] = yi[:, t * LANE:(t + 1) * LANE]
    for b in range(BATCH):
        y_ref[b] = y_ref[b] + jnp.concatenate(
            [yi_scr[t, pl.ds(b, NCH, stride=BATCH), :] for t in range(UNIT_K // LANE)], axis=-1)


def _s5_core(xu, kk, pb, cp, d, lam):
    return pl.pallas_call(
        _s5_core_kernel,
        grid=(UNITS,),
        in_specs=[
            pl.BlockSpec((BATCH, NCH, UNIT_K), lambda q: (0, 0, q)),
            pl.BlockSpec((CH_T, 2, 1, UNIT_G, S5_GROUP, S5_GROUP), lambda q: (0, 0, q, 0, 0, 0)),
            pl.BlockSpec((2, CH_T, 2, 1, UNIT_G, S5_GROUP, S5_STATE), lambda q: (0, 0, 0, q, 0, 0, 0)),
            pl.BlockSpec((2, CH_T, 2, 1, UNIT_G, S5_STATE, S5_GROUP), lambda q: (0, 0, 0, q, 0, 0, 0)),
            pl.BlockSpec((1, UNIT_CH, 1), lambda q: (q, 0, 0)),
            pl.BlockSpec((1, STATE_TILES, SUB, LANE), lambda q: (q, 0, 0, 0)),
        ],
        out_specs=pl.BlockSpec((BATCH, NCH, UNIT_K), lambda q: (0, 0, q)),
        out_shape=jax.ShapeDtypeStruct((BATCH, NCH, UNITS * UNIT_K), F32),
        scratch_shapes=[pltpu.VMEM((STATE_TILES, BATCH * NCH, LANE), F32),
                        pltpu.VMEM((UNIT_K // LANE, BATCH * NCH, LANE), F32)],
        compiler_params=_params(("arbitrary",)),
        name="s5_core",
    )(xu, kk, pb, cp, d, lam)


def _fin_kernel(y_ref, sz_ref, x_ref, mod_ref, wglu_ref, bglu_ref, wout_ref, fg_ref, o_ref, tok_scr):
    for m in range(S5_WIDTH // LANE):
        for hf in range(CH_T // 2):
            c0 = 2 * m * UNIT_K + hf * LANE
            va, vb = _swap_halves(y_ref[0, :, c0:c0 + LANE], y_ref[0, :, c0 + UNIT_K:c0 + UNIT_K + LANE])
            tok_scr[m, pl.ds(2 * hf, TM // CH_T, stride=CH_T), :] = va
            tok_scr[m, pl.ds(2 * hf + 1, TM // CH_T, stride=CH_T), :] = vb
    y = jnp.concatenate([tok_scr[m] for m in range(S5_WIDTH // LANE)], axis=-1)
    y = jax.nn.gelu(y)
    y = y * jax.nn.sigmoid(_dot(y.astype(BF16), wglu_ref[...]) + bglu_ref[...])
    a = (y * sz_ref[0].astype(F32)).astype(BF16)
    gt = mod_ref[0, 0][:, 2 * D_MODEL:]
    x2 = x_ref[0] + gt * _dot(a, wout_ref[...])
    o_ref[0] = _rms(x2, fg_ref[...])


def _finish(y, sz1, x1, mod1, wglu, bglu, wout, fg):
    nct = CTX_LEN // TM
    full = lambda shape: pl.BlockSpec(shape, lambda b, i: (0,) * len(shape))
    tok = pl.BlockSpec((1, TM, D_MODEL), lambda b, i: (b, i, 0))
    return pl.pallas_call(
        _fin_kernel,
        grid=(BATCH, SEQ // TM),
        in_specs=[pl.BlockSpec((1, TM // CH_T, UNITS * UNIT_K), lambda b, i: (b, i + nct, 0)), tok, tok,
                  pl.BlockSpec((1, 1, 1, 3 * D_MODEL), lambda b, i: (b, 1, 0, 0)),
                  full((S5_WIDTH, S5_WIDTH)), full((1, S5_WIDTH)),
                  full((S5_WIDTH, D_MODEL)), full((1, D_MODEL))],
        out_specs=pl.BlockSpec((1, TM, D_MODEL), lambda b, i: (b, i, 0)),
        out_shape=jax.ShapeDtypeStruct((BATCH, SEQ, D_MODEL), F32),
        scratch_shapes=[pltpu.VMEM((S5_WIDTH // LANE, TM, LANE), F32)],
        compiler_params=_params(("arbitrary", "arbitrary")),
        name="s5_finish",
    )(y, sz1, x1, mod1, wglu, bglu, wout, fg)


def _rot_partner(w):
    return jnp.concatenate([-w[..., 8:16], w[..., 0:8], -w[..., 24:32], w[..., 16:24]], axis=-1)


def _rope_tables():
    h = QK_ROPE_DIM // 2
    inv = 1.0 / (ROPE_THETA ** (jnp.arange(0, h, 2, dtype=F32) / h))
    pos = jnp.arange(SEQ, dtype=jnp.int32)
    ang_r = (pos // GRID_W).astype(F32)[:, None] * inv[None, :]
    ang_c = (pos % GRID_W).astype(F32)[:, None] * inv[None, :]
    cos32 = jnp.concatenate([jnp.cos(ang_r)] * 2 + [jnp.cos(ang_c)] * 2, axis=-1)
    sin32 = jnp.concatenate([jnp.sin(ang_r)] * 2 + [jnp.sin(ang_c)] * 2, axis=-1)
    ones = jnp.ones((SEQ, QK_NOPE_DIM), F32)
    zeros = jnp.zeros((SEQ, QK_NOPE_DIM), F32)
    pad = jnp.zeros((SEQ, HEAD_PAD - QK_DIM), F32)
    cos_l = jnp.concatenate([ones, cos32, pad], axis=-1)
    sin_l = jnp.concatenate([zeros, sin32, pad], axis=-1)
    kt_l = jnp.concatenate([cos32, sin32, zeros], axis=-1)
    cos_c = jnp.concatenate([jnp.ones((CTX_LEN, QK_DIM), F32), jnp.zeros((CTX_LEN, HEAD_PAD - QK_DIM), F32)], -1)
    sin_c = jnp.zeros((CTX_LEN, HEAD_PAD), F32)
    kt_c = jnp.concatenate([jnp.ones((CTX_LEN, 32), F32), jnp.zeros((CTX_LEN, HEAD_PAD - 32), F32)], -1)
    return (jnp.concatenate([cos_c, cos_l], 0), jnp.concatenate([sin_c, sin_l], 0),
            jnp.concatenate([kt_c, kt_l], 0))


def _mla_weights(w_in, w_uq, w_ukv):
    o1, o2, o3 = Q_LORA_RANK, Q_LORA_RANK + KV_LORA_RANK, Q_LORA_RANK + KV_LORA_RANK + QK_ROPE_DIM
    w_kr = w_in[:, o2:o3]
    win = jnp.concatenate([w_in[:, :o2], w_kr, _rot_partner(w_kr), jnp.zeros((D_MODEL, 64), F32), w_in[:, o3:]],
                          axis=-1).astype(BF16)
    wq = (w_uq * (SOFTMAX_SCALE * math.log2(math.e))).reshape(Q_LORA_RANK, MLA_HEADS, QK_DIM)
    zq = jnp.zeros((Q_LORA_RANK, MLA_HEADS, HEAD_PAD - QK_DIM), F32)
    wqa = jnp.concatenate([wq, zq], axis=-1).reshape(Q_LORA_RANK, QK_PAD).astype(BF16)
    wqb = jnp.concatenate([jnp.zeros_like(wq[..., :QK_NOPE_DIM]), _rot_partner(wq[..., QK_NOPE_DIM:]), zq],
                          axis=-1).reshape(Q_LORA_RANK, QK_PAD).astype(BF16)
    wkv = w_ukv.reshape(KV_LORA_RANK, MLA_HEADS, QK_NOPE_DIM + V_HEAD_DIM)
    wk_top = jnp.concatenate([wkv[..., :QK_NOPE_DIM], jnp.zeros((KV_LORA_RANK, MLA_HEADS, 64), F32)], axis=-1)
    eye = jnp.eye(QK_ROPE_DIM, dtype=F32)[:, None, :]
    place = jnp.concatenate([jnp.zeros((QK_ROPE_DIM, MLA_HEADS, QK_NOPE_DIM), F32),
                             jnp.broadcast_to(eye, (QK_ROPE_DIM, MLA_HEADS, QK_ROPE_DIM)),
                             jnp.zeros((QK_ROPE_DIM, MLA_HEADS, HEAD_PAD - QK_DIM), F32)], axis=-1)
    wk = jnp.concatenate([wk_top, place, place, jnp.zeros((64, MLA_HEADS, HEAD_PAD), F32)], axis=0)
    wk = wk.reshape(256, QK_PAD).astype(BF16)
    wv = wkv[..., QK_NOPE_DIM:].reshape(KV_LORA_RANK, MLA_WIDTH).astype(BF16)
    return win, wqa, wqb, wk, wv


def _lam_tiles(lam):
    lam = lam.reshape(2, 2, UNITS, UNIT_ST // LANE, LANE)
    lam = jnp.concatenate([lam[0, 0], lam[1, 0], lam[0, 1], lam[1, 1]], axis=1)
    return jnp.broadcast_to(lam[:, :, None, :], (UNITS, STATE_TILES, SUB, LANE))


def kernel(x, c, ctx, c_ctx, ada_w, ada_b, norm_g, mla_w_in, mla_q_norm, mla_w_uq, mla_kv_norm, mla_w_ukv, mla_w_out, s5_w_in, s5_a_re, s5_a_im, s5_log_step, s5_b_re, s5_b_im, s5_c_re, s5_c_im, s5_d, s5_w_glu, s5_b_glu, s5_w_out, final_g):
    cc = jnp.concatenate([c, c_ctx[None, :], jnp.zeros((7, D_MODEL), F32)], axis=0)
    mods = _modulation(cc, ada_w, ada_b)

    def mod_rows(i):
        ctx_row = jnp.broadcast_to(mods[i, 8][None, :], (BATCH, 3 * D_MODEL))
        return jnp.stack([ctx_row, mods[i, :BATCH]], axis=1)[:, :, None, :]

    mod0, mod1 = mod_rows(0), mod_rows(1)

    win, wqa, wqb, wk, wv = _mla_weights(mla_w_in[0], mla_w_uq[0], mla_w_ukv[0])
    cos, sin, kt = _rope_tables()
    q, k, v, sz = _mla_proj(ctx, x, mod0, norm_g[0][None, :], win, mla_q_norm[0][None, :],
                            mla_kv_norm[0][None, :], wqa, wqb, wk, wv, cos, sin, kt)
    o = _attention(q, k, v)
    x1, xu, sz1 = _mla_out(o, sz, ctx, x, mod0, mod1, norm_g[1][None, :], mla_w_out[0].astype(BF16),
                           s5_w_in[0].astype(BF16))

    n = 2 * S5_GROUPS
    lam, pb, cp, kk = _s5_prep(
        s5_a_re[0].reshape(n, S5_STATE), s5_a_im[0].reshape(n, S5_STATE), s5_log_step[0].reshape(n, 1),
        jnp.swapaxes(s5_b_re[0], -1, -2).reshape(n, S5_GROUP, S5_STATE),
        jnp.swapaxes(s5_b_im[0], -1, -2).reshape(n, S5_GROUP, S5_STATE),
        s5_c_re[0].reshape(n, S5_GROUP, S5_STATE), s5_c_im[0].reshape(n, S5_GROUP, S5_STATE))
    y = _s5_core(xu, kk.reshape(CH_T, 2, UNITS, UNIT_G, S5_GROUP, S5_GROUP),
                 pb.reshape(2, CH_T, 2, UNITS, UNIT_G, S5_GROUP, S5_STATE),
                 cp.reshape(2, CH_T, 2, UNITS, UNIT_G, S5_STATE, S5_GROUP),
                 s5_d[0].reshape(UNITS, UNIT_CH, 1), _lam_tiles(lam))
    return _finish(y, sz1, x1, mod1, s5_w_glu[0].astype(BF16), s5_b_glu[0][None, :], s5_w_out[0].astype(BF16),
                   final_g[None, :])
```

```python
import math

import jax
import jax.numpy as jnp
from jax import lax
from jax.experimental import pallas as pl
from jax.experimental.pallas import tpu as pltpu

D_MODEL = 1024
BATCH = 8
SEQ = 2048
GRID_W = 64
CTX_LEN = 256
TOK = CTX_LEN + SEQ
EPS = 1e-6

MLA_HEADS = 16
QK_NOPE_DIM = 64
QK_ROPE_DIM = 32
V_HEAD_DIM = 64
Q_LORA_RANK = 256
KV_LORA_RANK = 128
MLA_WIDTH = MLA_HEADS * V_HEAD_DIM
QK_DIM = QK_NOPE_DIM + QK_ROPE_DIM
SOFTMAX_SCALE = QK_DIM ** -0.5
ROPE_THETA = 10000.0
HEAD_PAD = 128
QK_PAD = MLA_HEADS * HEAD_PAD
PROJ_W = 1536

S5_WIDTH = D_MODEL
S5_GROUP = 16
S5_GROUPS = 64
S5_STATE = 64
CH_T = 4
UNIT_G = 4
UNIT_CH = UNIT_G * S5_GROUP
UNITS = S5_GROUPS // UNIT_G
UNIT_K = CH_T * UNIT_CH
UNIT_ST = UNIT_G * S5_STATE
NCH = TOK // CH_T
NCH_CTX = CTX_LEN // CH_T
LANE = 128
SUB = 8

TM = 256
TQ = 256
KCH = 256
VMEM_LIMIT = 56 * 1024 * 1024

F32 = jnp.float32
BF16 = jnp.bfloat16


def _params(sem, flags=None):
    return pltpu.CompilerParams(dimension_semantics=sem, vmem_limit_bytes=VMEM_LIMIT, flags=flags)


def _silu(v):
    return v * jax.nn.sigmoid(v)


def _rms(v, g):
    return v * lax.rsqrt(jnp.mean(v * v, axis=-1, keepdims=True) + EPS) * g


def _dot(a, b):
    return jnp.dot(a, b, preferred_element_type=F32)


def _mod_kernel(cc_ref, w_ref, b_ref, o_ref):
    a = _silu(cc_ref[...]).astype(BF16)
    o_ref[0] = _dot(a, w_ref[0].astype(BF16)) + b_ref[0]


def _modulation(cc, ada_w, ada_b):
    depth = ada_w.shape[0]
    tn = 768
    return pl.pallas_call(
        _mod_kernel,
        grid=(depth, 3 * D_MODEL // tn),
        in_specs=[
            pl.BlockSpec((16, D_MODEL), lambda i, j: (0, 0)),
            pl.BlockSpec((1, D_MODEL, tn), lambda i, j: (i, 0, j)),
            pl.BlockSpec((1, 1, tn), lambda i, j: (i, 0, j)),
        ],
        out_specs=pl.BlockSpec((1, 16, tn), lambda i, j: (i, 0, j)),
        out_shape=jax.ShapeDtypeStruct((depth, 16, 3 * D_MODEL), F32),
        compiler_params=_params(("arbitrary", "arbitrary")),
        name="modulation",
    )(cc, ada_w, ada_b.reshape(depth, 1, 3 * D_MODEL))


def _tok_specs():
    nct = CTX_LEN // TM
    return (pl.BlockSpec((1, TM, D_MODEL), lambda b, i: (b, jnp.minimum(i, nct - 1), 0)),
            pl.BlockSpec((1, TM, D_MODEL), lambda b, i: (b, jnp.maximum(i - nct, 0), 0)))


def _tok_tile(ctx_ref, x_ref):
    return jnp.where(pl.program_id(1) < CTX_LEN // TM, ctx_ref[0], x_ref[0])


def _mla_proj_kernel(ctx_ref, x_ref, mod_ref, g_ref, win_ref, qg_ref, kvg_ref, wqa_ref, wqb_ref, wk_ref, wv_ref,
                     cos_ref, sin_ref, kt_ref, q_ref, k_ref, v_ref, sz_ref):
    x = _tok_tile(ctx_ref, x_ref)
    mod = mod_ref[0, 0]
    sh = mod[:, :D_MODEL]
    sc = mod[:, D_MODEL:2 * D_MODEL]
    h = _rms(x, g_ref[...]) * (1.0 + sc) + sh
    p = _dot(h.astype(BF16), win_ref[...])
    cqn = _rms(p[:, :Q_LORA_RANK], qg_ref[...]).astype(BF16)
    ckvn = _rms(p[:, Q_LORA_RANK:Q_LORA_RANK + KV_LORA_RANK], kvg_ref[...]).astype(BF16)
    kr = p[:, 384:512]
    z = p[:, 512:]
    qa = _dot(cqn, wqa_ref[...])
    qb = _dot(cqn, wqb_ref[...])
    cos = cos_ref[...]
    sin = sin_ref[...]
    for hd in range(MLA_HEADS):
        sl = slice(hd * HEAD_PAD, (hd + 1) * HEAD_PAD)
        q_ref[0, :, sl] = (qa[:, sl] * cos + qb[:, sl] * sin).astype(BF16)
    kin = jnp.concatenate([ckvn, (kr * kt_ref[...]).astype(BF16)], axis=-1)
    k_ref[0] = _dot(kin, wk_ref[...]).astype(BF16)
    v_ref[0] = _dot(ckvn, wv_ref[...]).astype(BF16)
    sz_ref[0] = _silu(z).astype(BF16)


def _mla_proj(ctx, x, mod, g, win, qg, kvg, wqa, wqb, wk, wv, cos, sin, kt):
    nct = CTX_LEN // TM
    full = lambda shape: pl.BlockSpec(shape, lambda b, i: (0,) * len(shape))
    tok = lambda w: pl.BlockSpec((1, TM, w), lambda b, i: (b, i, 0))
    pos = pl.BlockSpec((TM, HEAD_PAD), lambda b, i: (i, 0))
    return pl.pallas_call(
        _mla_proj_kernel,
        grid=(BATCH, TOK // TM),
        in_specs=[
            *_tok_specs(),
            pl.BlockSpec((1, 1, 1, 3 * D_MODEL), lambda b, i: (b, jnp.where(i < nct, 0, 1), 0, 0)),
            full((1, D_MODEL)), full((D_MODEL, PROJ_W)), full((1, Q_LORA_RANK)), full((1, KV_LORA_RANK)),
            full((Q_LORA_RANK, QK_PAD)), full((Q_LORA_RANK, QK_PAD)), full((256, QK_PAD)),
            full((KV_LORA_RANK, MLA_WIDTH)), pos, pos, pos,
        ],
        out_specs=[tok(QK_PAD), tok(QK_PAD), tok(MLA_WIDTH), tok(MLA_WIDTH)],
        out_shape=[
            jax.ShapeDtypeStruct((BATCH, TOK, QK_PAD), BF16),
            jax.ShapeDtypeStruct((BATCH, TOK, QK_PAD), BF16),
            jax.ShapeDtypeStruct((BATCH, TOK, MLA_WIDTH), BF16),
            jax.ShapeDtypeStruct((BATCH, TOK, MLA_WIDTH), BF16),
        ],
        compiler_params=_params(("arbitrary", "arbitrary")),
        name="mla_proj",
    )(ctx, x, mod, g, win, qg, kvg, wqa, wqb, wk, wv, cos, sin, kt)


def _attn_kernel(q_ref, k_ref, v_ref, o_ref, s_buf, m_buf, vx_buf):
    nt = SEQ // TQ
    lane = lax.broadcasted_iota(jnp.int32, (TOK, 2 * V_HEAD_DIM), 1)
    v = v_ref[0]
    vx_buf[0] = jnp.where(lane < V_HEAD_DIM, v, (lane == V_HEAD_DIM).astype(BF16))
    vx_buf[1] = jnp.where(lane >= V_HEAD_DIM, v, (lane == 0).astype(BF16))

    def scores(row, nk, slot):
        for hh in range(2):
            sl = slice(hh * HEAD_PAD, (hh + 1) * HEAD_PAD)
            s = lax.dot_general(q_ref[0, pl.ds(row, TQ), sl], k_ref[0, :nk, sl], (((1,), (1,)), ((), ())),
                                preferred_element_type=F32)
            s_buf[slot, hh, :, :nk] = s
            m_buf[slot, hh] = jnp.broadcast_to(jnp.max(s, axis=-1, keepdims=True), (TQ, KCH))

    def values(row, nk, slot):
        outs = []
        for hh in range(2):
            m = m_buf[slot, hh]
            ps = [jnp.exp2(s_buf[slot, hh, :, n * KCH:(n + 1) * KCH] - m).astype(BF16) for n in range(nk // KCH)]
            acc = _dot(jnp.concatenate(ps, axis=-1), vx_buf[hh, :nk, :])
            l_col = V_HEAD_DIM if hh == 0 else 0
            outs.append(acc / acc[:, l_col:l_col + 1])
        olane = lax.broadcasted_iota(jnp.int32, outs[0].shape, 1)
        o_ref[0, pl.ds(row, TQ), :] = jnp.where(olane < V_HEAD_DIM, outs[0], outs[1]).astype(BF16)

    def lat_row(t):
        return pl.multiple_of(CTX_LEN + t * TQ, TQ)

    scores(0, CTX_LEN, 0)
    values(0, CTX_LEN, 0)

    scores(lat_row(0), TOK, 0)

    def pair(pp, carry):
        t = 1 + 2 * pp
        scores(lat_row(t), TOK, 1)
        values(lat_row(t - 1), TOK, 0)
        scores(lat_row(t + 1), TOK, 0)
        values(lat_row(t), TOK, 1)
        return carry

    lax.fori_loop(0, (nt - 2) // 2, pair, 0)
    scores(lat_row(nt - 1), TOK, 1)
    values(lat_row(nt - 2), TOK, 0)
    values(lat_row(nt - 1), TOK, 1)


def _attention(q, k, v):
    return pl.pallas_call(
        _attn_kernel,
        grid=(BATCH, MLA_HEADS // 2),
        in_specs=[
            pl.BlockSpec((1, TOK, 2 * HEAD_PAD), lambda b, h: (b, 0, h)),
            pl.BlockSpec((1, TOK, 2 * HEAD_PAD), lambda b, h: (b, 0, h)),
            pl.BlockSpec((1, TOK, 2 * V_HEAD_DIM), lambda b, h: (b, 0, h)),
        ],
        out_specs=pl.BlockSpec((1, TOK, 2 * V_HEAD_DIM), lambda b, h: (b, 0, h)),
        out_shape=jax.ShapeDtypeStruct((BATCH, TOK, MLA_WIDTH), BF16),
        scratch_shapes=[
            pltpu.VMEM((2, 2, TQ, TOK), F32),
            pltpu.VMEM((2, 2, TQ, KCH), F32),
            pltpu.VMEM((2, TOK, 2 * V_HEAD_DIM), BF16),
        ],
        compiler_params=_params(("arbitrary", "arbitrary")),
        name="attention",
    )(q, k, v)


def _swap_halves(va, vb):
    lo = lax.broadcasted_iota(jnp.int32, va.shape, 1) < UNIT_CH
    return (jnp.where(lo, va, pltpu.roll(vb, UNIT_CH, 1)),
            jnp.where(lo, pltpu.roll(va, UNIT_CH, 1), vb))


def _mla_out_kernel(o_ref, sz_ref, ctx_ref, x_ref, mod0_ref, mod1_ref, g1_ref, wout_ref, win_ref,
                    x1_ref, xu_ref, sz1_ref, tok_scr):
    a = (o_ref[0].astype(F32) * sz_ref[0].astype(F32)).astype(BF16)
    gt = mod0_ref[0, 0][:, 2 * D_MODEL:]
    x1 = _tok_tile(ctx_ref, x_ref) + gt * _dot(a, wout_ref[...])
    x1_ref[0] = x1
    mod1 = mod1_ref[0, 0]
    h = _rms(x1, g1_ref[...]) * (1.0 + mod1[:, D_MODEL:2 * D_MODEL]) + mod1[:, :D_MODEL]
    p = _dot(h.astype(BF16), win_ref[...])
    sz1_ref[0] = _silu(p[:, S5_WIDTH:]).astype(BF16)
    for m in range(S5_WIDTH // LANE):
        tok_scr[m] = p[:, m * LANE:(m + 1) * LANE]
    for m in range(S5_WIDTH // LANE):
        v = [tok_scr[m, pl.ds(t, TM // CH_T, stride=CH_T), :] for t in range(CH_T)]
        for hf in range(CH_T // 2):
            even, odd = _swap_halves(v[2 * hf], v[2 * hf + 1])
            c0 = 2 * m * UNIT_K + hf * LANE
            xu_ref[0, :, c0:c0 + LANE] = even.astype(BF16)
            xu_ref[0, :, c0 + UNIT_K:c0 + UNIT_K + LANE] = odd.astype(BF16)


def _mla_out(o, sz, ctx, x, mod0, mod1, g1, wout, win):
    nct = CTX_LEN // TM
    full = lambda shape: pl.BlockSpec(shape, lambda b, i: (0,) * len(shape))
    tok = lambda w: pl.BlockSpec((1, TM, w), lambda b, i: (b, i, 0))
    lat = lambda w: pl.BlockSpec((1, TM, w), lambda b, i: (b, jnp.maximum(i - nct, 0), 0))
    modspec = pl.BlockSpec((1, 1, 1, 3 * D_MODEL), lambda b, i: (b, jnp.where(i < nct, 0, 1), 0, 0))
    return pl.pallas_call(
        _mla_out_kernel,
        grid=(BATCH, TOK // TM),
        in_specs=[tok(MLA_WIDTH), tok(MLA_WIDTH), *_tok_specs(), modspec, modspec,
                  full((1, D_MODEL)), full((MLA_WIDTH, D_MODEL)), full((D_MODEL, 2 * S5_WIDTH))],
        out_specs=[lat(D_MODEL),
                   pl.BlockSpec((1, TM // CH_T, UNITS * UNIT_K), lambda b, i: (b, i, 0)),
                   lat(S5_WIDTH)],
        out_shape=[
            jax.ShapeDtypeStruct((BATCH, SEQ, D_MODEL), F32),
            jax.ShapeDtypeStruct((BATCH, NCH, UNITS * UNIT_K), BF16),
            jax.ShapeDtypeStruct((BATCH, SEQ, S5_WIDTH), BF16),
        ],
        scratch_shapes=[pltpu.VMEM((S5_WIDTH // LANE, TM, LANE), F32)],
        compiler_params=_params(("arbitrary", "arbitrary")),
        name="mla_out_s5_in",
    )(o, sz, ctx, x, mod0, mod1, g1, wout, win)


def _cmul(ar, ai, br, bi):
    return ar * br - ai * bi, ar * bi + ai * br


def _group_dot(a, b, precision=lax.Precision.HIGHEST):
    return lax.dot_general(a, b, (((2,), (2,)), ((0,), (0,))), precision=precision, preferred_element_type=F32)


def _group_transpose(eye, a):
    return _group_dot(eye, a.astype(BF16), precision=None)


def _s5_prep_kernel(are_ref, aim_ref, ls_ref, bre_ref, bim_ref, cre_ref, cim_ref, lam_ref, pb_ref, cp_ref, kk_ref):
    n = are_ref.shape[0]
    eye = (lax.broadcasted_iota(jnp.int32, (n, S5_STATE, S5_STATE), 1)
           == lax.broadcasted_iota(jnp.int32, (n, S5_STATE, S5_STATE), 2)).astype(BF16)
    ar = are_ref[...]
    ai = aim_ref[...]
    dt = jnp.exp(ls_ref[...])
    mag = jnp.exp(ar * dt)
    lb_re = mag * jnp.cos(ai * dt)
    lb_im = mag * jnp.sin(ai * dt)
    den = ar * ar + ai * ai
    nr = lb_re - 1.0
    f_re = ((nr * ar + lb_im * ai) / den)[:, None, :]
    f_im = ((lb_im * ar - nr * ai) / den)[:, None, :]
    bb_re, bb_im = _cmul(f_re, f_im, bre_ref[...], bim_ref[...])
    c_re = cre_ref[...]
    c_im = cim_ref[...]
    pw_re = jnp.ones_like(lb_re)
    pw_im = jnp.zeros_like(lb_re)
    for r in range(CH_T + 1):
        pr = pw_re[:, None, :]
        pi = pw_im[:, None, :]
        cl_re, cl_im = _cmul(c_re, c_im, pr, pi)
        if r < CH_T:
            q_re, q_im = _cmul(pr, pi, bb_re, bb_im)
            pb_ref[0, r] = q_re
            pb_ref[1, r] = q_im
            kk_ref[r] = _group_dot(bb_re, cl_re) - _group_dot(bb_im, cl_im)
        if r > 0:
            cp_ref[0, r - 1] = _group_transpose(eye, cl_re)
            cp_ref[1, r - 1] = _group_transpose(eye, -cl_im)
        if r == CH_T:
            lam_ref[0] = pw_re
            lam_ref[1] = pw_im
        else:
            pw_re, pw_im = _cmul(pw_re, pw_im, lb_re, lb_im)


def _s5_prep(a_re, a_im, log_step, b_re_t, b_im_t, c_re, c_im):
    n = a_re.shape[0]
    nb = 16
    row2 = pl.BlockSpec((nb, S5_STATE), lambda i: (i, 0))
    row3 = pl.BlockSpec((nb, S5_GROUP, S5_STATE), lambda i: (i, 0, 0))
    return pl.pallas_call(
        _s5_prep_kernel,
        grid=(n // nb,),
        in_specs=[row2, row2, pl.BlockSpec((nb, 1), lambda i: (i, 0)), row3, row3, row3, row3],
        out_specs=[
            pl.BlockSpec((2, nb, S5_STATE), lambda i: (0, i, 0)),
            pl.BlockSpec((2, CH_T, nb, S5_GROUP, S5_STATE), lambda i: (0, 0, i, 0, 0)),
            pl.BlockSpec((2, CH_T, nb, S5_STATE, S5_GROUP), lambda i: (0, 0, i, 0, 0)),
            pl.BlockSpec((CH_T, nb, S5_GROUP, S5_GROUP), lambda i: (0, i, 0, 0)),
        ],
        out_shape=[
            jax.ShapeDtypeStruct((2, n, S5_STATE), F32),
            jax.ShapeDtypeStruct((2, CH_T, n, S5_GROUP, S5_STATE), F32),
            jax.ShapeDtypeStruct((2, CH_T, n, S5_STATE, S5_GROUP), F32),
            jax.ShapeDtypeStruct((CH_T, n, S5_GROUP, S5_GROUP), F32),
        ],
        compiler_params=_params(("arbitrary",)),
        name="s5_prep",
    )(a_re, a_im, log_step, b_re_t, b_im_t, c_re, c_im)


STATE_TILES = 2 * 2 * UNIT_ST // LANE


def _hdot(a, rep):
    return _dot(a.astype(BF16), rep)


def _unit_operators(kk_ref, pb_ref, cp_ref, d_ref):
    def iota(shape, dim):
        return lax.broadcasted_iota(jnp.int32, shape, dim)

    rep16 = (iota((S5_GROUP, UNIT_K), 1) % S5_GROUP == iota((S5_GROUP, UNIT_K), 0)).astype(BF16)
    rep64 = (iota((S5_STATE, UNIT_ST), 1) % S5_STATE == iota((S5_STATE, UNIT_ST), 0)).astype(BF16)
    row = iota((UNIT_CH, UNIT_K), 0)
    col = iota((UNIT_CH, UNIT_K), 1)
    same_group_out = row // S5_GROUP == (col // S5_GROUP) % UNIT_G
    same_group_st = row // S5_GROUP == col // S5_STATE
    on_diag = row == col % UNIT_CH
    col_t = col // UNIT_CH
    srow = iota((UNIT_ST, UNIT_K), 0)
    scol = iota((UNIT_ST, UNIT_K), 1)
    st_same_group = srow // S5_STATE == (scol // S5_GROUP) % UNIT_G
    st_col_t = scol // UNIT_CH
    ms, bzs, cos = [], [], []
    for d in range(2):
        kexp = [_hdot(kk_ref[k, d, 0].reshape(UNIT_CH, S5_GROUP), rep16) for k in range(CH_T)]
        rows = []
        for j in range(CH_T):
            acc = jnp.zeros((UNIT_CH, UNIT_K), F32)
            for k in range(CH_T):
                lag_ok = (col_t - j == k) if d == 0 else (j - col_t == k)
                acc = acc + jnp.where(lag_ok & same_group_out, kexp[k], 0.0)
            if d == 0:
                acc = acc + jnp.where((col_t == j) & on_diag, d_ref[0], 0.0)
            rows.append(acc)
        ms.append(jnp.concatenate(rows, axis=0))
        rows = []
        for j in range(CH_T):
            r = CH_T - 1 - j if d == 0 else j
            rows.append(jnp.concatenate(
                [jnp.where(same_group_st, _hdot(pb_ref[ri, r, d, 0].reshape(UNIT_CH, S5_STATE), rep64), 0.0)
                 for ri in range(2)], axis=-1))
        bzs.append(jnp.concatenate(rows, axis=0))
        rows = []
        for ri in range(2):
            acc = jnp.zeros((UNIT_ST, UNIT_K), F32)
            for rr in range(CH_T):
                t = rr if d == 0 else CH_T - 1 - rr
                acc = acc + jnp.where((st_col_t == t) & st_same_group,
                                      _hdot(cp_ref[ri, rr, d, 0].reshape(UNIT_ST, S5_GROUP), rep16), 0.0)
            rows.append(acc)
        cos.append(jnp.concatenate(rows, axis=0))
    return jnp.concatenate(ms + bzs, axis=-1).astype(BF16), jnp.concatenate(cos, axis=0).astype(BF16)


def _s5_core_kernel(x_ref, kk_ref, pb_ref, cp_ref, d_ref, lam_ref, y_ref, st_scr, yi_scr):
    w1, co = _unit_operators(kk_ref, pb_ref, cp_ref, d_ref)
    for b in range(BATCH):
        r = _dot(x_ref[b], w1)
        y_ref[b] = r[:, :UNIT_K] + r[:, UNIT_K:2 * UNIT_K]
        for lt in range(STATE_TILES):
            c0 = 2 * UNIT_K + lt * LANE
            st_scr[lt, pl.ds(b, NCH, stride=BATCH), :] = r[:, c0:c0 + LANE]
    lam = [lam_ref[0, lt] for lt in range(STATE_TILES)]

    def rows(chunk):
        return pl.ds(pl.multiple_of(chunk * BATCH, BATCH), BATCH)

    def load_z(row, base):
        return [st_scr[base + k, rows(row), :] for k in range(4)]

    def advance(state, z, row, base):
        for k in range(4):
            st_scr[base + k, rows(row), :] = state[k]
        ar0, ar1, ai0, ai1 = lam[base:base + 4]
        return [ar0 * state[0] - ai0 * state[2] + z[0], ar1 * state[1] - ai1 * state[3] + z[1],
                ar0 * state[2] + ai0 * state[0] + z[2], ar1 * state[3] + ai1 * state[1] + z[3]]

    def bwd_row(i):
        return jnp.where(i < NCH_CTX, NCH_CTX - 1 - i, NCH + NCH_CTX - 1 - i)

    def step(i, carry):
        s_f, z_f, s_b, z_b = carry
        nxt = jnp.minimum(i + 1, NCH - 1)
        z_f_next = load_z(nxt, 0)
        z_b_next = load_z(bwd_row(nxt), 4)
        return advance(s_f, z_f, i, 0), z_f_next, advance(s_b, z_b, bwd_row(i), 4), z_b_next

    zero = [jnp.zeros((BATCH, LANE), F32)] * 4
    lax.fori_loop(0, NCH, step, (zero, load_z(0, 0), zero, load_z(NCH_CTX - 1, 4)), unroll=2)
    for rb in range(BATCH):
        sl = slice(rb * NCH, (rb + 1) * NCH)
        lhs = jnp.concatenate([st_scr[lt, sl, :] for lt in range(STATE_TILES)], axis=-1)
        yi = _dot(lhs.astype(BF16), co)
        for t in range(UNIT_K // LANE):
            yi_scr[t, sl, :] = yi[:, t * LANE:(t + 1) * LANE]
    for b in range(BATCH):
        y_ref[b] = y_ref[b] + jnp.concatenate(
            [yi_scr[t, pl.ds(b, NCH, stride=BATCH), :] for t in range(UNIT_K // LANE)], axis=-1)


def _s5_core(xu, kk, pb, cp, d, lam):
    return pl.pallas_call(
        _s5_core_kernel,
        grid=(UNITS,),
        in_specs=[
            pl.BlockSpec((BATCH, NCH, UNIT_K), lambda q: (0, 0, q)),
            pl.BlockSpec((CH_T, 2, 1, UNIT_G, S5_GROUP, S5_GROUP), lambda q: (0, 0, q, 0, 0, 0)),
            pl.BlockSpec((2, CH_T, 2, 1, UNIT_G, S5_GROUP, S5_STATE), lambda q: (0, 0, 0, q, 0, 0, 0)),
            pl.BlockSpec((2, CH_T, 2, 1, UNIT_G, S5_STATE, S5_GROUP), lambda q: (0, 0, 0, q, 0, 0, 0)),
            pl.BlockSpec((1, UNIT_CH, 1), lambda q: (q, 0, 0)),
            pl.BlockSpec((1, STATE_TILES, SUB, LANE), lambda q: (q, 0, 0, 0)),
        ],
        out_specs=pl.BlockSpec((BATCH, NCH, UNIT_K), lambda q: (0, 0, q)),
        out_shape=jax.ShapeDtypeStruct((BATCH, NCH, UNITS * UNIT_K), F32),
        scratch_shapes=[pltpu.VMEM((STATE_TILES, BATCH * NCH, LANE), F32),
                        pltpu.VMEM((UNIT_K // LANE, BATCH * NCH, LANE), F32)],
        compiler_params=_params(("arbitrary",)),
        name="s5_core",
    )(xu, kk, pb, cp, d, lam)


def _fin_kernel(y_ref, sz_ref, x_ref, mod_ref, wglu_ref, bglu_ref, wout_ref, fg_ref, o_ref, tok_scr):
    for m in range(S5_WIDTH // LANE):
        for hf in range(CH_T // 2):
            c0 = 2 * m * UNIT_K + hf * LANE
            va, vb = _swap_halves(y_ref[0, :, c0:c0 + LANE], y_ref[0, :, c0 + UNIT_K:c0 + UNIT_K + LANE])
            tok_scr[m, pl.ds(2 * hf, TM // CH_T, stride=CH_T), :] = va
            tok_scr[m, pl.ds(2 * hf + 1, TM // CH_T, stride=CH_T), :] = vb
    y = jnp.concatenate([tok_scr[m] for m in range(S5_WIDTH // LANE)], axis=-1)
    y = jax.nn.gelu(y)
    y = y * jax.nn.sigmoid(_dot(y.astype(BF16), wglu_ref[...]) + bglu_ref[...])
    a = (y * sz_ref[0].astype(F32)).astype(BF16)
    gt = mod_ref[0, 0][:, 2 * D_MODEL:]
    x2 = x_ref[0] + gt * _dot(a, wout_ref[...])
    o_ref[0] = _rms(x2, fg_ref[...])


def _finish(y, sz1, x1, mod1, wglu, bglu, wout, fg):
    nct = CTX_LEN // TM
    full = lambda shape: pl.BlockSpec(shape, lambda b, i: (0,) * len(shape))
    tok = pl.BlockSpec((1, TM, D_MODEL), lambda b, i: (b, i, 0))
    return pl.pallas_call(
        _fin_kernel,
        grid=(BATCH, SEQ // TM),
        in_specs=[pl.BlockSpec((1, TM // CH_T, UNITS * UNIT_K), lambda b, i: (b, i + nct, 0)), tok, tok,
                  pl.BlockSpec((1, 1, 1, 3 * D_MODEL), lambda b, i: (b, 1, 0, 0)),
                  full((S5_WIDTH, S5_WIDTH)), full((1, S5_WIDTH)),
                  full((S5_WIDTH, D_MODEL)), full((1, D_MODEL))],
        out_specs=pl.BlockSpec((1, TM, D_MODEL), lambda b, i: (b, i, 0)),
        out_shape=jax.ShapeDtypeStruct((BATCH, SEQ, D_MODEL), F32),
        scratch_shapes=[pltpu.VMEM((S5_WIDTH // LANE, TM, LANE), F32)],
        compiler_params=_params(("arbitrary", "arbitrary")),
        name="s5_finish",
    )(y, sz1, x1, mod1, wglu, bglu, wout, fg)


def _rot_partner(w):
    return jnp.concatenate([-w[..., 8:16], w[..., 0:8], -w[..., 24:32], w[..., 16:24]], axis=-1)


def _rope_tables():
    h = QK_ROPE_DIM // 2
    inv = 1.0 / (ROPE_THETA ** (jnp.arange(0, h, 2, dtype=F32) / h))
    pos = jnp.arange(SEQ, dtype=jnp.int32)
    ang_r = (pos // GRID_W).astype(F32)[:, None] * inv[None, :]
    ang_c = (pos % GRID_W).astype(F32)[:, None] * inv[None, :]
    cos32 = jnp.concatenate([jnp.cos(ang_r)] * 2 + [jnp.cos(ang_c)] * 2, axis=-1)
    sin32 = jnp.concatenate([jnp.sin(ang_r)] * 2 + [jnp.sin(ang_c)] * 2, axis=-1)
    ones = jnp.ones((SEQ, QK_NOPE_DIM), F32)
    zeros = jnp.zeros((SEQ, QK_NOPE_DIM), F32)
    pad = jnp.zeros((SEQ, HEAD_PAD - QK_DIM), F32)
    cos_l = jnp.concatenate([ones, cos32, pad], axis=-1)
    sin_l = jnp.concatenate([zeros, sin32, pad], axis=-1)
    kt_l = jnp.concatenate([cos32, sin32, zeros], axis=-1)
    cos_c = jnp.concatenate([jnp.ones((CTX_LEN, QK_DIM), F32), jnp.zeros((CTX_LEN, HEAD_PAD - QK_DIM), F32)], -1)
    sin_c = jnp.zeros((CTX_LEN, HEAD_PAD), F32)
    kt_c = jnp.concatenate([jnp.ones((CTX_LEN, 32), F32), jnp.zeros((CTX_LEN, HEAD_PAD - 32), F32)], -1)
    return (jnp.concatenate([cos_c, cos_l], 0), jnp.concatenate([sin_c, sin_l], 0),
            jnp.concatenate([kt_c, kt_l], 0))


def _mla_weights(w_in, w_uq, w_ukv):
    o1, o2, o3 = Q_LORA_RANK, Q_LORA_RANK + KV_LORA_RANK, Q_LORA_RANK + KV_LORA_RANK + QK_ROPE_DIM
    w_kr = w_in[:, o2:o3]
    win = jnp.concatenate([w_in[:, :o2], w_kr, _rot_partner(w_kr), jnp.zeros((D_MODEL, 64), F32), w_in[:, o3:]],
                          axis=-1).astype(BF16)
    wq = (w_uq * (SOFTMAX_SCALE * math.log2(math.e))).reshape(Q_LORA_RANK, MLA_HEADS, QK_DIM)
    zq = jnp.zeros((Q_LORA_RANK, MLA_HEADS, HEAD_PAD - QK_DIM), F32)
    wqa = jnp.concatenate([wq, zq], axis=-1).reshape(Q_LORA_RANK, QK_PAD).astype(BF16)
    wqb = jnp.concatenate([jnp.zeros_like(wq[..., :QK_NOPE_DIM]), _rot_partner(wq[..., QK_NOPE_DIM:]), zq],
                          axis=-1).reshape(Q_LORA_RANK, QK_PAD).astype(BF16)
    wkv = w_ukv.reshape(KV_LORA_RANK, MLA_HEADS, QK_NOPE_DIM + V_HEAD_DIM)
    wk_top = jnp.concatenate([wkv[..., :QK_NOPE_DIM], jnp.zeros((KV_LORA_RANK, MLA_HEADS, 64), F32)], axis=-1)
    eye = jnp.eye(QK_ROPE_DIM, dtype=F32)[:, None, :]
    place = jnp.concatenate([jnp.zeros((QK_ROPE_DIM, MLA_HEADS, QK_NOPE_DIM), F32),
                             jnp.broadcast_to(eye, (QK_ROPE_DIM, MLA_HEADS, QK_ROPE_DIM)),
                             jnp.zeros((QK_ROPE_DIM, MLA_HEADS, HEAD_PAD - QK_DIM), F32)], axis=-1)
    wk = jnp.concatenate([wk_top, place, place, jnp.zeros((64, MLA_HEADS, HEAD_PAD), F32)], axis=0)
    wk = wk.reshape(256, QK_PAD).astype(BF16)
    wv = wkv[..., QK_NOPE_DIM:].reshape(KV_LORA_RANK, MLA_WIDTH).astype(BF16)
    return win, wqa, wqb, wk, wv


def _lam_tiles(lam):
    lam = lam.reshape(2, 2, UNITS, UNIT_ST // LANE, LANE)
    lam = jnp.concatenate([lam[0, 0], lam[1, 0], lam[0, 1], lam[1, 1]], axis=1)
    return jnp.broadcast_to(lam[:, :, None, :], (UNITS, STATE_TILES, SUB, LANE))


def kernel(x, c, ctx, c_ctx, ada_w, ada_b, norm_g, mla_w_in, mla_q_norm, mla_w_uq, mla_kv_norm, mla_w_ukv, mla_w_out, s5_w_in, s5_a_re, s5_a_im, s5_log_step, s5_b_re, s5_b_im, s5_c_re, s5_c_im, s5_d, s5_w_glu, s5_b_glu, s5_w_out, final_g):
    cc = jnp.concatenate([c, c_ctx[None, :], jnp.zeros((7, D_MODEL), F32)], axis=0)
    mods = _modulation(cc, ada_w, ada_b)

    def mod_rows(i):
        ctx_row = jnp.broadcast_to(mods[i, 8][None, :], (BATCH, 3 * D_MODEL))
        return jnp.stack([ctx_row, mods[i, :BATCH]], axis=1)[:, :, None, :]

    mod0, mod1 = mod_rows(0), mod_rows(1)

    win, wqa, wqb, wk, wv = _mla_weights(mla_w_in[0], mla_w_uq[0], mla_w_ukv[0])
    cos, sin, kt = _rope_tables()
    q, k, v, sz = _mla_proj(ctx, x, mod0, norm_g[0][None, :], win, mla_q_norm[0][None, :],
                            mla_kv_norm[0][None, :], wqa, wqb, wk, wv, cos, sin, kt)
    o = _attention(q, k, v)
    x1, xu, sz1 = _mla_out(o, sz, ctx, x, mod0, mod1, norm_g[1][None, :], mla_w_out[0].astype(BF16),
                           s5_w_in[0].astype(BF16))

    n = 2 * S5_GROUPS
    lam, pb, cp, kk = _s5_prep(
        s5_a_re[0].reshape(n, S5_STATE), s5_a_im[0].reshape(n, S5_STATE), s5_log_step[0].reshape(n, 1),
        jnp.swapaxes(s5_b_re[0], -1, -2).reshape(n, S5_GROUP, S5_STATE),
        jnp.swapaxes(s5_b_im[0], -1, -2).reshape(n, S5_GROUP, S5_STATE),
        s5_c_re[0].reshape(n, S5_GROUP, S5_STATE), s5_c_im[0].reshape(n, S5_GROUP, S5_STATE))
    y = _s5_core(xu, kk.reshape(CH_T, 2, UNITS, UNIT_G, S5_GROUP, S5_GROUP),
                 pb.reshape(2, CH_T, 2, UNITS, UNIT_G, S5_GROUP, S5_STATE),
                 cp.reshape(2, CH_T, 2, UNITS, UNIT_G, S5_STATE, S5_GROUP),
                 s5_d[0].reshape(UNITS, UNIT_CH, 1), _lam_tiles(lam))
    return _finish(y, sz1, x1, mod1, s5_w_glu[0].astype(BF16), s5_b_glu[0][None, :], s5_w_out[0].astype(BF16),
                   final_g[None, :])
```

```python
import math

import jax
import jax.numpy as jnp
import numpy as np
from jax import lax
from jax.experimental import pallas as pl
from jax.experimental.pallas import tpu as pltpu

D_MODEL = 1024
BATCH = 8
SEQ = 2048
GRID_W = 64
CTX_LEN = 256
TOK = CTX_LEN + SEQ
EPS = 1e-6

MLA_HEADS = 16
QK_NOPE_DIM = 64
QK_ROPE_DIM = 32
V_HEAD_DIM = 64
Q_LORA_RANK = 256
KV_LORA_RANK = 128
MLA_WIDTH = MLA_HEADS * V_HEAD_DIM
QK_DIM = QK_NOPE_DIM + QK_ROPE_DIM
SOFTMAX_SCALE = QK_DIM ** -0.5
ROPE_THETA = 10000.0
HEAD_PAD = 128
QK_PAD = MLA_HEADS * HEAD_PAD
PROJ_W = 1536

S5_WIDTH = D_MODEL
S5_GROUP = 16
S5_GROUPS = 64
S5_STATE = 64
CH_T = 4
UNIT_G = 4
UNIT_CH = UNIT_G * S5_GROUP
UNITS = S5_GROUPS // UNIT_G
UNIT_K = CH_T * UNIT_CH
UNIT_ST = UNIT_G * S5_STATE
NCH = TOK // CH_T
NCH_CTX = CTX_LEN // CH_T
LANE = 128
SUB = 8

TM = 256
TQ = 256
KCH = 256
HPAIRS = 2
VMEM_LIMIT = 56 * 1024 * 1024

F32 = jnp.float32
BF16 = jnp.bfloat16


def _params(sem, flags=None):
    return pltpu.CompilerParams(dimension_semantics=sem, vmem_limit_bytes=VMEM_LIMIT, flags=flags)


def _silu(v):
    return v * jax.nn.sigmoid(v)


def _rms(v, g):
    return v * lax.rsqrt(jnp.mean(v * v, axis=-1, keepdims=True) + EPS) * g


def _dot(a, b):
    return jnp.dot(a, b, preferred_element_type=F32)


def _mod_kernel(cc_ref, w_ref, b_ref, o_ref):
    a = _silu(cc_ref[...]).astype(BF16)
    o_ref[0] = _dot(a, w_ref[0].astype(BF16)) + b_ref[0]


def _modulation(cc, ada_w, ada_b):
    depth = ada_w.shape[0]
    tn = 768
    return pl.pallas_call(
        _mod_kernel,
        grid=(depth, 3 * D_MODEL // tn),
        in_specs=[
            pl.BlockSpec((16, D_MODEL), lambda i, j: (0, 0)),
            pl.BlockSpec((1, D_MODEL, tn), lambda i, j: (i, 0, j)),
            pl.BlockSpec((1, 1, tn), lambda i, j: (i, 0, j)),
        ],
        out_specs=pl.BlockSpec((1, 16, tn), lambda i, j: (i, 0, j)),
        out_shape=jax.ShapeDtypeStruct((depth, 16, 3 * D_MODEL), F32),
        compiler_params=_params(("arbitrary", "arbitrary")),
        name="modulation",
    )(cc, ada_w, ada_b.reshape(depth, 1, 3 * D_MODEL))


def _tok_specs():
    nct = CTX_LEN // TM
    return (pl.BlockSpec((1, TM, D_MODEL), lambda b, i: (b, jnp.minimum(i, nct - 1), 0)),
            pl.BlockSpec((1, TM, D_MODEL), lambda b, i: (b, jnp.maximum(i - nct, 0), 0)))


def _tok_tile(ctx_ref, x_ref):
    return jnp.where(pl.program_id(1) < CTX_LEN // TM, ctx_ref[0], x_ref[0])


def _mla_proj_kernel(ctx_ref, x_ref, mod_ref, g_ref, win_ref, qg_ref, kvg_ref, wqa_ref, wqb_ref, wk_ref, wv_ref,
                     cos_ref, sin_ref, kt_ref, q_ref, k_ref, v_ref, sz_ref):
    x = _tok_tile(ctx_ref, x_ref)
    mod = mod_ref[0, 0]
    sh = mod[:, :D_MODEL]
    sc = mod[:, D_MODEL:2 * D_MODEL]
    h = _rms(x, g_ref[...]) * (1.0 + sc) + sh
    p = _dot(h.astype(BF16), win_ref[...])
    cqn = _rms(p[:, :Q_LORA_RANK], qg_ref[...]).astype(BF16)
    ckvn = _rms(p[:, Q_LORA_RANK:Q_LORA_RANK + KV_LORA_RANK], kvg_ref[...]).astype(BF16)
    kr = p[:, 384:512]
    z = p[:, 512:]
    qa = _dot(cqn, wqa_ref[...])
    qb = _dot(cqn, wqb_ref[...])
    cos = cos_ref[...]
    sin = sin_ref[...]
    for hd in range(MLA_HEADS):
        sl = slice(hd * HEAD_PAD, (hd + 1) * HEAD_PAD)
        q_ref[0, :, sl] = (qa[:, sl] * cos + qb[:, sl] * sin).astype(BF16)
    kin = jnp.concatenate([ckvn, (kr * kt_ref[...]).astype(BF16)], axis=-1)
    k_ref[0] = _dot(kin, wk_ref[...]).astype(BF16)
    v_ref[0] = _dot(ckvn, wv_ref[...]).astype(BF16)
    sz_ref[0] = _silu(z).astype(BF16)


def _mla_proj(ctx, x, mod, g, win, qg, kvg, wqa, wqb, wk, wv, cos, sin, kt):
    nct = CTX_LEN // TM
    full = lambda shape: pl.BlockSpec(shape, lambda b, i: (0,) * len(shape))
    tok = lambda w: pl.BlockSpec((1, TM, w), lambda b, i: (b, i, 0))
    pos = pl.BlockSpec((TM, HEAD_PAD), lambda b, i: (i, 0))
    return pl.pallas_call(
        _mla_proj_kernel,
        grid=(BATCH, TOK // TM),
        in_specs=[
            *_tok_specs(),
            pl.BlockSpec((1, 1, 1, 3 * D_MODEL), lambda b, i: (b, jnp.where(i < nct, 0, 1), 0, 0)),
            full((1, D_MODEL)), full((D_MODEL, PROJ_W)), full((1, Q_LORA_RANK)), full((1, KV_LORA_RANK)),
            full((Q_LORA_RANK, QK_PAD)), full((Q_LORA_RANK, QK_PAD)), full((256, QK_PAD)),
            full((KV_LORA_RANK, MLA_WIDTH)), pos, pos, pos,
        ],
        out_specs=[tok(QK_PAD), tok(QK_PAD), tok(MLA_WIDTH), tok(MLA_WIDTH)],
        out_shape=[
            jax.ShapeDtypeStruct((BATCH, TOK, QK_PAD), BF16),
            jax.ShapeDtypeStruct((BATCH, TOK, QK_PAD), BF16),
            jax.ShapeDtypeStruct((BATCH, TOK, MLA_WIDTH), BF16),
            jax.ShapeDtypeStruct((BATCH, TOK, MLA_WIDTH), BF16),
        ],
        compiler_params=_params(("arbitrary", "arbitrary")),
        name="mla_proj",
    )(ctx, x, mod, g, win, qg, kvg, wqa, wqb, wk, wv, cos, sin, kt)


def _attn_kernel(q_ref, k_ref, v_ref, o_ref, s_buf, m_buf, vx_buf):
    nt = SEQ // TQ
    lane = lax.broadcasted_iota(jnp.int32, (TOK, 2 * V_HEAD_DIM), 1)
    for hp in range(HPAIRS):
        v = v_ref[0, :, hp * 2 * V_HEAD_DIM:(hp + 1) * 2 * V_HEAD_DIM]
        vx_buf[hp, 0] = jnp.where(lane < V_HEAD_DIM, v, (lane == V_HEAD_DIM).astype(BF16))
        vx_buf[hp, 1] = jnp.where(lane >= V_HEAD_DIM, v, (lane == 0).astype(BF16))

    def scores(hp, row, nk, slot):
        for hh in range(2):
            c0 = (2 * hp + hh) * HEAD_PAD
            s = lax.dot_general(q_ref[0, pl.ds(row, TQ), c0:c0 + HEAD_PAD], k_ref[0, :nk, c0:c0 + HEAD_PAD],
                                (((1,), (1,)), ((), ())), preferred_element_type=F32)
            s_buf[slot, hh, :, :nk] = s
            m_buf[slot, hh] = jnp.broadcast_to(jnp.max(s, axis=-1, keepdims=True), (TQ, KCH))

    def values(hp, row, nk, slot):
        outs = []
        for hh in range(2):
            m = m_buf[slot, hh]
            ps = [jnp.exp2(s_buf[slot, hh, :, n * KCH:(n + 1) * KCH] - m).astype(BF16) for n in range(nk // KCH)]
            acc = _dot(jnp.concatenate(ps, axis=-1), vx_buf[hp, hh, :nk, :])
            l_col = V_HEAD_DIM if hh == 0 else 0
            outs.append(acc / acc[:, l_col:l_col + 1])
        olane = lax.broadcasted_iota(jnp.int32, outs[0].shape, 1)
        o_ref[0, pl.ds(row, TQ), hp * 2 * V_HEAD_DIM:(hp + 1) * 2 * V_HEAD_DIM] = jnp.where(
            olane < V_HEAD_DIM, outs[0], outs[1]).astype(BF16)

    def lat_row(t):
        return pl.multiple_of(CTX_LEN + t * TQ, TQ)

    for hp in range(HPAIRS):
        scores(hp, 0, CTX_LEN, 0)
        values(hp, 0, CTX_LEN, 0)

    scores(0, lat_row(0), TOK, 0)
    for hp in range(HPAIRS):
        def pair(pp, carry, hp=hp):
            t = 1 + 2 * pp
            scores(hp, lat_row(t), TOK, 1)
            values(hp, lat_row(t - 1), TOK, 0)
            scores(hp, lat_row(t + 1), TOK, 0)
            values(hp, lat_row(t), TOK, 1)
            return carry

        lax.fori_loop(0, (nt - 2) // 2, pair, 0)
        scores(hp, lat_row(nt - 1), TOK, 1)
        values(hp, lat_row(nt - 2), TOK, 0)
        if hp + 1 < HPAIRS:
            scores(hp + 1, lat_row(0), TOK, 0)
        values(hp, lat_row(nt - 1), TOK, 1)


def _attention(q, k, v):
    qk = pl.BlockSpec((1, TOK, HPAIRS * 2 * HEAD_PAD), lambda b, h: (b, 0, h))
    vo = pl.BlockSpec((1, TOK, HPAIRS * 2 * V_HEAD_DIM), lambda b, h: (b, 0, h))
    return pl.pallas_call(
        _attn_kernel,
        grid=(BATCH, MLA_HEADS // (2 * HPAIRS)),
        in_specs=[qk, qk, vo],
        out_specs=vo,
        out_shape=jax.ShapeDtypeStruct((BATCH, TOK, MLA_WIDTH), BF16),
        scratch_shapes=[
            pltpu.VMEM((2, 2, TQ, TOK), F32),
            pltpu.VMEM((2, 2, TQ, KCH), F32),
            pltpu.VMEM((HPAIRS, 2, TOK, 2 * V_HEAD_DIM), BF16),
        ],
        compiler_params=_params(("arbitrary", "arbitrary")),
        name="attention",
    )(q, k, v)


def _swap_halves(va, vb):
    lo = lax.broadcasted_iota(jnp.int32, va.shape, 1) < UNIT_CH
    return (jnp.where(lo, va, pltpu.roll(vb, UNIT_CH, 1)),
            jnp.where(lo, pltpu.roll(va, UNIT_CH, 1), vb))


def _mla_out_kernel(o_ref, sz_ref, ctx_ref, x_ref, mod0_ref, mod1_ref, g1_ref, wout_ref, win_ref,
                    x1_ref, xu_ref, sz1_ref, tok_scr):
    a = (o_ref[0].astype(F32) * sz_ref[0].astype(F32)).astype(BF16)
    gt = mod0_ref[0, 0][:, 2 * D_MODEL:]
    x1 = _tok_tile(ctx_ref, x_ref) + gt * _dot(a, wout_ref[...])
    x1_ref[0] = x1
    mod1 = mod1_ref[0, 0]
    h = _rms(x1, g1_ref[...]) * (1.0 + mod1[:, D_MODEL:2 * D_MODEL]) + mod1[:, :D_MODEL]
    p = _dot(h.astype(BF16), win_ref[...])
    sz1_ref[0] = _silu(p[:, S5_WIDTH:]).astype(BF16)
    for m in range(S5_WIDTH // LANE):
        tok_scr[m] = p[:, m * LANE:(m + 1) * LANE]
    for m in range(S5_WIDTH // LANE):
        v = [tok_scr[m, pl.ds(t, TM // CH_T, stride=CH_T), :] for t in range(CH_T)]
        for hf in range(CH_T // 2):
            even, odd = _swap_halves(v[2 * hf], v[2 * hf + 1])
            c0 = 2 * m * UNIT_K + hf * LANE
            xu_ref[0, :, c0:c0 + LANE] = even.astype(BF16)
            xu_ref[0, :, c0 + UNIT_K:c0 + UNIT_K + LANE] = odd.astype(BF16)


def _mla_out(o, sz, ctx, x, mod0, mod1, g1, wout, win):
    nct = CTX_LEN // TM
    full = lambda shape: pl.BlockSpec(shape, lambda b, i: (0,) * len(shape))
    tok = lambda w: pl.BlockSpec((1, TM, w), lambda b, i: (b, i, 0))
    lat = lambda w: pl.BlockSpec((1, TM, w), lambda b, i: (b, jnp.maximum(i - nct, 0), 0))
    modspec = pl.BlockSpec((1, 1, 1, 3 * D_MODEL), lambda b, i: (b, jnp.where(i < nct, 0, 1), 0, 0))
    return pl.pallas_call(
        _mla_out_kernel,
        grid=(BATCH, TOK // TM),
        in_specs=[tok(MLA_WIDTH), tok(MLA_WIDTH), *_tok_specs(), modspec, modspec,
                  full((1, D_MODEL)), full((MLA_WIDTH, D_MODEL)), full((D_MODEL, 2 * S5_WIDTH))],
        out_specs=[lat(D_MODEL),
                   pl.BlockSpec((1, TM // CH_T, UNITS * UNIT_K), lambda b, i: (b, i, 0)),
                   lat(S5_WIDTH)],
        out_shape=[
            jax.ShapeDtypeStruct((BATCH, SEQ, D_MODEL), F32),
            jax.ShapeDtypeStruct((BATCH, NCH, UNITS * UNIT_K), BF16),
            jax.ShapeDtypeStruct((BATCH, SEQ, S5_WIDTH), BF16),
        ],
        scratch_shapes=[pltpu.VMEM((S5_WIDTH // LANE, TM, LANE), F32)],
        compiler_params=_params(("arbitrary", "arbitrary")),
        name="mla_out_s5_in",
    )(o, sz, ctx, x, mod0, mod1, g1, wout, win)


def _cmul(ar, ai, br, bi):
    return ar * br - ai * bi, ar * bi + ai * br


def _group_dot(a, b, precision=lax.Precision.HIGHEST):
    return lax.dot_general(a, b, (((2,), (2,)), ((0,), (0,))), precision=precision, preferred_element_type=F32)


def _group_transpose(eye, a):
    return _group_dot(eye, a.astype(BF16), precision=None)


def _s5_prep_kernel(are_ref, aim_ref, ls_ref, bre_ref, bim_ref, cre_ref, cim_ref, lam_ref, pb_ref, cp_ref, kk_ref):
    n = are_ref.shape[0]
    eye = (lax.broadcasted_iota(jnp.int32, (n, S5_STATE, S5_STATE), 1)
           == lax.broadcasted_iota(jnp.int32, (n, S5_STATE, S5_STATE), 2)).astype(BF16)
    ar = are_ref[...]
    ai = aim_ref[...]
    dt = jnp.exp(ls_ref[...])
    mag = jnp.exp(ar * dt)
    lb_re = mag * jnp.cos(ai * dt)
    lb_im = mag * jnp.sin(ai * dt)
    den = ar * ar + ai * ai
    nr = lb_re - 1.0
    f_re = ((nr * ar + lb_im * ai) / den)[:, None, :]
    f_im = ((lb_im * ar - nr * ai) / den)[:, None, :]
    bb_re, bb_im = _cmul(f_re, f_im, bre_ref[...], bim_ref[...])
    c_re = cre_ref[...]
    c_im = cim_ref[...]
    pw_re = jnp.ones_like(lb_re)
    pw_im = jnp.zeros_like(lb_re)
    for r in range(CH_T + 1):
        pr = pw_re[:, None, :]
        pi = pw_im[:, None, :]
        cl_re, cl_im = _cmul(c_re, c_im, pr, pi)
        if r < CH_T:
            q_re, q_im = _cmul(pr, pi, bb_re, bb_im)
            pb_ref[0, r] = q_re
            pb_ref[1, r] = q_im
            kk_ref[r] = _group_dot(bb_re, cl_re) - _group_dot(bb_im, cl_im)
        if r > 0:
            cp_ref[0, r - 1] = _group_transpose(eye, cl_re)
            cp_ref[1, r - 1] = _group_transpose(eye, -cl_im)
        if r == CH_T:
            lam_ref[0] = pw_re
            lam_ref[1] = pw_im
        else:
            pw_re, pw_im = _cmul(pw_re, pw_im, lb_re, lb_im)


def _s5_prep(a_re, a_im, log_step, b_re_t, b_im_t, c_re, c_im):
    n = a_re.shape[0]
    nb = 16
    row2 = pl.BlockSpec((nb, S5_STATE), lambda i: (i, 0))
    row3 = pl.BlockSpec((nb, S5_GROUP, S5_STATE), lambda i: (i, 0, 0))
    return pl.pallas_call(
        _s5_prep_kernel,
        grid=(n // nb,),
        in_specs=[row2, row2, pl.BlockSpec((nb, 1), lambda i: (i, 0)), row3, row3, row3, row3],
        out_specs=[
            pl.BlockSpec((2, nb, S5_STATE), lambda i: (0, i, 0)),
            pl.BlockSpec((2, CH_T, nb, S5_GROUP, S5_STATE), lambda i: (0, 0, i, 0, 0)),
            pl.BlockSpec((2, CH_T, nb, S5_STATE, S5_GROUP), lambda i: (0, 0, i, 0, 0)),
            pl.BlockSpec((CH_T, nb, S5_GROUP, S5_GROUP), lambda i: (0, i, 0, 0)),
        ],
        out_shape=[
            jax.ShapeDtypeStruct((2, n, S5_STATE), F32),
            jax.ShapeDtypeStruct((2, CH_T, n, S5_GROUP, S5_STATE), F32),
            jax.ShapeDtypeStruct((2, CH_T, n, S5_STATE, S5_GROUP), F32),
            jax.ShapeDtypeStruct((CH_T, n, S5_GROUP, S5_GROUP), F32),
        ],
        compiler_params=_params(("arbitrary",)),
        name="s5_prep",
    )(a_re, a_im, log_step, b_re_t, b_im_t, c_re, c_im)


STATE_TILES = 2 * 2 * UNIT_ST // LANE


def _hdot(a, rep):
    return _dot(a.astype(BF16), rep)


def _unit_operators(kk_ref, pb_ref, cp_ref, d_ref):
    def iota(shape, dim):
        return lax.broadcasted_iota(jnp.int32, shape, dim)

    rep16 = (iota((S5_GROUP, UNIT_K), 1) % S5_GROUP == iota((S5_GROUP, UNIT_K), 0)).astype(BF16)
    rep64 = (iota((S5_STATE, UNIT_ST), 1) % S5_STATE == iota((S5_STATE, UNIT_ST), 0)).astype(BF16)
    row = iota((UNIT_CH, UNIT_K), 0)
    col = iota((UNIT_CH, UNIT_K), 1)
    same_group_out = row // S5_GROUP == (col // S5_GROUP) % UNIT_G
    same_group_st = row // S5_GROUP == col // S5_STATE
    on_diag = row == col % UNIT_CH
    col_t = col // UNIT_CH
    srow = iota((UNIT_ST, UNIT_K), 0)
    scol = iota((UNIT_ST, UNIT_K), 1)
    st_same_group = srow // S5_STATE == (scol // S5_GROUP) % UNIT_G
    st_col_t = scol // UNIT_CH
    ms, bzs, cos = [], [], []
    for d in range(2):
        kexp = [_hdot(kk_ref[k, d, 0].reshape(UNIT_CH, S5_GROUP), rep16) for k in range(CH_T)]
        rows = []
        for j in range(CH_T):
            acc = jnp.zeros((UNIT_CH, UNIT_K), F32)
            for k in range(CH_T):
                lag_ok = (col_t - j == k) if d == 0 else (j - col_t == k)
                acc = acc + jnp.where(lag_ok & same_group_out, kexp[k], 0.0)
            if d == 0:
                acc = acc + jnp.where((col_t == j) & on_diag, d_ref[0], 0.0)
            rows.append(acc)
        ms.append(jnp.concatenate(rows, axis=0))
        rows = []
        for j in range(CH_T):
            r = CH_T - 1 - j if d == 0 else j
            rows.append(jnp.concatenate(
                [jnp.where(same_group_st, _hdot(pb_ref[ri, r, d, 0].reshape(UNIT_CH, S5_STATE), rep64), 0.0)
                 for ri in range(2)], axis=-1))
        bzs.append(jnp.concatenate(rows, axis=0))
        rows = []
        for ri in range(2):
            acc = jnp.zeros((UNIT_ST, UNIT_K), F32)
            for rr in range(CH_T):
                t = rr if d == 0 else CH_T - 1 - rr
                acc = acc + jnp.where((st_col_t == t) & st_same_group,
                                      _hdot(cp_ref[ri, rr, d, 0].reshape(UNIT_ST, S5_GROUP), rep16), 0.0)
            rows.append(acc)
        cos.append(jnp.concatenate(rows, axis=0))
    return jnp.concatenate(ms + bzs, axis=-1).astype(BF16), jnp.concatenate(cos, axis=0).astype(BF16)


def _s5_core_kernel(x_ref, kk_ref, pb_ref, cp_ref, d_ref, lam_ref, y_ref, st_scr, yi_scr):
    w1, co = _unit_operators(kk_ref, pb_ref, cp_ref, d_ref)
    for b in range(BATCH):
        r = _dot(x_ref[b], w1)
        y_ref[b] = r[:, :UNIT_K] + r[:, UNIT_K:2 * UNIT_K]
        for lt in range(STATE_TILES):
            c0 = 2 * UNIT_K + lt * LANE
            st_scr[lt, pl.ds(b, NCH, stride=BATCH), :] = r[:, c0:c0 + LANE]
    lam = [lam_ref[0, lt] for lt in range(STATE_TILES)]

    def rows(chunk):
        return pl.ds(pl.multiple_of(chunk * BATCH, BATCH), BATCH)

    def load_z(row, base):
        return [st_scr[base + k, rows(row), :] for k in range(4)]

    def advance(state, z, row, base):
        for k in range(4):
            st_scr[base + k, rows(row), :] = state[k]
        ar0, ar1, ai0, ai1 = lam[base:base + 4]
        return [ar0 * state[0] - ai0 * state[2] + z[0], ar1 * state[1] - ai1 * state[3] + z[1],
                ar0 * state[2] + ai0 * state[0] + z[2], ar1 * state[3] + ai1 * state[1] + z[3]]

    def bwd_row(i):
        return jnp.where(i < NCH_CTX, NCH_CTX - 1 - i, NCH + NCH_CTX - 1 - i)

    def step(i, carry):
        s_f, z_f, s_b, z_b = carry
        nxt = jnp.minimum(i + 1, NCH - 1)
        z_f_next = load_z(nxt, 0)
        z_b_next = load_z(bwd_row(nxt), 4)
        return advance(s_f, z_f, i, 0), z_f_next, advance(s_b, z_b, bwd_row(i), 4), z_b_next

    zero = [jnp.zeros((BATCH, LANE), F32)] * 4
    lax.fori_loop(0, NCH, step, (zero, load_z(0, 0), zero, load_z(NCH_CTX - 1, 4)), unroll=2)
    for rb in range(BATCH):
        sl = slice(rb * NCH, (rb + 1) * NCH)
        lhs = jnp.concatenate([st_scr[lt, sl, :] for lt in range(STATE_TILES)], axis=-1)
        yi = _dot(lhs.astype(BF16), co)
        for t in range(UNIT_K // LANE):
            yi_scr[t, sl, :] = yi[:, t * LANE:(t + 1) * LANE]
    for b in range(BATCH):
        y_ref[b] = y_ref[b] + jnp.concatenate(
            [yi_scr[t, pl.ds(b, NCH, stride=BATCH), :] for t in range(UNIT_K // LANE)], axis=-1)


def _s5_core(xu, kk, pb, cp, d, lam):
    return pl.pallas_call(
        _s5_core_kernel,
        grid=(UNITS,),
        in_specs=[
            pl.BlockSpec((BATCH, NCH, UNIT_K), lambda q: (0, 0, q)),
            pl.BlockSpec((CH_T, 2, 1, UNIT_G, S5_GROUP, S5_GROUP), lambda q: (0, 0, q, 0, 0, 0)),
            pl.BlockSpec((2, CH_T, 2, 1, UNIT_G, S5_GROUP, S5_STATE), lambda q: (0, 0, 0, q, 0, 0, 0)),
            pl.BlockSpec((2, CH_T, 2, 1, UNIT_G, S5_STATE, S5_GROUP), lambda q: (0, 0, 0, q, 0, 0, 0)),
            pl.BlockSpec((1, UNIT_CH, 1), lambda q: (q, 0, 0)),
            pl.BlockSpec((1, STATE_TILES, SUB, LANE), lambda q: (q, 0, 0, 0)),
        ],
        out_specs=pl.BlockSpec((BATCH, NCH, UNIT_K), lambda q: (0, 0, q)),
        out_shape=jax.ShapeDtypeStruct((BATCH, NCH, UNITS * UNIT_K), F32),
        scratch_shapes=[pltpu.VMEM((STATE_TILES, BATCH * NCH, LANE), F32),
                        pltpu.VMEM((UNIT_K // LANE, BATCH * NCH, LANE), F32)],
        compiler_params=_params(("arbitrary",)),
        name="s5_core",
    )(xu, kk, pb, cp, d, lam)


def _fin_kernel(y_ref, sz_ref, x_ref, mod_ref, wglu_ref, bglu_ref, wout_ref, fg_ref, o_ref, tok_scr):
    for m in range(S5_WIDTH // LANE):
        for hf in range(CH_T // 2):
            c0 = 2 * m * UNIT_K + hf * LANE
            va, vb = _swap_halves(y_ref[0, :, c0:c0 + LANE], y_ref[0, :, c0 + UNIT_K:c0 + UNIT_K + LANE])
            tok_scr[m, pl.ds(2 * hf, TM // CH_T, stride=CH_T), :] = va
            tok_scr[m, pl.ds(2 * hf + 1, TM // CH_T, stride=CH_T), :] = vb
    y = jnp.concatenate([tok_scr[m] for m in range(S5_WIDTH // LANE)], axis=-1)
    y = jax.nn.gelu(y)
    y = y * jax.nn.sigmoid(_dot(y.astype(BF16), wglu_ref[...]) + bglu_ref[...])
    a = (y * sz_ref[0].astype(F32)).astype(BF16)
    gt = mod_ref[0, 0][:, 2 * D_MODEL:]
    x2 = x_ref[0] + gt * _dot(a, wout_ref[...])
    o_ref[0] = _rms(x2, fg_ref[...])


def _finish(y, sz1, x1, mod1, wglu, bglu, wout, fg):
    nct = CTX_LEN // TM
    full = lambda shape: pl.BlockSpec(shape, lambda b, i: (0,) * len(shape))
    tok = pl.BlockSpec((1, TM, D_MODEL), lambda b, i: (b, i, 0))
    return pl.pallas_call(
        _fin_kernel,
        grid=(BATCH, SEQ // TM),
        in_specs=[pl.BlockSpec((1, TM // CH_T, UNITS * UNIT_K), lambda b, i: (b, i + nct, 0)), tok, tok,
                  pl.BlockSpec((1, 1, 1, 3 * D_MODEL), lambda b, i: (b, 1, 0, 0)),
                  full((S5_WIDTH, S5_WIDTH)), full((1, S5_WIDTH)),
                  full((S5_WIDTH, D_MODEL)), full((1, D_MODEL))],
        out_specs=pl.BlockSpec((1, TM, D_MODEL), lambda b, i: (b, i, 0)),
        out_shape=jax.ShapeDtypeStruct((BATCH, SEQ, D_MODEL), F32),
        scratch_shapes=[pltpu.VMEM((S5_WIDTH // LANE, TM, LANE), F32)],
        compiler_params=_params(("arbitrary", "arbitrary")),
        name="s5_finish",
    )(y, sz1, x1, mod1, wglu, bglu, wout, fg)


def _rope_tables():
    h = QK_ROPE_DIM // 2
    inv = 1.0 / (ROPE_THETA ** (np.arange(0, h, 2, dtype=np.float64) / h))
    pos = np.arange(SEQ)
    ang_r = (pos // GRID_W)[:, None] * inv[None, :]
    ang_c = (pos % GRID_W)[:, None] * inv[None, :]
    cos32 = np.concatenate([np.cos(ang_r)] * 2 + [np.cos(ang_c)] * 2, axis=-1)
    sin32 = np.concatenate([np.sin(ang_r)] * 2 + [np.sin(ang_c)] * 2, axis=-1)
    cos = np.zeros((TOK, HEAD_PAD), np.float32)
    sin = np.zeros((TOK, HEAD_PAD), np.float32)
    kt = np.zeros((TOK, HEAD_PAD), np.float32)
    cos[:, :QK_NOPE_DIM] = 1.0
    cos[:CTX_LEN, QK_NOPE_DIM:QK_DIM] = 1.0
    kt[:CTX_LEN, :QK_ROPE_DIM] = 1.0
    cos[CTX_LEN:, QK_NOPE_DIM:QK_DIM] = cos32
    sin[CTX_LEN:, QK_NOPE_DIM:QK_DIM] = sin32
    kt[CTX_LEN:, :QK_ROPE_DIM] = cos32
    kt[CTX_LEN:, QK_ROPE_DIM:2 * QK_ROPE_DIM] = sin32
    return jnp.asarray(cos), jnp.asarray(sin), jnp.asarray(kt)


def _mla_selectors():
    def partner(d):
        return (d + 8, -1.0) if d % 16 < 8 else (d - 8, 1.0)

    o2 = Q_LORA_RANK + KV_LORA_RANK
    o3 = o2 + QK_ROPE_DIM
    pin = np.zeros((o3 + MLA_WIDTH, PROJ_W), np.float32)
    pin[np.arange(o3), np.arange(o3)] = 1.0
    pin[o3 + np.arange(MLA_WIDTH), 512 + np.arange(MLA_WIDTH)] = 1.0
    pa = np.zeros((MLA_HEADS * QK_DIM, QK_PAD), np.float32)
    pb = np.zeros((MLA_HEADS * QK_DIM, QK_PAD), np.float32)
    pk = np.zeros((MLA_HEADS * 128, QK_PAD), np.float32)
    pv = np.zeros((MLA_HEADS * 128, MLA_WIDTH), np.float32)
    kb = np.zeros((128, QK_PAD), np.float32)
    for d in range(QK_ROPE_DIM):
        src, sign = partner(d)
        pin[o2 + src, o3 + d] = sign
        for hd in range(MLA_HEADS):
            pb[hd * QK_DIM + QK_NOPE_DIM + src, hd * HEAD_PAD + QK_NOPE_DIM + d] = sign
            kb[d, hd * HEAD_PAD + QK_NOPE_DIM + d] = 1.0
            kb[QK_ROPE_DIM + d, hd * HEAD_PAD + QK_NOPE_DIM + d] = 1.0
    for hd in range(MLA_HEADS):
        pa[hd * QK_DIM + np.arange(QK_DIM), hd * HEAD_PAD + np.arange(QK_DIM)] = 1.0
        pk[hd * 128 + np.arange(QK_NOPE_DIM), hd * HEAD_PAD + np.arange(QK_NOPE_DIM)] = 1.0
        pv[hd * 128 + QK_NOPE_DIM + np.arange(V_HEAD_DIM), hd * V_HEAD_DIM + np.arange(V_HEAD_DIM)] = 1.0
    return [jnp.asarray(a, dtype=BF16) for a in (pin, pa, pb, pk, pv, kb)]


def _mla_wprep_kernel(win_ref, wuq_ref, wukv_ref, pin_ref, pa_ref, pb_ref, pk_ref, pv_ref, kb_ref,
                      o_in, o_qa, o_qb, o_k, o_v):
    o_in[...] = _dot(win_ref[...].astype(BF16), pin_ref[...]).astype(BF16)
    wq = (wuq_ref[...] * (SOFTMAX_SCALE * math.log2(math.e))).astype(BF16)
    o_qa[...] = _dot(wq, pa_ref[...]).astype(BF16)
    o_qb[...] = _dot(wq, pb_ref[...]).astype(BF16)
    wkv = wukv_ref[...].astype(BF16)
    o_k[:KV_LORA_RANK] = _dot(wkv, pk_ref[...]).astype(BF16)
    o_k[KV_LORA_RANK:] = kb_ref[...]
    o_v[...] = _dot(wkv, pv_ref[...]).astype(BF16)


def _mla_weights(w_in, w_uq, w_ukv):
    nj = 4
    full = lambda a: pl.BlockSpec(a.shape, lambda j: (0, 0))
    cols = lambda rows, width: pl.BlockSpec((rows, width // nj), lambda j: (0, j))
    sel = _mla_selectors()
    widths = (PROJ_W, QK_PAD, QK_PAD, QK_PAD, MLA_WIDTH, QK_PAD)
    out_rows = (D_MODEL, Q_LORA_RANK, Q_LORA_RANK, 256, KV_LORA_RANK)
    return pl.pallas_call(
        _mla_wprep_kernel,
        grid=(nj,),
        in_specs=[full(w_in), full(w_uq), full(w_ukv)] + [cols(a.shape[0], w) for a, w in zip(sel, widths)],
        out_specs=[cols(r, w) for r, w in zip(out_rows, widths)],
        out_shape=[jax.ShapeDtypeStruct((r, w), BF16) for r, w in zip(out_rows, widths)],
        compiler_params=_params(("arbitrary",)),
        name="mla_weight_prep",
    )(w_in, w_uq, w_ukv, *sel)


def _lam_tiles(lam):
    lam = lam.reshape(2, 2, UNITS, UNIT_ST // LANE, LANE)
    lam = jnp.concatenate([lam[0, 0], lam[1, 0], lam[0, 1], lam[1, 1]], axis=1)
    return jnp.broadcast_to(lam[:, :, None, :], (UNITS, STATE_TILES, SUB, LANE))


def kernel(x, c, ctx, c_ctx, ada_w, ada_b, norm_g, mla_w_in, mla_q_norm, mla_w_uq, mla_kv_norm, mla_w_ukv, mla_w_out, s5_w_in, s5_a_re, s5_a_im, s5_log_step, s5_b_re, s5_b_im, s5_c_re, s5_c_im, s5_d, s5_w_glu, s5_b_glu, s5_w_out, final_g):
    cc = jnp.concatenate([c, c_ctx[None, :], jnp.zeros((7, D_MODEL), F32)], axis=0)
    mods = _modulation(cc, ada_w, ada_b)

    def mod_rows(i):
        ctx_row = jnp.broadcast_to(mods[i, 8][None, :], (BATCH, 3 * D_MODEL))
        return jnp.stack([ctx_row, mods[i, :BATCH]], axis=1)[:, :, None, :]

    mod0, mod1 = mod_rows(0), mod_rows(1)

    win, wqa, wqb, wk, wv = _mla_weights(mla_w_in[0], mla_w_uq[0], mla_w_ukv[0])
    cos, sin, kt = _rope_tables()
    q, k, v, sz = _mla_proj(ctx, x, mod0, norm_g[0][None, :], win, mla_q_norm[0][None, :],
                            mla_kv_norm[0][None, :], wqa, wqb, wk, wv, cos, sin, kt)
    o = _attention(q, k, v)
    x1, xu, sz1 = _mla_out(o, sz, ctx, x, mod0, mod1, norm_g[1][None, :], mla_w_out[0].astype(BF16),
                           s5_w_in[0].astype(BF16))

    n = 2 * S5_GROUPS
    lam, pb, cp, kk = _s5_prep(
        s5_a_re[0].reshape(n, S5_STATE), s5_a_im[0].reshape(n, S5_STATE), s5_log_step[0].reshape(n, 1),
        jnp.swapaxes(s5_b_re[0], -1, -2).reshape(n, S5_GROUP, S5_STATE),
        jnp.swapaxes(s5_b_im[0], -1, -2).reshape(n, S5_GROUP, S5_STATE),
        s5_c_re[0].reshape(n, S5_GROUP, S5_STATE), s5_c_im[0].reshape(n, S5_GROUP, S5_STATE))
    y = _s5_core(xu, kk.reshape(CH_T, 2, UNITS, UNIT_G, S5_GROUP, S5_GROUP),
                 pb.reshape(2, CH_T, 2, UNITS, UNIT_G, S5_GROUP, S5_STATE),
                 cp.reshape(2, CH_T, 2, UNITS, UNIT_G, S5_STATE, S5_GROUP),
                 s5_d[0].reshape(UNITS, UNIT_CH, 1), _lam_tiles(lam))
    return _finish(y, sz1, x1, mod1, s5_w_glu[0].astype(BF16), s5_b_glu[0][None, :], s5_w_out[0].astype(BF16),
                   final_g[None, :])
```

```python
import math

import jax
import jax.numpy as jnp
import numpy as np
from jax import lax
from jax.experimental import pallas as pl
from jax.experimental.pallas import tpu as pltpu

D_MODEL = 1024
BATCH = 8
SEQ = 2048
GRID_W = 64
CTX_LEN = 256
TOK = CTX_LEN + SEQ
EPS = 1e-6

MLA_HEADS = 16
QK_NOPE_DIM = 64
QK_ROPE_DIM = 32
V_HEAD_DIM = 64
Q_LORA_RANK = 256
KV_LORA_RANK = 128
MLA_WIDTH = MLA_HEADS * V_HEAD_DIM
QK_DIM = QK_NOPE_DIM + QK_ROPE_DIM
SOFTMAX_SCALE = QK_DIM ** -0.5
ROPE_THETA = 10000.0
HEAD_PAD = 128
QK_PAD = MLA_HEADS * HEAD_PAD
PROJ_W = 1536

S5_WIDTH = D_MODEL
S5_GROUP = 16
S5_GROUPS = 64
S5_STATE = 64
CH_T = 4
UNIT_G = 4
UNIT_CH = UNIT_G * S5_GROUP
UNITS = S5_GROUPS // UNIT_G
UNIT_K = CH_T * UNIT_CH
UNIT_ST = UNIT_G * S5_STATE
NCH = TOK // CH_T
NCH_CTX = CTX_LEN // CH_T
LANE = 128
SUB = 8

TM = 256
TF = 512
TQ = 256
KCH = 256
HPAIRS = 2
assert TQ == CTX_LEN and TM == CTX_LEN
VMEM_LIMIT = 56 * 1024 * 1024

F32 = jnp.float32
BF16 = jnp.bfloat16


def _params(sem, flags=None):
    return pltpu.CompilerParams(dimension_semantics=sem, vmem_limit_bytes=VMEM_LIMIT, flags=flags)


def _silu(v):
    return v * jax.nn.sigmoid(v)


def _rms(v, g):
    return v * lax.rsqrt(jnp.mean(v * v, axis=-1, keepdims=True) + EPS) * g


def _dot(a, b):
    return jnp.dot(a, b, preferred_element_type=F32)


def _mod_kernel(cc_ref, w_ref, b_ref, o_ref):
    a = _silu(cc_ref[...]).astype(BF16)
    o_ref[0] = _dot(a, w_ref[0].astype(BF16)) + b_ref[0]


def _modulation(cc, ada_w, ada_b):
    depth = ada_w.shape[0]
    tn = 768
    return pl.pallas_call(
        _mod_kernel,
        grid=(depth, 3 * D_MODEL // tn),
        in_specs=[
            pl.BlockSpec((16, D_MODEL), lambda i, j: (0, 0)),
            pl.BlockSpec((1, D_MODEL, tn), lambda i, j: (i, 0, j)),
            pl.BlockSpec((1, 1, tn), lambda i, j: (i, 0, j)),
        ],
        out_specs=pl.BlockSpec((1, 16, tn), lambda i, j: (i, 0, j)),
        out_shape=jax.ShapeDtypeStruct((depth, 16, 3 * D_MODEL), F32),
        compiler_params=_params(("arbitrary", "arbitrary")),
        name="modulation",
    )(cc, ada_w, ada_b.reshape(depth, 1, 3 * D_MODEL))


def _tok_specs():
    nct = CTX_LEN // TM
    return (pl.BlockSpec((1, TM, D_MODEL), lambda b, i: (b, jnp.minimum(i, nct - 1), 0)),
            pl.BlockSpec((1, TM, D_MODEL), lambda b, i: (b, jnp.maximum(i - nct, 0), 0)))


def _tok_tile(ctx_ref, x_ref):
    return jnp.where(pl.program_id(1) < CTX_LEN // TM, ctx_ref[0], x_ref[0])


def _mla_proj_kernel(ctx_ref, x_ref, mod_ref, g_ref, win_ref, qg_ref, kvg_ref, wqa_ref, wqb_ref, wk_ref, wv_ref,
                     cos_ref, sin_ref, kt_ref, q_ref, k_ref, v_ref, sz_ref):
    x = _tok_tile(ctx_ref, x_ref)
    mod = mod_ref[0, 0]
    sh = mod[:, :D_MODEL]
    sc = mod[:, D_MODEL:2 * D_MODEL]
    h = _rms(x, g_ref[...]) * (1.0 + sc) + sh
    p = _dot(h.astype(BF16), win_ref[...])
    cqn = _rms(p[:, :Q_LORA_RANK], qg_ref[...]).astype(BF16)
    ckvn = _rms(p[:, Q_LORA_RANK:Q_LORA_RANK + KV_LORA_RANK], kvg_ref[...]).astype(BF16)
    kr = p[:, 384:512]
    z = p[:, 512:]
    qa = _dot(cqn, wqa_ref[...])
    qb = _dot(cqn, wqb_ref[...])
    cos = cos_ref[...]
    sin = sin_ref[...]
    for hd in range(MLA_HEADS):
        sl = slice(hd * HEAD_PAD, (hd + 1) * HEAD_PAD)
        q_ref[0, :, sl] = (qa[:, sl] * cos + qb[:, sl] * sin).astype(BF16)
    kin = jnp.concatenate([ckvn, (kr * kt_ref[...]).astype(BF16)], axis=-1)
    k_ref[0] = _dot(kin, wk_ref[...]).astype(BF16)
    v_ref[0] = _dot(ckvn, wv_ref[...]).astype(BF16)
    sz_ref[0] = _silu(z).astype(BF16)


def _mla_proj(ctx, x, mod, g, win, qg, kvg, wqa, wqb, wk, wv, cos, sin, kt):
    nct = CTX_LEN // TM
    full = lambda shape: pl.BlockSpec(shape, lambda b, i: (0,) * len(shape))
    tok = lambda w: pl.BlockSpec((1, TM, w), lambda b, i: (b, i, 0))
    pos = pl.BlockSpec((TM, HEAD_PAD), lambda b, i: (i, 0))
    return pl.pallas_call(
        _mla_proj_kernel,
        grid=(BATCH, TOK // TM),
        in_specs=[
            *_tok_specs(),
            pl.BlockSpec((1, 1, 1, 3 * D_MODEL), lambda b, i: (b, jnp.where(i < nct, 0, 1), 0, 0)),
            full((1, D_MODEL)), full((D_MODEL, PROJ_W)), full((1, Q_LORA_RANK)), full((1, KV_LORA_RANK)),
            full((Q_LORA_RANK, QK_PAD)), full((Q_LORA_RANK, QK_PAD)), full((256, QK_PAD)),
            full((KV_LORA_RANK, MLA_WIDTH)), pos, pos, pos,
        ],
        out_specs=[tok(QK_PAD), tok(QK_PAD), tok(MLA_WIDTH), tok(MLA_WIDTH)],
        out_shape=[
            jax.ShapeDtypeStruct((BATCH, TOK, QK_PAD), BF16),
            jax.ShapeDtypeStruct((BATCH, TOK, QK_PAD), BF16),
            jax.ShapeDtypeStruct((BATCH, TOK, MLA_WIDTH), BF16),
            jax.ShapeDtypeStruct((BATCH, TOK, MLA_WIDTH), BF16),
        ],
        compiler_params=_params(("arbitrary", "arbitrary")),
        name="mla_proj",
    )(ctx, x, mod, g, win, qg, kvg, wqa, wqb, wk, wv, cos, sin, kt)


def _attn_kernel(q_ref, k_ref, v_ref, o_ref, s_buf, m_buf, vx_buf, cs_buf, cm_buf):
    nt = SEQ // TQ
    lane = lax.broadcasted_iota(jnp.int32, (TOK, 2 * V_HEAD_DIM), 1)
    for hp in range(HPAIRS):
        v = v_ref[0, :, hp * 2 * V_HEAD_DIM:(hp + 1) * 2 * V_HEAD_DIM]
        vx_buf[hp, 0] = jnp.where(lane < V_HEAD_DIM, v, (lane == V_HEAD_DIM).astype(BF16))
        vx_buf[hp, 1] = jnp.where(lane >= V_HEAD_DIM, v, (lane == 0).astype(BF16))

    def scores(hp, row, nk, slot):
        sb, mb = (cs_buf.at[hp], cm_buf.at[hp]) if slot is None else (s_buf.at[slot], m_buf.at[slot])
        for hh in range(2):
            c0 = (2 * hp + hh) * HEAD_PAD
            s = lax.dot_general(q_ref[0, pl.ds(row, TQ), c0:c0 + HEAD_PAD], k_ref[0, :nk, c0:c0 + HEAD_PAD],
                                (((1,), (1,)), ((), ())), preferred_element_type=F32)
            sb[hh, :, :nk] = s
            mb[hh] = jnp.broadcast_to(jnp.max(s, axis=-1, keepdims=True), (TQ, KCH))

    def values(hp, row, nk, slot):
        sb, mb = (cs_buf.at[hp], cm_buf.at[hp]) if slot is None else (s_buf.at[slot], m_buf.at[slot])
        outs = []
        for hh in range(2):
            m = mb[hh]
            ps = [jnp.exp2(sb[hh, :, n * KCH:(n + 1) * KCH] - m).astype(BF16) for n in range(nk // KCH)]
            acc = _dot(jnp.concatenate(ps, axis=-1), vx_buf[hp, hh, :nk, :])
            l_col = V_HEAD_DIM if hh == 0 else 0
            outs.append(acc / acc[:, l_col:l_col + 1])
        olane = lax.broadcasted_iota(jnp.int32, outs[0].shape, 1)
        o_ref[0, pl.ds(row, TQ), hp * 2 * V_HEAD_DIM:(hp + 1) * 2 * V_HEAD_DIM] = jnp.where(
            olane < V_HEAD_DIM, outs[0], outs[1]).astype(BF16)

    def lat_row(t):
        return pl.multiple_of(CTX_LEN + t * TQ, TQ)

    for hp in range(HPAIRS):
        scores(hp, 0, CTX_LEN, None)
    scores(0, lat_row(0), TOK, 0)
    for hp in range(HPAIRS):
        values(hp, 0, CTX_LEN, None)
    for hp in range(HPAIRS):
        def pair(pp, carry, hp=hp):
            t = 1 + 2 * pp
            scores(hp, lat_row(t), TOK, 1)
            values(hp, lat_row(t - 1), TOK, 0)
            scores(hp, lat_row(t + 1), TOK, 0)
            values(hp, lat_row(t), TOK, 1)
            return carry

        lax.fori_loop(0, (nt - 2) // 2, pair, 0)
        scores(hp, lat_row(nt - 1), TOK, 1)
        values(hp, lat_row(nt - 2), TOK, 0)
        if hp + 1 < HPAIRS:
            scores(hp + 1, lat_row(0), TOK, 0)
        values(hp, lat_row(nt - 1), TOK, 1)


def _attention(q, k, v):
    qk = pl.BlockSpec((1, TOK, HPAIRS * 2 * HEAD_PAD), lambda b, h: (b, 0, h))
    vo = pl.BlockSpec((1, TOK, HPAIRS * 2 * V_HEAD_DIM), lambda b, h: (b, 0, h))
    return pl.pallas_call(
        _attn_kernel,
        grid=(BATCH, MLA_HEADS // (2 * HPAIRS)),
        in_specs=[qk, qk, vo],
        out_specs=vo,
        out_shape=jax.ShapeDtypeStruct((BATCH, TOK, MLA_WIDTH), BF16),
        scratch_shapes=[
            pltpu.VMEM((2, 2, TQ, TOK), F32),
            pltpu.VMEM((2, 2, TQ, KCH), F32),
            pltpu.VMEM((HPAIRS, 2, TOK, 2 * V_HEAD_DIM), BF16),
            pltpu.VMEM((HPAIRS, 2, TQ, CTX_LEN), F32),
            pltpu.VMEM((HPAIRS, 2, TQ, KCH), F32),
        ],
        compiler_params=_params(("arbitrary", "arbitrary")),
        name="attention",
    )(q, k, v)


def _swap_halves(va, vb):
    lo = lax.broadcasted_iota(jnp.int32, va.shape, 1) < UNIT_CH
    return (jnp.where(lo, va, pltpu.roll(vb, UNIT_CH, 1)),
            jnp.where(lo, pltpu.roll(va, UNIT_CH, 1), vb))


def _mla_out_kernel(o_ref, sz_ref, ctx_ref, x_ref, mod0_ref, mod1_ref, g1_ref, wout_ref, win_ref,
                    x1_ref, xu_ref, sz1_ref, tok_scr):
    a = (o_ref[0].astype(F32) * sz_ref[0].astype(F32)).astype(BF16)
    gt = mod0_ref[0, 0][:, 2 * D_MODEL:]
    x1 = _tok_tile(ctx_ref, x_ref) + gt * _dot(a, wout_ref[...])
    x1_ref[0] = x1
    mod1 = mod1_ref[0, 0]
    h = _rms(x1, g1_ref[...]) * (1.0 + mod1[:, D_MODEL:2 * D_MODEL]) + mod1[:, :D_MODEL]
    p = _dot(h.astype(BF16), win_ref[...])
    sz1_ref[0] = _silu(p[:, S5_WIDTH:]).astype(BF16)
    for m in range(S5_WIDTH // LANE):
        tok_scr[m] = p[:, m * LANE:(m + 1) * LANE]
    for m in range(S5_WIDTH // LANE):
        v = [tok_scr[m, pl.ds(t, TM // CH_T, stride=CH_T), :] for t in range(CH_T)]
        for hf in range(CH_T // 2):
            even, odd = _swap_halves(v[2 * hf], v[2 * hf + 1])
            c0 = 2 * m * UNIT_K + hf * LANE
            xu_ref[0, :, c0:c0 + LANE] = even.astype(BF16)
            xu_ref[0, :, c0 + UNIT_K:c0 + UNIT_K + LANE] = odd.astype(BF16)


def _mla_out(o, sz, ctx, x, mod0, mod1, g1, wout, win):
    nct = CTX_LEN // TM
    full = lambda shape: pl.BlockSpec(shape, lambda b, i: (0,) * len(shape))
    tok = lambda w: pl.BlockSpec((1, TM, w), lambda b, i: (b, i, 0))
    lat = lambda w: pl.BlockSpec((1, TM, w), lambda b, i: (b, jnp.maximum(i - nct, 0), 0))
    modspec = pl.BlockSpec((1, 1, 1, 3 * D_MODEL), lambda b, i: (b, jnp.where(i < nct, 0, 1), 0, 0))
    return pl.pallas_call(
        _mla_out_kernel,
        grid=(BATCH, TOK // TM),
        in_specs=[tok(MLA_WIDTH), tok(MLA_WIDTH), *_tok_specs(), modspec, modspec,
                  full((1, D_MODEL)), full((MLA_WIDTH, D_MODEL)), full((D_MODEL, 2 * S5_WIDTH))],
        out_specs=[lat(D_MODEL),
                   pl.BlockSpec((1, TM // CH_T, UNITS * UNIT_K), lambda b, i: (b, i, 0)),
                   lat(S5_WIDTH)],
        out_shape=[
            jax.ShapeDtypeStruct((BATCH, SEQ, D_MODEL), F32),
            jax.ShapeDtypeStruct((BATCH, NCH, UNITS * UNIT_K), BF16),
            jax.ShapeDtypeStruct((BATCH, SEQ, S5_WIDTH), BF16),
        ],
        scratch_shapes=[pltpu.VMEM((S5_WIDTH // LANE, TM, LANE), F32)],
        compiler_params=_params(("arbitrary", "arbitrary")),
        name="mla_out_s5_in",
    )(o, sz, ctx, x, mod0, mod1, g1, wout, win)


def _cmul(ar, ai, br, bi):
    return ar * br - ai * bi, ar * bi + ai * br


def _group_dot(a, b, precision=lax.Precision.HIGHEST):
    return lax.dot_general(a, b, (((2,), (2,)), ((0,), (0,))), precision=precision, preferred_element_type=F32)


def _group_transpose(eye, a):
    return _group_dot(eye, a.astype(BF16), precision=None)


def _s5_prep_kernel(are_ref, aim_ref, ls_ref, bre_ref, bim_ref, cre_ref, cim_ref, lam_ref, pb_ref, cp_ref, kk_ref):
    n = are_ref.shape[0]
    eye = (lax.broadcasted_iota(jnp.int32, (n, S5_STATE, S5_STATE), 1)
           == lax.broadcasted_iota(jnp.int32, (n, S5_STATE, S5_STATE), 2)).astype(BF16)
    ar = are_ref[...]
    ai = aim_ref[...]
    dt = jnp.exp(ls_ref[...])
    mag = jnp.exp(ar * dt)
    lb_re = mag * jnp.cos(ai * dt)
    lb_im = mag * jnp.sin(ai * dt)
    den = ar * ar + ai * ai
    nr = lb_re - 1.0
    f_re = ((nr * ar + lb_im * ai) / den)[:, None, :]
    f_im = ((lb_im * ar - nr * ai) / den)[:, None, :]
    bb_re, bb_im = _cmul(f_re, f_im, bre_ref[...], bim_ref[...])
    c_re = cre_ref[...]
    c_im = cim_ref[...]
    pw_re = jnp.ones_like(lb_re)
    pw_im = jnp.zeros_like(lb_re)
    for r in range(CH_T + 1):
        pr = pw_re[:, None, :]
        pi = pw_im[:, None, :]
        cl_re, cl_im = _cmul(c_re, c_im, pr, pi)
        if r < CH_T:
            q_re, q_im = _cmul(pr, pi, bb_re, bb_im)
            pb_ref[0, r] = q_re
            pb_ref[1, r] = q_im
            kk_ref[r] = _group_dot(bb_re, cl_re) - _group_dot(bb_im, cl_im)
        if r > 0:
            cp_ref[0, r - 1] = _group_transpose(eye, cl_re)
            cp_ref[1, r - 1] = _group_transpose(eye, -cl_im)
        if r == CH_T:
            lam_ref[0] = pw_re
            lam_ref[1] = pw_im
        else:
            pw_re, pw_im = _cmul(pw_re, pw_im, lb_re, lb_im)


def _s5_prep(a_re, a_im, log_step, b_re_t, b_im_t, c_re, c_im):
    n = a_re.shape[0]
    nb = 16
    row2 = pl.BlockSpec((nb, S5_STATE), lambda i: (i, 0))
    row3 = pl.BlockSpec((nb, S5_GROUP, S5_STATE), lambda i: (i, 0, 0))
    return pl.pallas_call(
        _s5_prep_kernel,
        grid=(n // nb,),
        in_specs=[row2, row2, pl.BlockSpec((nb, 1), lambda i: (i, 0)), row3, row3, row3, row3],
        out_specs=[
            pl.BlockSpec((2, nb, S5_STATE), lambda i: (0, i, 0)),
            pl.BlockSpec((2, CH_T, nb, S5_GROUP, S5_STATE), lambda i: (0, 0, i, 0, 0)),
            pl.BlockSpec((2, CH_T, nb, S5_STATE, S5_GROUP), lambda i: (0, 0, i, 0, 0)),
            pl.BlockSpec((CH_T, nb, S5_GROUP, S5_GROUP), lambda i: (0, i, 0, 0)),
        ],
        out_shape=[
            jax.ShapeDtypeStruct((2, n, S5_STATE), F32),
            jax.ShapeDtypeStruct((2, CH_T, n, S5_GROUP, S5_STATE), F32),
            jax.ShapeDtypeStruct((2, CH_T, n, S5_STATE, S5_GROUP), F32),
            jax.ShapeDtypeStruct((CH_T, n, S5_GROUP, S5_GROUP), F32),
        ],
        compiler_params=_params(("arbitrary",)),
        name="s5_prep",
    )(a_re, a_im, log_step, b_re_t, b_im_t, c_re, c_im)


STATE_TILES = 2 * 2 * UNIT_ST // LANE


def _hdot(a, rep):
    return _dot(a.astype(BF16), rep)


def _unit_operators(kk_ref, pb_ref, cp_ref, d_ref):
    def iota(shape, dim):
        return lax.broadcasted_iota(jnp.int32, shape, dim)

    rep16 = (iota((S5_GROUP, UNIT_K), 1) % S5_GROUP == iota((S5_GROUP, UNIT_K), 0)).astype(BF16)
    rep64 = (iota((S5_STATE, UNIT_ST), 1) % S5_STATE == iota((S5_STATE, UNIT_ST), 0)).astype(BF16)
    row = iota((UNIT_CH, UNIT_K), 0)
    col = iota((UNIT_CH, UNIT_K), 1)
    same_group_out = row // S5_GROUP == (col // S5_GROUP) % UNIT_G
    same_group_st = row // S5_GROUP == col // S5_STATE
    on_diag = row == col % UNIT_CH
    col_t = col // UNIT_CH
    srow = iota((UNIT_ST, UNIT_K), 0)
    scol = iota((UNIT_ST, UNIT_K), 1)
    st_same_group = srow // S5_STATE == (scol // S5_GROUP) % UNIT_G
    st_col_t = scol // UNIT_CH
    ms, bzs, cos = [], [], []
    for d in range(2):
        kexp = [_hdot(kk_ref[k, d, 0].reshape(UNIT_CH, S5_GROUP), rep16) for k in range(CH_T)]
        rows = []
        for j in range(CH_T):
            acc = jnp.zeros((UNIT_CH, UNIT_K), F32)
            for k in range(CH_T):
                lag_ok = (col_t - j == k) if d == 0 else (j - col_t == k)
                acc = acc + jnp.where(lag_ok & same_group_out, kexp[k], 0.0)
            if d == 0:
                acc = acc + jnp.where((col_t == j) & on_diag, d_ref[0], 0.0)
            rows.append(acc)
        ms.append(jnp.concatenate(rows, axis=0))
        rows = []
        for j in range(CH_T):
            r = CH_T - 1 - j if d == 0 else j
            rows.append(jnp.concatenate(
                [jnp.where(same_group_st, _hdot(pb_ref[ri, r, d, 0].reshape(UNIT_CH, S5_STATE), rep64), 0.0)
                 for ri in range(2)], axis=-1))
        bzs.append(jnp.concatenate(rows, axis=0))
        rows = []
        for ri in range(2):
            acc = jnp.zeros((UNIT_ST, UNIT_K), F32)
            for rr in range(CH_T):
                t = rr if d == 0 else CH_T - 1 - rr
                acc = acc + jnp.where((st_col_t == t) & st_same_group,
                                      _hdot(cp_ref[ri, rr, d, 0].reshape(UNIT_ST, S5_GROUP), rep16), 0.0)
            rows.append(acc)
        cos.append(jnp.concatenate(rows, axis=0))
    return jnp.concatenate(ms + bzs, axis=-1).astype(BF16), jnp.concatenate(cos, axis=0).astype(BF16)


def _s5_core_kernel(x_ref, kk_ref, pb_ref, cp_ref, d_ref, lam_ref, y_ref, st_scr, yi_scr):
    w1, co = _unit_operators(kk_ref, pb_ref, cp_ref, d_ref)
    for b in range(BATCH):
        r = _dot(x_ref[b], w1)
        y_ref[b] = r[NCH_CTX:, :UNIT_K] + r[NCH_CTX:, UNIT_K:2 * UNIT_K]
        for lt in range(STATE_TILES):
            c0 = 2 * UNIT_K + lt * LANE
            st_scr[lt, pl.ds(b, NCH, stride=BATCH), :] = r[:, c0:c0 + LANE]
    lam = [lam_ref[0, lt] for lt in range(STATE_TILES)]

    def rows(chunk):
        return pl.ds(pl.multiple_of(chunk * BATCH, BATCH), BATCH)

    def load_z(row, base):
        return [st_scr[base + k, rows(row), :] for k in range(4)]

    def advance(state, z, row, base):
        for k in range(4):
            st_scr[base + k, rows(row), :] = state[k]
        ar0, ar1, ai0, ai1 = lam[base:base + 4]
        return [ar0 * state[0] - ai0 * state[2] + z[0], ar1 * state[1] - ai1 * state[3] + z[1],
                ar0 * state[2] + ai0 * state[0] + z[2], ar1 * state[3] + ai1 * state[1] + z[3]]

    def bwd_row(i):
        return jnp.where(i < NCH_CTX, NCH_CTX - 1 - i, NCH + NCH_CTX - 1 - i)

    def step(i, carry):
        s_f, z_f, s_b, z_b = carry
        nxt = jnp.minimum(i + 1, NCH - 1)
        z_f_next = load_z(nxt, 0)
        z_b_next = load_z(bwd_row(nxt), 4)
        return advance(s_f, z_f, i, 0), z_f_next, advance(s_b, z_b, bwd_row(i), 4), z_b_next

    zero = [jnp.zeros((BATCH, LANE), F32)] * 4
    lax.fori_loop(0, NCH, step, (zero, load_z(0, 0), zero, load_z(NCH_CTX - 1, 4)), unroll=2)
    nlat = NCH - NCH_CTX
    for rb in range(BATCH):
        sl = slice(NCH_CTX * BATCH + rb * nlat, NCH_CTX * BATCH + (rb + 1) * nlat)
        lhs = jnp.concatenate([st_scr[lt, sl, :] for lt in range(STATE_TILES)], axis=-1)
        yi = _dot(lhs.astype(BF16), co)
        for t in range(UNIT_K // LANE):
            yi_scr[t, sl, :] = yi[:, t * LANE:(t + 1) * LANE]
    for b in range(BATCH):
        y_ref[b] = y_ref[b] + jnp.concatenate(
            [yi_scr[t, pl.ds(NCH_CTX * BATCH + b, NCH - NCH_CTX, stride=BATCH), :] for t in range(UNIT_K // LANE)],
            axis=-1)


def _s5_core(xu, kk, pb, cp, d, lam):
    return pl.pallas_call(
        _s5_core_kernel,
        grid=(UNITS,),
        in_specs=[
            pl.BlockSpec((BATCH, NCH, UNIT_K), lambda q: (0, 0, q)),
            pl.BlockSpec((CH_T, 2, 1, UNIT_G, S5_GROUP, S5_GROUP), lambda q: (0, 0, q, 0, 0, 0)),
            pl.BlockSpec((2, CH_T, 2, 1, UNIT_G, S5_GROUP, S5_STATE), lambda q: (0, 0, 0, q, 0, 0, 0)),
            pl.BlockSpec((2, CH_T, 2, 1, UNIT_G, S5_STATE, S5_GROUP), lambda q: (0, 0, 0, q, 0, 0, 0)),
            pl.BlockSpec((1, UNIT_CH, 1), lambda q: (q, 0, 0)),
            pl.BlockSpec((1, STATE_TILES, SUB, LANE), lambda q: (q, 0, 0, 0)),
        ],
        out_specs=pl.BlockSpec((BATCH, NCH - NCH_CTX, UNIT_K), lambda q: (0, 0, q)),
        out_shape=jax.ShapeDtypeStruct((BATCH, NCH - NCH_CTX, UNITS * UNIT_K), F32),
        scratch_shapes=[pltpu.VMEM((STATE_TILES, BATCH * NCH, LANE), F32),
                        pltpu.VMEM((UNIT_K // LANE, BATCH * NCH, LANE), F32)],
        compiler_params=_params(("arbitrary",)),
        name="s5_core",
    )(xu, kk, pb, cp, d, lam)


def _fin_kernel(y_ref, sz_ref, x_ref, mod_ref, wglu_ref, bglu_ref, wout_ref, fg_ref, o_ref, tok_scr):
    for m in range(S5_WIDTH // LANE):
        for hf in range(CH_T // 2):
            c0 = 2 * m * UNIT_K + hf * LANE
            va, vb = _swap_halves(y_ref[0, :, c0:c0 + LANE], y_ref[0, :, c0 + UNIT_K:c0 + UNIT_K + LANE])
            tok_scr[m, pl.ds(2 * hf, TF // CH_T, stride=CH_T), :] = va
            tok_scr[m, pl.ds(2 * hf + 1, TF // CH_T, stride=CH_T), :] = vb
    y = jnp.concatenate([tok_scr[m] for m in range(S5_WIDTH // LANE)], axis=-1)
    y = jax.nn.gelu(y)
    y = y * jax.nn.sigmoid(_dot(y.astype(BF16), wglu_ref[...]) + bglu_ref[...])
    a = (y * sz_ref[0].astype(F32)).astype(BF16)
    gt = mod_ref[0, 0][:, 2 * D_MODEL:]
    x2 = x_ref[0] + gt * _dot(a, wout_ref[...])
    o_ref[0] = _rms(x2, fg_ref[...])


def _finish(y, sz1, x1, mod1, wglu, bglu, wout, fg):
    full = lambda shape: pl.BlockSpec(shape, lambda b, i: (0,) * len(shape))
    tok = pl.BlockSpec((1, TF, D_MODEL), lambda b, i: (b, i, 0))
    return pl.pallas_call(
        _fin_kernel,
        grid=(BATCH, SEQ // TF),
        in_specs=[pl.BlockSpec((1, TF // CH_T, UNITS * UNIT_K), lambda b, i: (b, i, 0)), tok, tok,
                  pl.BlockSpec((1, 1, 1, 3 * D_MODEL), lambda b, i: (b, 1, 0, 0)),
                  full((S5_WIDTH, S5_WIDTH)), full((1, S5_WIDTH)),
                  full((S5_WIDTH, D_MODEL)), full((1, D_MODEL))],
        out_specs=tok,
        out_shape=jax.ShapeDtypeStruct((BATCH, SEQ, D_MODEL), F32),
        scratch_shapes=[pltpu.VMEM((S5_WIDTH // LANE, TF, LANE), F32)],
        compiler_params=_params(("arbitrary", "arbitrary")),
        name="s5_finish",
    )(y, sz1, x1, mod1, wglu, bglu, wout, fg)


def _rope_tables():
    h = QK_ROPE_DIM // 2
    inv = 1.0 / (ROPE_THETA ** (np.arange(0, h, 2, dtype=np.float64) / h))
    pos = np.arange(SEQ)
    ang_r = (pos // GRID_W)[:, None] * inv[None, :]
    ang_c = (pos % GRID_W)[:, None] * inv[None, :]
    cos32 = np.concatenate([np.cos(ang_r)] * 2 + [np.cos(ang_c)] * 2, axis=-1)
    sin32 = np.concatenate([np.sin(ang_r)] * 2 + [np.sin(ang_c)] * 2, axis=-1)
    cos = np.zeros((TOK, HEAD_PAD), np.float32)
    sin = np.zeros((TOK, HEAD_PAD), np.float32)
    kt = np.zeros((TOK, HEAD_PAD), np.float32)
    cos[:, :QK_NOPE_DIM] = 1.0
    cos[:CTX_LEN, QK_NOPE_DIM:QK_DIM] = 1.0
    kt[:CTX_LEN, :QK_ROPE_DIM] = 1.0
    cos[CTX_LEN:, QK_NOPE_DIM:QK_DIM] = cos32
    sin[CTX_LEN:, QK_NOPE_DIM:QK_DIM] = sin32
    kt[CTX_LEN:, :QK_ROPE_DIM] = cos32
    kt[CTX_LEN:, QK_ROPE_DIM:2 * QK_ROPE_DIM] = sin32
    return jnp.asarray(cos), jnp.asarray(sin), jnp.asarray(kt)


def _mla_selectors():
    def partner(d):
        return (d + 8, -1.0) if d % 16 < 8 else (d - 8, 1.0)

    o2 = Q_LORA_RANK + KV_LORA_RANK
    o3 = o2 + QK_ROPE_DIM
    pin = np.zeros((o3 + MLA_WIDTH, PROJ_W), np.float32)
    pin[np.arange(o3), np.arange(o3)] = 1.0
    pin[o3 + np.arange(MLA_WIDTH), 512 + np.arange(MLA_WIDTH)] = 1.0
    pa = np.zeros((MLA_HEADS * QK_DIM, QK_PAD), np.float32)
    pb = np.zeros((MLA_HEADS * QK_DIM, QK_PAD), np.float32)
    pk = np.zeros((MLA_HEADS * 128, QK_PAD), np.float32)
    pv = np.zeros((MLA_HEADS * 128, MLA_WIDTH), np.float32)
    kb = np.zeros((128, QK_PAD), np.float32)
    for d in range(QK_ROPE_DIM):
        src, sign = partner(d)
        pin[o2 + src, o3 + d] = sign
        for hd in range(MLA_HEADS):
            pb[hd * QK_DIM + QK_NOPE_DIM + src, hd * HEAD_PAD + QK_NOPE_DIM + d] = sign
            kb[d, hd * HEAD_PAD + QK_NOPE_DIM + d] = 1.0
            kb[QK_ROPE_DIM + d, hd * HEAD_PAD + QK_NOPE_DIM + d] = 1.0
    for hd in range(MLA_HEADS):
        pa[hd * QK_DIM + np.arange(QK_DIM), hd * HEAD_PAD + np.arange(QK_DIM)] = 1.0
        pk[hd * 128 + np.arange(QK_NOPE_DIM), hd * HEAD_PAD + np.arange(QK_NOPE_DIM)] = 1.0
        pv[hd * 128 + QK_NOPE_DIM + np.arange(V_HEAD_DIM), hd * V_HEAD_DIM + np.arange(V_HEAD_DIM)] = 1.0
    return [jnp.asarray(a, dtype=BF16) for a in (pin, pa, pb, pk, pv, kb)]


def _mla_wprep_kernel(win_ref, wuq_ref, wukv_ref, pin_ref, pa_ref, pb_ref, pk_ref, pv_ref, kb_ref,
                      o_in, o_qa, o_qb, o_k, o_v):
    o_in[...] = _dot(win_ref[...].astype(BF16), pin_ref[...]).astype(BF16)
    wq = (wuq_ref[...] * (SOFTMAX_SCALE * math.log2(math.e))).astype(BF16)
    o_qa[...] = _dot(wq, pa_ref[...]).astype(BF16)
    o_qb[...] = _dot(wq, pb_ref[...]).astype(BF16)
    wkv = wukv_ref[...].astype(BF16)
    o_k[:KV_LORA_RANK] = _dot(wkv, pk_ref[...]).astype(BF16)
    o_k[KV_LORA_RANK:] = kb_ref[...]
    o_v[...] = _dot(wkv, pv_ref[...]).astype(BF16)


def _mla_weights(w_in, w_uq, w_ukv):
    nj = 4
    full = lambda a: pl.BlockSpec(a.shape, lambda j: (0, 0))
    cols = lambda rows, width: pl.BlockSpec((rows, width // nj), lambda j: (0, j))
    sel = _mla_selectors()
    widths = (PROJ_W, QK_PAD, QK_PAD, QK_PAD, MLA_WIDTH, QK_PAD)
    out_rows = (D_MODEL, Q_LORA_RANK, Q_LORA_RANK, 256, KV_LORA_RANK)
    return pl.pallas_call(
        _mla_wprep_kernel,
        grid=(nj,),
        in_specs=[full(w_in), full(w_uq), full(w_ukv)] + [cols(a.shape[0], w) for a, w in zip(sel, widths)],
        out_specs=[cols(r, w) for r, w in zip(out_rows, widths)],
        out_shape=[jax.ShapeDtypeStruct((r, w), BF16) for r, w in zip(out_rows, widths)],
        compiler_params=_params(("arbitrary",)),
        name="mla_weight_prep",
    )(w_in, w_uq, w_ukv, *sel)


def _lam_tiles(lam):
    lam = lam.reshape(2, 2, UNITS, UNIT_ST // LANE, LANE)
    lam = jnp.concatenate([lam[0, 0], lam[1, 0], lam[0, 1], lam[1, 1]], axis=1)
    return jnp.broadcast_to(lam[:, :, None, :], (UNITS, STATE_TILES, SUB, LANE))


def kernel(x, c, ctx, c_ctx, ada_w, ada_b, norm_g, mla_w_in, mla_q_norm, mla_w_uq, mla_kv_norm, mla_w_ukv, mla_w_out, s5_w_in, s5_a_re, s5_a_im, s5_log_step, s5_b_re, s5_b_im, s5_c_re, s5_c_im, s5_d, s5_w_glu, s5_b_glu, s5_w_out, final_g):
    cc = jnp.concatenate([c, c_ctx[None, :], jnp.zeros((7, D_MODEL), F32)], axis=0)
    mods = _modulation(cc, ada_w, ada_b)

    def mod_rows(i):
        ctx_row = jnp.broadcast_to(mods[i, 8][None, :], (BATCH, 3 * D_MODEL))
        return jnp.stack([ctx_row, mods[i, :BATCH]], axis=1)[:, :, None, :]

    mod0, mod1 = mod_rows(0), mod_rows(1)

    win, wqa, wqb, wk, wv = _mla_weights(mla_w_in[0], mla_w_uq[0], mla_w_ukv[0])
    cos, sin, kt = _rope_tables()
    q, k, v, sz = _mla_proj(ctx, x, mod0, norm_g[0][None, :], win, mla_q_norm[0][None, :],
                            mla_kv_norm[0][None, :], wqa, wqb, wk, wv, cos, sin, kt)
    o = _attention(q, k, v)
    x1, xu, sz1 = _mla_out(o, sz, ctx, x, mod0, mod1, norm_g[1][None, :], mla_w_out[0].astype(BF16),
                           s5_w_in[0].astype(BF16))

    n = 2 * S5_GROUPS
    lam, pb, cp, kk = _s5_prep(
        s5_a_re[0].reshape(n, S5_STATE), s5_a_im[0].reshape(n, S5_STATE), s5_log_step[0].reshape(n, 1),
        jnp.swapaxes(s5_b_re[0], -1, -2).reshape(n, S5_GROUP, S5_STATE),
        jnp.swapaxes(s5_b_im[0], -1, -2).reshape(n, S5_GROUP, S5_STATE),
        s5_c_re[0].reshape(n, S5_GROUP, S5_STATE), s5_c_im[0].reshape(n, S5_GROUP, S5_STATE))
    y = _s5_core(xu, kk.reshape(CH_T, 2, UNITS, UNIT_G, S5_GROUP, S5_GROUP),
                 pb.reshape(2, CH_T, 2, UNITS, UNIT_G, S5_GROUP, S5_STATE),
                 cp.reshape(2, CH_T, 2, UNITS, UNIT_G, S5_STATE, S5_GROUP),
                 s5_d[0].reshape(UNITS, UNIT_CH, 1), _lam_tiles(lam))
    return _finish(y, sz1, x1, mod1, s5_w_glu[0].astype(BF16), s5_b_glu[0][None, :], s5_w_out[0].astype(BF16),
                   final_g[None, :])
```

```python
import math

import jax
import jax.numpy as jnp
import numpy as np
from jax import lax
from jax.experimental import pallas as pl
from jax.experimental.pallas import tpu as pltpu

D_MODEL = 1024
BATCH = 8
SEQ = 2048
GRID_W = 64
CTX_LEN = 256
TOK = CTX_LEN + SEQ
EPS = 1e-6

MLA_HEADS = 16
QK_NOPE_DIM = 64
QK_ROPE_DIM = 32
V_HEAD_DIM = 64
Q_LORA_RANK = 256
KV_LORA_RANK = 128
MLA_WIDTH = MLA_HEADS * V_HEAD_DIM
QK_DIM = QK_NOPE_DIM + QK_ROPE_DIM
SOFTMAX_SCALE = QK_DIM ** -0.5
ROPE_THETA = 10000.0
HEAD_PAD = 128
QK_PAD = MLA_HEADS * HEAD_PAD
PROJ_W = 1536

S5_WIDTH = D_MODEL
S5_GROUP = 16
S5_GROUPS = 64
S5_STATE = 64
CH_T = 4
UNIT_G = 4
UNIT_CH = UNIT_G * S5_GROUP
UNITS = S5_GROUPS // UNIT_G
UNIT_K = CH_T * UNIT_CH
UNIT_ST = UNIT_G * S5_STATE
NCH = TOK // CH_T
NCH_CTX = CTX_LEN // CH_T
LANE = 128
SUB = 8

TM = 256
TF = 512
TQ = 256
KCH = 256
HPAIRS = 2
assert TQ == CTX_LEN and TM == CTX_LEN
VMEM_LIMIT = 56 * 1024 * 1024

F32 = jnp.float32
BF16 = jnp.bfloat16


def _params(sem, flags=None):
    return pltpu.CompilerParams(dimension_semantics=sem, vmem_limit_bytes=VMEM_LIMIT, flags=flags)


def _silu(v):
    return v * jax.nn.sigmoid(v)


def _rms(v, g):
    return v * lax.rsqrt(jnp.mean(v * v, axis=-1, keepdims=True) + EPS) * g


def _dot(a, b):
    return jnp.dot(a, b, preferred_element_type=F32)


def _mod_kernel(cc_ref, w_ref, b_ref, o_ref):
    a = _silu(cc_ref[...]).astype(BF16)
    o_ref[0] = _dot(a, w_ref[0].astype(BF16)) + b_ref[0]


def _modulation(cc, ada_w, ada_b):
    depth = ada_w.shape[0]
    tn = 768
    return pl.pallas_call(
        _mod_kernel,
        grid=(depth, 3 * D_MODEL // tn),
        in_specs=[
            pl.BlockSpec((16, D_MODEL), lambda i, j: (0, 0)),
            pl.BlockSpec((1, D_MODEL, tn), lambda i, j: (i, 0, j)),
            pl.BlockSpec((1, 1, tn), lambda i, j: (i, 0, j)),
        ],
        out_specs=pl.BlockSpec((1, 16, tn), lambda i, j: (i, 0, j)),
        out_shape=jax.ShapeDtypeStruct((depth, 16, 3 * D_MODEL), F32),
        compiler_params=_params(("arbitrary", "arbitrary")),
        name="modulation",
    )(cc, ada_w, ada_b.reshape(depth, 1, 3 * D_MODEL))


def _tok_specs():
    nct = CTX_LEN // TM
    return (pl.BlockSpec((1, TM, D_MODEL), lambda b, i: (b, jnp.minimum(i, nct - 1), 0)),
            pl.BlockSpec((1, TM, D_MODEL), lambda b, i: (b, jnp.maximum(i - nct, 0), 0)))


def _tok_tile(ctx_ref, x_ref):
    return jnp.where(pl.program_id(1) < CTX_LEN // TM, ctx_ref[0], x_ref[0])


def _mla_proj_kernel(ctx_ref, x_ref, mod_ref, g_ref, win_ref, qg_ref, kvg_ref, wqa_ref, wqb_ref, wk_ref, wv_ref,
                     cos_ref, sin_ref, kt_ref, q_ref, k_ref, v_ref, sz_ref):
    x = _tok_tile(ctx_ref, x_ref)
    mod = mod_ref[0, 0]
    sh = mod[:, :D_MODEL]
    sc = mod[:, D_MODEL:2 * D_MODEL]
    h = _rms(x, g_ref[...]) * (1.0 + sc) + sh
    p = _dot(h.astype(BF16), win_ref[...])
    cqn = _rms(p[:, :Q_LORA_RANK], qg_ref[...]).astype(BF16)
    ckvn = _rms(p[:, Q_LORA_RANK:Q_LORA_RANK + KV_LORA_RANK], kvg_ref[...]).astype(BF16)
    kr = p[:, 384:512]
    z = p[:, 512:]
    qa = _dot(cqn, wqa_ref[...])
    qb = _dot(cqn, wqb_ref[...])
    cos = cos_ref[...]
    sin = sin_ref[...]
    for hd in range(MLA_HEADS):
        sl = slice(hd * HEAD_PAD, (hd + 1) * HEAD_PAD)
        q_ref[0, :, sl] = (qa[:, sl] * cos + qb[:, sl] * sin).astype(BF16)
    kin = jnp.concatenate([ckvn, (kr * kt_ref[...]).astype(BF16)], axis=-1)
    k_ref[0] = _dot(kin, wk_ref[...]).astype(BF16)
    v_ref[0] = _dot(ckvn, wv_ref[...]).astype(BF16)
    sz_ref[0] = _silu(z).astype(BF16)


def _mla_proj(ctx, x, mod, g, win, qg, kvg, wqa, wqb, wk, wv, cos, sin, kt):
    nct = CTX_LEN // TM
    full = lambda shape: pl.BlockSpec(shape, lambda b, i: (0,) * len(shape))
    tok = lambda w: pl.BlockSpec((1, TM, w), lambda b, i: (b, i, 0))
    pos = pl.BlockSpec((TM, HEAD_PAD), lambda b, i: (i, 0))
    return pl.pallas_call(
        _mla_proj_kernel,
        grid=(BATCH, TOK // TM),
        in_specs=[
            *_tok_specs(),
            pl.BlockSpec((1, 1, 1, 3 * D_MODEL), lambda b, i: (b, jnp.where(i < nct, 0, 1), 0, 0)),
            full((1, D_MODEL)), full((D_MODEL, PROJ_W)), full((1, Q_LORA_RANK)), full((1, KV_LORA_RANK)),
            full((Q_LORA_RANK, QK_PAD)), full((Q_LORA_RANK, QK_PAD)), full((256, QK_PAD)),
            full((KV_LORA_RANK, MLA_WIDTH)), pos, pos, pos,
        ],
        out_specs=[tok(QK_PAD), tok(QK_PAD), tok(MLA_WIDTH), tok(MLA_WIDTH)],
        out_shape=[
            jax.ShapeDtypeStruct((BATCH, TOK, QK_PAD), BF16),
            jax.ShapeDtypeStruct((BATCH, TOK, QK_PAD), BF16),
            jax.ShapeDtypeStruct((BATCH, TOK, MLA_WIDTH), BF16),
            jax.ShapeDtypeStruct((BATCH, TOK, MLA_WIDTH), BF16),
        ],
        compiler_params=_params(("arbitrary", "arbitrary")),
        name="mla_proj",
    )(ctx, x, mod, g, win, qg, kvg, wqa, wqb, wk, wv, cos, sin, kt)


def _attn_kernel(q_ref, k_ref, v_ref, o_ref, s_buf, m_buf, vx_buf, cs_buf, cm_buf):
    nt = SEQ // TQ
    lane = lax.broadcasted_iota(jnp.int32, (TOK, 2 * V_HEAD_DIM), 1)
    for hp in range(HPAIRS):
        v = v_ref[0, :, hp * 2 * V_HEAD_DIM:(hp + 1) * 2 * V_HEAD_DIM]
        vx_buf[hp, 0] = jnp.where(lane < V_HEAD_DIM, v, (lane == V_HEAD_DIM).astype(BF16))
        vx_buf[hp, 1] = jnp.where(lane >= V_HEAD_DIM, v, (lane == 0).astype(BF16))

    def scores(hp, row, nk, slot):
        sb, mb = (cs_buf.at[hp], cm_buf.at[hp]) if slot is None else (s_buf.at[slot], m_buf.at[slot])
        for hh in range(2):
            c0 = (2 * hp + hh) * HEAD_PAD
            s = lax.dot_general(q_ref[0, pl.ds(row, TQ), c0:c0 + HEAD_PAD], k_ref[0, :nk, c0:c0 + HEAD_PAD],
                                (((1,), (1,)), ((), ())), preferred_element_type=F32)
            sb[hh, :, :nk] = s
            mb[hh] = jnp.broadcast_to(jnp.max(s, axis=-1, keepdims=True), (TQ, KCH))

    def values(hp, row, nk, slot):
        sb, mb = (cs_buf.at[hp], cm_buf.at[hp]) if slot is None else (s_buf.at[slot], m_buf.at[slot])
        outs = []
        for hh in range(2):
            m = mb[hh]
            ps = [jnp.exp2(sb[hh, :, n * KCH:(n + 1) * KCH] - m).astype(BF16) for n in range(nk // KCH)]
            acc = _dot(jnp.concatenate(ps, axis=-1), vx_buf[hp, hh, :nk, :])
            l_col = V_HEAD_DIM if hh == 0 else 0
            outs.append(acc / acc[:, l_col:l_col + 1])
        olane = lax.broadcasted_iota(jnp.int32, outs[0].shape, 1)
        o_ref[0, pl.ds(row, TQ), hp * 2 * V_HEAD_DIM:(hp + 1) * 2 * V_HEAD_DIM] = jnp.where(
            olane < V_HEAD_DIM, outs[0], outs[1]).astype(BF16)

    def lat_row(t):
        return CTX_LEN + t * TQ

    for hp in range(HPAIRS):
        scores(hp, 0, CTX_LEN, None)
    scores(0, lat_row(0), TOK, 0)
    for hp in range(HPAIRS):
        values(hp, 0, CTX_LEN, None)
    for hp in range(HPAIRS):
        for t in range(1, nt):
            scores(hp, lat_row(t), TOK, t % 2)
            values(hp, lat_row(t - 1), TOK, (t - 1) % 2)
        if hp + 1 < HPAIRS:
            scores(hp + 1, lat_row(0), TOK, 0)
        values(hp, lat_row(nt - 1), TOK, (nt - 1) % 2)


def _attention(q, k, v):
    qk = pl.BlockSpec((1, TOK, HPAIRS * 2 * HEAD_PAD), lambda b, h: (b, 0, h))
    vo = pl.BlockSpec((1, TOK, HPAIRS * 2 * V_HEAD_DIM), lambda b, h: (b, 0, h))
    return pl.pallas_call(
        _attn_kernel,
        grid=(BATCH, MLA_HEADS // (2 * HPAIRS)),
        in_specs=[qk, qk, vo],
        out_specs=vo,
        out_shape=jax.ShapeDtypeStruct((BATCH, TOK, MLA_WIDTH), BF16),
        scratch_shapes=[
            pltpu.VMEM((2, 2, TQ, TOK), F32),
            pltpu.VMEM((2, 2, TQ, KCH), F32),
            pltpu.VMEM((HPAIRS, 2, TOK, 2 * V_HEAD_DIM), BF16),
            pltpu.VMEM((HPAIRS, 2, TQ, CTX_LEN), F32),
            pltpu.VMEM((HPAIRS, 2, TQ, KCH), F32),
        ],
        compiler_params=_params(("arbitrary", "arbitrary")),
        name="attention",
    )(q, k, v)


def _swap_halves(va, vb):
    lo = lax.broadcasted_iota(jnp.int32, va.shape, 1) < UNIT_CH
    return (jnp.where(lo, va, pltpu.roll(vb, UNIT_CH, 1)),
            jnp.where(lo, pltpu.roll(va, UNIT_CH, 1), vb))


def _mla_out_kernel(o_ref, sz_ref, ctx_ref, x_ref, mod0_ref, mod1_ref, g1_ref, wout_ref, win_ref,
                    x1_ref, xu_ref, sz1_ref, tok_scr):
    a = (o_ref[0].astype(F32) * sz_ref[0].astype(F32)).astype(BF16)
    gt = mod0_ref[0, 0][:, 2 * D_MODEL:]
    x1 = _tok_tile(ctx_ref, x_ref) + gt * _dot(a, wout_ref[...])
    x1_ref[0] = x1
    mod1 = mod1_ref[0, 0]
    h = _rms(x1, g1_ref[...]) * (1.0 + mod1[:, D_MODEL:2 * D_MODEL]) + mod1[:, :D_MODEL]
    p = _dot(h.astype(BF16), win_ref[...])
    sz1_ref[0] = _silu(p[:, S5_WIDTH:]).astype(BF16)
    for m in range(S5_WIDTH // LANE):
        tok_scr[m] = p[:, m * LANE:(m + 1) * LANE]
    for m in range(S5_WIDTH // LANE):
        v = [tok_scr[m, pl.ds(t, TM // CH_T, stride=CH_T), :] for t in range(CH_T)]
        for hf in range(CH_T // 2):
            even, odd = _swap_halves(v[2 * hf], v[2 * hf + 1])
            c0 = 2 * m * UNIT_K + hf * LANE
            xu_ref[0, :, c0:c0 + LANE] = even.astype(BF16)
            xu_ref[0, :, c0 + UNIT_K:c0 + UNIT_K + LANE] = odd.astype(BF16)


def _mla_out(o, sz, ctx, x, mod0, mod1, g1, wout, win):
    nct = CTX_LEN // TM
    full = lambda shape: pl.BlockSpec(shape, lambda b, i: (0,) * len(shape))
    tok = lambda w: pl.BlockSpec((1, TM, w), lambda b, i: (b, i, 0))
    lat = lambda w: pl.BlockSpec((1, TM, w), lambda b, i: (b, jnp.maximum(i - nct, 0), 0))
    modspec = pl.BlockSpec((1, 1, 1, 3 * D_MODEL), lambda b, i: (b, jnp.where(i < nct, 0, 1), 0, 0))
    return pl.pallas_call(
        _mla_out_kernel,
        grid=(BATCH, TOK // TM),
        in_specs=[tok(MLA_WIDTH), tok(MLA_WIDTH), *_tok_specs(), modspec, modspec,
                  full((1, D_MODEL)), full((MLA_WIDTH, D_MODEL)), full((D_MODEL, 2 * S5_WIDTH))],
        out_specs=[lat(D_MODEL),
                   pl.BlockSpec((1, TM // CH_T, UNITS * UNIT_K), lambda b, i: (b, i, 0)),
                   lat(S5_WIDTH)],
        out_shape=[
            jax.ShapeDtypeStruct((BATCH, SEQ, D_MODEL), F32),
            jax.ShapeDtypeStruct((BATCH, NCH, UNITS * UNIT_K), BF16),
            jax.ShapeDtypeStruct((BATCH, SEQ, S5_WIDTH), BF16),
        ],
        scratch_shapes=[pltpu.VMEM((S5_WIDTH // LANE, TM, LANE), F32)],
        compiler_params=_params(("arbitrary", "arbitrary")),
        name="mla_out_s5_in",
    )(o, sz, ctx, x, mod0, mod1, g1, wout, win)


def _cmul(ar, ai, br, bi):
    return ar * br - ai * bi, ar * bi + ai * br


def _group_dot(a, b, precision=lax.Precision.HIGHEST):
    return lax.dot_general(a, b, (((2,), (2,)), ((0,), (0,))), precision=precision, preferred_element_type=F32)


def _group_transpose(eye, a):
    return _group_dot(eye, a.astype(BF16), precision=None)


def _s5_prep_kernel(are_ref, aim_ref, ls_ref, bre_ref, bim_ref, cre_ref, cim_ref, lam_ref, pb_ref, cp_ref, kk_ref):
    n = are_ref.shape[0]
    eye = (lax.broadcasted_iota(jnp.int32, (n, S5_STATE, S5_STATE), 1)
           == lax.broadcasted_iota(jnp.int32, (n, S5_STATE, S5_STATE), 2)).astype(BF16)
    ar = are_ref[...]
    ai = aim_ref[...]
    dt = jnp.exp(ls_ref[...])
    mag = jnp.exp(ar * dt)
    lb_re = mag * jnp.cos(ai * dt)
    lb_im = mag * jnp.sin(ai * dt)
    den = ar * ar + ai * ai
    nr = lb_re - 1.0
    f_re = ((nr * ar + lb_im * ai) / den)[:, None, :]
    f_im = ((lb_im * ar - nr * ai) / den)[:, None, :]
    bb_re, bb_im = _cmul(f_re, f_im, bre_ref[...], bim_ref[...])
    c_re = cre_ref[...]
    c_im = cim_ref[...]
    pw_re = jnp.ones_like(lb_re)
    pw_im = jnp.zeros_like(lb_re)
    for r in range(CH_T + 1):
        pr = pw_re[:, None, :]
        pi = pw_im[:, None, :]
        cl_re, cl_im = _cmul(c_re, c_im, pr, pi)
        if r < CH_T:
            q_re, q_im = _cmul(pr, pi, bb_re, bb_im)
            pb_ref[0, r] = q_re
            pb_ref[1, r] = q_im
            kk_ref[r] = _group_dot(bb_re, cl_re) - _group_dot(bb_im, cl_im)
        if r > 0:
            cp_ref[0, r - 1] = _group_transpose(eye, cl_re)
            cp_ref[1, r - 1] = _group_transpose(eye, -cl_im)
        if r == CH_T:
            lam_ref[0] = pw_re
            lam_ref[1] = pw_im
        else:
            pw_re, pw_im = _cmul(pw_re, pw_im, lb_re, lb_im)


def _s5_prep(a_re, a_im, log_step, b_re_t, b_im_t, c_re, c_im):
    n = a_re.shape[0]
    nb = 16
    row2 = pl.BlockSpec((nb, S5_STATE), lambda i: (i, 0))
    row3 = pl.BlockSpec((nb, S5_GROUP, S5_STATE), lambda i: (i, 0, 0))
    return pl.pallas_call(
        _s5_prep_kernel,
        grid=(n // nb,),
        in_specs=[row2, row2, pl.BlockSpec((nb, 1), lambda i: (i, 0)), row3, row3, row3, row3],
        out_specs=[
            pl.BlockSpec((2, nb, S5_STATE), lambda i: (0, i, 0)),
            pl.BlockSpec((2, CH_T, nb, S5_GROUP, S5_STATE), lambda i: (0, 0, i, 0, 0)),
            pl.BlockSpec((2, CH_T, nb, S5_STATE, S5_GROUP), lambda i: (0, 0, i, 0, 0)),
            pl.BlockSpec((CH_T, nb, S5_GROUP, S5_GROUP), lambda i: (0, i, 0, 0)),
        ],
        out_shape=[
            jax.ShapeDtypeStruct((2, n, S5_STATE), F32),
            jax.ShapeDtypeStruct((2, CH_T, n, S5_GROUP, S5_STATE), F32),
            jax.ShapeDtypeStruct((2, CH_T, n, S5_STATE, S5_GROUP), F32),
            jax.ShapeDtypeStruct((CH_T, n, S5_GROUP, S5_GROUP), F32),
        ],
        compiler_params=_params(("arbitrary",)),
        name="s5_prep",
    )(a_re, a_im, log_step, b_re_t, b_im_t, c_re, c_im)


STATE_TILES = 2 * 2 * UNIT_ST // LANE


def _hdot(a, rep):
    return _dot(a.astype(BF16), rep)


def _unit_operators(kk_ref, pb_ref, cp_ref, d_ref):
    def iota(shape, dim):
        return lax.broadcasted_iota(jnp.int32, shape, dim)

    rep16 = (iota((S5_GROUP, UNIT_K), 1) % S5_GROUP == iota((S5_GROUP, UNIT_K), 0)).astype(BF16)
    rep64 = (iota((S5_STATE, UNIT_ST), 1) % S5_STATE == iota((S5_STATE, UNIT_ST), 0)).astype(BF16)
    row = iota((UNIT_CH, UNIT_K), 0)
    col = iota((UNIT_CH, UNIT_K), 1)
    same_group_out = row // S5_GROUP == (col // S5_GROUP) % UNIT_G
    same_group_st = row // S5_GROUP == col // S5_STATE
    on_diag = row == col % UNIT_CH
    col_t = col // UNIT_CH
    srow = iota((UNIT_ST, UNIT_K), 0)
    scol = iota((UNIT_ST, UNIT_K), 1)
    st_same_group = srow // S5_STATE == (scol // S5_GROUP) % UNIT_G
    st_col_t = scol // UNIT_CH
    ms, bzs, cos = [], [], []
    for d in range(2):
        kexp = [_hdot(kk_ref[k, d, 0].reshape(UNIT_CH, S5_GROUP), rep16) for k in range(CH_T)]
        rows = []
        for j in range(CH_T):
            acc = jnp.zeros((UNIT_CH, UNIT_K), F32)
            for k in range(CH_T):
                lag_ok = (col_t - j == k) if d == 0 else (j - col_t == k)
                acc = acc + jnp.where(lag_ok & same_group_out, kexp[k], 0.0)
            if d == 0:
                acc = acc + jnp.where((col_t == j) & on_diag, d_ref[0], 0.0)
            rows.append(acc)
        ms.append(jnp.concatenate(rows, axis=0))
        rows = []
        for j in range(CH_T):
            r = CH_T - 1 - j if d == 0 else j
            rows.append(jnp.concatenate(
                [jnp.where(same_group_st, _hdot(pb_ref[ri, r, d, 0].reshape(UNIT_CH, S5_STATE), rep64), 0.0)
                 for ri in range(2)], axis=-1))
        bzs.append(jnp.concatenate(rows, axis=0))
        rows = []
        for ri in range(2):
            acc = jnp.zeros((UNIT_ST, UNIT_K), F32)
            for rr in range(CH_T):
                t = rr if d == 0 else CH_T - 1 - rr
                acc = acc + jnp.where((st_col_t == t) & st_same_group,
                                      _hdot(cp_ref[ri, rr, d, 0].reshape(UNIT_ST, S5_GROUP), rep16), 0.0)
            rows.append(acc)
        cos.append(jnp.concatenate(rows, axis=0))
    return jnp.concatenate(ms + bzs, axis=-1).astype(BF16), jnp.concatenate(cos, axis=0).astype(BF16)


def _s5_core_kernel(x_ref, kk_ref, pb_ref, cp_ref, d_ref, lam_ref, y_ref, st_scr, yi_scr):
    w1, co = _unit_operators(kk_ref, pb_ref, cp_ref, d_ref)
    for b in range(BATCH):
        r = _dot(x_ref[b], w1)
        y_ref[b] = r[NCH_CTX:, :UNIT_K] + r[NCH_CTX:, UNIT_K:2 * UNIT_K]
        for lt in range(STATE_TILES):
            c0 = 2 * UNIT_K + lt * LANE
            st_scr[lt, pl.ds(b, NCH, stride=BATCH), :] = r[:, c0:c0 + LANE]
    lam = [lam_ref[0, lt] for lt in range(STATE_TILES)]

    def rows(chunk):
        return pl.ds(pl.multiple_of(chunk * BATCH, BATCH), BATCH)

    def load_z(row, base):
        return [st_scr[base + k, rows(row), :] for k in range(4)]

    def cfma(a, s, z):
        ar0, ar1, ai0, ai1 = a
        return [ar0 * s[0] - ai0 * s[2] + z[0], ar1 * s[1] - ai1 * s[3] + z[1],
                ar0 * s[2] + ai0 * s[0] + z[2], ar1 * s[3] + ai1 * s[1] + z[3]]

    def enter(state, row, base):
        for k in range(4):
            st_scr[base + k, rows(row), :] = state[k]

    lam_sq = [None] * STATE_TILES
    for base in (0, 4):
        ar0, ar1, ai0, ai1 = lam[base:base + 4]
        lam_sq[base:base + 4] = [ar0 * ar0 - ai0 * ai0, ar1 * ar1 - ai1 * ai1, 2.0 * ar0 * ai0, 2.0 * ar1 * ai1]

    def bwd_row(i):
        return jnp.where(i < NCH_CTX, NCH_CTX - 1 - i, NCH + NCH_CTX - 1 - i)

    def two_chunks(state, za, zb, row_a, row_b, base):
        enter(state, row_a, base)
        enter(cfma(lam[base:base + 4], state, za), row_b, base)
        return cfma(lam_sq[base:base + 4], state, cfma(lam[base:base + 4], za, zb))

    def step(m, carry):
        s_f, za_f, zb_f, s_b, za_b, zb_b = carry
        i = 2 * m
        na = jnp.minimum(i + 2, NCH - 1)
        nb = jnp.minimum(i + 3, NCH - 1)
        nxt = (load_z(na, 0), load_z(nb, 0), load_z(bwd_row(na), 4), load_z(bwd_row(nb), 4))
        s_f = two_chunks(s_f, za_f, zb_f, i, i + 1, 0)
        s_b = two_chunks(s_b, za_b, zb_b, bwd_row(i), bwd_row(i + 1), 4)
        return s_f, nxt[0], nxt[1], s_b, nxt[2], nxt[3]

    zero = [jnp.zeros((BATCH, LANE), F32)] * 4
    lax.fori_loop(0, NCH // 2, step,
                  (zero, load_z(0, 0), load_z(1, 0), zero, load_z(bwd_row(0), 4), load_z(bwd_row(1), 4)))
    nlat = NCH - NCH_CTX
    for rb in range(BATCH):
        sl = slice(NCH_CTX * BATCH + rb * nlat, NCH_CTX * BATCH + (rb + 1) * nlat)
        lhs = jnp.concatenate([st_scr[lt, sl, :] for lt in range(STATE_TILES)], axis=-1)
        yi = _dot(lhs.astype(BF16), co)
        for t in range(UNIT_K // LANE):
            yi_scr[t, sl, :] = yi[:, t * LANE:(t + 1) * LANE]
    for b in range(BATCH):
        y_ref[b] = y_ref[b] + jnp.concatenate(
            [yi_scr[t, pl.ds(NCH_CTX * BATCH + b, NCH - NCH_CTX, stride=BATCH), :] for t in range(UNIT_K // LANE)],
            axis=-1)


def _s5_core(xu, kk, pb, cp, d, lam):
    return pl.pallas_call(
        _s5_core_kernel,
        grid=(UNITS,),
        in_specs=[
            pl.BlockSpec((BATCH, NCH, UNIT_K), lambda q: (0, 0, q)),
            pl.BlockSpec((CH_T, 2, 1, UNIT_G, S5_GROUP, S5_GROUP), lambda q: (0, 0, q, 0, 0, 0)),
            pl.BlockSpec((2, CH_T, 2, 1, UNIT_G, S5_GROUP, S5_STATE), lambda q: (0, 0, 0, q, 0, 0, 0)),
            pl.BlockSpec((2, CH_T, 2, 1, UNIT_G, S5_STATE, S5_GROUP), lambda q: (0, 0, 0, q, 0, 0, 0)),
            pl.BlockSpec((1, UNIT_CH, 1), lambda q: (q, 0, 0)),
            pl.BlockSpec((1, STATE_TILES, SUB, LANE), lambda q: (q, 0, 0, 0)),
        ],
        out_specs=pl.BlockSpec((BATCH, NCH - NCH_CTX, UNIT_K), lambda q: (0, 0, q)),
        out_shape=jax.ShapeDtypeStruct((BATCH, NCH - NCH_CTX, UNITS * UNIT_K), F32),
        scratch_shapes=[pltpu.VMEM((STATE_TILES, BATCH * NCH, LANE), F32),
                        pltpu.VMEM((UNIT_K // LANE, BATCH * NCH, LANE), F32)],
        compiler_params=_params(("arbitrary",)),
        name="s5_core",
    )(xu, kk, pb, cp, d, lam)


def _fin_kernel(y_ref, sz_ref, x_ref, mod_ref, wglu_ref, bglu_ref, wout_ref, fg_ref, o_ref, tok_scr):
    for m in range(S5_WIDTH // LANE):
        for hf in range(CH_T // 2):
            c0 = 2 * m * UNIT_K + hf * LANE
            va, vb = _swap_halves(y_ref[0, :, c0:c0 + LANE], y_ref[0, :, c0 + UNIT_K:c0 + UNIT_K + LANE])
            tok_scr[m, pl.ds(2 * hf, TF // CH_T, stride=CH_T), :] = va
            tok_scr[m, pl.ds(2 * hf + 1, TF // CH_T, stride=CH_T), :] = vb
    y = jnp.concatenate([tok_scr[m] for m in range(S5_WIDTH // LANE)], axis=-1)
    y = jax.nn.gelu(y)
    y = y * jax.nn.sigmoid(_dot(y.astype(BF16), wglu_ref[...]) + bglu_ref[...])
    a = (y * sz_ref[0].astype(F32)).astype(BF16)
    gt = mod_ref[0, 0][:, 2 * D_MODEL:]
    x2 = x_ref[0] + gt * _dot(a, wout_ref[...])
    o_ref[0] = _rms(x2, fg_ref[...])


def _finish(y, sz1, x1, mod1, wglu, bglu, wout, fg):
    full = lambda shape: pl.BlockSpec(shape, lambda b, i: (0,) * len(shape))
    tok = pl.BlockSpec((1, TF, D_MODEL), lambda b, i: (b, i, 0))
    return pl.pallas_call(
        _fin_kernel,
        grid=(BATCH, SEQ // TF),
        in_specs=[pl.BlockSpec((1, TF // CH_T, UNITS * UNIT_K), lambda b, i: (b, i, 0)), tok, tok,
                  pl.BlockSpec((1, 1, 1, 3 * D_MODEL), lambda b, i: (b, 1, 0, 0)),
                  full((S5_WIDTH, S5_WIDTH)), full((1, S5_WIDTH)),
                  full((S5_WIDTH, D_MODEL)), full((1, D_MODEL))],
        out_specs=tok,
        out_shape=jax.ShapeDtypeStruct((BATCH, SEQ, D_MODEL), F32),
        scratch_shapes=[pltpu.VMEM((S5_WIDTH // LANE, TF, LANE), F32)],
        compiler_params=_params(("arbitrary", "arbitrary")),
        name="s5_finish",
    )(y, sz1, x1, mod1, wglu, bglu, wout, fg)


def _rope_tables():
    h = QK_ROPE_DIM // 2
    inv = 1.0 / (ROPE_THETA ** (np.arange(0, h, 2, dtype=np.float64) / h))
    pos = np.arange(SEQ)
    ang_r = (pos // GRID_W)[:, None] * inv[None, :]
    ang_c = (pos % GRID_W)[:, None] * inv[None, :]
    cos32 = np.concatenate([np.cos(ang_r)] * 2 + [np.cos(ang_c)] * 2, axis=-1)
    sin32 = np.concatenate([np.sin(ang_r)] * 2 + [np.sin(ang_c)] * 2, axis=-1)
    cos = np.zeros((TOK, HEAD_PAD), np.float32)
    sin = np.zeros((TOK, HEAD_PAD), np.float32)
    kt = np.zeros((TOK, HEAD_PAD), np.float32)
    cos[:, :QK_NOPE_DIM] = 1.0
    cos[:CTX_LEN, QK_NOPE_DIM:QK_DIM] = 1.0
    kt[:CTX_LEN, :QK_ROPE_DIM] = 1.0
    cos[CTX_LEN:, QK_NOPE_DIM:QK_DIM] = cos32
    sin[CTX_LEN:, QK_NOPE_DIM:QK_DIM] = sin32
    kt[CTX_LEN:, :QK_ROPE_DIM] = cos32
    kt[CTX_LEN:, QK_ROPE_DIM:2 * QK_ROPE_DIM] = sin32
    return jnp.asarray(cos), jnp.asarray(sin), jnp.asarray(kt)


def _mla_selectors():
    def partner(d):
        return (d + 8, -1.0) if d % 16 < 8 else (d - 8, 1.0)

    o2 = Q_LORA_RANK + KV_LORA_RANK
    o3 = o2 + QK_ROPE_DIM
    pin = np.zeros((o3 + MLA_WIDTH, PROJ_W), np.float32)
    pin[np.arange(o3), np.arange(o3)] = 1.0
    pin[o3 + np.arange(MLA_WIDTH), 512 + np.arange(MLA_WIDTH)] = 1.0
    pa = np.zeros((MLA_HEADS * QK_DIM, QK_PAD), np.float32)
    pb = np.zeros((MLA_HEADS * QK_DIM, QK_PAD), np.float32)
    pk = np.zeros((MLA_HEADS * 128, QK_PAD), np.float32)
    pv = np.zeros((MLA_HEADS * 128, MLA_WIDTH), np.float32)
    kb = np.zeros((128, QK_PAD), np.float32)
    for d in range(QK_ROPE_DIM):
        src, sign = partner(d)
        pin[o2 + src, o3 + d] = sign
        for hd in range(MLA_HEADS):
            pb[hd * QK_DIM + QK_NOPE_DIM + src, hd * HEAD_PAD + QK_NOPE_DIM + d] = sign
            kb[d, hd * HEAD_PAD + QK_NOPE_DIM + d] = 1.0
            kb[QK_ROPE_DIM + d, hd * HEAD_PAD + QK_NOPE_DIM + d] = 1.0
    for hd in range(MLA_HEADS):
        pa[hd * QK_DIM + np.arange(QK_DIM), hd * HEAD_PAD + np.arange(QK_DIM)] = 1.0
        pk[hd * 128 + np.arange(QK_NOPE_DIM), hd * HEAD_PAD + np.arange(QK_NOPE_DIM)] = 1.0
        pv[hd * 128 + QK_NOPE_DIM + np.arange(V_HEAD_DIM), hd * V_HEAD_DIM + np.arange(V_HEAD_DIM)] = 1.0
    return [jnp.asarray(a, dtype=BF16) for a in (pin, pa, pb, pk, pv, kb)]


def _mla_wprep_kernel(win_ref, wuq_ref, wukv_ref, pin_ref, pa_ref, pb_ref, pk_ref, pv_ref, kb_ref,
                      o_in, o_qa, o_qb, o_k, o_v):
    o_in[...] = _dot(win_ref[...].astype(BF16), pin_ref[...]).astype(BF16)
    wq = (wuq_ref[...] * (SOFTMAX_SCALE * math.log2(math.e))).astype(BF16)
    o_qa[...] = _dot(wq, pa_ref[...]).astype(BF16)
    o_qb[...] = _dot(wq, pb_ref[...]).astype(BF16)
    wkv = wukv_ref[...].astype(BF16)
    o_k[:KV_LORA_RANK] = _dot(wkv, pk_ref[...]).astype(BF16)
    o_k[KV_LORA_RANK:] = kb_ref[...]
    o_v[...] = _dot(wkv, pv_ref[...]).astype(BF16)


def _mla_weights(w_in, w_uq, w_ukv):
    nj = 4
    full = lambda a: pl.BlockSpec(a.shape, lambda j: (0, 0))
    cols = lambda rows, width: pl.BlockSpec((rows, width // nj), lambda j: (0, j))
    sel = _mla_selectors()
    widths = (PROJ_W, QK_PAD, QK_PAD, QK_PAD, MLA_WIDTH, QK_PAD)
    out_rows = (D_MODEL, Q_LORA_RANK, Q_LORA_RANK, 256, KV_LORA_RANK)
    return pl.pallas_call(
        _mla_wprep_kernel,
        grid=(nj,),
        in_specs=[full(w_in), full(w_uq), full(w_ukv)] + [cols(a.shape[0], w) for a, w in zip(sel, widths)],
        out_specs=[cols(r, w) for r, w in zip(out_rows, widths)],
        out_shape=[jax.ShapeDtypeStruct((r, w), BF16) for r, w in zip(out_rows, widths)],
        compiler_params=_params(("arbitrary",)),
        name="mla_weight_prep",
    )(w_in, w_uq, w_ukv, *sel)


def _lam_tiles(lam):
    lam = lam.reshape(2, 2, UNITS, UNIT_ST // LANE, LANE)
    lam = jnp.concatenate([lam[0, 0], lam[1, 0], lam[0, 1], lam[1, 1]], axis=1)
    return jnp.broadcast_to(lam[:, :, None, :], (UNITS, STATE_TILES, SUB, LANE))


def kernel(x, c, ctx, c_ctx, ada_w, ada_b, norm_g, mla_w_in, mla_q_norm, mla_w_uq, mla_kv_norm, mla_w_ukv, mla_w_out, s5_w_in, s5_a_re, s5_a_im, s5_log_step, s5_b_re, s5_b_im, s5_c_re, s5_c_im, s5_d, s5_w_glu, s5_b_glu, s5_w_out, final_g):
    cc = jnp.concatenate([c, c_ctx[None, :], jnp.zeros((7, D_MODEL), F32)], axis=0)
    mods = _modulation(cc, ada_w, ada_b)

    def mod_rows(i):
        ctx_row = jnp.broadcast_to(mods[i, 8][None, :], (BATCH, 3 * D_MODEL))
        return jnp.stack([ctx_row, mods[i, :BATCH]], axis=1)[:, :, None, :]

    mod0, mod1 = mod_rows(0), mod_rows(1)

    win, wqa, wqb, wk, wv = _mla_weights(mla_w_in[0], mla_w_uq[0], mla_w_ukv[0])
    cos, sin, kt = _rope_tables()
    q, k, v, sz = _mla_proj(ctx, x, mod0, norm_g[0][None, :], win, mla_q_norm[0][None, :],
                            mla_kv_norm[0][None, :], wqa, wqb, wk, wv, cos, sin, kt)
    o = _attention(q, k, v)
    x1, xu, sz1 = _mla_out(o, sz, ctx, x, mod0, mod1, norm_g[1][None, :], mla_w_out[0].astype(BF16),
                           s5_w_in[0].astype(BF16))

    n = 2 * S5_GROUPS
    lam, pb, cp, kk = _s5_prep(
        s5_a_re[0].reshape(n, S5_STATE), s5_a_im[0].reshape(n, S5_STATE), s5_log_step[0].reshape(n, 1),
        jnp.swapaxes(s5_b_re[0], -1, -2).reshape(n, S5_GROUP, S5_STATE),
        jnp.swapaxes(s5_b_im[0], -1, -2).reshape(n, S5_GROUP, S5_STATE),
        s5_c_re[0].reshape(n, S5_GROUP, S5_STATE), s5_c_im[0].reshape(n, S5_GROUP, S5_STATE))
    y = _s5_core(xu, kk.reshape(CH_T, 2, UNITS, UNIT_G, S5_GROUP, S5_GROUP),
                 pb.reshape(2, CH_T, 2, UNITS, UNIT_G, S5_GROUP, S5_STATE),
                 cp.reshape(2, CH_T, 2, UNITS, UNIT_G, S5_STATE, S5_GROUP),
                 s5_d[0].reshape(UNITS, UNIT_CH, 1), _lam_tiles(lam))
    return _finish(y, sz1, x1, mod1, s5_w_glu[0].astype(BF16), s5_b_glu[0][None, :], s5_w_out[0].astype(BF16),
                   final_g[None, :])
```

```python
import math

import jax
import jax.numpy as jnp
import numpy as np
from jax import lax
from jax.experimental import pallas as pl
from jax.experimental.pallas import tpu as pltpu

D_MODEL = 1024
BATCH = 8
SEQ = 2048
GRID_W = 64
CTX_LEN = 256
TOK = CTX_LEN + SEQ
EPS = 1e-6

MLA_HEADS = 16
QK_NOPE_DIM = 64
QK_ROPE_DIM = 32
V_HEAD_DIM = 64
Q_LORA_RANK = 256
KV_LORA_RANK = 128
MLA_WIDTH = MLA_HEADS * V_HEAD_DIM
QK_DIM = QK_NOPE_DIM + QK_ROPE_DIM
SOFTMAX_SCALE = QK_DIM ** -0.5
ROPE_THETA = 10000.0
HEAD_PAD = 128
QK_PAD = MLA_HEADS * HEAD_PAD
PROJ_W = 1536

S5_WIDTH = D_MODEL
S5_GROUP = 16
S5_GROUPS = 64
S5_STATE = 64
CH_T = 4
UNIT_G = 4
UNIT_CH = UNIT_G * S5_GROUP
UNITS = S5_GROUPS // UNIT_G
UNIT_K = CH_T * UNIT_CH
UNIT_ST = UNIT_G * S5_STATE
NCH = TOK // CH_T
NCH_CTX = CTX_LEN // CH_T
LANE = 128
SUB = 8

TM = 256
TF = 512
TQ = 256
KCH = 256
HPAIRS = 2
assert TQ == CTX_LEN and TM == CTX_LEN
VMEM_LIMIT = 56 * 1024 * 1024

F32 = jnp.float32
BF16 = jnp.bfloat16


def _params(sem, flags=None):
    return pltpu.CompilerParams(dimension_semantics=sem, vmem_limit_bytes=VMEM_LIMIT, flags=flags)


def _silu(v):
    return v * jax.nn.sigmoid(v)


def _rms(v, g):
    return v * lax.rsqrt(jnp.mean(v * v, axis=-1, keepdims=True) + EPS) * g


def _dot(a, b):
    return jnp.dot(a, b, preferred_element_type=F32)


def _mod_kernel(cc_ref, w_ref, b_ref, o_ref):
    a = _silu(cc_ref[...]).astype(BF16)
    o_ref[0] = _dot(a, w_ref[0].astype(BF16)) + b_ref[0]


def _modulation(cc, ada_w, ada_b):
    depth = ada_w.shape[0]
    tn = 768
    return pl.pallas_call(
        _mod_kernel,
        grid=(depth, 3 * D_MODEL // tn),
        in_specs=[
            pl.BlockSpec((16, D_MODEL), lambda i, j: (0, 0)),
            pl.BlockSpec((1, D_MODEL, tn), lambda i, j: (i, 0, j)),
            pl.BlockSpec((1, 1, tn), lambda i, j: (i, 0, j)),
        ],
        out_specs=pl.BlockSpec((1, 16, tn), lambda i, j: (i, 0, j)),
        out_shape=jax.ShapeDtypeStruct((depth, 16, 3 * D_MODEL), F32),
        compiler_params=_params(("arbitrary", "arbitrary")),
        name="modulation",
    )(cc, ada_w, ada_b.reshape(depth, 1, 3 * D_MODEL))


def _tok_specs():
    nct = CTX_LEN // TM
    return (pl.BlockSpec((1, TM, D_MODEL), lambda b, i: (b, jnp.minimum(i, nct - 1), 0)),
            pl.BlockSpec((1, TM, D_MODEL), lambda b, i: (b, jnp.maximum(i - nct, 0), 0)))


def _tok_tile(ctx_ref, x_ref):
    return jnp.where(pl.program_id(1) < CTX_LEN // TM, ctx_ref[0], x_ref[0])


def _mla_proj_kernel(ctx_ref, x_ref, mod_ref, g_ref, win_ref, qg_ref, kvg_ref, wqa_ref, wqb_ref, wk_ref, wv_ref,
                     cos_ref, sin_ref, kt_ref, q_ref, k_ref, v_ref, sz_ref):
    x = _tok_tile(ctx_ref, x_ref)
    mod = mod_ref[0, 0]
    sh = mod[:, :D_MODEL]
    sc = mod[:, D_MODEL:2 * D_MODEL]
    h = _rms(x, g_ref[...]) * (1.0 + sc) + sh
    p = _dot(h.astype(BF16), win_ref[...])
    cqn = _rms(p[:, :Q_LORA_RANK], qg_ref[...]).astype(BF16)
    ckvn = _rms(p[:, Q_LORA_RANK:Q_LORA_RANK + KV_LORA_RANK], kvg_ref[...]).astype(BF16)
    kr = p[:, 384:512]
    z = p[:, 512:]
    qa = _dot(cqn, wqa_ref[...])
    qb = _dot(cqn, wqb_ref[...])
    cos = cos_ref[...]
    sin = sin_ref[...]
    for hd in range(MLA_HEADS):
        sl = slice(hd * HEAD_PAD, (hd + 1) * HEAD_PAD)
        q_ref[0, :, sl] = (qa[:, sl] * cos + qb[:, sl] * sin).astype(BF16)
    kin = jnp.concatenate([ckvn, (kr * kt_ref[...]).astype(BF16)], axis=-1)
    k_ref[0] = _dot(kin, wk_ref[...]).astype(BF16)
    v_ref[0] = _dot(ckvn, wv_ref[...]).astype(BF16)
    sz_ref[0] = _silu(z).astype(BF16)


def _mla_proj(ctx, x, mod, g, win, qg, kvg, wqa, wqb, wk, wv, cos, sin, kt):
    nct = CTX_LEN // TM
    full = lambda shape: pl.BlockSpec(shape, lambda b, i: (0,) * len(shape))
    tok = lambda w: pl.BlockSpec((1, TM, w), lambda b, i: (b, i, 0))
    pos = pl.BlockSpec((TM, HEAD_PAD), lambda b, i: (i, 0))
    return pl.pallas_call(
        _mla_proj_kernel,
        grid=(BATCH, TOK // TM),
        in_specs=[
            *_tok_specs(),
            pl.BlockSpec((1, 1, 1, 3 * D_MODEL), lambda b, i: (b, jnp.where(i < nct, 0, 1), 0, 0)),
            full((1, D_MODEL)), full((D_MODEL, PROJ_W)), full((1, Q_LORA_RANK)), full((1, KV_LORA_RANK)),
            full((Q_LORA_RANK, QK_PAD)), full((Q_LORA_RANK, QK_PAD)), full((256, QK_PAD)),
            full((KV_LORA_RANK, MLA_WIDTH)), pos, pos, pos,
        ],
        out_specs=[tok(QK_PAD), tok(QK_PAD), tok(MLA_WIDTH), tok(MLA_WIDTH)],
        out_shape=[
            jax.ShapeDtypeStruct((BATCH, TOK, QK_PAD), BF16),
            jax.ShapeDtypeStruct((BATCH, TOK, QK_PAD), BF16),
            jax.ShapeDtypeStruct((BATCH, TOK, MLA_WIDTH), BF16),
            jax.ShapeDtypeStruct((BATCH, TOK, MLA_WIDTH), BF16),
        ],
        compiler_params=_params(("arbitrary", "arbitrary")),
        name="mla_proj",
    )(ctx, x, mod, g, win, qg, kvg, wqa, wqb, wk, wv, cos, sin, kt)


def _attn_kernel(q_ref, k_ref, v_ref, o_ref, s_buf, m_buf, vx_buf, cs_buf, cm_buf):
    nt = SEQ // TQ
    lane = lax.broadcasted_iota(jnp.int32, (TOK, 2 * V_HEAD_DIM), 1)
    for hp in range(HPAIRS):
        v = v_ref[0, :, hp * 2 * V_HEAD_DIM:(hp + 1) * 2 * V_HEAD_DIM]
        vx_buf[hp, 0] = jnp.where(lane < V_HEAD_DIM, v, (lane == V_HEAD_DIM).astype(BF16))
        vx_buf[hp, 1] = jnp.where(lane >= V_HEAD_DIM, v, (lane == 0).astype(BF16))

    def scores(hp, row, nk, slot):
        sb, mb = (cs_buf.at[hp], cm_buf.at[hp]) if slot is None else (s_buf.at[slot], m_buf.at[slot])
        for hh in range(2):
            c0 = (2 * hp + hh) * HEAD_PAD
            s = lax.dot_general(q_ref[0, pl.ds(row, TQ), c0:c0 + HEAD_PAD], k_ref[0, :nk, c0:c0 + HEAD_PAD],
                                (((1,), (1,)), ((), ())), preferred_element_type=F32)
            sb[hh, :, :nk] = s
            mb[hh] = jnp.broadcast_to(jnp.max(s, axis=-1, keepdims=True), (TQ, KCH))

    def values(hp, row, nk, slot):
        sb, mb = (cs_buf.at[hp], cm_buf.at[hp]) if slot is None else (s_buf.at[slot], m_buf.at[slot])
        outs = []
        for hh in range(2):
            m = mb[hh]
            ps = [jnp.exp2(sb[hh, :, n * KCH:(n + 1) * KCH] - m).astype(BF16) for n in range(nk // KCH)]
            acc = _dot(jnp.concatenate(ps, axis=-1), vx_buf[hp, hh, :nk, :])
            l_col = V_HEAD_DIM if hh == 0 else 0
            outs.append(acc / acc[:, l_col:l_col + 1])
        olane = lax.broadcasted_iota(jnp.int32, outs[0].shape, 1)
        o_ref[0, pl.ds(row, TQ), hp * 2 * V_HEAD_DIM:(hp + 1) * 2 * V_HEAD_DIM] = jnp.where(
            olane < V_HEAD_DIM, outs[0], outs[1]).astype(BF16)

    def lat_row(t):
        return CTX_LEN + t * TQ

    for hp in range(HPAIRS):
        scores(hp, 0, CTX_LEN, None)
    scores(0, lat_row(0), TOK, 0)
    for hp in range(HPAIRS):
        values(hp, 0, CTX_LEN, None)
    for hp in range(HPAIRS):
        for t in range(1, nt):
            scores(hp, lat_row(t), TOK, t % 2)
            values(hp, lat_row(t - 1), TOK, (t - 1) % 2)
        if hp + 1 < HPAIRS:
            scores(hp + 1, lat_row(0), TOK, 0)
        values(hp, lat_row(nt - 1), TOK, (nt - 1) % 2)


def _attention(q, k, v):
    qk = pl.BlockSpec((1, TOK, HPAIRS * 2 * HEAD_PAD), lambda b, h: (b, 0, h))
    vo = pl.BlockSpec((1, TOK, HPAIRS * 2 * V_HEAD_DIM), lambda b, h: (b, 0, h))
    return pl.pallas_call(
        _attn_kernel,
        grid=(BATCH, MLA_HEADS // (2 * HPAIRS)),
        in_specs=[qk, qk, vo],
        out_specs=vo,
        out_shape=jax.ShapeDtypeStruct((BATCH, TOK, MLA_WIDTH), BF16),
        scratch_shapes=[
            pltpu.VMEM((2, 2, TQ, TOK), F32),
            pltpu.VMEM((2, 2, TQ, KCH), F32),
            pltpu.VMEM((HPAIRS, 2, TOK, 2 * V_HEAD_DIM), BF16),
            pltpu.VMEM((HPAIRS, 2, TQ, CTX_LEN), F32),
            pltpu.VMEM((HPAIRS, 2, TQ, KCH), F32),
        ],
        compiler_params=_params(("arbitrary", "arbitrary")),
        name="attention",
    )(q, k, v)


def _swap_halves(va, vb):
    lo = lax.broadcasted_iota(jnp.int32, va.shape, 1) < UNIT_CH
    return (jnp.where(lo, va, pltpu.roll(vb, UNIT_CH, 1)),
            jnp.where(lo, pltpu.roll(va, UNIT_CH, 1), vb))


def _mla_out_kernel(o_ref, sz_ref, ctx_ref, x_ref, mod0_ref, mod1_ref, g1_ref, wout_ref, win_ref,
                    x1_ref, xu_ref, sz1_ref, tok_scr):
    a = (o_ref[0].astype(F32) * sz_ref[0].astype(F32)).astype(BF16)
    gt = mod0_ref[0, 0][:, 2 * D_MODEL:]
    x1 = _tok_tile(ctx_ref, x_ref) + gt * _dot(a, wout_ref[...])
    x1_ref[0] = x1
    mod1 = mod1_ref[0, 0]
    h = _rms(x1, g1_ref[...]) * (1.0 + mod1[:, D_MODEL:2 * D_MODEL]) + mod1[:, :D_MODEL]
    p = _dot(h.astype(BF16), win_ref[...])
    sz1_ref[0] = _silu(p[:, S5_WIDTH:]).astype(BF16)
    for m in range(S5_WIDTH // LANE):
        tok_scr[m] = p[:, m * LANE:(m + 1) * LANE]
    for m in range(S5_WIDTH // LANE):
        v = [tok_scr[m, pl.ds(t, TM // CH_T, stride=CH_T), :] for t in range(CH_T)]
        for hf in range(CH_T // 2):
            even, odd = _swap_halves(v[2 * hf], v[2 * hf + 1])
            c0 = 2 * m * UNIT_K + hf * LANE
            xu_ref[0, :, c0:c0 + LANE] = even.astype(BF16)
            xu_ref[0, :, c0 + UNIT_K:c0 + UNIT_K + LANE] = odd.astype(BF16)


def _mla_out(o, sz, ctx, x, mod0, mod1, g1, wout, win):
    nct = CTX_LEN // TM
    full = lambda shape: pl.BlockSpec(shape, lambda b, i: (0,) * len(shape))
    tok = lambda w: pl.BlockSpec((1, TM, w), lambda b, i: (b, i, 0))
    lat = lambda w: pl.BlockSpec((1, TM, w), lambda b, i: (b, jnp.maximum(i - nct, 0), 0))
    modspec = pl.BlockSpec((1, 1, 1, 3 * D_MODEL), lambda b, i: (b, jnp.where(i < nct, 0, 1), 0, 0))
    return pl.pallas_call(
        _mla_out_kernel,
        grid=(BATCH, TOK // TM),
        in_specs=[tok(MLA_WIDTH), tok(MLA_WIDTH), *_tok_specs(), modspec, modspec,
                  full((1, D_MODEL)), full((MLA_WIDTH, D_MODEL)), full((D_MODEL, 2 * S5_WIDTH))],
        out_specs=[lat(D_MODEL),
                   pl.BlockSpec((1, TM // CH_T, UNITS * UNIT_K), lambda b, i: (b, i, 0)),
                   lat(S5_WIDTH)],
        out_shape=[
            jax.ShapeDtypeStruct((BATCH, SEQ, D_MODEL), F32),
            jax.ShapeDtypeStruct((BATCH, NCH, UNITS * UNIT_K), BF16),
            jax.ShapeDtypeStruct((BATCH, SEQ, S5_WIDTH), BF16),
        ],
        scratch_shapes=[pltpu.VMEM((S5_WIDTH // LANE, TM, LANE), F32)],
        compiler_params=_params(("arbitrary", "arbitrary")),
        name="mla_out_s5_in",
    )(o, sz, ctx, x, mod0, mod1, g1, wout, win)


def _cmul(ar, ai, br, bi):
    return ar * br - ai * bi, ar * bi + ai * br


def _group_dot(a, b, precision=lax.Precision.HIGHEST):
    return lax.dot_general(a, b, (((2,), (2,)), ((0,), (0,))), precision=precision, preferred_element_type=F32)


def _group_transpose(eye, a):
    return _group_dot(eye, a.astype(BF16), precision=None)


def _s5_prep_kernel(are_ref, aim_ref, ls_ref, bre_ref, bim_ref, cre_ref, cim_ref, lam_ref, pb_ref, cp_ref, kk_ref):
    n = are_ref.shape[0]
    eye = (lax.broadcasted_iota(jnp.int32, (n, S5_STATE, S5_STATE), 1)
           == lax.broadcasted_iota(jnp.int32, (n, S5_STATE, S5_STATE), 2)).astype(BF16)
    ar = are_ref[...]
    ai = aim_ref[...]
    dt = jnp.exp(ls_ref[...])
    mag = jnp.exp(ar * dt)
    lb_re = mag * jnp.cos(ai * dt)
    lb_im = mag * jnp.sin(ai * dt)
    den = ar * ar + ai * ai
    nr = lb_re - 1.0
    f_re = ((nr * ar + lb_im * ai) / den)[:, None, :]
    f_im = ((lb_im * ar - nr * ai) / den)[:, None, :]
    bb_re, bb_im = _cmul(f_re, f_im, bre_ref[...], bim_ref[...])
    c_re = cre_ref[...]
    c_im = cim_ref[...]
    pw_re = jnp.ones_like(lb_re)
    pw_im = jnp.zeros_like(lb_re)
    for r in range(CH_T + 1):
        pr = pw_re[:, None, :]
        pi = pw_im[:, None, :]
        cl_re, cl_im = _cmul(c_re, c_im, pr, pi)
        if r < CH_T:
            q_re, q_im = _cmul(pr, pi, bb_re, bb_im)
            pb_ref[0, r] = q_re
            pb_ref[1, r] = q_im
            kk_ref[r] = _group_dot(bb_re, cl_re) - _group_dot(bb_im, cl_im)
        if r > 0:
            cp_ref[0, r - 1] = _group_transpose(eye, cl_re)
            cp_ref[1, r - 1] = _group_transpose(eye, -cl_im)
        if r == CH_T:
            lam_ref[0] = pw_re
            lam_ref[1] = pw_im
        else:
            pw_re, pw_im = _cmul(pw_re, pw_im, lb_re, lb_im)


def _s5_prep(a_re, a_im, log_step, b_re_t, b_im_t, c_re, c_im):
    n = a_re.shape[0]
    nb = 16
    row2 = pl.BlockSpec((nb, S5_STATE), lambda i: (i, 0))
    row3 = pl.BlockSpec((nb, S5_GROUP, S5_STATE), lambda i: (i, 0, 0))
    return pl.pallas_call(
        _s5_prep_kernel,
        grid=(n // nb,),
        in_specs=[row2, row2, pl.BlockSpec((nb, 1), lambda i: (i, 0)), row3, row3, row3, row3],
        out_specs=[
            pl.BlockSpec((2, nb, S5_STATE), lambda i: (0, i, 0)),
            pl.BlockSpec((2, CH_T, nb, S5_GROUP, S5_STATE), lambda i: (0, 0, i, 0, 0)),
            pl.BlockSpec((2, CH_T, nb, S5_STATE, S5_GROUP), lambda i: (0, 0, i, 0, 0)),
            pl.BlockSpec((CH_T, nb, S5_GROUP, S5_GROUP), lambda i: (0, i, 0, 0)),
        ],
        out_shape=[
            jax.ShapeDtypeStruct((2, n, S5_STATE), F32),
            jax.ShapeDtypeStruct((2, CH_T, n, S5_GROUP, S5_STATE), F32),
            jax.ShapeDtypeStruct((2, CH_T, n, S5_STATE, S5_GROUP), F32),
            jax.ShapeDtypeStruct((CH_T, n, S5_GROUP, S5_GROUP), F32),
        ],
        compiler_params=_params(("arbitrary",)),
        name="s5_prep",
    )(a_re, a_im, log_step, b_re_t, b_im_t, c_re, c_im)


STATE_TILES = 2 * 2 * UNIT_ST // LANE


def _hdot(a, rep):
    return _dot(a.astype(BF16), rep)


def _unit_operators(kk_ref, pb_ref, cp_ref, d_ref):
    def iota(shape, dim):
        return lax.broadcasted_iota(jnp.int32, shape, dim)

    rep16 = (iota((S5_GROUP, UNIT_K), 1) % S5_GROUP == iota((S5_GROUP, UNIT_K), 0)).astype(BF16)
    rep64 = (iota((S5_STATE, UNIT_ST), 1) % S5_STATE == iota((S5_STATE, UNIT_ST), 0)).astype(BF16)
    row = iota((UNIT_CH, UNIT_K), 0)
    col = iota((UNIT_CH, UNIT_K), 1)
    same_group_out = row // S5_GROUP == (col // S5_GROUP) % UNIT_G
    same_group_st = row // S5_GROUP == col // S5_STATE
    on_diag = row == col % UNIT_CH
    col_t = col // UNIT_CH
    srow = iota((UNIT_ST, UNIT_K), 0)
    scol = iota((UNIT_ST, UNIT_K), 1)
    st_same_group = srow // S5_STATE == (scol // S5_GROUP) % UNIT_G
    st_col_t = scol // UNIT_CH
    kexp = _hdot(jnp.concatenate([kk_ref[k, d, 0].reshape(UNIT_CH, S5_GROUP)
                                  for d in range(2) for k in range(CH_T)], axis=0), rep16)
    pexp = _hdot(jnp.concatenate([pb_ref[ri, r, d, 0].reshape(UNIT_CH, S5_STATE)
                                  for d in range(2) for ri in range(2) for r in range(CH_T)], axis=0), rep64)
    cexp = _hdot(jnp.concatenate([cp_ref[ri, rr, d, 0].reshape(UNIT_ST, S5_GROUP)
                                  for d in range(2) for ri in range(2) for rr in range(CH_T)], axis=0), rep16)

    def blk(a, idx, nrows):
        return a[idx * nrows:(idx + 1) * nrows]

    rows = []
    for j in range(CH_T):
        acc = jnp.where((col_t == j) & on_diag, d_ref[0], 0.0)
        for d in range(2):
            for k in range(CH_T):
                lag_ok = (col_t - j == k) if d == 0 else (j - col_t == k)
                acc = acc + jnp.where(lag_ok & same_group_out, blk(kexp, d * CH_T + k, UNIT_CH), 0.0)
        rows.append(acc)
    parts = [jnp.concatenate(rows, axis=0)]
    cos = []
    for d in range(2):
        rows = []
        for j in range(CH_T):
            r = CH_T - 1 - j if d == 0 else j
            rows.append(jnp.concatenate(
                [jnp.where(same_group_st, blk(pexp, (d * 2 + ri) * CH_T + r, UNIT_CH), 0.0) for ri in range(2)],
                axis=-1))
        parts.append(jnp.concatenate(rows, axis=0))
        for ri in range(2):
            acc = jnp.zeros((UNIT_ST, UNIT_K), F32)
            for rr in range(CH_T):
                t = rr if d == 0 else CH_T - 1 - rr
                acc = acc + jnp.where((st_col_t == t) & st_same_group,
                                      blk(cexp, (d * 2 + ri) * CH_T + rr, UNIT_ST), 0.0)
            cos.append(acc)
    return jnp.concatenate(parts, axis=-1).astype(BF16), jnp.concatenate(cos, axis=0).astype(BF16)


def _s5_core_kernel(x_ref, kk_ref, pb_ref, cp_ref, d_ref, lam_ref, y_ref, st_scr, yi_scr):
    w1, co = _unit_operators(kk_ref, pb_ref, cp_ref, d_ref)
    for b in range(BATCH):
        r = _dot(x_ref[b], w1)
        y_ref[b] = r[NCH_CTX:, :UNIT_K]
        for lt in range(STATE_TILES):
            c0 = UNIT_K + lt * LANE
            st_scr[lt, pl.ds(b, NCH, stride=BATCH), :] = r[:, c0:c0 + LANE]
    lam = [lam_ref[0, lt] for lt in range(STATE_TILES)]

    def rows(chunk):
        return pl.ds(pl.multiple_of(chunk * BATCH, BATCH), BATCH)

    def load_z(row, base):
        return [st_scr[base + k, rows(row), :] for k in range(4)]

    def advance(state, z, row, base):
        for k in range(4):
            st_scr[base + k, rows(row), :] = state[k]
        ar0, ar1, ai0, ai1 = lam[base:base + 4]
        return [ar0 * state[0] - ai0 * state[2] + z[0], ar1 * state[1] - ai1 * state[3] + z[1],
                ar0 * state[2] + ai0 * state[0] + z[2], ar1 * state[3] + ai1 * state[1] + z[3]]

    def bwd_row(i):
        return jnp.where(i < NCH_CTX, NCH_CTX - 1 - i, NCH + NCH_CTX - 1 - i)

    def step(i, carry):
        s_f, z_f, s_b, z_b = carry
        nxt = jnp.minimum(i + 1, NCH - 1)
        z_f_next = load_z(nxt, 0)
        z_b_next = load_z(bwd_row(nxt), 4)
        return advance(s_f, z_f, i, 0), z_f_next, advance(s_b, z_b, bwd_row(i), 4), z_b_next

    zero = [jnp.zeros((BATCH, LANE), F32)] * 4
    lax.fori_loop(0, NCH, step, (zero, load_z(0, 0), zero, load_z(NCH_CTX - 1, 4)), unroll=2)
    nlat = NCH - NCH_CTX
    for rb in range(BATCH):
        sl = slice(NCH_CTX * BATCH + rb * nlat, NCH_CTX * BATCH + (rb + 1) * nlat)
        lhs = jnp.concatenate([st_scr[lt, sl, :] for lt in range(STATE_TILES)], axis=-1)
        yi = _dot(lhs.astype(BF16), co)
        for t in range(UNIT_K // LANE):
            yi_scr[t, sl, :] = yi[:, t * LANE:(t + 1) * LANE]
    for b in range(BATCH):
        y_ref[b] = y_ref[b] + jnp.concatenate(
            [yi_scr[t, pl.ds(NCH_CTX * BATCH + b, NCH - NCH_CTX, stride=BATCH), :] for t in range(UNIT_K // LANE)],
            axis=-1)


def _s5_core(xu, kk, pb, cp, d, lam):
    return pl.pallas_call(
        _s5_core_kernel,
        grid=(UNITS,),
        in_specs=[
            pl.BlockSpec((BATCH, NCH, UNIT_K), lambda q: (0, 0, q)),
            pl.BlockSpec((CH_T, 2, 1, UNIT_G, S5_GROUP, S5_GROUP), lambda q: (0, 0, q, 0, 0, 0)),
            pl.BlockSpec((2, CH_T, 2, 1, UNIT_G, S5_GROUP, S5_STATE), lambda q: (0, 0, 0, q, 0, 0, 0)),
            pl.BlockSpec((2, CH_T, 2, 1, UNIT_G, S5_STATE, S5_GROUP), lambda q: (0, 0, 0, q, 0, 0, 0)),
            pl.BlockSpec((1, UNIT_CH, 1), lambda q: (q, 0, 0)),
            pl.BlockSpec((1, STATE_TILES, SUB, LANE), lambda q: (q, 0, 0, 0)),
        ],
        out_specs=pl.BlockSpec((BATCH, NCH - NCH_CTX, UNIT_K), lambda q: (0, 0, q)),
        out_shape=jax.ShapeDtypeStruct((BATCH, NCH - NCH_CTX, UNITS * UNIT_K), F32),
        scratch_shapes=[pltpu.VMEM((STATE_TILES, BATCH * NCH, LANE), F32),
                        pltpu.VMEM((UNIT_K // LANE, BATCH * NCH, LANE), F32)],
        compiler_params=_params(("arbitrary",)),
        name="s5_core",
    )(xu, kk, pb, cp, d, lam)


def _fin_kernel(y_ref, sz_ref, x_ref, mod_ref, wglu_ref, bglu_ref, wout_ref, fg_ref, o_ref, tok_scr):
    for m in range(S5_WIDTH // LANE):
        for hf in range(CH_T // 2):
            c0 = 2 * m * UNIT_K + hf * LANE
            va, vb = _swap_halves(y_ref[0, :, c0:c0 + LANE], y_ref[0, :, c0 + UNIT_K:c0 + UNIT_K + LANE])
            tok_scr[m, pl.ds(2 * hf, TF // CH_T, stride=CH_T), :] = va
            tok_scr[m, pl.ds(2 * hf + 1, TF // CH_T, stride=CH_T), :] = vb
    y = jnp.concatenate([tok_scr[m] for m in range(S5_WIDTH // LANE)], axis=-1)
    y = jax.nn.gelu(y)
    y = y * jax.nn.sigmoid(_dot(y.astype(BF16), wglu_ref[...]) + bglu_ref[...])
    a = (y * sz_ref[0].astype(F32)).astype(BF16)
    gt = mod_ref[0, 0][:, 2 * D_MODEL:]
    x2 = x_ref[0] + gt * _dot(a, wout_ref[...])
    o_ref[0] = _rms(x2, fg_ref[...])


def _finish(y, sz1, x1, mod1, wglu, bglu, wout, fg):
    full = lambda shape: pl.BlockSpec(shape, lambda b, i: (0,) * len(shape))
    tok = pl.BlockSpec((1, TF, D_MODEL), lambda b, i: (b, i, 0))
    return pl.pallas_call(
        _fin_kernel,
        grid=(BATCH, SEQ // TF),
        in_specs=[pl.BlockSpec((1, TF // CH_T, UNITS * UNIT_K), lambda b, i: (b, i, 0)), tok, tok,
                  pl.BlockSpec((1, 1, 1, 3 * D_MODEL), lambda b, i: (b, 1, 0, 0)),
                  full((S5_WIDTH, S5_WIDTH)), full((1, S5_WIDTH)),
                  full((S5_WIDTH, D_MODEL)), full((1, D_MODEL))],
        out_specs=tok,
        out_shape=jax.ShapeDtypeStruct((BATCH, SEQ, D_MODEL), F32),
        scratch_shapes=[pltpu.VMEM((S5_WIDTH // LANE, TF, LANE), F32)],
        compiler_params=_params(("arbitrary", "arbitrary")),
        name="s5_finish",
    )(y, sz1, x1, mod1, wglu, bglu, wout, fg)


def _rope_tables():
    h = QK_ROPE_DIM // 2
    inv = 1.0 / (ROPE_THETA ** (np.arange(0, h, 2, dtype=np.float64) / h))
    pos = np.arange(SEQ)
    ang_r = (pos // GRID_W)[:, None] * inv[None, :]
    ang_c = (pos % GRID_W)[:, None] * inv[None, :]
    cos32 = np.concatenate([np.cos(ang_r)] * 2 + [np.cos(ang_c)] * 2, axis=-1)
    sin32 = np.concatenate([np.sin(ang_r)] * 2 + [np.sin(ang_c)] * 2, axis=-1)
    cos = np.zeros((TOK, HEAD_PAD), np.float32)
    sin = np.zeros((TOK, HEAD_PAD), np.float32)
    kt = np.zeros((TOK, HEAD_PAD), np.float32)
    cos[:, :QK_NOPE_DIM] = 1.0
    cos[:CTX_LEN, QK_NOPE_DIM:QK_DIM] = 1.0
    kt[:CTX_LEN, :QK_ROPE_DIM] = 1.0
    cos[CTX_LEN:, QK_NOPE_DIM:QK_DIM] = cos32
    sin[CTX_LEN:, QK_NOPE_DIM:QK_DIM] = sin32
    kt[CTX_LEN:, :QK_ROPE_DIM] = cos32
    kt[CTX_LEN:, QK_ROPE_DIM:2 * QK_ROPE_DIM] = sin32
    return jnp.asarray(cos), jnp.asarray(sin), jnp.asarray(kt)


def _mla_selectors():
    def partner(d):
        return (d + 8, -1.0) if d % 16 < 8 else (d - 8, 1.0)

    o2 = Q_LORA_RANK + KV_LORA_RANK
    o3 = o2 + QK_ROPE_DIM
    pin = np.zeros((o3 + MLA_WIDTH, PROJ_W), np.float32)
    pin[np.arange(o3), np.arange(o3)] = 1.0
    pin[o3 + np.arange(MLA_WIDTH), 512 + np.arange(MLA_WIDTH)] = 1.0
    pa = np.zeros((MLA_HEADS * QK_DIM, QK_PAD), np.float32)
    pb = np.zeros((MLA_HEADS * QK_DIM, QK_PAD), np.float32)
    pk = np.zeros((MLA_HEADS * 128, QK_PAD), np.float32)
    pv = np.zeros((MLA_HEADS * 128, MLA_WIDTH), np.float32)
    kb = np.zeros((128, QK_PAD), np.float32)
    for d in range(QK_ROPE_DIM):
        src, sign = partner(d)
        pin[o2 + src, o3 + d] = sign
        for hd in range(MLA_HEADS):
            pb[hd * QK_DIM + QK_NOPE_DIM + src, hd * HEAD_PAD + QK_NOPE_DIM + d] = sign
            kb[d, hd * HEAD_PAD + QK_NOPE_DIM + d] = 1.0
            kb[QK_ROPE_DIM + d, hd * HEAD_PAD + QK_NOPE_DIM + d] = 1.0
    for hd in range(MLA_HEADS):
        pa[hd * QK_DIM + np.arange(QK_DIM), hd * HEAD_PAD + np.arange(QK_DIM)] = 1.0
        pk[hd * 128 + np.arange(QK_NOPE_DIM), hd * HEAD_PAD + np.arange(QK_NOPE_DIM)] = 1.0
        pv[hd * 128 + QK_NOPE_DIM + np.arange(V_HEAD_DIM), hd * V_HEAD_DIM + np.arange(V_HEAD_DIM)] = 1.0
    return [jnp.asarray(a, dtype=BF16) for a in (pin, pa, pb, pk, pv, kb)]


def _mla_wprep_kernel(win_ref, wuq_ref, wukv_ref, pin_ref, pa_ref, pb_ref, pk_ref, pv_ref, kb_ref,
                      o_in, o_qa, o_qb, o_k, o_v):
    o_in[...] = _dot(win_ref[...].astype(BF16), pin_ref[...]).astype(BF16)
    wq = (wuq_ref[...] * (SOFTMAX_SCALE * math.log2(math.e))).astype(BF16)
    o_qa[...] = _dot(wq, pa_ref[...]).astype(BF16)
    o_qb[...] = _dot(wq, pb_ref[...]).astype(BF16)
    wkv = wukv_ref[...].astype(BF16)
    o_k[:KV_LORA_RANK] = _dot(wkv, pk_ref[...]).astype(BF16)
    o_k[KV_LORA_RANK:] = kb_ref[...]
    o_v[...] = _dot(wkv, pv_ref[...]).astype(BF16)


def _mla_weights(w_in, w_uq, w_ukv):
    nj = 4
    full = lambda a: pl.BlockSpec(a.shape, lambda j: (0, 0))
    cols = lambda rows, width: pl.BlockSpec((rows, width // nj), lambda j: (0, j))
    sel = _mla_selectors()
    widths = (PROJ_W, QK_PAD, QK_PAD, QK_PAD, MLA_WIDTH, QK_PAD)
    out_rows = (D_MODEL, Q_LORA_RANK, Q_LORA_RANK, 256, KV_LORA_RANK)
    return pl.pallas_call(
        _mla_wprep_kernel,
        grid=(nj,),
        in_specs=[full(w_in), full(w_uq), full(w_ukv)] + [cols(a.shape[0], w) for a, w in zip(sel, widths)],
        out_specs=[cols(r, w) for r, w in zip(out_rows, widths)],
        out_shape=[jax.ShapeDtypeStruct((r, w), BF16) for r, w in zip(out_rows, widths)],
        compiler_params=_params(("arbitrary",)),
        name="mla_weight_prep",
    )(w_in, w_uq, w_ukv, *sel)


def _lam_tiles(lam):
    lam = lam.reshape(2, 2, UNITS, UNIT_ST // LANE, LANE)
    lam = jnp.concatenate([lam[0, 0], lam[1, 0], lam[0, 1], lam[1, 1]], axis=1)
    return jnp.broadcast_to(lam[:, :, None, :], (UNITS, STATE_TILES, SUB, LANE))


def kernel(x, c, ctx, c_ctx, ada_w, ada_b, norm_g, mla_w_in, mla_q_norm, mla_w_uq, mla_kv_norm, mla_w_ukv, mla_w_out, s5_w_in, s5_a_re, s5_a_im, s5_log_step, s5_b_re, s5_b_im, s5_c_re, s5_c_im, s5_d, s5_w_glu, s5_b_glu, s5_w_out, final_g):
    cc = jnp.concatenate([c, c_ctx[None, :], jnp.zeros((7, D_MODEL), F32)], axis=0)
    mods = _modulation(cc, ada_w, ada_b)

    def mod_rows(i):
        ctx_row = jnp.broadcast_to(mods[i, 8][None, :], (BATCH, 3 * D_MODEL))
        return jnp.stack([ctx_row, mods[i, :BATCH]], axis=1)[:, :, None, :]

    mod0, mod1 = mod_rows(0), mod_rows(1)

    win, wqa, wqb, wk, wv = _mla_weights(mla_w_in[0], mla_w_uq[0], mla_w_ukv[0])
    cos, sin, kt = _rope_tables()
    q, k, v, sz = _mla_proj(ctx, x, mod0, norm_g[0][None, :], win, mla_q_norm[0][None, :],
                            mla_kv_norm[0][None, :], wqa, wqb, wk, wv, cos, sin, kt)
    o = _attention(q, k, v)
    x1, xu, sz1 = _mla_out(o, sz, ctx, x, mod0, mod1, norm_g[1][None, :], mla_w_out[0].astype(BF16),
                           s5_w_in[0].astype(BF16))

    n = 2 * S5_GROUPS
    lam, pb, cp, kk = _s5_prep(
        s5_a_re[0].reshape(n, S5_STATE), s5_a_im[0].reshape(n, S5_STATE), s5_log_step[0].reshape(n, 1),
        jnp.swapaxes(s5_b_re[0], -1, -2).reshape(n, S5_GROUP, S5_STATE),
        jnp.swapaxes(s5_b_im[0], -1, -2).reshape(n, S5_GROUP, S5_STATE),
        s5_c_re[0].reshape(n, S5_GROUP, S5_STATE), s5_c_im[0].reshape(n, S5_GROUP, S5_STATE))
    y = _s5_core(xu, kk.reshape(CH_T, 2, UNITS, UNIT_G, S5_GROUP, S5_GROUP),
                 pb.reshape(2, CH_T, 2, UNITS, UNIT_G, S5_GROUP, S5_STATE),
                 cp.reshape(2, CH_T, 2, UNITS, UNIT_G, S5_STATE, S5_GROUP),
                 s5_d[0].reshape(UNITS, UNIT_CH, 1), _lam_tiles(lam))
    return _finish(y, sz1, x1, mod1, s5_w_glu[0].astype(BF16), s5_b_glu[0][None, :], s5_w_out[0].astype(BF16),
                   final_g[None, :])
```

```python
import functools
import math

import jax
import jax.numpy as jnp
import numpy as np
from jax import lax
from jax.experimental import pallas as pl
from jax.experimental.pallas import tpu as pltpu

D_MODEL = 1024
BATCH = 8
SEQ = 2048
GRID_W = 64
CTX_LEN = 256
TOK = CTX_LEN + SEQ
EPS = 1e-6

MLA_HEADS = 16
QK_NOPE_DIM = 64
QK_ROPE_DIM = 32
V_HEAD_DIM = 64
Q_LORA_RANK = 256
KV_LORA_RANK = 128
MLA_WIDTH = MLA_HEADS * V_HEAD_DIM
QK_DIM = QK_NOPE_DIM + QK_ROPE_DIM
SOFTMAX_SCALE = QK_DIM ** -0.5
ROPE_THETA = 10000.0
HEAD_PAD = 128
QK_PAD = MLA_HEADS * HEAD_PAD
PROJ_W = 1536

S5_WIDTH = D_MODEL
S5_GROUP = 16
S5_GROUPS = 64
S5_STATE = 64
CH_T = 4
UNIT_G = 4
UNIT_CH = UNIT_G * S5_GROUP
UNITS = S5_GROUPS // UNIT_G
UNIT_K = CH_T * UNIT_CH
UNIT_ST = UNIT_G * S5_STATE
NCH = TOK // CH_T
NCH_CTX = CTX_LEN // CH_T
LANE = 128
SUB = 8

TL = 512
TF = 512
TQ = 256
KCH = 256
HPAIRS = 2
assert TQ == CTX_LEN
VMEM_LIMIT = 56 * 1024 * 1024

F32 = jnp.float32
BF16 = jnp.bfloat16


def _params(sem, flags=None):
    return pltpu.CompilerParams(dimension_semantics=sem, vmem_limit_bytes=VMEM_LIMIT, flags=flags)


def _silu(v):
    return v * jax.nn.sigmoid(v)


def _rms(v, g):
    return v * lax.rsqrt(jnp.mean(v * v, axis=-1, keepdims=True) + EPS) * g


def _dot(a, b):
    return jnp.dot(a, b, preferred_element_type=F32)


def _mod_kernel(cc_ref, w_ref, b_ref, o_ref):
    a = _silu(cc_ref[...]).astype(BF16)
    o_ref[0] = _dot(a, w_ref[0].astype(BF16)) + b_ref[0]


def _modulation(cc, ada_w, ada_b):
    depth = ada_w.shape[0]
    tn = 768
    return pl.pallas_call(
        _mod_kernel,
        grid=(depth, 3 * D_MODEL // tn),
        in_specs=[
            pl.BlockSpec((16, D_MODEL), lambda i, j: (0, 0)),
            pl.BlockSpec((1, D_MODEL, tn), lambda i, j: (i, 0, j)),
            pl.BlockSpec((1, 1, tn), lambda i, j: (i, 0, j)),
        ],
        out_specs=pl.BlockSpec((1, 16, tn), lambda i, j: (i, 0, j)),
        out_shape=jax.ShapeDtypeStruct((depth, 16, 3 * D_MODEL), F32),
        compiler_params=_params(("arbitrary", "arbitrary")),
        name="modulation",
    )(cc, ada_w, ada_b.reshape(depth, 1, 3 * D_MODEL))


def _mod_spec(per_batch):
    return pl.BlockSpec((1, 1, 3 * D_MODEL), (lambda b, i: (b, 0, 0)) if per_batch else (lambda b, i: (0, 0, 0)))


def _mla_proj_kernel(x_ref, mod_ref, g_ref, win_ref, qg_ref, kvg_ref, wqa_ref, wqb_ref, wk_ref, wv_ref,
                     cos_ref, sin_ref, kt_ref, *refs):
    q_ref, k_ref, v_ref, sz_ref = refs[-4:]
    x = x_ref[0]
    mod = mod_ref[0]
    sh = mod[:, :D_MODEL]
    sc = mod[:, D_MODEL:2 * D_MODEL]
    h = _rms(x, g_ref[...]) * (1.0 + sc) + sh
    p = _dot(h.astype(BF16), win_ref[...])
    cqn = _rms(p[:, :Q_LORA_RANK], qg_ref[...]).astype(BF16)
    ckvn = _rms(p[:, Q_LORA_RANK:Q_LORA_RANK + KV_LORA_RANK], kvg_ref[...]).astype(BF16)
    kr = p[:, 384:512]
    z = p[:, 512:]
    qa = _dot(cqn, wqa_ref[...])
    qb = _dot(cqn, wqb_ref[...])
    cos = cos_ref[...]
    sin = sin_ref[...]
    for hd in range(MLA_HEADS):
        sl = slice(hd * HEAD_PAD, (hd + 1) * HEAD_PAD)
        q_ref[0, :, sl] = (qa[:, sl] * cos + qb[:, sl] * sin).astype(BF16)
    kin = jnp.concatenate([ckvn, (kr * kt_ref[...]).astype(BF16)], axis=-1)
    k_ref[0] = _dot(kin, wk_ref[...]).astype(BF16)
    v_ref[0] = _dot(ckvn, wv_ref[...]).astype(BF16)
    sz_ref[0] = _silu(z).astype(BF16)


def _mla_proj(xs, mod, weights, tables, filled=None):
    is_ctx = filled is not None
    tile = CTX_LEN if is_ctx else TL
    off = SEQ // tile if is_ctx else 0
    full = lambda a: pl.BlockSpec(a.shape, lambda b, i: (0,) * a.ndim)
    tok = lambda w: pl.BlockSpec((1, tile, w), lambda b, i: (b, i + off, 0))
    pos = pl.BlockSpec((tile, HEAD_PAD), lambda b, i: (i + off, 0))
    widths = (QK_PAD, QK_PAD, MLA_WIDTH, MLA_WIDTH)
    in_specs = [pl.BlockSpec((1, tile, D_MODEL), lambda b, i: (b, i, 0)), _mod_spec(not is_ctx)]
    in_specs += [full(w) for w in weights] + [pos, pos, pos]
    args = [xs, mod, *weights, *tables]
    aliases = {}
    if is_ctx:
        aliases = {len(args) + n: n for n in range(4)}
        in_specs += [pl.BlockSpec(memory_space=pl.ANY)] * 4
        args += list(filled)
    return pl.pallas_call(
        _mla_proj_kernel,
        grid=(BATCH, xs.shape[1] // tile),
        in_specs=in_specs,
        out_specs=[tok(w) for w in widths],
        out_shape=[jax.ShapeDtypeStruct((BATCH, TOK, w), BF16) for w in widths],
        input_output_aliases=aliases,
        compiler_params=_params(("arbitrary", "arbitrary")),
        name="mla_proj_ctx" if is_ctx else "mla_proj",
    )(*args)


def _attn_kernel(q_ref, k_ref, v_ref, o_ref, s_buf, m_buf, vx_buf, cs_buf, cm_buf):
    nt = SEQ // TQ
    lane = lax.broadcasted_iota(jnp.int32, (TOK, 2 * V_HEAD_DIM), 1)
    for hp in range(HPAIRS):
        v = v_ref[0, :, hp * 2 * V_HEAD_DIM:(hp + 1) * 2 * V_HEAD_DIM]
        vx_buf[hp, 0] = jnp.where(lane < V_HEAD_DIM, v, (lane == V_HEAD_DIM).astype(BF16))
        vx_buf[hp, 1] = jnp.where(lane >= V_HEAD_DIM, v, (lane == 0).astype(BF16))

    def scores(hp, row, k0, nk, slot):
        sb, mb = (cs_buf.at[hp], cm_buf.at[hp]) if slot is None else (s_buf.at[slot], m_buf.at[slot])
        for hh in range(2):
            c0 = (2 * hp + hh) * HEAD_PAD
            s = lax.dot_general(q_ref[0, pl.ds(row, TQ), c0:c0 + HEAD_PAD], k_ref[0, k0:k0 + nk, c0:c0 + HEAD_PAD],
                                (((1,), (1,)), ((), ())), preferred_element_type=F32)
            sb[hh, :, :nk] = s
            mb[hh] = jnp.broadcast_to(jnp.max(s, axis=-1, keepdims=True), (TQ, KCH))

    def values(hp, row, k0, nk, slot):
        sb, mb = (cs_buf.at[hp], cm_buf.at[hp]) if slot is None else (s_buf.at[slot], m_buf.at[slot])
        outs = []
        for hh in range(2):
            m = mb[hh]
            ps = [jnp.exp2(sb[hh, :, n * KCH:(n + 1) * KCH] - m).astype(BF16) for n in range(nk // KCH)]
            acc = _dot(jnp.concatenate(ps, axis=-1), vx_buf[hp, hh, k0:k0 + nk, :])
            l_col = V_HEAD_DIM if hh == 0 else 0
            outs.append(acc / acc[:, l_col:l_col + 1])
        olane = lax.broadcasted_iota(jnp.int32, outs[0].shape, 1)
        o_ref[0, pl.ds(row, TQ), hp * 2 * V_HEAD_DIM:(hp + 1) * 2 * V_HEAD_DIM] = jnp.where(
            olane < V_HEAD_DIM, outs[0], outs[1]).astype(BF16)

    for hp in range(HPAIRS):
        scores(hp, SEQ, SEQ, CTX_LEN, None)
    scores(0, 0, 0, TOK, 0)
    for hp in range(HPAIRS):
        values(hp, SEQ, SEQ, CTX_LEN, None)
    for hp in range(HPAIRS):
        for t in range(1, nt):
            scores(hp, t * TQ, 0, TOK, t % 2)
            values(hp, (t - 1) * TQ, 0, TOK, (t - 1) % 2)
        if hp + 1 < HPAIRS:
            scores(hp + 1, 0, 0, TOK, 0)
        values(hp, (nt - 1) * TQ, 0, TOK, (nt - 1) % 2)


def _attention(q, k, v):
    qk = pl.BlockSpec((1, TOK, HPAIRS * 2 * HEAD_PAD), lambda b, h: (b, 0, h))
    vo = pl.BlockSpec((1, TOK, HPAIRS * 2 * V_HEAD_DIM), lambda b, h: (b, 0, h))
    return pl.pallas_call(
        _attn_kernel,
        grid=(BATCH, MLA_HEADS // (2 * HPAIRS)),
        in_specs=[qk, qk, vo],
        out_specs=vo,
        out_shape=jax.ShapeDtypeStruct((BATCH, TOK, MLA_WIDTH), BF16),
        scratch_shapes=[
            pltpu.VMEM((2, 2, TQ, TOK), F32),
            pltpu.VMEM((2, 2, TQ, KCH), F32),
            pltpu.VMEM((HPAIRS, 2, TOK, 2 * V_HEAD_DIM), BF16),
            pltpu.VMEM((HPAIRS, 2, TQ, CTX_LEN), F32),
            pltpu.VMEM((HPAIRS, 2, TQ, KCH), F32),
        ],
        compiler_params=_params(("arbitrary", "arbitrary")),
        name="attention",
    )(q, k, v)


def _swap_halves(va, vb):
    lo = lax.broadcasted_iota(jnp.int32, va.shape, 1) < UNIT_CH
    return (jnp.where(lo, va, pltpu.roll(vb, UNIT_CH, 1)),
            jnp.where(lo, pltpu.roll(va, UNIT_CH, 1), vb))


def _mla_out_kernel(o_ref, sz_ref, x_ref, mod0_ref, mod1_ref, g1_ref, wout_ref, win_ref, *refs, is_ctx):
    if is_ctx:
        xu_ref, tok_scr = refs[-2:]
    else:
        x1_ref, xu_ref, sz1_ref, tok_scr = refs
    a = (o_ref[0].astype(F32) * sz_ref[0].astype(F32)).astype(BF16)
    gt = mod0_ref[0][:, 2 * D_MODEL:]
    x1 = x_ref[0] + gt * _dot(a, wout_ref[...])
    mod1 = mod1_ref[0]
    h = _rms(x1, g1_ref[...]) * (1.0 + mod1[:, D_MODEL:2 * D_MODEL]) + mod1[:, :D_MODEL]
    if is_ctx:
        u = _dot(h.astype(BF16), win_ref[:, :S5_WIDTH])
    else:
        x1_ref[0] = x1
        p = _dot(h.astype(BF16), win_ref[...])
        sz1_ref[0] = _silu(p[:, S5_WIDTH:]).astype(BF16)
        u = p[:, :S5_WIDTH]
    rows = u.shape[0]
    for m in range(S5_WIDTH // LANE):
        tok_scr[m] = u[:, m * LANE:(m + 1) * LANE]
    for m in range(S5_WIDTH // LANE):
        v = [tok_scr[m, pl.ds(t, rows // CH_T, stride=CH_T), :] for t in range(CH_T)]
        for hf in range(CH_T // 2):
            even, odd = _swap_halves(v[2 * hf], v[2 * hf + 1])
            c0 = 2 * m * UNIT_K + hf * LANE
            xu_ref[0, :, c0:c0 + LANE] = even.astype(BF16)
            xu_ref[0, :, c0 + UNIT_K:c0 + UNIT_K + LANE] = odd.astype(BF16)


def _mla_out(o, sz, xs, mod0, mod1, g1, wout, win, xu_filled=None):
    is_ctx = xu_filled is not None
    tile = CTX_LEN if is_ctx else TL
    off = SEQ // tile if is_ctx else 0
    full = lambda a: pl.BlockSpec(a.shape, lambda b, i: (0,) * a.ndim)
    shared = lambda w: pl.BlockSpec((1, tile, w), lambda b, i: (b, i + off, 0))
    own = lambda w: pl.BlockSpec((1, tile, w), lambda b, i: (b, i, 0))
    xu_spec = pl.BlockSpec((1, tile // CH_T, UNITS * UNIT_K), lambda b, i: (b, i + off, 0))
    xu_shape = jax.ShapeDtypeStruct((BATCH, NCH, UNITS * UNIT_K), BF16)
    in_specs = [shared(MLA_WIDTH), shared(MLA_WIDTH), own(D_MODEL), _mod_spec(not is_ctx), _mod_spec(not is_ctx),
                full(g1), full(wout), full(win)]
    args = [o, sz, xs, mod0, mod1, g1, wout, win]
    if is_ctx:
        in_specs.append(pl.BlockSpec(memory_space=pl.ANY))
        args.append(xu_filled)
        out_specs, out_shape, aliases = [xu_spec], [xu_shape], {len(args) - 1: 0}
    else:
        out_specs = [own(D_MODEL), xu_spec, own(S5_WIDTH)]
        out_shape = [jax.ShapeDtypeStruct((BATCH, SEQ, D_MODEL), F32), xu_shape,
                     jax.ShapeDtypeStruct((BATCH, SEQ, S5_WIDTH), BF16)]
        aliases = {}
    return pl.pallas_call(
        functools.partial(_mla_out_kernel, is_ctx=is_ctx),
        grid=(BATCH, xs.shape[1] // tile),
        in_specs=in_specs,
        out_specs=out_specs,
        out_shape=out_shape,
        input_output_aliases=aliases,
        scratch_shapes=[pltpu.VMEM((S5_WIDTH // LANE, tile, LANE), F32)],
        compiler_params=_params(("arbitrary", "arbitrary")),
        name="mla_out_s5_in_ctx" if is_ctx else "mla_out_s5_in",
    )(*args)


def _cmul(ar, ai, br, bi):
    return ar * br - ai * bi, ar * bi + ai * br


def _group_dot(a, b, precision=lax.Precision.HIGHEST):
    return lax.dot_general(a, b, (((2,), (2,)), ((0,), (0,))), precision=precision, preferred_element_type=F32)


def _group_transpose(eye, a):
    return _group_dot(eye, a.astype(BF16), precision=None)


def _s5_prep_kernel(are_ref, aim_ref, ls_ref, bre_ref, bim_ref, cre_ref, cim_ref, lam_ref, pb_ref, cp_ref, kk_ref):
    n = are_ref.shape[0]
    eye = (lax.broadcasted_iota(jnp.int32, (n, S5_STATE, S5_STATE), 1)
           == lax.broadcasted_iota(jnp.int32, (n, S5_STATE, S5_STATE), 2)).astype(BF16)
    ar = are_ref[...]
    ai = aim_ref[...]
    dt = jnp.exp(ls_ref[...])
    mag = jnp.exp(ar * dt)
    lb_re = mag * jnp.cos(ai * dt)
    lb_im = mag * jnp.sin(ai * dt)
    den = ar * ar + ai * ai
    nr = lb_re - 1.0
    f_re = ((nr * ar + lb_im * ai) / den)[:, None, :]
    f_im = ((lb_im * ar - nr * ai) / den)[:, None, :]
    bb_re, bb_im = _cmul(f_re, f_im, bre_ref[...], bim_ref[...])
    c_re = cre_ref[...]
    c_im = cim_ref[...]
    pw_re = jnp.ones_like(lb_re)
    pw_im = jnp.zeros_like(lb_re)
    for r in range(CH_T + 1):
        pr = pw_re[:, None, :]
        pi = pw_im[:, None, :]
        cl_re, cl_im = _cmul(c_re, c_im, pr, pi)
        if r < CH_T:
            q_re, q_im = _cmul(pr, pi, bb_re, bb_im)
            pb_ref[0, r] = q_re
            pb_ref[1, r] = q_im
            kk_ref[r] = _group_dot(bb_re, cl_re) - _group_dot(bb_im, cl_im)
        if r > 0:
            cp_ref[0, r - 1] = _group_transpose(eye, cl_re)
            cp_ref[1, r - 1] = _group_transpose(eye, -cl_im)
        if r == CH_T:
            lam_ref[0] = pw_re
            lam_ref[1] = pw_im
        else:
            pw_re, pw_im = _cmul(pw_re, pw_im, lb_re, lb_im)


def _s5_prep(a_re, a_im, log_step, b_re_t, b_im_t, c_re, c_im):
    n = a_re.shape[0]
    nb = 16
    row2 = pl.BlockSpec((nb, S5_STATE), lambda i: (i, 0))
    row3 = pl.BlockSpec((nb, S5_GROUP, S5_STATE), lambda i: (i, 0, 0))
    return pl.pallas_call(
        _s5_prep_kernel,
        grid=(n // nb,),
        in_specs=[row2, row2, pl.BlockSpec((nb, 1), lambda i: (i, 0)), row3, row3, row3, row3],
        out_specs=[
            pl.BlockSpec((2, nb, S5_STATE), lambda i: (0, i, 0)),
            pl.BlockSpec((2, CH_T, nb, S5_GROUP, S5_STATE), lambda i: (0, 0, i, 0, 0)),
            pl.BlockSpec((2, CH_T, nb, S5_STATE, S5_GROUP), lambda i: (0, 0, i, 0, 0)),
            pl.BlockSpec((CH_T, nb, S5_GROUP, S5_GROUP), lambda i: (0, i, 0, 0)),
        ],
        out_shape=[
            jax.ShapeDtypeStruct((2, n, S5_STATE), F32),
            jax.ShapeDtypeStruct((2, CH_T, n, S5_GROUP, S5_STATE), F32),
            jax.ShapeDtypeStruct((2, CH_T, n, S5_STATE, S5_GROUP), F32),
            jax.ShapeDtypeStruct((CH_T, n, S5_GROUP, S5_GROUP), F32),
        ],
        compiler_params=_params(("arbitrary",)),
        name="s5_prep",
    )(a_re, a_im, log_step, b_re_t, b_im_t, c_re, c_im)


STATE_TILES = 2 * 2 * UNIT_ST // LANE


def _hdot(a, rep):
    return _dot(a.astype(BF16), rep)


def _unit_operators(kk_ref, pb_ref, cp_ref, d_ref):
    def iota(shape, dim):
        return lax.broadcasted_iota(jnp.int32, shape, dim)

    rep16 = (iota((S5_GROUP, UNIT_K), 1) % S5_GROUP == iota((S5_GROUP, UNIT_K), 0)).astype(BF16)
    rep64 = (iota((S5_STATE, UNIT_ST), 1) % S5_STATE == iota((S5_STATE, UNIT_ST), 0)).astype(BF16)
    row = iota((UNIT_CH, UNIT_K), 0)
    col = iota((UNIT_CH, UNIT_K), 1)
    same_group_out = row // S5_GROUP == (col // S5_GROUP) % UNIT_G
    same_group_st = row // S5_GROUP == col // S5_STATE
    on_diag = row == col % UNIT_CH
    col_t = col // UNIT_CH
    srow = iota((UNIT_ST, UNIT_K), 0)
    scol = iota((UNIT_ST, UNIT_K), 1)
    st_same_group = srow // S5_STATE == (scol // S5_GROUP) % UNIT_G
    st_col_t = scol // UNIT_CH
    kexp = _hdot(jnp.concatenate([kk_ref[k, d, 0].reshape(UNIT_CH, S5_GROUP)
                                  for d in range(2) for k in range(CH_T)], axis=0), rep16)
    pexp = _hdot(jnp.concatenate([pb_ref[ri, r, d, 0].reshape(UNIT_CH, S5_STATE)
                                  for d in range(2) for ri in range(2) for r in range(CH_T)], axis=0), rep64)
    cexp = _hdot(jnp.concatenate([cp_ref[ri, rr, d, 0].reshape(UNIT_ST, S5_GROUP)
                                  for d in range(2) for ri in range(2) for rr in range(CH_T)], axis=0), rep16)

    def blk(a, idx, nrows):
        return a[idx * nrows:(idx + 1) * nrows]

    rows = []
    for j in range(CH_T):
        acc = jnp.where((col_t == j) & on_diag, d_ref[0], 0.0)
        for d in range(2):
            for k in range(CH_T):
                lag_ok = (col_t - j == k) if d == 0 else (j - col_t == k)
                acc = acc + jnp.where(lag_ok & same_group_out, blk(kexp, d * CH_T + k, UNIT_CH), 0.0)
        rows.append(acc)
    parts = [jnp.concatenate(rows, axis=0)]
    cos = []
    for d in range(2):
        rows = []
        for j in range(CH_T):
            r = CH_T - 1 - j if d == 0 else j
            rows.append(jnp.concatenate(
                [jnp.where(same_group_st, blk(pexp, (d * 2 + ri) * CH_T + r, UNIT_CH), 0.0) for ri in range(2)],
                axis=-1))
        parts.append(jnp.concatenate(rows, axis=0))
        for ri in range(2):
            acc = jnp.zeros((UNIT_ST, UNIT_K), F32)
            for rr in range(CH_T):
                t = rr if d == 0 else CH_T - 1 - rr
                acc = acc + jnp.where((st_col_t == t) & st_same_group,
                                      blk(cexp, (d * 2 + ri) * CH_T + rr, UNIT_ST), 0.0)
            cos.append(acc)
    return jnp.concatenate(parts, axis=-1).astype(BF16), jnp.concatenate(cos, axis=0).astype(BF16)


def _s5_core_kernel(x_ref, kk_ref, pb_ref, cp_ref, d_ref, lam_ref, y_ref, st_scr, yi_scr):
    w1, co = _unit_operators(kk_ref, pb_ref, cp_ref, d_ref)
    nlat = NCH - NCH_CTX
    for b in range(BATCH):
        r = _dot(x_ref[b], w1)
        y_ref[b] = r[:nlat, :UNIT_K]
        for lt in range(STATE_TILES):
            c0 = UNIT_K + lt * LANE
            st_scr[lt, pl.ds(b, NCH, stride=BATCH), :] = r[:, c0:c0 + LANE]
    lam = [lam_ref[0, lt] for lt in range(STATE_TILES)]

    def rows(chunk):
        return pl.ds(pl.multiple_of(chunk * BATCH, BATCH), BATCH)

    def load_z(row, base):
        return [st_scr[base + k, rows(row), :] for k in range(4)]

    def advance(state, z, row, base):
        for k in range(4):
            st_scr[base + k, rows(row), :] = state[k]
        ar0, ar1, ai0, ai1 = lam[base:base + 4]
        return [ar0 * state[0] - ai0 * state[2] + z[0], ar1 * state[1] - ai1 * state[3] + z[1],
                ar0 * state[2] + ai0 * state[0] + z[2], ar1 * state[3] + ai1 * state[1] + z[3]]

    def fwd_row(i):
        return jnp.where(i < NCH_CTX, nlat + i, i - NCH_CTX)

    def bwd_row(i):
        return NCH - 1 - i

    def step(i, carry):
        s_f, z_f, s_b, z_b = carry
        nxt = jnp.minimum(i + 1, NCH - 1)
        z_f_next = load_z(fwd_row(nxt), 0)
        z_b_next = load_z(bwd_row(nxt), 4)
        return advance(s_f, z_f, fwd_row(i), 0), z_f_next, advance(s_b, z_b, bwd_row(i), 4), z_b_next

    zero = [jnp.zeros((BATCH, LANE), F32)] * 4
    lax.fori_loop(0, NCH, step, (zero, load_z(nlat, 0), zero, load_z(NCH - 1, 4)), unroll=2)
    for rb in range(BATCH):
        sl = slice(rb * nlat, (rb + 1) * nlat)
        lhs = jnp.concatenate([st_scr[lt, sl, :] for lt in range(STATE_TILES)], axis=-1)
        yi = _dot(lhs.astype(BF16), co)
        for t in range(UNIT_K // LANE):
            yi_scr[t, sl, :] = yi[:, t * LANE:(t + 1) * LANE]
    for b in range(BATCH):
        y_ref[b] = y_ref[b] + jnp.concatenate(
            [yi_scr[t, pl.ds(b, nlat, stride=BATCH), :] for t in range(UNIT_K // LANE)], axis=-1)


def _s5_core(xu, kk, pb, cp, d, lam):
    return pl.pallas_call(
        _s5_core_kernel,
        grid=(UNITS,),
        in_specs=[
            pl.BlockSpec((BATCH, NCH, UNIT_K), lambda q: (0, 0, q)),
            pl.BlockSpec((CH_T, 2, 1, UNIT_G, S5_GROUP, S5_GROUP), lambda q: (0, 0, q, 0, 0, 0)),
            pl.BlockSpec((2, CH_T, 2, 1, UNIT_G, S5_GROUP, S5_STATE), lambda q: (0, 0, 0, q, 0, 0, 0)),
            pl.BlockSpec((2, CH_T, 2, 1, UNIT_G, S5_STATE, S5_GROUP), lambda q: (0, 0, 0, q, 0, 0, 0)),
            pl.BlockSpec((1, UNIT_CH, 1), lambda q: (q, 0, 0)),
            pl.BlockSpec((1, STATE_TILES, SUB, LANE), lambda q: (q, 0, 0, 0)),
        ],
        out_specs=pl.BlockSpec((BATCH, NCH - NCH_CTX, UNIT_K), lambda q: (0, 0, q)),
        out_shape=jax.ShapeDtypeStruct((BATCH, NCH - NCH_CTX, UNITS * UNIT_K), F32),
        scratch_shapes=[pltpu.VMEM((STATE_TILES, BATCH * NCH, LANE), F32),
                        pltpu.VMEM((UNIT_K // LANE, BATCH * NCH, LANE), F32)],
        compiler_params=_params(("arbitrary",)),
        name="s5_core",
    )(xu, kk, pb, cp, d, lam)


def _fin_kernel(y_ref, sz_ref, x_ref, mod_ref, wglu_ref, bglu_ref, wout_ref, fg_ref, o_ref, tok_scr):
    for m in range(S5_WIDTH // LANE):
        for hf in range(CH_T // 2):
            c0 = 2 * m * UNIT_K + hf * LANE
            va, vb = _swap_halves(y_ref[0, :, c0:c0 + LANE], y_ref[0, :, c0 + UNIT_K:c0 + UNIT_K + LANE])
            tok_scr[m, pl.ds(2 * hf, TF // CH_T, stride=CH_T), :] = va
            tok_scr[m, pl.ds(2 * hf + 1, TF // CH_T, stride=CH_T), :] = vb
    y = jnp.concatenate([tok_scr[m] for m in range(S5_WIDTH // LANE)], axis=-1)
    y = jax.nn.gelu(y)
    y = y * jax.nn.sigmoid(_dot(y.astype(BF16), wglu_ref[...]) + bglu_ref[...])
    a = (y * sz_ref[0].astype(F32)).astype(BF16)
    gt = mod_ref[0][:, 2 * D_MODEL:]
    x2 = x_ref[0] + gt * _dot(a, wout_ref[...])
    o_ref[0] = _rms(x2, fg_ref[...])


def _finish(y, sz1, x1, mod1, wglu, bglu, wout, fg):
    full = lambda shape: pl.BlockSpec(shape, lambda b, i: (0,) * len(shape))
    tok = pl.BlockSpec((1, TF, D_MODEL), lambda b, i: (b, i, 0))
    return pl.pallas_call(
        _fin_kernel,
        grid=(BATCH, SEQ // TF),
        in_specs=[pl.BlockSpec((1, TF // CH_T, UNITS * UNIT_K), lambda b, i: (b, i, 0)), tok, tok,
                  _mod_spec(True),
                  full((S5_WIDTH, S5_WIDTH)), full((1, S5_WIDTH)),
                  full((S5_WIDTH, D_MODEL)), full((1, D_MODEL))],
        out_specs=tok,
        out_shape=jax.ShapeDtypeStruct((BATCH, SEQ, D_MODEL), F32),
        scratch_shapes=[pltpu.VMEM((S5_WIDTH // LANE, TF, LANE), F32)],
        compiler_params=_params(("arbitrary", "arbitrary")),
        name="s5_finish",
    )(y, sz1, x1, mod1, wglu, bglu, wout, fg)


def _rope_tables():
    h = QK_ROPE_DIM // 2
    inv = 1.0 / (ROPE_THETA ** (np.arange(0, h, 2, dtype=np.float64) / h))
    pos = np.arange(SEQ)
    ang_r = (pos // GRID_W)[:, None] * inv[None, :]
    ang_c = (pos % GRID_W)[:, None] * inv[None, :]
    cos32 = np.concatenate([np.cos(ang_r)] * 2 + [np.cos(ang_c)] * 2, axis=-1)
    sin32 = np.concatenate([np.sin(ang_r)] * 2 + [np.sin(ang_c)] * 2, axis=-1)
    cos = np.zeros((TOK, HEAD_PAD), np.float32)
    sin = np.zeros((TOK, HEAD_PAD), np.float32)
    kt = np.zeros((TOK, HEAD_PAD), np.float32)
    cos[:, :QK_NOPE_DIM] = 1.0
    cos[SEQ:, QK_NOPE_DIM:QK_DIM] = 1.0
    kt[SEQ:, :QK_ROPE_DIM] = 1.0
    cos[:SEQ, QK_NOPE_DIM:QK_DIM] = cos32
    sin[:SEQ, QK_NOPE_DIM:QK_DIM] = sin32
    kt[:SEQ, :QK_ROPE_DIM] = cos32
    kt[:SEQ, QK_ROPE_DIM:2 * QK_ROPE_DIM] = sin32
    return jnp.asarray(cos), jnp.asarray(sin), jnp.asarray(kt)


def _mla_selectors():
    def partner(d):
        return (d + 8, -1.0) if d % 16 < 8 else (d - 8, 1.0)

    o2 = Q_LORA_RANK + KV_LORA_RANK
    o3 = o2 + QK_ROPE_DIM
    pin = np.zeros((o3 + MLA_WIDTH, PROJ_W), np.float32)
    pin[np.arange(o3), np.arange(o3)] = 1.0
    pin[o3 + np.arange(MLA_WIDTH), 512 + np.arange(MLA_WIDTH)] = 1.0
    pa = np.zeros((MLA_HEADS * QK_DIM, QK_PAD), np.float32)
    pb = np.zeros((MLA_HEADS * QK_DIM, QK_PAD), np.float32)
    pk = np.zeros((MLA_HEADS * 128, QK_PAD), np.float32)
    pv = np.zeros((MLA_HEADS * 128, MLA_WIDTH), np.float32)
    kb = np.zeros((128, QK_PAD), np.float32)
    for d in range(QK_ROPE_DIM):
        src, sign = partner(d)
        pin[o2 + src, o3 + d] = sign
        for hd in range(MLA_HEADS):
            pb[hd * QK_DIM + QK_NOPE_DIM + src, hd * HEAD_PAD + QK_NOPE_DIM + d] = sign
            kb[d, hd * HEAD_PAD + QK_NOPE_DIM + d] = 1.0
            kb[QK_ROPE_DIM + d, hd * HEAD_PAD + QK_NOPE_DIM + d] = 1.0
    for hd in range(MLA_HEADS):
        pa[hd * QK_DIM + np.arange(QK_DIM), hd * HEAD_PAD + np.arange(QK_DIM)] = 1.0
        pk[hd * 128 + np.arange(QK_NOPE_DIM), hd * HEAD_PAD + np.arange(QK_NOPE_DIM)] = 1.0
        pv[hd * 128 + QK_NOPE_DIM + np.arange(V_HEAD_DIM), hd * V_HEAD_DIM + np.arange(V_HEAD_DIM)] = 1.0
    return [jnp.asarray(a, dtype=BF16) for a in (pin, pa, pb, pk, pv, kb)]


def _mla_wprep_kernel(win_ref, wuq_ref, wukv_ref, pin_ref, pa_ref, pb_ref, pk_ref, pv_ref, kb_ref,
                      o_in, o_qa, o_qb, o_k, o_v):
    o_in[...] = _dot(win_ref[...].astype(BF16), pin_ref[...]).astype(BF16)
    wq = (wuq_ref[...] * (SOFTMAX_SCALE * math.log2(math.e))).astype(BF16)
    o_qa[...] = _dot(wq, pa_ref[...]).astype(BF16)
    o_qb[...] = _dot(wq, pb_ref[...]).astype(BF16)
    wkv = wukv_ref[...].astype(BF16)
    o_k[:KV_LORA_RANK] = _dot(wkv, pk_ref[...]).astype(BF16)
    o_k[KV_LORA_RANK:] = kb_ref[...]
    o_v[...] = _dot(wkv, pv_ref[...]).astype(BF16)


def _mla_weights(w_in, w_uq, w_ukv):
    nj = 4
    full = lambda a: pl.BlockSpec(a.shape, lambda j: (0, 0))
    cols = lambda rows, width: pl.BlockSpec((rows, width // nj), lambda j: (0, j))
    sel = _mla_selectors()
    widths = (PROJ_W, QK_PAD, QK_PAD, QK_PAD, MLA_WIDTH, QK_PAD)
    out_rows = (D_MODEL, Q_LORA_RANK, Q_LORA_RANK, 256, KV_LORA_RANK)
    return pl.pallas_call(
        _mla_wprep_kernel,
        grid=(nj,),
        in_specs=[full(w_in), full(w_uq), full(w_ukv)] + [cols(a.shape[0], w) for a, w in zip(sel, widths)],
        out_specs=[cols(r, w) for r, w in zip(out_rows, widths)],
        out_shape=[jax.ShapeDtypeStruct((r, w), BF16) for r, w in zip(out_rows, widths)],
        compiler_params=_params(("arbitrary",)),
        name="mla_weight_prep",
    )(w_in, w_uq, w_ukv, *sel)


def _lam_tiles(lam):
    lam = lam.reshape(2, 2, UNITS, UNIT_ST // LANE, LANE)
    lam = jnp.concatenate([lam[0, 0], lam[1, 0], lam[0, 1], lam[1, 1]], axis=1)
    return jnp.broadcast_to(lam[:, :, None, :], (UNITS, STATE_TILES, SUB, LANE))


def kernel(x, c, ctx, c_ctx, ada_w, ada_b, norm_g, mla_w_in, mla_q_norm, mla_w_uq, mla_kv_norm, mla_w_ukv, mla_w_out, s5_w_in, s5_a_re, s5_a_im, s5_log_step, s5_b_re, s5_b_im, s5_c_re, s5_c_im, s5_d, s5_w_glu, s5_b_glu, s5_w_out, final_g):
    cc = jnp.concatenate([c, c_ctx[None, :], jnp.zeros((7, D_MODEL), F32)], axis=0)
    mods = _modulation(cc, ada_w, ada_b)

    mod_lat = [mods[i, :BATCH, None, :] for i in range(2)]
    mod_ctx = [mods[i, BATCH:BATCH + 1, None, :] for i in range(2)]

    win, wqa, wqb, wk, wv = _mla_weights(mla_w_in[0], mla_w_uq[0], mla_w_ukv[0])
    weights = (norm_g[0][None, :], win, mla_q_norm[0][None, :], mla_kv_norm[0][None, :], wqa, wqb, wk, wv)
    tables = _rope_tables()
    qkvz = _mla_proj(x, mod_lat[0], weights, tables)
    q, k, v, sz = _mla_proj(ctx, mod_ctx[0], weights, tables, filled=qkvz)
    o = _attention(q, k, v)
    out_w = (norm_g[1][None, :], mla_w_out[0].astype(BF16), s5_w_in[0].astype(BF16))
    x1, xu, sz1 = _mla_out(o, sz, x, mod_lat[0], mod_lat[1], *out_w)
    xu, = _mla_out(o, sz, ctx, mod_ctx[0], mod_ctx[1], *out_w, xu_filled=xu)
    mod1 = mod_lat[1]

    n = 2 * S5_GROUPS
    lam, pb, cp, kk = _s5_prep(
        s5_a_re[0].reshape(n, S5_STATE), s5_a_im[0].reshape(n, S5_STATE), s5_log_step[0].reshape(n, 1),
        jnp.swapaxes(s5_b_re[0], -1, -2).reshape(n, S5_GROUP, S5_STATE),
        jnp.swapaxes(s5_b_im[0], -1, -2).reshape(n, S5_GROUP, S5_STATE),
        s5_c_re[0].reshape(n, S5_GROUP, S5_STATE), s5_c_im[0].reshape(n, S5_GROUP, S5_STATE))
    y = _s5_core(xu, kk.reshape(CH_T, 2, UNITS, UNIT_G, S5_GROUP, S5_GROUP),
                 pb.reshape(2, CH_T, 2, UNITS, UNIT_G, S5_GROUP, S5_STATE),
                 cp.reshape(2, CH_T, 2, UNITS, UNIT_G, S5_STATE, S5_GROUP),
                 s5_d[0].reshape(UNITS, UNIT_CH, 1), _lam_tiles(lam))
    return _finish(y, sz1, x1, mod1, s5_w_glu[0].astype(BF16), s5_b_glu[0][None, :], s5_w_out[0].astype(BF16),
                   final_g[None, :])
```

```python
import functools
import math

import jax
import jax.numpy as jnp
import numpy as np
from jax import lax
from jax.experimental import pallas as pl
from jax.experimental.pallas import tpu as pltpu

D_MODEL = 1024
BATCH = 8
SEQ = 2048
GRID_W = 64
CTX_LEN = 256
TOK = CTX_LEN + SEQ
EPS = 1e-6

MLA_HEADS = 16
QK_NOPE_DIM = 64
QK_ROPE_DIM = 32
V_HEAD_DIM = 64
Q_LORA_RANK = 256
KV_LORA_RANK = 128
MLA_WIDTH = MLA_HEADS * V_HEAD_DIM
QK_DIM = QK_NOPE_DIM + QK_ROPE_DIM
SOFTMAX_SCALE = QK_DIM ** -0.5
ROPE_THETA = 10000.0
HEAD_PAD = 128
QK_PAD = MLA_HEADS * HEAD_PAD
PROJ_W = 1536

S5_WIDTH = D_MODEL
S5_GROUP = 16
S5_GROUPS = 64
S5_STATE = 64
CH_T = 4
UNIT_G = 4
UNIT_CH = UNIT_G * S5_GROUP
UNITS = S5_GROUPS // UNIT_G
UNIT_K = CH_T * UNIT_CH
UNIT_ST = UNIT_G * S5_STATE
NCH = TOK // CH_T
NCH_CTX = CTX_LEN // CH_T
LANE = 128
SUB = 8

TL = 512
TF = 512
TQ = 256
KCH = 256
HPAIRS = 2
assert TQ == CTX_LEN
VMEM_LIMIT = 56 * 1024 * 1024

F32 = jnp.float32
BF16 = jnp.bfloat16


def _params(sem, flags=None):
    return pltpu.CompilerParams(dimension_semantics=sem, vmem_limit_bytes=VMEM_LIMIT, flags=flags)


def _silu(v):
    return v * jax.nn.sigmoid(v)


def _rms(v, g):
    return v * lax.rsqrt(jnp.mean(v * v, axis=-1, keepdims=True) + EPS) * g


def _dot(a, b):
    return jnp.dot(a, b, preferred_element_type=F32)


def _mod_kernel(cc_ref, w_ref, b_ref, o_ref):
    a = _silu(cc_ref[...]).astype(BF16)
    o_ref[0] = _dot(a, w_ref[0].astype(BF16)) + b_ref[0]


def _modulation(cc, ada_w, ada_b):
    depth = ada_w.shape[0]
    tn = 768
    return pl.pallas_call(
        _mod_kernel,
        grid=(depth, 3 * D_MODEL // tn),
        in_specs=[
            pl.BlockSpec((16, D_MODEL), lambda i, j: (0, 0)),
            pl.BlockSpec((1, D_MODEL, tn), lambda i, j: (i, 0, j)),
            pl.BlockSpec((1, 1, tn), lambda i, j: (i, 0, j)),
        ],
        out_specs=pl.BlockSpec((1, 16, tn), lambda i, j: (i, 0, j)),
        out_shape=jax.ShapeDtypeStruct((depth, 16, 3 * D_MODEL), F32),
        compiler_params=_params(("arbitrary", "arbitrary")),
        name="modulation",
    )(cc, ada_w, ada_b.reshape(depth, 1, 3 * D_MODEL))


def _mod_spec(per_batch):
    return pl.BlockSpec((1, 1, 3 * D_MODEL), (lambda b, i: (b, 0, 0)) if per_batch else (lambda b, i: (0, 0, 0)))


def _mla_proj_kernel(x_ref, mod_ref, g_ref, win_ref, qg_ref, kvg_ref, wqa_ref, wqb_ref, wk_ref, wv_ref,
                     cos_ref, sin_ref, kt_ref, *refs):
    q_ref, k_ref, v_ref, sz_ref = refs[-4:]
    x = x_ref[0]
    mod = mod_ref[0]
    sh = mod[:, :D_MODEL]
    sc = mod[:, D_MODEL:2 * D_MODEL]
    h = _rms(x, g_ref[...]) * (1.0 + sc) + sh
    p = _dot(h.astype(BF16), win_ref[...])
    cqn = _rms(p[:, :Q_LORA_RANK], qg_ref[...]).astype(BF16)
    ckvn = _rms(p[:, Q_LORA_RANK:Q_LORA_RANK + KV_LORA_RANK], kvg_ref[...]).astype(BF16)
    kr = p[:, 384:512]
    z = p[:, 512:]
    qa = _dot(cqn, wqa_ref[...])
    qb = _dot(cqn, wqb_ref[...])
    cos = cos_ref[...]
    sin = sin_ref[...]
    for hd in range(MLA_HEADS):
        sl = slice(hd * HEAD_PAD, (hd + 1) * HEAD_PAD)
        q_ref[0, :, sl] = (qa[:, sl] * cos + qb[:, sl] * sin).astype(BF16)
    kin = jnp.concatenate([ckvn, (kr * kt_ref[...]).astype(BF16)], axis=-1)
    k_ref[0] = _dot(kin, wk_ref[...]).astype(BF16)
    v_ref[0] = _dot(ckvn, wv_ref[...]).astype(BF16)
    sz_ref[0] = _silu(z).astype(BF16)


def _mla_proj(xs, mod, weights, tables, filled=None):
    is_ctx = filled is not None
    tile = CTX_LEN if is_ctx else TL
    off = SEQ // tile if is_ctx else 0
    full = lambda a: pl.BlockSpec(a.shape, lambda b, i: (0,) * a.ndim)
    tok = lambda w: pl.BlockSpec((1, tile, w), lambda b, i: (b, i + off, 0))
    pos = pl.BlockSpec((tile, HEAD_PAD), lambda b, i: (i + off, 0))
    widths = (QK_PAD, QK_PAD, MLA_WIDTH, MLA_WIDTH)
    in_specs = [pl.BlockSpec((1, tile, D_MODEL), lambda b, i: (b, i, 0)), _mod_spec(not is_ctx)]
    in_specs += [full(w) for w in weights] + [pos, pos, pos]
    args = [xs, mod, *weights, *tables]
    aliases = {}
    if is_ctx:
        aliases = {len(args) + n: n for n in range(4)}
        in_specs += [pl.BlockSpec(memory_space=pl.ANY)] * 4
        args += list(filled)
    return pl.pallas_call(
        _mla_proj_kernel,
        grid=(BATCH, xs.shape[1] // tile),
        in_specs=in_specs,
        out_specs=[tok(w) for w in widths],
        out_shape=[jax.ShapeDtypeStruct((BATCH, TOK, w), BF16) for w in widths],
        input_output_aliases=aliases,
        compiler_params=_params(("arbitrary", "arbitrary")),
        name="mla_proj_ctx" if is_ctx else "mla_proj",
    )(*args)


def _attn_kernel(q_ref, k_ref, v_ref, o_ref, s_buf, m_buf, vx_buf, cs_buf, cm_buf):
    nt = SEQ // TQ
    lane = lax.broadcasted_iota(jnp.int32, (TOK, 2 * V_HEAD_DIM), 1)
    for hp in range(HPAIRS):
        v = v_ref[0, :, hp * 2 * V_HEAD_DIM:(hp + 1) * 2 * V_HEAD_DIM]
        vx_buf[hp, 0] = jnp.where(lane < V_HEAD_DIM, v, (lane == V_HEAD_DIM).astype(BF16))
        vx_buf[hp, 1] = jnp.where(lane >= V_HEAD_DIM, v, (lane == 0).astype(BF16))

    def scores(hp, row, k0, nk, slot):
        sb, mb = (cs_buf.at[hp], cm_buf.at[hp]) if slot is None else (s_buf.at[slot], m_buf.at[slot])
        for hh in range(2):
            c0 = (2 * hp + hh) * HEAD_PAD
            s = lax.dot_general(q_ref[0, pl.ds(row, TQ), c0:c0 + HEAD_PAD], k_ref[0, k0:k0 + nk, c0:c0 + HEAD_PAD],
                                (((1,), (1,)), ((), ())), preferred_element_type=F32)
            sb[hh, :, :nk] = s
            mb[hh] = jnp.broadcast_to(jnp.max(s, axis=-1, keepdims=True), (TQ, KCH))

    def values(hp, row, k0, nk, slot):
        sb, mb = (cs_buf.at[hp], cm_buf.at[hp]) if slot is None else (s_buf.at[slot], m_buf.at[slot])
        outs = []
        for hh in range(2):
            m = mb[hh]
            ps = [jnp.exp2(sb[hh, :, n * KCH:(n + 1) * KCH] - m).astype(BF16) for n in range(nk // KCH)]
            acc = _dot(jnp.concatenate(ps, axis=-1), vx_buf[hp, hh, k0:k0 + nk, :])
            l_col = V_HEAD_DIM if hh == 0 else 0
            outs.append(acc / acc[:, l_col:l_col + 1])
        olane = lax.broadcasted_iota(jnp.int32, outs[0].shape, 1)
        o_ref[0, pl.ds(row, TQ), hp * 2 * V_HEAD_DIM:(hp + 1) * 2 * V_HEAD_DIM] = jnp.where(
            olane < V_HEAD_DIM, outs[0], outs[1]).astype(BF16)

    for hp in range(HPAIRS):
        scores(hp, SEQ, SEQ, CTX_LEN, None)
    scores(0, 0, 0, TOK, 0)
    for hp in range(HPAIRS):
        values(hp, SEQ, SEQ, CTX_LEN, None)
    for hp in range(HPAIRS):
        for t in range(1, nt):
            scores(hp, t * TQ, 0, TOK, t % 2)
            values(hp, (t - 1) * TQ, 0, TOK, (t - 1) % 2)
        if hp + 1 < HPAIRS:
            scores(hp + 1, 0, 0, TOK, 0)
        values(hp, (nt - 1) * TQ, 0, TOK, (nt - 1) % 2)


def _attention(q, k, v):
    qk = pl.BlockSpec((1, TOK, HPAIRS * 2 * HEAD_PAD), lambda b, h: (b, 0, h))
    vo = pl.BlockSpec((1, TOK, HPAIRS * 2 * V_HEAD_DIM), lambda b, h: (b, 0, h))
    return pl.pallas_call(
        _attn_kernel,
        grid=(BATCH, MLA_HEADS // (2 * HPAIRS)),
        in_specs=[qk, qk, vo],
        out_specs=vo,
        out_shape=jax.ShapeDtypeStruct((BATCH, TOK, MLA_WIDTH), BF16),
        scratch_shapes=[
            pltpu.VMEM((2, 2, TQ, TOK), F32),
            pltpu.VMEM((2, 2, TQ, KCH), F32),
            pltpu.VMEM((HPAIRS, 2, TOK, 2 * V_HEAD_DIM), BF16),
            pltpu.VMEM((HPAIRS, 2, TQ, CTX_LEN), F32),
            pltpu.VMEM((HPAIRS, 2, TQ, KCH), F32),
        ],
        compiler_params=_params(("arbitrary", "arbitrary")),
        name="attention",
    )(q, k, v)


def _swap_halves(va, vb):
    lo = lax.broadcasted_iota(jnp.int32, va.shape, 1) < UNIT_CH
    return (jnp.where(lo, va, pltpu.roll(vb, UNIT_CH, 1)),
            jnp.where(lo, pltpu.roll(va, UNIT_CH, 1), vb))


def _mla_out_kernel(o_ref, sz_ref, x_ref, mod0_ref, mod1_ref, g1_ref, wout_ref, win_ref, *refs, is_ctx):
    if is_ctx:
        xu_ref, tok_scr = refs[-2:]
    else:
        x1_ref, xu_ref, sz1_ref, tok_scr = refs
    a = (o_ref[0].astype(F32) * sz_ref[0].astype(F32)).astype(BF16)
    gt = mod0_ref[0][:, 2 * D_MODEL:]
    x1 = x_ref[0] + gt * _dot(a, wout_ref[...])
    mod1 = mod1_ref[0]
    h = _rms(x1, g1_ref[...]) * (1.0 + mod1[:, D_MODEL:2 * D_MODEL]) + mod1[:, :D_MODEL]
    if is_ctx:
        u = _dot(h.astype(BF16), win_ref[:, :S5_WIDTH])
    else:
        x1_ref[0] = x1
        p = _dot(h.astype(BF16), win_ref[...])
        sz1_ref[0] = _silu(p[:, S5_WIDTH:]).astype(BF16)
        u = p[:, :S5_WIDTH]
    rows = u.shape[0]
    for m in range(S5_WIDTH // LANE):
        tok_scr[m] = u[:, m * LANE:(m + 1) * LANE]
    for m in range(S5_WIDTH // LANE):
        v = [tok_scr[m, pl.ds(t, rows // CH_T, stride=CH_T), :] for t in range(CH_T)]
        for hf in range(CH_T // 2):
            even, odd = _swap_halves(v[2 * hf], v[2 * hf + 1])
            c0 = 2 * m * UNIT_K + hf * LANE
            xu_ref[0, :, c0:c0 + LANE] = even.astype(BF16)
            xu_ref[0, :, c0 + UNIT_K:c0 + UNIT_K + LANE] = odd.astype(BF16)


def _mla_out(o, sz, xs, mod0, mod1, g1, wout, win, xu_filled=None):
    is_ctx = xu_filled is not None
    tile = CTX_LEN if is_ctx else TL
    off = SEQ // tile if is_ctx else 0
    full = lambda a: pl.BlockSpec(a.shape, lambda b, i: (0,) * a.ndim)
    shared = lambda w: pl.BlockSpec((1, tile, w), lambda b, i: (b, i + off, 0))
    own = lambda w: pl.BlockSpec((1, tile, w), lambda b, i: (b, i, 0))
    xu_spec = pl.BlockSpec((1, tile // CH_T, UNITS * UNIT_K), lambda b, i: (b, i + off, 0))
    xu_shape = jax.ShapeDtypeStruct((BATCH, NCH, UNITS * UNIT_K), BF16)
    in_specs = [shared(MLA_WIDTH), shared(MLA_WIDTH), own(D_MODEL), _mod_spec(not is_ctx), _mod_spec(not is_ctx),
                full(g1), full(wout), full(win)]
    args = [o, sz, xs, mod0, mod1, g1, wout, win]
    if is_ctx:
        in_specs.append(pl.BlockSpec(memory_space=pl.ANY))
        args.append(xu_filled)
        out_specs, out_shape, aliases = [xu_spec], [xu_shape], {len(args) - 1: 0}
    else:
        out_specs = [own(D_MODEL), xu_spec, own(S5_WIDTH)]
        out_shape = [jax.ShapeDtypeStruct((BATCH, SEQ, D_MODEL), F32), xu_shape,
                     jax.ShapeDtypeStruct((BATCH, SEQ, S5_WIDTH), BF16)]
        aliases = {}
    return pl.pallas_call(
        functools.partial(_mla_out_kernel, is_ctx=is_ctx),
        grid=(BATCH, xs.shape[1] // tile),
        in_specs=in_specs,
        out_specs=out_specs,
        out_shape=out_shape,
        input_output_aliases=aliases,
        scratch_shapes=[pltpu.VMEM((S5_WIDTH // LANE, tile, LANE), F32)],
        compiler_params=_params(("arbitrary", "arbitrary")),
        name="mla_out_s5_in_ctx" if is_ctx else "mla_out_s5_in",
    )(*args)


def _cmul(ar, ai, br, bi):
    return ar * br - ai * bi, ar * bi + ai * br


def _group_dot(a, b, precision=lax.Precision.HIGHEST):
    return lax.dot_general(a, b, (((2,), (2,)), ((0,), (0,))), precision=precision, preferred_element_type=F32)


def _group_transpose(eye, a):
    return _group_dot(eye, a.astype(BF16), precision=None)


def _s5_prep_kernel(are_ref, aim_ref, ls_ref, bre_ref, bim_ref, cre_ref, cim_ref, lam_ref, pb_ref, cp_ref, kk_ref):
    n = are_ref.shape[0]
    eye = (lax.broadcasted_iota(jnp.int32, (n, S5_STATE, S5_STATE), 1)
           == lax.broadcasted_iota(jnp.int32, (n, S5_STATE, S5_STATE), 2)).astype(BF16)
    ar = are_ref[...]
    ai = aim_ref[...]
    dt = jnp.exp(ls_ref[...])
    mag = jnp.exp(ar * dt)
    lb_re = mag * jnp.cos(ai * dt)
    lb_im = mag * jnp.sin(ai * dt)
    den = ar * ar + ai * ai
    nr = lb_re - 1.0
    f_re = ((nr * ar + lb_im * ai) / den)[:, None, :]
    f_im = ((lb_im * ar - nr * ai) / den)[:, None, :]
    bb_re, bb_im = _cmul(f_re, f_im, bre_ref[...], bim_ref[...])
    c_re = cre_ref[...]
    c_im = cim_ref[...]
    pw_re = jnp.ones_like(lb_re)
    pw_im = jnp.zeros_like(lb_re)
    for r in range(CH_T + 1):
        pr = pw_re[:, None, :]
        pi = pw_im[:, None, :]
        cl_re, cl_im = _cmul(c_re, c_im, pr, pi)
        if r < CH_T:
            q_re, q_im = _cmul(pr, pi, bb_re, bb_im)
            pb_ref[0, r] = q_re
            pb_ref[1, r] = q_im
            kk_ref[r] = _group_dot(bb_re, cl_re) - _group_dot(bb_im, cl_im)
        if r > 0:
            cp_ref[0, r - 1] = _group_transpose(eye, cl_re)
            cp_ref[1, r - 1] = _group_transpose(eye, -cl_im)
        if r == CH_T:
            lam_ref[0] = pw_re
            lam_ref[1] = pw_im
        else:
            pw_re, pw_im = _cmul(pw_re, pw_im, lb_re, lb_im)


def _s5_prep(a_re, a_im, log_step, b_re_t, b_im_t, c_re, c_im):
    n = a_re.shape[0]
    nb = 16
    row2 = pl.BlockSpec((nb, S5_STATE), lambda i: (i, 0))
    row3 = pl.BlockSpec((nb, S5_GROUP, S5_STATE), lambda i: (i, 0, 0))
    return pl.pallas_call(
        _s5_prep_kernel,
        grid=(n // nb,),
        in_specs=[row2, row2, pl.BlockSpec((nb, 1), lambda i: (i, 0)), row3, row3, row3, row3],
        out_specs=[
            pl.BlockSpec((2, nb, S5_STATE), lambda i: (0, i, 0)),
            pl.BlockSpec((2, CH_T, nb, S5_GROUP, S5_STATE), lambda i: (0, 0, i, 0, 0)),
            pl.BlockSpec((2, CH_T, nb, S5_STATE, S5_GROUP), lambda i: (0, 0, i, 0, 0)),
            pl.BlockSpec((CH_T, nb, S5_GROUP, S5_GROUP), lambda i: (0, i, 0, 0)),
        ],
        out_shape=[
            jax.ShapeDtypeStruct((2, n, S5_STATE), F32),
            jax.ShapeDtypeStruct((2, CH_T, n, S5_GROUP, S5_STATE), F32),
            jax.ShapeDtypeStruct((2, CH_T, n, S5_STATE, S5_GROUP), F32),
            jax.ShapeDtypeStruct((CH_T, n, S5_GROUP, S5_GROUP), F32),
        ],
        compiler_params=_params(("arbitrary",)),
        name="s5_prep",
    )(a_re, a_im, log_step, b_re_t, b_im_t, c_re, c_im)


STATE_TILES = 2 * 2 * UNIT_ST // LANE


def _hdot(a, rep):
    return _dot(a.astype(BF16), rep)


def _unit_operators(kk_ref, pb_ref, cp_ref, d_ref):
    def iota(shape, dim):
        return lax.broadcasted_iota(jnp.int32, shape, dim)

    rep16 = (iota((S5_GROUP, UNIT_K), 1) % S5_GROUP == iota((S5_GROUP, UNIT_K), 0)).astype(BF16)
    rep64 = (iota((S5_STATE, UNIT_ST), 1) % S5_STATE == iota((S5_STATE, UNIT_ST), 0)).astype(BF16)
    row = iota((UNIT_CH, UNIT_K), 0)
    col = iota((UNIT_CH, UNIT_K), 1)
    same_group_out = row // S5_GROUP == (col // S5_GROUP) % UNIT_G
    same_group_st = row // S5_GROUP == col // S5_STATE
    on_diag = row == col % UNIT_CH
    col_t = col // UNIT_CH
    srow = iota((UNIT_ST, UNIT_K), 0)
    scol = iota((UNIT_ST, UNIT_K), 1)
    st_same_group = srow // S5_STATE == (scol // S5_GROUP) % UNIT_G
    st_col_t = scol // UNIT_CH
    kexp = _hdot(jnp.concatenate([kk_ref[k, d, 0].reshape(UNIT_CH, S5_GROUP)
                                  for d in range(2) for k in range(CH_T)], axis=0), rep16)
    pexp = _hdot(jnp.concatenate([pb_ref[ri, r, d, 0].reshape(UNIT_CH, S5_STATE)
                                  for d in range(2) for ri in range(2) for r in range(CH_T)], axis=0), rep64)
    cexp = _hdot(jnp.concatenate([cp_ref[ri, rr, d, 0].reshape(UNIT_ST, S5_GROUP)
                                  for d in range(2) for ri in range(2) for rr in range(CH_T)], axis=0), rep16)

    def blk(a, idx, nrows):
        return a[idx * nrows:(idx + 1) * nrows]

    rows = []
    for j in range(CH_T):
        acc = jnp.where((col_t == j) & on_diag, d_ref[0], 0.0)
        for d in range(2):
            for k in range(CH_T):
                lag_ok = (col_t - j == k) if d == 0 else (j - col_t == k)
                acc = acc + jnp.where(lag_ok & same_group_out, blk(kexp, d * CH_T + k, UNIT_CH), 0.0)
        rows.append(acc)
    parts = [jnp.concatenate(rows, axis=0)]
    cos = []
    for d in range(2):
        rows = []
        for j in range(CH_T):
            r = CH_T - 1 - j if d == 0 else j
            rows.append(jnp.concatenate(
                [jnp.where(same_group_st, blk(pexp, (d * 2 + ri) * CH_T + r, UNIT_CH), 0.0) for ri in range(2)],
                axis=-1))
        parts.append(jnp.concatenate(rows, axis=0))
        for ri in range(2):
            acc = jnp.zeros((UNIT_ST, UNIT_K), F32)
            for rr in range(CH_T):
                t = rr if d == 0 else CH_T - 1 - rr
                acc = acc + jnp.where((st_col_t == t) & st_same_group,
                                      blk(cexp, (d * 2 + ri) * CH_T + rr, UNIT_ST), 0.0)
            cos.append(acc)
    return jnp.concatenate(parts, axis=-1).astype(BF16), jnp.concatenate(cos, axis=0).astype(BF16)


def _s5_core_kernel(x_ref, kk_ref, pb_ref, cp_ref, d_ref, lam_ref, y_ref, st_scr, sp_scr, yi_scr):
    w1, co = _unit_operators(kk_ref, pb_ref, cp_ref, d_ref)
    nlat = NCH - NCH_CTX
    for b in range(BATCH):
        r = _dot(x_ref[b], w1)
        y_ref[b] = r[:nlat, :UNIT_K]
        for lt in range(STATE_TILES):
            c0 = UNIT_K + lt * LANE
            st_scr[lt, pl.ds(b, NCH, stride=BATCH), :] = r[:, c0:c0 + LANE]
    lam = [lam_ref[0, lt] for lt in range(STATE_TILES)]

    def rows(chunk):
        return pl.ds(pl.multiple_of(chunk * BATCH, BATCH), BATCH)

    def load_z(row, base):
        return [st_scr[base + k, rows(row), :] for k in range(4)]

    def advance(state, z, base):
        ar0, ar1, ai0, ai1 = lam[base:base + 4]
        return [ar0 * state[0] - ai0 * state[2] + z[0], ar1 * state[1] - ai1 * state[3] + z[1],
                ar0 * state[2] + ai0 * state[0] + z[2], ar1 * state[3] + ai1 * state[1] + z[3]]

    def keep(lo, hi, chunk_lo, base):
        r16 = pl.ds(pl.multiple_of(chunk_lo * BATCH, 2 * BATCH), 2 * BATCH)
        for k in range(4):
            sp_scr[base + k, r16, :] = jnp.concatenate([lo[k], hi[k]], axis=0).astype(BF16)

    def fwd_row(i):
        return jnp.where(i < NCH_CTX, nlat + i, i - NCH_CTX)

    def two_steps(m, carry):
        s_f, z_f, s_b, z_b = carry
        i = 2 * m
        rf = fwd_row(i)
        rb = NCH - 1 - i
        z_f1 = load_z(rf + 1, 0)
        z_b1 = load_z(rb - 1, 4)
        nxt = jnp.minimum(i + 2, NCH - 2)
        z_f2 = load_z(fwd_row(nxt), 0)
        z_b2 = load_z(NCH - 1 - nxt, 4)
        s_f1 = advance(s_f, z_f, 0)
        s_b1 = advance(s_b, z_b, 4)
        keep(s_f, s_f1, rf, 0)
        keep(s_b1, s_b, rb - 1, 4)
        return advance(s_f1, z_f1, 0), z_f2, advance(s_b1, z_b1, 4), z_b2

    zero = [jnp.zeros((BATCH, LANE), F32)] * 4
    lax.fori_loop(0, NCH // 2, two_steps, (zero, load_z(nlat, 0), zero, load_z(NCH - 1, 4)))
    for rb in range(BATCH):
        sl = slice(rb * nlat, (rb + 1) * nlat)
        lhs = jnp.concatenate([sp_scr[lt, sl, :] for lt in range(STATE_TILES)], axis=-1)
        yi = _dot(lhs, co)
        for t in range(UNIT_K // LANE):
            yi_scr[t, sl, :] = yi[:, t * LANE:(t + 1) * LANE]
    for b in range(BATCH):
        y_ref[b] = y_ref[b] + jnp.concatenate(
            [yi_scr[t, pl.ds(b, nlat, stride=BATCH), :] for t in range(UNIT_K // LANE)], axis=-1)


def _s5_core(xu, kk, pb, cp, d, lam):
    return pl.pallas_call(
        _s5_core_kernel,
        grid=(UNITS,),
        in_specs=[
            pl.BlockSpec((BATCH, NCH, UNIT_K), lambda q: (0, 0, q)),
            pl.BlockSpec((CH_T, 2, 1, UNIT_G, S5_GROUP, S5_GROUP), lambda q: (0, 0, q, 0, 0, 0)),
            pl.BlockSpec((2, CH_T, 2, 1, UNIT_G, S5_GROUP, S5_STATE), lambda q: (0, 0, 0, q, 0, 0, 0)),
            pl.BlockSpec((2, CH_T, 2, 1, UNIT_G, S5_STATE, S5_GROUP), lambda q: (0, 0, 0, q, 0, 0, 0)),
            pl.BlockSpec((1, UNIT_CH, 1), lambda q: (q, 0, 0)),
            pl.BlockSpec((1, STATE_TILES, SUB, LANE), lambda q: (q, 0, 0, 0)),
        ],
        out_specs=pl.BlockSpec((BATCH, NCH - NCH_CTX, UNIT_K), lambda q: (0, 0, q)),
        out_shape=jax.ShapeDtypeStruct((BATCH, NCH - NCH_CTX, UNITS * UNIT_K), F32),
        scratch_shapes=[pltpu.VMEM((STATE_TILES, BATCH * NCH, LANE), F32),
                        pltpu.VMEM((STATE_TILES, BATCH * NCH, LANE), BF16),
                        pltpu.VMEM((UNIT_K // LANE, BATCH * NCH, LANE), F32)],
        compiler_params=_params(("arbitrary",)),
        name="s5_core",
    )(xu, kk, pb, cp, d, lam)


def _fin_kernel(y_ref, sz_ref, x_ref, mod_ref, wglu_ref, bglu_ref, wout_ref, fg_ref, o_ref, tok_scr):
    for m in range(S5_WIDTH // LANE):
        for hf in range(CH_T // 2):
            c0 = 2 * m * UNIT_K + hf * LANE
            va, vb = _swap_halves(y_ref[0, :, c0:c0 + LANE], y_ref[0, :, c0 + UNIT_K:c0 + UNIT_K + LANE])
            tok_scr[m, pl.ds(2 * hf, TF // CH_T, stride=CH_T), :] = va
            tok_scr[m, pl.ds(2 * hf + 1, TF // CH_T, stride=CH_T), :] = vb
    y = jnp.concatenate([tok_scr[m] for m in range(S5_WIDTH // LANE)], axis=-1)
    y = jax.nn.gelu(y)
    y = y * jax.nn.sigmoid(_dot(y.astype(BF16), wglu_ref[...]) + bglu_ref[...])
    a = (y * sz_ref[0].astype(F32)).astype(BF16)
    gt = mod_ref[0][:, 2 * D_MODEL:]
    x2 = x_ref[0] + gt * _dot(a, wout_ref[...])
    o_ref[0] = _rms(x2, fg_ref[...])


def _finish(y, sz1, x1, mod1, wglu, bglu, wout, fg):
    full = lambda shape: pl.BlockSpec(shape, lambda b, i: (0,) * len(shape))
    tok = pl.BlockSpec((1, TF, D_MODEL), lambda b, i: (b, i, 0))
    return pl.pallas_call(
        _fin_kernel,
        grid=(BATCH, SEQ // TF),
        in_specs=[pl.BlockSpec((1, TF // CH_T, UNITS * UNIT_K), lambda b, i: (b, i, 0)), tok, tok,
                  _mod_spec(True),
                  full((S5_WIDTH, S5_WIDTH)), full((1, S5_WIDTH)),
                  full((S5_WIDTH, D_MODEL)), full((1, D_MODEL))],
        out_specs=tok,
        out_shape=jax.ShapeDtypeStruct((BATCH, SEQ, D_MODEL), F32),
        scratch_shapes=[pltpu.VMEM((S5_WIDTH // LANE, TF, LANE), F32)],
        compiler_params=_params(("arbitrary", "arbitrary")),
        name="s5_finish",
    )(y, sz1, x1, mod1, wglu, bglu, wout, fg)


def _rope_tables():
    h = QK_ROPE_DIM // 2
    inv = 1.0 / (ROPE_THETA ** (np.arange(0, h, 2, dtype=np.float64) / h))
    pos = np.arange(SEQ)
    ang_r = (pos // GRID_W)[:, None] * inv[None, :]
    ang_c = (pos % GRID_W)[:, None] * inv[None, :]
    cos32 = np.concatenate([np.cos(ang_r)] * 2 + [np.cos(ang_c)] * 2, axis=-1)
    sin32 = np.concatenate([np.sin(ang_r)] * 2 + [np.sin(ang_c)] * 2, axis=-1)
    cos = np.zeros((TOK, HEAD_PAD), np.float32)
    sin = np.zeros((TOK, HEAD_PAD), np.float32)
    kt = np.zeros((TOK, HEAD_PAD), np.float32)
    cos[:, :QK_NOPE_DIM] = 1.0
    cos[SEQ:, QK_NOPE_DIM:QK_DIM] = 1.0
    kt[SEQ:, :QK_ROPE_DIM] = 1.0
    cos[:SEQ, QK_NOPE_DIM:QK_DIM] = cos32
    sin[:SEQ, QK_NOPE_DIM:QK_DIM] = sin32
    kt[:SEQ, :QK_ROPE_DIM] = cos32
    kt[:SEQ, QK_ROPE_DIM:2 * QK_ROPE_DIM] = sin32
    return jnp.asarray(cos), jnp.asarray(sin), jnp.asarray(kt)


def _mla_selectors():
    def partner(d):
        return (d + 8, -1.0) if d % 16 < 8 else (d - 8, 1.0)

    o2 = Q_LORA_RANK + KV_LORA_RANK
    o3 = o2 + QK_ROPE_DIM
    pin = np.zeros((o3 + MLA_WIDTH, PROJ_W), np.float32)
    pin[np.arange(o3), np.arange(o3)] = 1.0
    pin[o3 + np.arange(MLA_WIDTH), 512 + np.arange(MLA_WIDTH)] = 1.0
    pa = np.zeros((MLA_HEADS * QK_DIM, QK_PAD), np.float32)
    pb = np.zeros((MLA_HEADS * QK_DIM, QK_PAD), np.float32)
    pk = np.zeros((MLA_HEADS * 128, QK_PAD), np.float32)
    pv = np.zeros((MLA_HEADS * 128, MLA_WIDTH), np.float32)
    kb = np.zeros((128, QK_PAD), np.float32)
    for d in range(QK_ROPE_DIM):
        src, sign = partner(d)
        pin[o2 + src, o3 + d] = sign
        for hd in range(MLA_HEADS):
            pb[hd * QK_DIM + QK_NOPE_DIM + src, hd * HEAD_PAD + QK_NOPE_DIM + d] = sign
            kb[d, hd * HEAD_PAD + QK_NOPE_DIM + d] = 1.0
            kb[QK_ROPE_DIM + d, hd * HEAD_PAD + QK_NOPE_DIM + d] = 1.0
    for hd in range(MLA_HEADS):
        pa[hd * QK_DIM + np.arange(QK_DIM), hd * HEAD_PAD + np.arange(QK_DIM)] = 1.0
        pk[hd * 128 + np.arange(QK_NOPE_DIM), hd * HEAD_PAD + np.arange(QK_NOPE_DIM)] = 1.0
        pv[hd * 128 + QK_NOPE_DIM + np.arange(V_HEAD_DIM), hd * V_HEAD_DIM + np.arange(V_HEAD_DIM)] = 1.0
    return [jnp.asarray(a, dtype=BF16) for a in (pin, pa, pb, pk, pv, kb)]


def _mla_wprep_kernel(win_ref, wuq_ref, wukv_ref, pin_ref, pa_ref, pb_ref, pk_ref, pv_ref, kb_ref,
                      o_in, o_qa, o_qb, o_k, o_v):
    o_in[...] = _dot(win_ref[...].astype(BF16), pin_ref[...]).astype(BF16)
    wq = (wuq_ref[...] * (SOFTMAX_SCALE * math.log2(math.e))).astype(BF16)
    o_qa[...] = _dot(wq, pa_ref[...]).astype(BF16)
    o_qb[...] = _dot(wq, pb_ref[...]).astype(BF16)
    wkv = wukv_ref[...].astype(BF16)
    o_k[:KV_LORA_RANK] = _dot(wkv, pk_ref[...]).astype(BF16)
    o_k[KV_LORA_RANK:] = kb_ref[...]
    o_v[...] = _dot(wkv, pv_ref[...]).astype(BF16)


def _mla_weights(w_in, w_uq, w_ukv):
    nj = 4
    full = lambda a: pl.BlockSpec(a.shape, lambda j: (0, 0))
    cols = lambda rows, width: pl.BlockSpec((rows, width // nj), lambda j: (0, j))
    sel = _mla_selectors()
    widths = (PROJ_W, QK_PAD, QK_PAD, QK_PAD, MLA_WIDTH, QK_PAD)
    out_rows = (D_MODEL, Q_LORA_RANK, Q_LORA_RANK, 256, KV_LORA_RANK)
    return pl.pallas_call(
        _mla_wprep_kernel,
        grid=(nj,),
        in_specs=[full(w_in), full(w_uq), full(w_ukv)] + [cols(a.shape[0], w) for a, w in zip(sel, widths)],
        out_specs=[cols(r, w) for r, w in zip(out_rows, widths)],
        out_shape=[jax.ShapeDtypeStruct((r, w), BF16) for r, w in zip(out_rows, widths)],
        compiler_params=_params(("arbitrary",)),
        name="mla_weight_prep",
    )(w_in, w_uq, w_ukv, *sel)


def _lam_tiles(lam):
    lam = lam.reshape(2, 2, UNITS, UNIT_ST // LANE, LANE)
    lam = jnp.concatenate([lam[0, 0], lam[1, 0], lam[0, 1], lam[1, 1]], axis=1)
    return jnp.broadcast_to(lam[:, :, None, :], (UNITS, STATE_TILES, SUB, LANE))


def kernel(x, c, ctx, c_ctx, ada_w, ada_b, norm_g, mla_w_in, mla_q_norm, mla_w_uq, mla_kv_norm, mla_w_ukv, mla_w_out, s5_w_in, s5_a_re, s5_a_im, s5_log_step, s5_b_re, s5_b_im, s5_c_re, s5_c_im, s5_d, s5_w_glu, s5_b_glu, s5_w_out, final_g):
    cc = jnp.concatenate([c, c_ctx[None, :], jnp.zeros((7, D_MODEL), F32)], axis=0)
    mods = _modulation(cc, ada_w, ada_b)

    mod_lat = [mods[i, :BATCH, None, :] for i in range(2)]
    mod_ctx = [mods[i, BATCH:BATCH + 1, None, :] for i in range(2)]

    win, wqa, wqb, wk, wv = _mla_weights(mla_w_in[0], mla_w_uq[0], mla_w_ukv[0])
    weights = (norm_g[0][None, :], win, mla_q_norm[0][None, :], mla_kv_norm[0][None, :], wqa, wqb, wk, wv)
    tables = _rope_tables()
    qkvz = _mla_proj(x, mod_lat[0], weights, tables)
    q, k, v, sz = _mla_proj(ctx, mod_ctx[0], weights, tables, filled=qkvz)
    o = _attention(q, k, v)
    out_w = (norm_g[1][None, :], mla_w_out[0].astype(BF16), s5_w_in[0].astype(BF16))
    x1, xu, sz1 = _mla_out(o, sz, x, mod_lat[0], mod_lat[1], *out_w)
    xu, = _mla_out(o, sz, ctx, mod_ctx[0], mod_ctx[1], *out_w, xu_filled=xu)
    mod1 = mod_lat[1]

    n = 2 * S5_GROUPS
    lam, pb, cp, kk = _s5_prep(
        s5_a_re[0].reshape(n, S5_STATE), s5_a_im[0].reshape(n, S5_STATE), s5_log_step[0].reshape(n, 1),
        jnp.swapaxes(s5_b_re[0], -1, -2).reshape(n, S5_GROUP, S5_STATE),
        jnp.swapaxes(s5_b_im[0], -1, -2).reshape(n, S5_GROUP, S5_STATE),
        s5_c_re[0].reshape(n, S5_GROUP, S5_STATE), s5_c_im[0].reshape(n, S5_GROUP, S5_STATE))
    y = _s5_core(xu, kk.reshape(CH_T, 2, UNITS, UNIT_G, S5_GROUP, S5_GROUP),
                 pb.reshape(2, CH_T, 2, UNITS, UNIT_G, S5_GROUP, S5_STATE),
                 cp.reshape(2, CH_T, 2, UNITS, UNIT_G, S5_STATE, S5_GROUP),
                 s5_d[0].reshape(UNITS, UNIT_CH, 1), _lam_tiles(lam))
    return _finish(y, sz1, x1, mod1, s5_w_glu[0].astype(BF16), s5_b_glu[0][None, :], s5_w_out[0].astype(BF16),
                   final_g[None, :])
```

```python
import functools
import math

import jax
import jax.numpy as jnp
import numpy as np
from jax import lax
from jax.experimental import pallas as pl
from jax.experimental.pallas import tpu as pltpu

D_MODEL = 1024
BATCH = 8
SEQ = 2048
GRID_W = 64
CTX_LEN = 256
TOK = CTX_LEN + SEQ
EPS = 1e-6

MLA_HEADS = 16
QK_NOPE_DIM = 64
QK_ROPE_DIM = 32
V_HEAD_DIM = 64
Q_LORA_RANK = 256
KV_LORA_RANK = 128
MLA_WIDTH = MLA_HEADS * V_HEAD_DIM
QK_DIM = QK_NOPE_DIM + QK_ROPE_DIM
SOFTMAX_SCALE = QK_DIM ** -0.5
ROPE_THETA = 10000.0
HEAD_PAD = 128
QK_PAD = MLA_HEADS * HEAD_PAD
PROJ_W = 1536

S5_WIDTH = D_MODEL
S5_GROUP = 16
S5_GROUPS = 64
S5_STATE = 64
CH_T = 4
UNIT_G = 4
UNIT_CH = UNIT_G * S5_GROUP
UNITS = S5_GROUPS // UNIT_G
UNIT_K = CH_T * UNIT_CH
UNIT_ST = UNIT_G * S5_STATE
NCH = TOK // CH_T
NCH_CTX = CTX_LEN // CH_T
LANE = 128
SUB = 8

TL = 512
CTX_NB = 2
TF = 1024
TQ = 256
KCH = 256
HPAIRS = 2
assert TQ == CTX_LEN
VMEM_LIMIT = 56 * 1024 * 1024

F32 = jnp.float32
BF16 = jnp.bfloat16


def _params(sem, flags=None):
    return pltpu.CompilerParams(dimension_semantics=sem, vmem_limit_bytes=VMEM_LIMIT, flags=flags)


def _silu(v):
    return v * jax.nn.sigmoid(v)


def _rms(v, g):
    return v * lax.rsqrt(jnp.mean(v * v, axis=-1, keepdims=True) + EPS) * g


def _dot(a, b):
    return jnp.dot(a, b, preferred_element_type=F32)


def _mod_kernel(cc_ref, w_ref, b_ref, o_ref):
    a = _silu(cc_ref[...]).astype(BF16)
    o_ref[0] = _dot(a, w_ref[0].astype(BF16)) + b_ref[0]


def _modulation(cc, ada_w, ada_b):
    depth = ada_w.shape[0]
    tn = 768
    return pl.pallas_call(
        _mod_kernel,
        grid=(depth, 3 * D_MODEL // tn),
        in_specs=[
            pl.BlockSpec((16, D_MODEL), lambda i, j: (0, 0)),
            pl.BlockSpec((1, D_MODEL, tn), lambda i, j: (i, 0, j)),
            pl.BlockSpec((1, 1, tn), lambda i, j: (i, 0, j)),
        ],
        out_specs=pl.BlockSpec((1, 16, tn), lambda i, j: (i, 0, j)),
        out_shape=jax.ShapeDtypeStruct((depth, 16, 3 * D_MODEL), F32),
        compiler_params=_params(("arbitrary", "arbitrary")),
        name="modulation",
    )(cc, ada_w, ada_b.reshape(depth, 1, 3 * D_MODEL))


def _mod_spec(per_batch):
    return pl.BlockSpec((1, 1, 3 * D_MODEL), (lambda b, i: (b, 0, 0)) if per_batch else (lambda b, i: (0, 0, 0)))


def _mla_proj_kernel(x_ref, mod_ref, g_ref, win_ref, qg_ref, kvg_ref, wqa_ref, wqb_ref, wk_ref, wv_ref,
                     cos_ref, sin_ref, kt_ref, *refs):
    q_ref, k_ref, v_ref, sz_ref = refs[-4:]
    nb, tile = x_ref.shape[0], x_ref.shape[1]
    x = x_ref[...].reshape(nb * tile, D_MODEL)
    mod = mod_ref[0]
    sh = mod[:, :D_MODEL]
    sc = mod[:, D_MODEL:2 * D_MODEL]
    h = _rms(x, g_ref[...]) * (1.0 + sc) + sh
    p = _dot(h.astype(BF16), win_ref[...])
    cqn = _rms(p[:, :Q_LORA_RANK], qg_ref[...]).astype(BF16)
    ckvn = _rms(p[:, Q_LORA_RANK:Q_LORA_RANK + KV_LORA_RANK], kvg_ref[...]).astype(BF16)
    kr = p[:, 384:512]
    z = p[:, 512:]
    qa = _dot(cqn, wqa_ref[...])
    qb = _dot(cqn, wqb_ref[...])
    cos, sin, kt = (jnp.concatenate([t[...]] * nb, axis=0) for t in (cos_ref, sin_ref, kt_ref))
    for hd in range(MLA_HEADS):
        sl = slice(hd * HEAD_PAD, (hd + 1) * HEAD_PAD)
        q_ref[:, :, sl] = (qa[:, sl] * cos + qb[:, sl] * sin).astype(BF16).reshape(nb, tile, HEAD_PAD)
    kin = jnp.concatenate([ckvn, (kr * kt).astype(BF16)], axis=-1)
    k_ref[...] = _dot(kin, wk_ref[...]).astype(BF16).reshape(nb, tile, QK_PAD)
    v_ref[...] = _dot(ckvn, wv_ref[...]).astype(BF16).reshape(nb, tile, MLA_WIDTH)
    sz_ref[...] = _silu(z).astype(BF16).reshape(nb, tile, MLA_WIDTH)


def _mla_proj(xs, mod, weights, tables, filled=None):
    is_ctx = filled is not None
    tile, nb = (CTX_LEN, CTX_NB) if is_ctx else (TL, 1)
    off = SEQ // tile if is_ctx else 0
    full = lambda a: pl.BlockSpec(a.shape, lambda b, i: (0,) * a.ndim)
    tok = lambda w: pl.BlockSpec((nb, tile, w), lambda b, i: (b, i + off, 0))
    pos = pl.BlockSpec((tile, HEAD_PAD), lambda b, i: (i + off, 0))
    widths = (QK_PAD, QK_PAD, MLA_WIDTH, MLA_WIDTH)
    in_specs = [pl.BlockSpec((nb, tile, D_MODEL), lambda b, i: (b, i, 0)), _mod_spec(not is_ctx)]
    in_specs += [full(w) for w in weights] + [pos, pos, pos]
    args = [xs, mod, *weights, *tables]
    aliases = {}
    if is_ctx:
        aliases = {len(args) + n: n for n in range(4)}
        in_specs += [pl.BlockSpec(memory_space=pl.ANY)] * 4
        args += list(filled)
    return pl.pallas_call(
        _mla_proj_kernel,
        grid=(BATCH // nb, xs.shape[1] // tile),
        in_specs=in_specs,
        out_specs=[tok(w) for w in widths],
        out_shape=[jax.ShapeDtypeStruct((BATCH, TOK, w), BF16) for w in widths],
        input_output_aliases=aliases,
        compiler_params=_params(("arbitrary", "arbitrary")),
        name="mla_proj_ctx" if is_ctx else "mla_proj",
    )(*args)


def _attn_kernel(q_ref, k_ref, v_ref, o_ref, s_buf, m_buf, vx_buf, cs_buf, cm_buf):
    nt = SEQ // TQ
    lane = lax.broadcasted_iota(jnp.int32, (TOK, 2 * V_HEAD_DIM), 1)
    for hp in range(HPAIRS):
        v = v_ref[0, :, hp * 2 * V_HEAD_DIM:(hp + 1) * 2 * V_HEAD_DIM]
        vx_buf[hp, 0] = jnp.where(lane < V_HEAD_DIM, v, (lane == V_HEAD_DIM).astype(BF16))
        vx_buf[hp, 1] = jnp.where(lane >= V_HEAD_DIM, v, (lane == 0).astype(BF16))

    def scores(hp, row, k0, nk, slot):
        sb, mb = (cs_buf.at[hp], cm_buf.at[hp]) if slot is None else (s_buf.at[slot], m_buf.at[slot])
        for hh in range(2):
            c0 = (2 * hp + hh) * HEAD_PAD
            s = lax.dot_general(q_ref[0, pl.ds(row, TQ), c0:c0 + HEAD_PAD], k_ref[0, k0:k0 + nk, c0:c0 + HEAD_PAD],
                                (((1,), (1,)), ((), ())), preferred_element_type=F32)
            sb[hh, :, :nk] = s
            mb[hh] = jnp.broadcast_to(jnp.max(s, axis=-1, keepdims=True), (TQ, KCH))

    def values(hp, row, k0, nk, slot):
        sb, mb = (cs_buf.at[hp], cm_buf.at[hp]) if slot is None else (s_buf.at[slot], m_buf.at[slot])
        outs = []
        for hh in range(2):
            m = mb[hh]
            ps = [jnp.exp2(sb[hh, :, n * KCH:(n + 1) * KCH] - m).astype(BF16) for n in range(nk // KCH)]
            acc = _dot(jnp.concatenate(ps, axis=-1), vx_buf[hp, hh, k0:k0 + nk, :])
            l_col = V_HEAD_DIM if hh == 0 else 0
            outs.append(acc / acc[:, l_col:l_col + 1])
        olane = lax.broadcasted_iota(jnp.int32, outs[0].shape, 1)
        o_ref[0, pl.ds(row, TQ), hp * 2 * V_HEAD_DIM:(hp + 1) * 2 * V_HEAD_DIM] = jnp.where(
            olane < V_HEAD_DIM, outs[0], outs[1]).astype(BF16)

    for hp in range(HPAIRS):
        scores(hp, SEQ, SEQ, CTX_LEN, None)
    scores(0, 0, 0, TOK, 0)
    for hp in range(HPAIRS):
        values(hp, SEQ, SEQ, CTX_LEN, None)
    for hp in range(HPAIRS):
        for t in range(1, nt):
            scores(hp, t * TQ, 0, TOK, t % 2)
            values(hp, (t - 1) * TQ, 0, TOK, (t - 1) % 2)
        if hp + 1 < HPAIRS:
            scores(hp + 1, 0, 0, TOK, 0)
        values(hp, (nt - 1) * TQ, 0, TOK, (nt - 1) % 2)


def _attention(q, k, v):
    qk = pl.BlockSpec((1, TOK, HPAIRS * 2 * HEAD_PAD), lambda b, h: (b, 0, h))
    vo = pl.BlockSpec((1, TOK, HPAIRS * 2 * V_HEAD_DIM), lambda b, h: (b, 0, h))
    return pl.pallas_call(
        _attn_kernel,
        grid=(BATCH, MLA_HEADS // (2 * HPAIRS)),
        in_specs=[qk, qk, vo],
        out_specs=vo,
        out_shape=jax.ShapeDtypeStruct((BATCH, TOK, MLA_WIDTH), BF16),
        scratch_shapes=[
            pltpu.VMEM((2, 2, TQ, TOK), F32),
            pltpu.VMEM((2, 2, TQ, KCH), F32),
            pltpu.VMEM((HPAIRS, 2, TOK, 2 * V_HEAD_DIM), BF16),
            pltpu.VMEM((HPAIRS, 2, TQ, CTX_LEN), F32),
            pltpu.VMEM((HPAIRS, 2, TQ, KCH), F32),
        ],
        compiler_params=_params(("arbitrary", "arbitrary")),
        name="attention",
    )(q, k, v)


def _swap_halves(va, vb):
    lo = lax.broadcasted_iota(jnp.int32, va.shape, 1) < UNIT_CH
    return (jnp.where(lo, va, pltpu.roll(vb, UNIT_CH, 1)),
            jnp.where(lo, pltpu.roll(va, UNIT_CH, 1), vb))


def _mla_out_kernel(o_ref, sz_ref, x_ref, mod0_ref, mod1_ref, g1_ref, wout_ref, win_ref, *refs, is_ctx):
    if is_ctx:
        xu_ref, tok_scr = refs[-2:]
    else:
        x1_ref, xu_ref, sz1_ref, tok_scr = refs
    nb, tile = x_ref.shape[0], x_ref.shape[1]
    rows = nb * tile
    a = (o_ref[...].astype(F32) * sz_ref[...].astype(F32)).astype(BF16).reshape(rows, MLA_WIDTH)
    gt = mod0_ref[0][:, 2 * D_MODEL:]
    x1 = x_ref[...].reshape(rows, D_MODEL) + gt * _dot(a, wout_ref[...])
    mod1 = mod1_ref[0]
    h = _rms(x1, g1_ref[...]) * (1.0 + mod1[:, D_MODEL:2 * D_MODEL]) + mod1[:, :D_MODEL]
    if is_ctx:
        u = _dot(h.astype(BF16), win_ref[:, :S5_WIDTH])
    else:
        x1_ref[0] = x1
        p = _dot(h.astype(BF16), win_ref[...])
        sz1_ref[0] = _silu(p[:, S5_WIDTH:]).astype(BF16)
        u = p[:, :S5_WIDTH]
    for m in range(S5_WIDTH // LANE):
        tok_scr[m] = u[:, m * LANE:(m + 1) * LANE]
    for m in range(S5_WIDTH // LANE):
        v = [tok_scr[m, pl.ds(t, rows // CH_T, stride=CH_T), :] for t in range(CH_T)]
        for hf in range(CH_T // 2):
            even, odd = _swap_halves(v[2 * hf], v[2 * hf + 1])
            c0 = 2 * m * UNIT_K + hf * LANE
            xu_ref[:, :, c0:c0 + LANE] = even.astype(BF16).reshape(nb, tile // CH_T, LANE)
            xu_ref[:, :, c0 + UNIT_K:c0 + UNIT_K + LANE] = odd.astype(BF16).reshape(nb, tile // CH_T, LANE)


def _mla_out(o, sz, xs, mod0, mod1, g1, wout, win, xu_filled=None):
    is_ctx = xu_filled is not None
    tile, nb = (CTX_LEN, CTX_NB) if is_ctx else (TL, 1)
    off = SEQ // tile if is_ctx else 0
    full = lambda a: pl.BlockSpec(a.shape, lambda b, i: (0,) * a.ndim)
    shared = lambda w: pl.BlockSpec((nb, tile, w), lambda b, i: (b, i + off, 0))
    own = lambda w: pl.BlockSpec((nb, tile, w), lambda b, i: (b, i, 0))
    xu_spec = pl.BlockSpec((nb, tile // CH_T, UNITS * UNIT_K), lambda b, i: (b, i + off, 0))
    xu_shape = jax.ShapeDtypeStruct((BATCH, NCH, UNITS * UNIT_K), BF16)
    in_specs = [shared(MLA_WIDTH), shared(MLA_WIDTH), own(D_MODEL), _mod_spec(not is_ctx), _mod_spec(not is_ctx),
                full(g1), full(wout), full(win)]
    args = [o, sz, xs, mod0, mod1, g1, wout, win]
    if is_ctx:
        in_specs.append(pl.BlockSpec(memory_space=pl.ANY))
        args.append(xu_filled)
        out_specs, out_shape, aliases = [xu_spec], [xu_shape], {len(args) - 1: 0}
    else:
        out_specs = [own(D_MODEL), xu_spec, own(S5_WIDTH)]
        out_shape = [jax.ShapeDtypeStruct((BATCH, SEQ, D_MODEL), F32), xu_shape,
                     jax.ShapeDtypeStruct((BATCH, SEQ, S5_WIDTH), BF16)]
        aliases = {}
    return pl.pallas_call(
        functools.partial(_mla_out_kernel, is_ctx=is_ctx),
        grid=(BATCH // nb, xs.shape[1] // tile),
        in_specs=in_specs,
        out_specs=out_specs,
        out_shape=out_shape,
        input_output_aliases=aliases,
        scratch_shapes=[pltpu.VMEM((S5_WIDTH // LANE, nb * tile, LANE), F32)],
        compiler_params=_params(("arbitrary", "arbitrary")),
        name="mla_out_s5_in_ctx" if is_ctx else "mla_out_s5_in",
    )(*args)


def _cmul(ar, ai, br, bi):
    return ar * br - ai * bi, ar * bi + ai * br


def _group_dot(a, b, precision=lax.Precision.HIGHEST):
    return lax.dot_general(a, b, (((2,), (2,)), ((0,), (0,))), precision=precision, preferred_element_type=F32)


def _group_transpose(eye, a):
    return _group_dot(eye, a.astype(BF16), precision=None)


def _s5_prep_kernel(are_ref, aim_ref, ls_ref, bre_ref, bim_ref, cre_ref, cim_ref, lam_ref, pb_ref, cp_ref, kk_ref):
    n = are_ref.shape[0]
    eye = (lax.broadcasted_iota(jnp.int32, (n, S5_STATE, S5_STATE), 1)
           == lax.broadcasted_iota(jnp.int32, (n, S5_STATE, S5_STATE), 2)).astype(BF16)
    ar = are_ref[...]
    ai = aim_ref[...]
    dt = jnp.exp(ls_ref[...])
    mag = jnp.exp(ar * dt)
    lb_re = mag * jnp.cos(ai * dt)
    lb_im = mag * jnp.sin(ai * dt)
    den = ar * ar + ai * ai
    nr = lb_re - 1.0
    f_re = ((nr * ar + lb_im * ai) / den)[:, None, :]
    f_im = ((lb_im * ar - nr * ai) / den)[:, None, :]
    bb_re, bb_im = _cmul(f_re, f_im, bre_ref[...], bim_ref[...])
    c_re = cre_ref[...]
    c_im = cim_ref[...]
    pw_re = jnp.ones_like(lb_re)
    pw_im = jnp.zeros_like(lb_re)
    for r in range(CH_T + 1):
        pr = pw_re[:, None, :]
        pi = pw_im[:, None, :]
        cl_re, cl_im = _cmul(c_re, c_im, pr, pi)
        if r < CH_T:
            q_re, q_im = _cmul(pr, pi, bb_re, bb_im)
            pb_ref[0, r] = q_re
            pb_ref[1, r] = q_im
            kk_ref[r] = _group_dot(bb_re, cl_re) - _group_dot(bb_im, cl_im)
        if r > 0:
            cp_ref[0, r - 1] = _group_transpose(eye, cl_re)
            cp_ref[1, r - 1] = _group_transpose(eye, -cl_im)
        if r == CH_T:
            lam_ref[0] = pw_re
            lam_ref[1] = pw_im
        else:
            pw_re, pw_im = _cmul(pw_re, pw_im, lb_re, lb_im)


def _s5_prep(a_re, a_im, log_step, b_re_t, b_im_t, c_re, c_im):
    n = a_re.shape[0]
    nb = 16
    row2 = pl.BlockSpec((nb, S5_STATE), lambda i: (i, 0))
    row3 = pl.BlockSpec((nb, S5_GROUP, S5_STATE), lambda i: (i, 0, 0))
    return pl.pallas_call(
        _s5_prep_kernel,
        grid=(n // nb,),
        in_specs=[row2, row2, pl.BlockSpec((nb, 1), lambda i: (i, 0)), row3, row3, row3, row3],
        out_specs=[
            pl.BlockSpec((2, nb, S5_STATE), lambda i: (0, i, 0)),
            pl.BlockSpec((2, CH_T, nb, S5_GROUP, S5_STATE), lambda i: (0, 0, i, 0, 0)),
            pl.BlockSpec((2, CH_T, nb, S5_STATE, S5_GROUP), lambda i: (0, 0, i, 0, 0)),
            pl.BlockSpec((CH_T, nb, S5_GROUP, S5_GROUP), lambda i: (0, i, 0, 0)),
        ],
        out_shape=[
            jax.ShapeDtypeStruct((2, n, S5_STATE), F32),
            jax.ShapeDtypeStruct((2, CH_T, n, S5_GROUP, S5_STATE), F32),
            jax.ShapeDtypeStruct((2, CH_T, n, S5_STATE, S5_GROUP), F32),
            jax.ShapeDtypeStruct((CH_T, n, S5_GROUP, S5_GROUP), F32),
        ],
        compiler_params=_params(("arbitrary",)),
        name="s5_prep",
    )(a_re, a_im, log_step, b_re_t, b_im_t, c_re, c_im)


STATE_TILES = 2 * 2 * UNIT_ST // LANE


def _hdot(a, rep):
    return _dot(a.astype(BF16), rep)


def _unit_operators(kk_ref, pb_ref, cp_ref, d_ref):
    def iota(shape, dim):
        return lax.broadcasted_iota(jnp.int32, shape, dim)

    rep16 = (iota((S5_GROUP, UNIT_K), 1) % S5_GROUP == iota((S5_GROUP, UNIT_K), 0)).astype(BF16)
    rep64 = (iota((S5_STATE, UNIT_ST), 1) % S5_STATE == iota((S5_STATE, UNIT_ST), 0)).astype(BF16)
    row = iota((UNIT_CH, UNIT_K), 0)
    col = iota((UNIT_CH, UNIT_K), 1)
    same_group_out = row // S5_GROUP == (col // S5_GROUP) % UNIT_G
    same_group_st = row // S5_GROUP == col // S5_STATE
    on_diag = row == col % UNIT_CH
    col_t = col // UNIT_CH
    srow = iota((UNIT_ST, UNIT_K), 0)
    scol = iota((UNIT_ST, UNIT_K), 1)
    st_same_group = srow // S5_STATE == (scol // S5_GROUP) % UNIT_G
    st_col_t = scol // UNIT_CH
    kexp = _hdot(jnp.concatenate([kk_ref[k, d, 0].reshape(UNIT_CH, S5_GROUP)
                                  for d in range(2) for k in range(CH_T)], axis=0), rep16)
    pexp = _hdot(jnp.concatenate([pb_ref[ri, r, d, 0].reshape(UNIT_CH, S5_STATE)
                                  for d in range(2) for ri in range(2) for r in range(CH_T)], axis=0), rep64)
    cexp = _hdot(jnp.concatenate([cp_ref[ri, rr, d, 0].reshape(UNIT_ST, S5_GROUP)
                                  for d in range(2) for ri in range(2) for rr in range(CH_T)], axis=0), rep16)

    def blk(a, idx, nrows):
        return a[idx * nrows:(idx + 1) * nrows]

    rows = []
    for j in range(CH_T):
        acc = jnp.where((col_t == j) & on_diag, d_ref[0], 0.0)
        for d in range(2):
            for k in range(CH_T):
                lag_ok = (col_t - j == k) if d == 0 else (j - col_t == k)
                acc = acc + jnp.where(lag_ok & same_group_out, blk(kexp, d * CH_T + k, UNIT_CH), 0.0)
        rows.append(acc)
    parts = [jnp.concatenate(rows, axis=0)]
    cos = []
    for d in range(2):
        rows = []
        for j in range(CH_T):
            r = CH_T - 1 - j if d == 0 else j
            rows.append(jnp.concatenate(
                [jnp.where(same_group_st, blk(pexp, (d * 2 + ri) * CH_T + r, UNIT_CH), 0.0) for ri in range(2)],
                axis=-1))
        parts.append(jnp.concatenate(rows, axis=0))
        for ri in range(2):
            acc = jnp.zeros((UNIT_ST, UNIT_K), F32)
            for rr in range(CH_T):
                t = rr if d == 0 else CH_T - 1 - rr
                acc = acc + jnp.where((st_col_t == t) & st_same_group,
                                      blk(cexp, (d * 2 + ri) * CH_T + rr, UNIT_ST), 0.0)
            cos.append(acc)
    return jnp.concatenate(parts, axis=-1).astype(BF16), jnp.concatenate(cos, axis=0).astype(BF16)


def _s5_core_kernel(x_ref, kk_ref, pb_ref, cp_ref, d_ref, lam_ref, y_ref, st_scr, sp_scr, yi_scr):
    w1, co = _unit_operators(kk_ref, pb_ref, cp_ref, d_ref)
    nlat = NCH - NCH_CTX
    for b in range(BATCH):
        r = _dot(x_ref[b], w1)
        y_ref[b] = r[:nlat, :UNIT_K]
        for lt in range(STATE_TILES):
            c0 = UNIT_K + lt * LANE
            st_scr[lt, pl.ds(b, NCH, stride=BATCH), :] = r[:, c0:c0 + LANE]
    lam = [lam_ref[0, lt] for lt in range(STATE_TILES)]

    def rows(chunk):
        return pl.ds(pl.multiple_of(chunk * BATCH, BATCH), BATCH)

    def load_z(row, base):
        return [st_scr[base + k, rows(row), :] for k in range(4)]

    def advance(state, z, base):
        ar0, ar1, ai0, ai1 = lam[base:base + 4]
        return [ar0 * state[0] - ai0 * state[2] + z[0], ar1 * state[1] - ai1 * state[3] + z[1],
                ar0 * state[2] + ai0 * state[0] + z[2], ar1 * state[3] + ai1 * state[1] + z[3]]

    def keep(lo, hi, chunk_lo, base):
        r16 = pl.ds(pl.multiple_of(chunk_lo * BATCH, 2 * BATCH), 2 * BATCH)
        for k in range(4):
            sp_scr[base + k, r16, :] = jnp.concatenate([lo[k], hi[k]], axis=0).astype(BF16)

    def fwd_row(i):
        return jnp.where(i < NCH_CTX, nlat + i, i - NCH_CTX)

    def two_steps(m, carry):
        s_f, z_f, s_b, z_b = carry
        i = 2 * m
        rf = fwd_row(i)
        rb = NCH - 1 - i
        z_f1 = load_z(rf + 1, 0)
        z_b1 = load_z(rb - 1, 4)
        nxt = jnp.minimum(i + 2, NCH - 2)
        z_f2 = load_z(fwd_row(nxt), 0)
        z_b2 = load_z(NCH - 1 - nxt, 4)
        s_f1 = advance(s_f, z_f, 0)
        s_b1 = advance(s_b, z_b, 4)
        keep(s_f, s_f1, rf, 0)
        keep(s_b1, s_b, rb - 1, 4)
        return advance(s_f1, z_f1, 0), z_f2, advance(s_b1, z_b1, 4), z_b2

    zero = [jnp.zeros((BATCH, LANE), F32)] * 4
    lax.fori_loop(0, NCH // 2, two_steps, (zero, load_z(nlat, 0), zero, load_z(NCH - 1, 4)))
    for rb in range(BATCH):
        sl = slice(rb * nlat, (rb + 1) * nlat)
        lhs = jnp.concatenate([sp_scr[lt, sl, :] for lt in range(STATE_TILES)], axis=-1)
        yi = _dot(lhs, co)
        for t in range(UNIT_K // LANE):
            yi_scr[t, sl, :] = yi[:, t * LANE:(t + 1) * LANE]
    for b in range(BATCH):
        y_ref[b] = y_ref[b] + jnp.concatenate(
            [yi_scr[t, pl.ds(b, nlat, stride=BATCH), :] for t in range(UNIT_K // LANE)], axis=-1)


def _s5_core(xu, kk, pb, cp, d, lam):
    return pl.pallas_call(
        _s5_core_kernel,
        grid=(UNITS,),
        in_specs=[
            pl.BlockSpec((BATCH, NCH, UNIT_K), lambda q: (0, 0, q)),
            pl.BlockSpec((CH_T, 2, 1, UNIT_G, S5_GROUP, S5_GROUP), lambda q: (0, 0, q, 0, 0, 0)),
            pl.BlockSpec((2, CH_T, 2, 1, UNIT_G, S5_GROUP, S5_STATE), lambda q: (0, 0, 0, q, 0, 0, 0)),
            pl.BlockSpec((2, CH_T, 2, 1, UNIT_G, S5_STATE, S5_GROUP), lambda q: (0, 0, 0, q, 0, 0, 0)),
            pl.BlockSpec((1, UNIT_CH, 1), lambda q: (q, 0, 0)),
            pl.BlockSpec((1, STATE_TILES, SUB, LANE), lambda q: (q, 0, 0, 0)),
        ],
        out_specs=pl.BlockSpec((BATCH, NCH - NCH_CTX, UNIT_K), lambda q: (0, 0, q)),
        out_shape=jax.ShapeDtypeStruct((BATCH, NCH - NCH_CTX, UNITS * UNIT_K), F32),
        scratch_shapes=[pltpu.VMEM((STATE_TILES, BATCH * NCH, LANE), F32),
                        pltpu.VMEM((STATE_TILES, BATCH * NCH, LANE), BF16),
                        pltpu.VMEM((UNIT_K // LANE, BATCH * NCH, LANE), F32)],
        compiler_params=_params(("arbitrary",)),
        name="s5_core",
    )(xu, kk, pb, cp, d, lam)


def _fin_kernel(y_ref, sz_ref, x_ref, mod_ref, wglu_ref, bglu_ref, wout_ref, fg_ref, o_ref, tok_scr):
    for m in range(S5_WIDTH // LANE):
        for hf in range(CH_T // 2):
            c0 = 2 * m * UNIT_K + hf * LANE
            va, vb = _swap_halves(y_ref[0, :, c0:c0 + LANE], y_ref[0, :, c0 + UNIT_K:c0 + UNIT_K + LANE])
            tok_scr[m, pl.ds(2 * hf, TF // CH_T, stride=CH_T), :] = va
            tok_scr[m, pl.ds(2 * hf + 1, TF // CH_T, stride=CH_T), :] = vb
    y = jnp.concatenate([tok_scr[m] for m in range(S5_WIDTH // LANE)], axis=-1)
    y = jax.nn.gelu(y)
    y = y * jax.nn.sigmoid(_dot(y.astype(BF16), wglu_ref[...]) + bglu_ref[...])
    a = (y * sz_ref[0].astype(F32)).astype(BF16)
    gt = mod_ref[0][:, 2 * D_MODEL:]
    x2 = x_ref[0] + gt * _dot(a, wout_ref[...])
    o_ref[0] = _rms(x2, fg_ref[...])


def _finish(y, sz1, x1, mod1, wglu, bglu, wout, fg):
    full = lambda shape: pl.BlockSpec(shape, lambda b, i: (0,) * len(shape))
    tok = pl.BlockSpec((1, TF, D_MODEL), lambda b, i: (b, i, 0))
    return pl.pallas_call(
        _fin_kernel,
        grid=(BATCH, SEQ // TF),
        in_specs=[pl.BlockSpec((1, TF // CH_T, UNITS * UNIT_K), lambda b, i: (b, i, 0)), tok, tok,
                  _mod_spec(True),
                  full((S5_WIDTH, S5_WIDTH)), full((1, S5_WIDTH)),
                  full((S5_WIDTH, D_MODEL)), full((1, D_MODEL))],
        out_specs=tok,
        out_shape=jax.ShapeDtypeStruct((BATCH, SEQ, D_MODEL), F32),
        scratch_shapes=[pltpu.VMEM((S5_WIDTH // LANE, TF, LANE), F32)],
        compiler_params=_params(("arbitrary", "arbitrary")),
        name="s5_finish",
    )(y, sz1, x1, mod1, wglu, bglu, wout, fg)


def _rope_tables():
    h = QK_ROPE_DIM // 2
    inv = 1.0 / (ROPE_THETA ** (np.arange(0, h, 2, dtype=np.float64) / h))
    pos = np.arange(SEQ)
    ang_r = (pos // GRID_W)[:, None] * inv[None, :]
    ang_c = (pos % GRID_W)[:, None] * inv[None, :]
    cos32 = np.concatenate([np.cos(ang_r)] * 2 + [np.cos(ang_c)] * 2, axis=-1)
    sin32 = np.concatenate([np.sin(ang_r)] * 2 + [np.sin(ang_c)] * 2, axis=-1)
    cos = np.zeros((TOK, HEAD_PAD), np.float32)
    sin = np.zeros((TOK, HEAD_PAD), np.float32)
    kt = np.zeros((TOK, HEAD_PAD), np.float32)
    cos[:, :QK_NOPE_DIM] = 1.0
    cos[SEQ:, QK_NOPE_DIM:QK_DIM] = 1.0
    kt[SEQ:, :QK_ROPE_DIM] = 1.0
    cos[:SEQ, QK_NOPE_DIM:QK_DIM] = cos32
    sin[:SEQ, QK_NOPE_DIM:QK_DIM] = sin32
    kt[:SEQ, :QK_ROPE_DIM] = cos32
    kt[:SEQ, QK_ROPE_DIM:2 * QK_ROPE_DIM] = sin32
    return jnp.asarray(cos), jnp.asarray(sin), jnp.asarray(kt)


def _mla_selectors():
    def partner(d):
        return (d + 8, -1.0) if d % 16 < 8 else (d - 8, 1.0)

    o2 = Q_LORA_RANK + KV_LORA_RANK
    o3 = o2 + QK_ROPE_DIM
    pin = np.zeros((o3 + MLA_WIDTH, PROJ_W), np.float32)
    pin[np.arange(o3), np.arange(o3)] = 1.0
    pin[o3 + np.arange(MLA_WIDTH), 512 + np.arange(MLA_WIDTH)] = 1.0
    pa = np.zeros((MLA_HEADS * QK_DIM, QK_PAD), np.float32)
    pb = np.zeros((MLA_HEADS * QK_DIM, QK_PAD), np.float32)
    pk = np.zeros((MLA_HEADS * 128, QK_PAD), np.float32)
    pv = np.zeros((MLA_HEADS * 128, MLA_WIDTH), np.float32)
    kb = np.zeros((128, QK_PAD), np.float32)
    for d in range(QK_ROPE_DIM):
        src, sign = partner(d)
        pin[o2 + src, o3 + d] = sign
        for hd in range(MLA_HEADS):
            pb[hd * QK_DIM + QK_NOPE_DIM + src, hd * HEAD_PAD + QK_NOPE_DIM + d] = sign
            kb[d, hd * HEAD_PAD + QK_NOPE_DIM + d] = 1.0
            kb[QK_ROPE_DIM + d, hd * HEAD_PAD + QK_NOPE_DIM + d] = 1.0
    for hd in range(MLA_HEADS):
        pa[hd * QK_DIM + np.arange(QK_DIM), hd * HEAD_PAD + np.arange(QK_DIM)] = 1.0
        pk[hd * 128 + np.arange(QK_NOPE_DIM), hd * HEAD_PAD + np.arange(QK_NOPE_DIM)] = 1.0
        pv[hd * 128 + QK_NOPE_DIM + np.arange(V_HEAD_DIM), hd * V_HEAD_DIM + np.arange(V_HEAD_DIM)] = 1.0
    return [jnp.asarray(a, dtype=BF16) for a in (pin, pa, pb, pk, pv, kb)]


def _mla_wprep_kernel(win_ref, wuq_ref, wukv_ref, pin_ref, pa_ref, pb_ref, pk_ref, pv_ref, kb_ref,
                      o_in, o_qa, o_qb, o_k, o_v):
    o_in[...] = _dot(win_ref[...].astype(BF16), pin_ref[...]).astype(BF16)
    wq = (wuq_ref[...] * (SOFTMAX_SCALE * math.log2(math.e))).astype(BF16)
    o_qa[...] = _dot(wq, pa_ref[...]).astype(BF16)
    o_qb[...] = _dot(wq, pb_ref[...]).astype(BF16)
    wkv = wukv_ref[...].astype(BF16)
    o_k[:KV_LORA_RANK] = _dot(wkv, pk_ref[...]).astype(BF16)
    o_k[KV_LORA_RANK:] = kb_ref[...]
    o_v[...] = _dot(wkv, pv_ref[...]).astype(BF16)


def _mla_weights(w_in, w_uq, w_ukv):
    nj = 4
    full = lambda a: pl.BlockSpec(a.shape, lambda j: (0, 0))
    cols = lambda rows, width: pl.BlockSpec((rows, width // nj), lambda j: (0, j))
    sel = _mla_selectors()
    widths = (PROJ_W, QK_PAD, QK_PAD, QK_PAD, MLA_WIDTH, QK_PAD)
    out_rows = (D_MODEL, Q_LORA_RANK, Q_LORA_RANK, 256, KV_LORA_RANK)
    return pl.pallas_call(
        _mla_wprep_kernel,
        grid=(nj,),
        in_specs=[full(w_in), full(w_uq), full(w_ukv)] + [cols(a.shape[0], w) for a, w in zip(sel, widths)],
        out_specs=[cols(r, w) for r, w in zip(out_rows, widths)],
        out_shape=[jax.ShapeDtypeStruct((r, w), BF16) for r, w in zip(out_rows, widths)],
        compiler_params=_params(("arbitrary",)),
        name="mla_weight_prep",
    )(w_in, w_uq, w_ukv, *sel)


def _lam_tiles(lam):
    lam = lam.reshape(2, 2, UNITS, UNIT_ST // LANE, LANE)
    lam = jnp.concatenate([lam[0, 0], lam[1, 0], lam[0, 1], lam[1, 1]], axis=1)
    return jnp.broadcast_to(lam[:, :, None, :], (UNITS, STATE_TILES, SUB, LANE))


def kernel(x, c, ctx, c_ctx, ada_w, ada_b, norm_g, mla_w_in, mla_q_norm, mla_w_uq, mla_kv_norm, mla_w_ukv, mla_w_out, s5_w_in, s5_a_re, s5_a_im, s5_log_step, s5_b_re, s5_b_im, s5_c_re, s5_c_im, s5_d, s5_w_glu, s5_b_glu, s5_w_out, final_g):
    cc = jnp.concatenate([c, c_ctx[None, :], jnp.zeros((7, D_MODEL), F32)], axis=0)
    mods = _modulation(cc, ada_w, ada_b)

    mod_lat = [mods[i, :BATCH, None, :] for i in range(2)]
    mod_ctx = [mods[i, BATCH:BATCH + 1, None, :] for i in range(2)]

    win, wqa, wqb, wk, wv = _mla_weights(mla_w_in[0], mla_w_uq[0], mla_w_ukv[0])
    weights = (norm_g[0][None, :], win, mla_q_norm[0][None, :], mla_kv_norm[0][None, :], wqa, wqb, wk, wv)
    tables = _rope_tables()
    qkvz = _mla_proj(x, mod_lat[0], weights, tables)
    q, k, v, sz = _mla_proj(ctx, mod_ctx[0], weights, tables, filled=qkvz)
    o = _attention(q, k, v)
    out_w = (norm_g[1][None, :], mla_w_out[0].astype(BF16), s5_w_in[0].astype(BF16))
    x1, xu, sz1 = _mla_out(o, sz, x, mod_lat[0], mod_lat[1], *out_w)
    xu, = _mla_out(o, sz, ctx, mod_ctx[0], mod_ctx[1], *out_w, xu_filled=xu)
    mod1 = mod_lat[1]

    n = 2 * S5_GROUPS
    lam, pb, cp, kk = _s5_prep(
        s5_a_re[0].reshape(n, S5_STATE), s5_a_im[0].reshape(n, S5_STATE), s5_log_step[0].reshape(n, 1),
        jnp.swapaxes(s5_b_re[0], -1, -2).reshape(n, S5_GROUP, S5_STATE),
        jnp.swapaxes(s5_b_im[0], -1, -2).reshape(n, S5_GROUP, S5_STATE),
        s5_c_re[0].reshape(n, S5_GROUP, S5_STATE), s5_c_im[0].reshape(n, S5_GROUP, S5_STATE))
    y = _s5_core(xu, kk.reshape(CH_T, 2, UNITS, UNIT_G, S5_GROUP, S5_GROUP),
                 pb.reshape(2, CH_T, 2, UNITS, UNIT_G, S5_GROUP, S5_STATE),
                 cp.reshape(2, CH_T, 2, UNITS, UNIT_G, S5_STATE, S5_GROUP),
                 s5_d[0].reshape(UNITS, UNIT_CH, 1), _lam_tiles(lam))
    return _finish(y, sz1, x1, mod1, s5_w_glu[0].astype(BF16), s5_b_glu[0][None, :], s5_w_out[0].astype(BF16),
                   final_g[None, :])
```

```python
import functools
import math

import jax
import jax.numpy as jnp
import numpy as np
from jax import lax
from jax.experimental import pallas as pl
from jax.experimental.pallas import tpu as pltpu

D_MODEL = 1024
BATCH = 8
SEQ = 2048
GRID_W = 64
CTX_LEN = 256
TOK = CTX_LEN + SEQ
EPS = 1e-6

MLA_HEADS = 16
QK_NOPE_DIM = 64
QK_ROPE_DIM = 32
V_HEAD_DIM = 64
Q_LORA_RANK = 256
KV_LORA_RANK = 128
MLA_WIDTH = MLA_HEADS * V_HEAD_DIM
QK_DIM = QK_NOPE_DIM + QK_ROPE_DIM
SOFTMAX_SCALE = QK_DIM ** -0.5
ROPE_THETA = 10000.0
HEAD_PAD = 128
QK_PAD = MLA_HEADS * HEAD_PAD
PROJ_W = 1536

S5_WIDTH = D_MODEL
S5_GROUP = 16
S5_GROUPS = 64
S5_STATE = 64
CH_T = 4
UNIT_G = 4
UNIT_CH = UNIT_G * S5_GROUP
UNITS = S5_GROUPS // UNIT_G
UNIT_K = CH_T * UNIT_CH
UNIT_ST = UNIT_G * S5_STATE
NCH = TOK // CH_T
NCH_CTX = CTX_LEN // CH_T
LANE = 128
SUB = 8

TL = 512
CTX_NB = 2
TF = 1024
TQ = 256
KCH = 256
HPAIRS = 4
assert TQ == CTX_LEN
VMEM_LIMIT = 56 * 1024 * 1024

F32 = jnp.float32
BF16 = jnp.bfloat16


def _params(sem, flags=None):
    return pltpu.CompilerParams(dimension_semantics=sem, vmem_limit_bytes=VMEM_LIMIT, flags=flags)


def _silu(v):
    return v * jax.nn.sigmoid(v)


def _rms(v, g):
    return v * lax.rsqrt(jnp.mean(v * v, axis=-1, keepdims=True) + EPS) * g


def _dot(a, b):
    return jnp.dot(a, b, preferred_element_type=F32)


def _mod_kernel(cc_ref, w_ref, b_ref, o_ref):
    a = _silu(cc_ref[...]).astype(BF16)
    o_ref[0] = _dot(a, w_ref[0].astype(BF16)) + b_ref[0]


def _modulation(cc, ada_w, ada_b):
    depth = ada_w.shape[0]
    tn = 768
    return pl.pallas_call(
        _mod_kernel,
        grid=(depth, 3 * D_MODEL // tn),
        in_specs=[
            pl.BlockSpec((16, D_MODEL), lambda i, j: (0, 0)),
            pl.BlockSpec((1, D_MODEL, tn), lambda i, j: (i, 0, j)),
            pl.BlockSpec((1, 1, tn), lambda i, j: (i, 0, j)),
        ],
        out_specs=pl.BlockSpec((1, 16, tn), lambda i, j: (i, 0, j)),
        out_shape=jax.ShapeDtypeStruct((depth, 16, 3 * D_MODEL), F32),
        compiler_params=_params(("arbitrary", "arbitrary")),
        name="modulation",
    )(cc, ada_w, ada_b.reshape(depth, 1, 3 * D_MODEL))


def _mod_spec(per_batch):
    return pl.BlockSpec((1, 1, 3 * D_MODEL), (lambda b, i: (b, 0, 0)) if per_batch else (lambda b, i: (0, 0, 0)))


def _mla_proj_kernel(x_ref, mod_ref, g_ref, win_ref, qg_ref, kvg_ref, wqa_ref, wqb_ref, wk_ref, wv_ref,
                     cos_ref, sin_ref, kt_ref, *refs):
    q_ref, k_ref, v_ref, sz_ref = refs[-4:]
    nb, tile = x_ref.shape[0], x_ref.shape[1]
    x = x_ref[...].reshape(nb * tile, D_MODEL)
    mod = mod_ref[0]
    sh = mod[:, :D_MODEL]
    sc = mod[:, D_MODEL:2 * D_MODEL]
    h = _rms(x, g_ref[...]) * (1.0 + sc) + sh
    p = _dot(h.astype(BF16), win_ref[...])
    cqn = _rms(p[:, :Q_LORA_RANK], qg_ref[...]).astype(BF16)
    ckvn = _rms(p[:, Q_LORA_RANK:Q_LORA_RANK + KV_LORA_RANK], kvg_ref[...]).astype(BF16)
    kr = p[:, 384:512]
    z = p[:, 512:]
    qa = _dot(cqn, wqa_ref[...])
    qb = _dot(cqn, wqb_ref[...])
    cos, sin, kt = (jnp.concatenate([t[...]] * nb, axis=0) for t in (cos_ref, sin_ref, kt_ref))
    for hd in range(MLA_HEADS):
        sl = slice(hd * HEAD_PAD, (hd + 1) * HEAD_PAD)
        q_ref[:, :, sl] = (qa[:, sl] * cos + qb[:, sl] * sin).astype(BF16).reshape(nb, tile, HEAD_PAD)
    kin = jnp.concatenate([ckvn, (kr * kt).astype(BF16)], axis=-1)
    k_ref[...] = _dot(kin, wk_ref[...]).astype(BF16).reshape(nb, tile, QK_PAD)
    v_ref[...] = _dot(ckvn, wv_ref[...]).astype(BF16).reshape(nb, tile, MLA_WIDTH)
    sz_ref[...] = _silu(z).astype(BF16).reshape(nb, tile, MLA_WIDTH)


def _mla_proj(xs, mod, weights, tables, filled=None):
    is_ctx = filled is not None
    tile, nb = (CTX_LEN, CTX_NB) if is_ctx else (TL, 1)
    off = SEQ // tile if is_ctx else 0
    full = lambda a: pl.BlockSpec(a.shape, lambda b, i: (0,) * a.ndim)
    tok = lambda w: pl.BlockSpec((nb, tile, w), lambda b, i: (b, i + off, 0))
    pos = pl.BlockSpec((tile, HEAD_PAD), lambda b, i: (i + off, 0))
    widths = (QK_PAD, QK_PAD, MLA_WIDTH, MLA_WIDTH)
    in_specs = [pl.BlockSpec((nb, tile, D_MODEL), lambda b, i: (b, i, 0)), _mod_spec(not is_ctx)]
    in_specs += [full(w) for w in weights] + [pos, pos, pos]
    args = [xs, mod, *weights, *tables]
    aliases = {}
    if is_ctx:
        aliases = {len(args) + n: n for n in range(4)}
        in_specs += [pl.BlockSpec(memory_space=pl.ANY)] * 4
        args += list(filled)
    return pl.pallas_call(
        _mla_proj_kernel,
        grid=(BATCH // nb, xs.shape[1] // tile),
        in_specs=in_specs,
        out_specs=[tok(w) for w in widths],
        out_shape=[jax.ShapeDtypeStruct((BATCH, TOK, w), BF16) for w in widths],
        input_output_aliases=aliases,
        compiler_params=_params(("arbitrary", "arbitrary")),
        name="mla_proj_ctx" if is_ctx else "mla_proj",
    )(*args)


def _attn_kernel(q_ref, k_ref, v_ref, o_ref, s_buf, m_buf, vx_buf, cs_buf, cm_buf):
    nt = SEQ // TQ
    lane = lax.broadcasted_iota(jnp.int32, (TOK, 2 * V_HEAD_DIM), 1)
    for hp in range(HPAIRS):
        v = v_ref[0, :, hp * 2 * V_HEAD_DIM:(hp + 1) * 2 * V_HEAD_DIM]
        vx_buf[hp, 0] = jnp.where(lane < V_HEAD_DIM, v, (lane == V_HEAD_DIM).astype(BF16))
        vx_buf[hp, 1] = jnp.where(lane >= V_HEAD_DIM, v, (lane == 0).astype(BF16))

    def scores(hp, row, k0, nk, slot):
        sb, mb = (cs_buf.at[hp], cm_buf.at[hp]) if slot is None else (s_buf.at[slot], m_buf.at[slot])
        for hh in range(2):
            c0 = (2 * hp + hh) * HEAD_PAD
            s = lax.dot_general(q_ref[0, pl.ds(row, TQ), c0:c0 + HEAD_PAD], k_ref[0, k0:k0 + nk, c0:c0 + HEAD_PAD],
                                (((1,), (1,)), ((), ())), preferred_element_type=F32)
            sb[hh, :, :nk] = s
            mb[hh] = jnp.broadcast_to(jnp.max(s, axis=-1, keepdims=True), (TQ, KCH))

    def values(hp, row, k0, nk, slot):
        sb, mb = (cs_buf.at[hp], cm_buf.at[hp]) if slot is None else (s_buf.at[slot], m_buf.at[slot])
        outs = []
        for hh in range(2):
            m = mb[hh]
            ps = [jnp.exp2(sb[hh, :, n * KCH:(n + 1) * KCH] - m).astype(BF16) for n in range(nk // KCH)]
            acc = _dot(jnp.concatenate(ps, axis=-1), vx_buf[hp, hh, k0:k0 + nk, :])
            l_col = V_HEAD_DIM if hh == 0 else 0
            outs.append(acc / acc[:, l_col:l_col + 1])
        olane = lax.broadcasted_iota(jnp.int32, outs[0].shape, 1)
        o_ref[0, pl.ds(row, TQ), hp * 2 * V_HEAD_DIM:(hp + 1) * 2 * V_HEAD_DIM] = jnp.where(
            olane < V_HEAD_DIM, outs[0], outs[1]).astype(BF16)

    for hp in range(HPAIRS):
        scores(hp, SEQ, SEQ, CTX_LEN, None)
    scores(0, 0, 0, TOK, 0)
    for hp in range(HPAIRS):
        values(hp, SEQ, SEQ, CTX_LEN, None)
    for hp in range(HPAIRS):
        for t in range(1, nt):
            scores(hp, t * TQ, 0, TOK, t % 2)
            values(hp, (t - 1) * TQ, 0, TOK, (t - 1) % 2)
        if hp + 1 < HPAIRS:
            scores(hp + 1, 0, 0, TOK, 0)
        values(hp, (nt - 1) * TQ, 0, TOK, (nt - 1) % 2)


def _attention(q, k, v):
    qk = pl.BlockSpec((1, TOK, HPAIRS * 2 * HEAD_PAD), lambda b, h: (b, 0, h))
    vo = pl.BlockSpec((1, TOK, HPAIRS * 2 * V_HEAD_DIM), lambda b, h: (b, 0, h))
    return pl.pallas_call(
        _attn_kernel,
        grid=(BATCH, MLA_HEADS // (2 * HPAIRS)),
        in_specs=[qk, qk, vo],
        out_specs=vo,
        out_shape=jax.ShapeDtypeStruct((BATCH, TOK, MLA_WIDTH), BF16),
        scratch_shapes=[
            pltpu.VMEM((2, 2, TQ, TOK), F32),
            pltpu.VMEM((2, 2, TQ, KCH), F32),
            pltpu.VMEM((HPAIRS, 2, TOK, 2 * V_HEAD_DIM), BF16),
            pltpu.VMEM((HPAIRS, 2, TQ, CTX_LEN), F32),
            pltpu.VMEM((HPAIRS, 2, TQ, KCH), F32),
        ],
        compiler_params=_params(("arbitrary", "arbitrary")),
        name="attention",
    )(q, k, v)


def _swap_halves(va, vb):
    lo = lax.broadcasted_iota(jnp.int32, va.shape, 1) < UNIT_CH
    return (jnp.where(lo, va, pltpu.roll(vb, UNIT_CH, 1)),
            jnp.where(lo, pltpu.roll(va, UNIT_CH, 1), vb))


def _mla_out_kernel(o_ref, sz_ref, x_ref, mod0_ref, mod1_ref, g1_ref, wout_ref, win_ref, *refs, is_ctx):
    if is_ctx:
        xu_ref, tok_scr = refs[-2:]
    else:
        x1_ref, xu_ref, sz1_ref, tok_scr = refs
    nb, tile = x_ref.shape[0], x_ref.shape[1]
    rows = nb * tile
    a = (o_ref[...].astype(F32) * sz_ref[...].astype(F32)).astype(BF16).reshape(rows, MLA_WIDTH)
    gt = mod0_ref[0][:, 2 * D_MODEL:]
    x1 = x_ref[...].reshape(rows, D_MODEL) + gt * _dot(a, wout_ref[...])
    mod1 = mod1_ref[0]
    h = _rms(x1, g1_ref[...]) * (1.0 + mod1[:, D_MODEL:2 * D_MODEL]) + mod1[:, :D_MODEL]
    if is_ctx:
        u = _dot(h.astype(BF16), win_ref[:, :S5_WIDTH])
    else:
        x1_ref[0] = x1
        p = _dot(h.astype(BF16), win_ref[...])
        sz1_ref[0] = _silu(p[:, S5_WIDTH:]).astype(BF16)
        u = p[:, :S5_WIDTH]
    for m in range(S5_WIDTH // LANE):
        tok_scr[m] = u[:, m * LANE:(m + 1) * LANE]
    for m in range(S5_WIDTH // LANE):
        v = [tok_scr[m, pl.ds(t, rows // CH_T, stride=CH_T), :] for t in range(CH_T)]
        for hf in range(CH_T // 2):
            even, odd = _swap_halves(v[2 * hf], v[2 * hf + 1])
            c0 = 2 * m * UNIT_K + hf * LANE
            xu_ref[:, :, c0:c0 + LANE] = even.astype(BF16).reshape(nb, tile // CH_T, LANE)
            xu_ref[:, :, c0 + UNIT_K:c0 + UNIT_K + LANE] = odd.astype(BF16).reshape(nb, tile // CH_T, LANE)


def _mla_out(o, sz, xs, mod0, mod1, g1, wout, win, xu_filled=None):
    is_ctx = xu_filled is not None
    tile, nb = (CTX_LEN, CTX_NB) if is_ctx else (TL, 1)
    off = SEQ // tile if is_ctx else 0
    full = lambda a: pl.BlockSpec(a.shape, lambda b, i: (0,) * a.ndim)
    shared = lambda w: pl.BlockSpec((nb, tile, w), lambda b, i: (b, i + off, 0))
    own = lambda w: pl.BlockSpec((nb, tile, w), lambda b, i: (b, i, 0))
    xu_spec = pl.BlockSpec((nb, tile // CH_T, UNITS * UNIT_K), lambda b, i: (b, i + off, 0))
    xu_shape = jax.ShapeDtypeStruct((BATCH, NCH, UNITS * UNIT_K), BF16)
    in_specs = [shared(MLA_WIDTH), shared(MLA_WIDTH), own(D_MODEL), _mod_spec(not is_ctx), _mod_spec(not is_ctx),
                full(g1), full(wout), full(win)]
    args = [o, sz, xs, mod0, mod1, g1, wout, win]
    if is_ctx:
        in_specs.append(pl.BlockSpec(memory_space=pl.ANY))
        args.append(xu_filled)
        out_specs, out_shape, aliases = [xu_spec], [xu_shape], {len(args) - 1: 0}
    else:
        out_specs = [own(D_MODEL), xu_spec, own(S5_WIDTH)]
        out_shape = [jax.ShapeDtypeStruct((BATCH, SEQ, D_MODEL), F32), xu_shape,
                     jax.ShapeDtypeStruct((BATCH, SEQ, S5_WIDTH), BF16)]
        aliases = {}
    return pl.pallas_call(
        functools.partial(_mla_out_kernel, is_ctx=is_ctx),
        grid=(BATCH // nb, xs.shape[1] // tile),
        in_specs=in_specs,
        out_specs=out_specs,
        out_shape=out_shape,
        input_output_aliases=aliases,
        scratch_shapes=[pltpu.VMEM((S5_WIDTH // LANE, nb * tile, LANE), F32)],
        compiler_params=_params(("arbitrary", "arbitrary")),
        name="mla_out_s5_in_ctx" if is_ctx else "mla_out_s5_in",
    )(*args)


def _cmul(ar, ai, br, bi):
    return ar * br - ai * bi, ar * bi + ai * br


def _group_dot(a, b, precision=lax.Precision.HIGHEST):
    return lax.dot_general(a, b, (((2,), (2,)), ((0,), (0,))), precision=precision, preferred_element_type=F32)


def _group_transpose(eye, a):
    return _group_dot(eye, a.astype(BF16), precision=None)


def _s5_prep_kernel(are_ref, aim_ref, ls_ref, bre_ref, bim_ref, cre_ref, cim_ref, lam_ref, pb_ref, cp_ref, kk_ref):
    n = are_ref.shape[0]
    eye = (lax.broadcasted_iota(jnp.int32, (n, S5_STATE, S5_STATE), 1)
           == lax.broadcasted_iota(jnp.int32, (n, S5_STATE, S5_STATE), 2)).astype(BF16)
    ar = are_ref[...]
    ai = aim_ref[...]
    dt = jnp.exp(ls_ref[...])
    mag = jnp.exp(ar * dt)
    lb_re = mag * jnp.cos(ai * dt)
    lb_im = mag * jnp.sin(ai * dt)
    den = ar * ar + ai * ai
    nr = lb_re - 1.0
    f_re = ((nr * ar + lb_im * ai) / den)[:, None, :]
    f_im = ((lb_im * ar - nr * ai) / den)[:, None, :]
    bb_re, bb_im = _cmul(f_re, f_im, bre_ref[...], bim_ref[...])
    c_re = cre_ref[...]
    c_im = cim_ref[...]
    pw_re = jnp.ones_like(lb_re)
    pw_im = jnp.zeros_like(lb_re)
    for r in range(CH_T + 1):
        pr = pw_re[:, None, :]
        pi = pw_im[:, None, :]
        cl_re, cl_im = _cmul(c_re, c_im, pr, pi)
        if r < CH_T:
            q_re, q_im = _cmul(pr, pi, bb_re, bb_im)
            pb_ref[0, r] = q_re
            pb_ref[1, r] = q_im
            kk_ref[r] = _group_dot(bb_re, cl_re) - _group_dot(bb_im, cl_im)
        if r > 0:
            cp_ref[0, r - 1] = _group_transpose(eye, cl_re)
            cp_ref[1, r - 1] = _group_transpose(eye, -cl_im)
        if r == CH_T:
            lam_ref[0] = pw_re
            lam_ref[1] = pw_im
        else:
            pw_re, pw_im = _cmul(pw_re, pw_im, lb_re, lb_im)


def _s5_prep(a_re, a_im, log_step, b_re_t, b_im_t, c_re, c_im):
    n = a_re.shape[0]
    nb = 32
    row2 = pl.BlockSpec((nb, S5_STATE), lambda i: (i, 0))
    row3 = pl.BlockSpec((nb, S5_GROUP, S5_STATE), lambda i: (i, 0, 0))
    return pl.pallas_call(
        _s5_prep_kernel,
        grid=(n // nb,),
        in_specs=[row2, row2, pl.BlockSpec((nb, 1), lambda i: (i, 0)), row3, row3, row3, row3],
        out_specs=[
            pl.BlockSpec((2, nb, S5_STATE), lambda i: (0, i, 0)),
            pl.BlockSpec((2, CH_T, nb, S5_GROUP, S5_STATE), lambda i: (0, 0, i, 0, 0)),
            pl.BlockSpec((2, CH_T, nb, S5_STATE, S5_GROUP), lambda i: (0, 0, i, 0, 0)),
            pl.BlockSpec((CH_T, nb, S5_GROUP, S5_GROUP), lambda i: (0, i, 0, 0)),
        ],
        out_shape=[
            jax.ShapeDtypeStruct((2, n, S5_STATE), F32),
            jax.ShapeDtypeStruct((2, CH_T, n, S5_GROUP, S5_STATE), F32),
            jax.ShapeDtypeStruct((2, CH_T, n, S5_STATE, S5_GROUP), F32),
            jax.ShapeDtypeStruct((CH_T, n, S5_GROUP, S5_GROUP), F32),
        ],
        compiler_params=_params(("arbitrary",)),
        name="s5_prep",
    )(a_re, a_im, log_step, b_re_t, b_im_t, c_re, c_im)


STATE_TILES = 2 * 2 * UNIT_ST // LANE


def _hdot(a, rep):
    return _dot(a.astype(BF16), rep)


def _unit_operators(kk_ref, pb_ref, cp_ref, d_ref):
    def iota(shape, dim):
        return lax.broadcasted_iota(jnp.int32, shape, dim)

    rep16 = (iota((S5_GROUP, UNIT_K), 1) % S5_GROUP == iota((S5_GROUP, UNIT_K), 0)).astype(BF16)
    rep64 = (iota((S5_STATE, UNIT_ST), 1) % S5_STATE == iota((S5_STATE, UNIT_ST), 0)).astype(BF16)
    row = iota((UNIT_CH, UNIT_K), 0)
    col = iota((UNIT_CH, UNIT_K), 1)
    same_group_out = row // S5_GROUP == (col // S5_GROUP) % UNIT_G
    same_group_st = row // S5_GROUP == col // S5_STATE
    on_diag = row == col % UNIT_CH
    col_t = col // UNIT_CH
    srow = iota((UNIT_ST, UNIT_K), 0)
    scol = iota((UNIT_ST, UNIT_K), 1)
    st_same_group = srow // S5_STATE == (scol // S5_GROUP) % UNIT_G
    st_col_t = scol // UNIT_CH
    kexp = _hdot(jnp.concatenate([kk_ref[k, d, 0].reshape(UNIT_CH, S5_GROUP)
                                  for d in range(2) for k in range(CH_T)], axis=0), rep16)
    pexp = _hdot(jnp.concatenate([pb_ref[ri, r, d, 0].reshape(UNIT_CH, S5_STATE)
                                  for d in range(2) for ri in range(2) for r in range(CH_T)], axis=0), rep64)
    cexp = _hdot(jnp.concatenate([cp_ref[ri, rr, d, 0].reshape(UNIT_ST, S5_GROUP)
                                  for d in range(2) for ri in range(2) for rr in range(CH_T)], axis=0), rep16)

    def blk(a, idx, nrows):
        return a[idx * nrows:(idx + 1) * nrows]

    rows = []
    for j in range(CH_T):
        acc = jnp.where((col_t == j) & on_diag, d_ref[0], 0.0)
        for d in range(2):
            for k in range(CH_T):
                lag_ok = (col_t - j == k) if d == 0 else (j - col_t == k)
                acc = acc + jnp.where(lag_ok & same_group_out, blk(kexp, d * CH_T + k, UNIT_CH), 0.0)
        rows.append(acc)
    parts = [jnp.concatenate(rows, axis=0)]
    cos = []
    for d in range(2):
        rows = []
        for j in range(CH_T):
            r = CH_T - 1 - j if d == 0 else j
            rows.append(jnp.concatenate(
                [jnp.where(same_group_st, blk(pexp, (d * 2 + ri) * CH_T + r, UNIT_CH), 0.0) for ri in range(2)],
                axis=-1))
        parts.append(jnp.concatenate(rows, axis=0))
        for ri in range(2):
            acc = jnp.zeros((UNIT_ST, UNIT_K), F32)
            for rr in range(CH_T):
                t = rr if d == 0 else CH_T - 1 - rr
                acc = acc + jnp.where((st_col_t == t) & st_same_group,
                                      blk(cexp, (d * 2 + ri) * CH_T + rr, UNIT_ST), 0.0)
            cos.append(acc)
    return jnp.concatenate(parts, axis=-1).astype(BF16), jnp.concatenate(cos, axis=0).astype(BF16)


def _s5_core_kernel(x_ref, kk_ref, pb_ref, cp_ref, d_ref, lam_ref, y_ref, st_scr, sp_scr, yi_scr):
    w1, co = _unit_operators(kk_ref, pb_ref, cp_ref, d_ref)
    nlat = NCH - NCH_CTX
    for b in range(BATCH):
        r = _dot(x_ref[b], w1)
        y_ref[b] = r[:nlat, :UNIT_K]
        for lt in range(STATE_TILES):
            c0 = UNIT_K + lt * LANE
            st_scr[lt, pl.ds(b, NCH, stride=BATCH), :] = r[:, c0:c0 + LANE]
    lam = [lam_ref[0, lt] for lt in range(STATE_TILES)]

    def rows(chunk):
        return pl.ds(pl.multiple_of(chunk * BATCH, BATCH), BATCH)

    def load_z(row, base):
        return [st_scr[base + k, rows(row), :] for k in range(4)]

    def advance(state, z, base):
        ar0, ar1, ai0, ai1 = lam[base:base + 4]
        return [ar0 * state[0] - ai0 * state[2] + z[0], ar1 * state[1] - ai1 * state[3] + z[1],
                ar0 * state[2] + ai0 * state[0] + z[2], ar1 * state[3] + ai1 * state[1] + z[3]]

    def keep(lo, hi, chunk_lo, base):
        r16 = pl.ds(pl.multiple_of(chunk_lo * BATCH, 2 * BATCH), 2 * BATCH)
        for k in range(4):
            sp_scr[base + k, r16, :] = jnp.concatenate([lo[k], hi[k]], axis=0).astype(BF16)

    def fwd_row(i):
        return jnp.where(i < NCH_CTX, nlat + i, i - NCH_CTX)

    def two_steps(m, carry):
        s_f, z_f, s_b, z_b = carry
        i = 2 * m
        rf = fwd_row(i)
        rb = NCH - 1 - i
        z_f1 = load_z(rf + 1, 0)
        z_b1 = load_z(rb - 1, 4)
        nxt = jnp.minimum(i + 2, NCH - 2)
        z_f2 = load_z(fwd_row(nxt), 0)
        z_b2 = load_z(NCH - 1 - nxt, 4)
        s_f1 = advance(s_f, z_f, 0)
        s_b1 = advance(s_b, z_b, 4)
        keep(s_f, s_f1, rf, 0)
        keep(s_b1, s_b, rb - 1, 4)
        return advance(s_f1, z_f1, 0), z_f2, advance(s_b1, z_b1, 4), z_b2

    zero = [jnp.zeros((BATCH, LANE), F32)] * 4
    lax.fori_loop(0, NCH // 2, two_steps, (zero, load_z(nlat, 0), zero, load_z(NCH - 1, 4)))
    for rb in range(BATCH):
        sl = slice(rb * nlat, (rb + 1) * nlat)
        lhs = jnp.concatenate([sp_scr[lt, sl, :] for lt in range(STATE_TILES)], axis=-1)
        yi = _dot(lhs, co)
        for t in range(UNIT_K // LANE):
            yi_scr[t, sl, :] = yi[:, t * LANE:(t + 1) * LANE]
    for b in range(BATCH):
        y_ref[b] = y_ref[b] + jnp.concatenate(
            [yi_scr[t, pl.ds(b, nlat, stride=BATCH), :] for t in range(UNIT_K // LANE)], axis=-1)


def _s5_core(xu, kk, pb, cp, d, lam):
    return pl.pallas_call(
        _s5_core_kernel,
        grid=(UNITS,),
        in_specs=[
            pl.BlockSpec((BATCH, NCH, UNIT_K), lambda q: (0, 0, q)),
            pl.BlockSpec((CH_T, 2, 1, UNIT_G, S5_GROUP, S5_GROUP), lambda q: (0, 0, q, 0, 0, 0)),
            pl.BlockSpec((2, CH_T, 2, 1, UNIT_G, S5_GROUP, S5_STATE), lambda q: (0, 0, 0, q, 0, 0, 0)),
            pl.BlockSpec((2, CH_T, 2, 1, UNIT_G, S5_STATE, S5_GROUP), lambda q: (0, 0, 0, q, 0, 0, 0)),
            pl.BlockSpec((1, UNIT_CH, 1), lambda q: (q, 0, 0)),
            pl.BlockSpec((1, STATE_TILES, SUB, LANE), lambda q: (q, 0, 0, 0)),
        ],
        out_specs=pl.BlockSpec((BATCH, NCH - NCH_CTX, UNIT_K), lambda q: (0, 0, q)),
        out_shape=jax.ShapeDtypeStruct((BATCH, NCH - NCH_CTX, UNITS * UNIT_K), F32),
        scratch_shapes=[pltpu.VMEM((STATE_TILES, BATCH * NCH, LANE), F32),
                        pltpu.VMEM((STATE_TILES, BATCH * NCH, LANE), BF16),
                        pltpu.VMEM((UNIT_K // LANE, BATCH * NCH, LANE), F32)],
        compiler_params=_params(("arbitrary",)),
        name="s5_core",
    )(xu, kk, pb, cp, d, lam)


def _fin_kernel(y_ref, sz_ref, x_ref, mod_ref, wglu_ref, bglu_ref, wout_ref, fg_ref, o_ref, tok_scr):
    for m in range(S5_WIDTH // LANE):
        for hf in range(CH_T // 2):
            c0 = 2 * m * UNIT_K + hf * LANE
            va, vb = _swap_halves(y_ref[0, :, c0:c0 + LANE], y_ref[0, :, c0 + UNIT_K:c0 + UNIT_K + LANE])
            tok_scr[m, pl.ds(2 * hf, TF // CH_T, stride=CH_T), :] = va
            tok_scr[m, pl.ds(2 * hf + 1, TF // CH_T, stride=CH_T), :] = vb
    y = jnp.concatenate([tok_scr[m] for m in range(S5_WIDTH // LANE)], axis=-1)
    y = jax.nn.gelu(y)
    y = y * jax.nn.sigmoid(_dot(y.astype(BF16), wglu_ref[...]) + bglu_ref[...])
    a = (y * sz_ref[0].astype(F32)).astype(BF16)
    gt = mod_ref[0][:, 2 * D_MODEL:]
    x2 = x_ref[0] + gt * _dot(a, wout_ref[...])
    o_ref[0] = _rms(x2, fg_ref[...])


def _finish(y, sz1, x1, mod1, wglu, bglu, wout, fg):
    full = lambda shape: pl.BlockSpec(shape, lambda b, i: (0,) * len(shape))
    tok = pl.BlockSpec((1, TF, D_MODEL), lambda b, i: (b, i, 0))
    return pl.pallas_call(
        _fin_kernel,
        grid=(BATCH, SEQ // TF),
        in_specs=[pl.BlockSpec((1, TF // CH_T, UNITS * UNIT_K), lambda b, i: (b, i, 0)), tok, tok,
                  _mod_spec(True),
                  full((S5_WIDTH, S5_WIDTH)), full((1, S5_WIDTH)),
                  full((S5_WIDTH, D_MODEL)), full((1, D_MODEL))],
        out_specs=tok,
        out_shape=jax.ShapeDtypeStruct((BATCH, SEQ, D_MODEL), F32),
        scratch_shapes=[pltpu.VMEM((S5_WIDTH // LANE, TF, LANE), F32)],
        compiler_params=_params(("arbitrary", "arbitrary")),
        name="s5_finish",
    )(y, sz1, x1, mod1, wglu, bglu, wout, fg)


def _rope_tables():
    h = QK_ROPE_DIM // 2
    inv = 1.0 / (ROPE_THETA ** (np.arange(0, h, 2, dtype=np.float64) / h))
    pos = np.arange(SEQ)
    ang_r = (pos // GRID_W)[:, None] * inv[None, :]
    ang_c = (pos % GRID_W)[:, None] * inv[None, :]
    cos32 = np.concatenate([np.cos(ang_r)] * 2 + [np.cos(ang_c)] * 2, axis=-1)
    sin32 = np.concatenate([np.sin(ang_r)] * 2 + [np.sin(ang_c)] * 2, axis=-1)
    cos = np.zeros((TOK, HEAD_PAD), np.float32)
    sin = np.zeros((TOK, HEAD_PAD), np.float32)
    kt = np.zeros((TOK, HEAD_PAD), np.float32)
    cos[:, :QK_NOPE_DIM] = 1.0
    cos[SEQ:, QK_NOPE_DIM:QK_DIM] = 1.0
    kt[SEQ:, :QK_ROPE_DIM] = 1.0
    cos[:SEQ, QK_NOPE_DIM:QK_DIM] = cos32
    sin[:SEQ, QK_NOPE_DIM:QK_DIM] = sin32
    kt[:SEQ, :QK_ROPE_DIM] = cos32
    kt[:SEQ, QK_ROPE_DIM:2 * QK_ROPE_DIM] = sin32
    return jnp.asarray(cos), jnp.asarray(sin), jnp.asarray(kt)


def _mla_selectors():
    def partner(d):
        return (d + 8, -1.0) if d % 16 < 8 else (d - 8, 1.0)

    o2 = Q_LORA_RANK + KV_LORA_RANK
    o3 = o2 + QK_ROPE_DIM
    pin = np.zeros((o3 + MLA_WIDTH, PROJ_W), np.float32)
    pin[np.arange(o3), np.arange(o3)] = 1.0
    pin[o3 + np.arange(MLA_WIDTH), 512 + np.arange(MLA_WIDTH)] = 1.0
    pa = np.zeros((MLA_HEADS * QK_DIM, QK_PAD), np.float32)
    pb = np.zeros((MLA_HEADS * QK_DIM, QK_PAD), np.float32)
    pk = np.zeros((MLA_HEADS * 128, QK_PAD), np.float32)
    pv = np.zeros((MLA_HEADS * 128, MLA_WIDTH), np.float32)
    kb = np.zeros((128, QK_PAD), np.float32)
    for d in range(QK_ROPE_DIM):
        src, sign = partner(d)
        pin[o2 + src, o3 + d] = sign
        for hd in range(MLA_HEADS):
            pb[hd * QK_DIM + QK_NOPE_DIM + src, hd * HEAD_PAD + QK_NOPE_DIM + d] = sign
            kb[d, hd * HEAD_PAD + QK_NOPE_DIM + d] = 1.0
            kb[QK_ROPE_DIM + d, hd * HEAD_PAD + QK_NOPE_DIM + d] = 1.0
    for hd in range(MLA_HEADS):
        pa[hd * QK_DIM + np.arange(QK_DIM), hd * HEAD_PAD + np.arange(QK_DIM)] = 1.0
        pk[hd * 128 + np.arange(QK_NOPE_DIM), hd * HEAD_PAD + np.arange(QK_NOPE_DIM)] = 1.0
        pv[hd * 128 + QK_NOPE_DIM + np.arange(V_HEAD_DIM), hd * V_HEAD_DIM + np.arange(V_HEAD_DIM)] = 1.0
    return [jnp.asarray(a, dtype=BF16) for a in (pin, pa, pb, pk, pv, kb)]


def _mla_wprep_kernel(win_ref, wuq_ref, wukv_ref, pin_ref, pa_ref, pb_ref, pk_ref, pv_ref, kb_ref,
                      o_in, o_qa, o_qb, o_k, o_v):
    o_in[...] = _dot(win_ref[...].astype(BF16), pin_ref[...]).astype(BF16)
    wq = (wuq_ref[...] * (SOFTMAX_SCALE * math.log2(math.e))).astype(BF16)
    o_qa[...] = _dot(wq, pa_ref[...]).astype(BF16)
    o_qb[...] = _dot(wq, pb_ref[...]).astype(BF16)
    wkv = wukv_ref[...].astype(BF16)
    o_k[:KV_LORA_RANK] = _dot(wkv, pk_ref[...]).astype(BF16)
    o_k[KV_LORA_RANK:] = kb_ref[...]
    o_v[...] = _dot(wkv, pv_ref[...]).astype(BF16)


def _mla_weights(w_in, w_uq, w_ukv):
    nj = 4
    full = lambda a: pl.BlockSpec(a.shape, lambda j: (0, 0))
    cols = lambda rows, width: pl.BlockSpec((rows, width // nj), lambda j: (0, j))
    sel = _mla_selectors()
    widths = (PROJ_W, QK_PAD, QK_PAD, QK_PAD, MLA_WIDTH, QK_PAD)
    out_rows = (D_MODEL, Q_LORA_RANK, Q_LORA_RANK, 256, KV_LORA_RANK)
    return pl.pallas_call(
        _mla_wprep_kernel,
        grid=(nj,),
        in_specs=[full(w_in), full(w_uq), full(w_ukv)] + [cols(a.shape[0], w) for a, w in zip(sel, widths)],
        out_specs=[cols(r, w) for r, w in zip(out_rows, widths)],
        out_shape=[jax.ShapeDtypeStruct((r, w), BF16) for r, w in zip(out_rows, widths)],
        compiler_params=_params(("arbitrary",)),
        name="mla_weight_prep",
    )(w_in, w_uq, w_ukv, *sel)


def _lam_tiles(lam):
    lam = lam.reshape(2, 2, UNITS, UNIT_ST // LANE, LANE)
    lam = jnp.concatenate([lam[0, 0], lam[1, 0], lam[0, 1], lam[1, 1]], axis=1)
    return jnp.broadcast_to(lam[:, :, None, :], (UNITS, STATE_TILES, SUB, LANE))


def kernel(x, c, ctx, c_ctx, ada_w, ada_b, norm_g, mla_w_in, mla_q_norm, mla_w_uq, mla_kv_norm, mla_w_ukv, mla_w_out, s5_w_in, s5_a_re, s5_a_im, s5_log_step, s5_b_re, s5_b_im, s5_c_re, s5_c_im, s5_d, s5_w_glu, s5_b_glu, s5_w_out, final_g):
    cc = jnp.concatenate([c, c_ctx[None, :], jnp.zeros((7, D_MODEL), F32)], axis=0)
    mods = _modulation(cc, ada_w, ada_b)

    mod_lat = [mods[i, :BATCH, None, :] for i in range(2)]
    mod_ctx = [mods[i, BATCH:BATCH + 1, None, :] for i in range(2)]

    win, wqa, wqb, wk, wv = _mla_weights(mla_w_in[0], mla_w_uq[0], mla_w_ukv[0])
    weights = (norm_g[0][None, :], win, mla_q_norm[0][None, :], mla_kv_norm[0][None, :], wqa, wqb, wk, wv)
    tables = _rope_tables()
    qkvz = _mla_proj(x, mod_lat[0], weights, tables)
    q, k, v, sz = _mla_proj(ctx, mod_ctx[0], weights, tables, filled=qkvz)
    o = _attention(q, k, v)
    out_w = (norm_g[1][None, :], mla_w_out[0].astype(BF16), s5_w_in[0].astype(BF16))
    x1, xu, sz1 = _mla_out(o, sz, x, mod_lat[0], mod_lat[1], *out_w)
    xu, = _mla_out(o, sz, ctx, mod_ctx[0], mod_ctx[1], *out_w, xu_filled=xu)
    mod1 = mod_lat[1]

    n = 2 * S5_GROUPS
    lam, pb, cp, kk = _s5_prep(
        s5_a_re[0].reshape(n, S5_STATE), s5_a_im[0].reshape(n, S5_STATE), s5_log_step[0].reshape(n, 1),
        jnp.swapaxes(s5_b_re[0], -1, -2).reshape(n, S5_GROUP, S5_STATE),
        jnp.swapaxes(s5_b_im[0], -1, -2).reshape(n, S5_GROUP, S5_STATE),
        s5_c_re[0].reshape(n, S5_GROUP, S5_STATE), s5_c_im[0].reshape(n, S5_GROUP, S5_STATE))
    y = _s5_core(xu, kk.reshape(CH_T, 2, UNITS, UNIT_G, S5_GROUP, S5_GROUP),
                 pb.reshape(2, CH_T, 2, UNITS, UNIT_G, S5_GROUP, S5_STATE),
                 cp.reshape(2, CH_T, 2, UNITS, UNIT_G, S5_STATE, S5_GROUP),
                 s5_d[0].reshape(UNITS, UNIT_CH, 1), _lam_tiles(lam))
    return _finish(y, sz1, x1, mod1, s5_w_glu[0].astype(BF16), s5_b_glu[0][None, :], s5_w_out[0].astype(BF16),
                   final_g[None, :])
```

```python
import functools
import math

import jax
import jax.numpy as jnp
import numpy as np
from jax import lax
from jax.experimental import pallas as pl
from jax.experimental.pallas import tpu as pltpu

D_MODEL = 1024
BATCH = 8
SEQ = 2048
GRID_W = 64
CTX_LEN = 256
TOK = CTX_LEN + SEQ
EPS = 1e-6

MLA_HEADS = 16
QK_NOPE_DIM = 64
QK_ROPE_DIM = 32
V_HEAD_DIM = 64
Q_LORA_RANK = 256
KV_LORA_RANK = 128
MLA_WIDTH = MLA_HEADS * V_HEAD_DIM
QK_DIM = QK_NOPE_DIM + QK_ROPE_DIM
SOFTMAX_SCALE = QK_DIM ** -0.5
ROPE_THETA = 10000.0
HEAD_PAD = 128
QK_PAD = MLA_HEADS * HEAD_PAD
PROJ_W = 1536

S5_WIDTH = D_MODEL
S5_GROUP = 16
S5_GROUPS = 64
S5_STATE = 64
CH_T = 4
UNIT_G = 4
UNIT_CH = UNIT_G * S5_GROUP
UNITS = S5_GROUPS // UNIT_G
UNIT_K = CH_T * UNIT_CH
UNIT_ST = UNIT_G * S5_STATE
NCH = TOK // CH_T
NCH_CTX = CTX_LEN // CH_T
LANE = 128
SUB = 8

TL = 512
CTX_NB = 2
TF = 1024
TQ = 256
KCH = 256
HPAIRS = 2
assert TQ == CTX_LEN
VMEM_LIMIT = 56 * 1024 * 1024

F32 = jnp.float32
BF16 = jnp.bfloat16


def _params(sem, flags=None):
    return pltpu.CompilerParams(dimension_semantics=sem, vmem_limit_bytes=VMEM_LIMIT, flags=flags)


def _silu(v):
    return v * jax.nn.sigmoid(v)


def _rms(v, g):
    return v * lax.rsqrt(jnp.mean(v * v, axis=-1, keepdims=True) + EPS) * g


def _dot(a, b):
    return jnp.dot(a, b, preferred_element_type=F32)


def _mod_kernel(cc_ref, w_ref, b_ref, o_ref):
    a = _silu(cc_ref[...]).astype(BF16)
    o_ref[0] = _dot(a, w_ref[0].astype(BF16)) + b_ref[0]


def _modulation(cc, ada_w, ada_b):
    depth = ada_w.shape[0]
    tn = 768
    return pl.pallas_call(
        _mod_kernel,
        grid=(depth, 3 * D_MODEL // tn),
        in_specs=[
            pl.BlockSpec((16, D_MODEL), lambda i, j: (0, 0)),
            pl.BlockSpec((1, D_MODEL, tn), lambda i, j: (i, 0, j)),
            pl.BlockSpec((1, 1, tn), lambda i, j: (i, 0, j)),
        ],
        out_specs=pl.BlockSpec((1, 16, tn), lambda i, j: (i, 0, j)),
        out_shape=jax.ShapeDtypeStruct((depth, 16, 3 * D_MODEL), F32),
        compiler_params=_params(("arbitrary", "arbitrary")),
        name="modulation",
    )(cc, ada_w, ada_b.reshape(depth, 1, 3 * D_MODEL))


def _mod_spec(per_batch):
    return pl.BlockSpec((1, 1, 3 * D_MODEL), (lambda b, i: (b, 0, 0)) if per_batch else (lambda b, i: (0, 0, 0)))


def _mla_proj_kernel(x_ref, mod_ref, g_ref, win_ref, qg_ref, kvg_ref, wqa_ref, wqb_ref, wk_ref, wv_ref,
                     cos_ref, sin_ref, kt_ref, *refs):
    q_ref, k_ref, v_ref, sz_ref = refs[-4:]
    nb, tile = x_ref.shape[0], x_ref.shape[1]
    x = x_ref[...].reshape(nb * tile, D_MODEL)
    mod = mod_ref[0]
    sh = mod[:, :D_MODEL]
    sc = mod[:, D_MODEL:2 * D_MODEL]
    h = _rms(x, g_ref[...]) * (1.0 + sc) + sh
    p = _dot(h.astype(BF16), win_ref[...])
    cqn = _rms(p[:, :Q_LORA_RANK], qg_ref[...]).astype(BF16)
    ckvn = _rms(p[:, Q_LORA_RANK:Q_LORA_RANK + KV_LORA_RANK], kvg_ref[...]).astype(BF16)
    kr = p[:, 384:512]
    z = p[:, 512:]
    qa = _dot(cqn, wqa_ref[...])
    qb = _dot(cqn, wqb_ref[...])
    cos, sin, kt = (jnp.concatenate([t[...]] * nb, axis=0) for t in (cos_ref, sin_ref, kt_ref))
    for hd in range(MLA_HEADS):
        sl = slice(hd * HEAD_PAD, (hd + 1) * HEAD_PAD)
        q_ref[:, :, sl] = (qa[:, sl] * cos + qb[:, sl] * sin).astype(BF16).reshape(nb, tile, HEAD_PAD)
    kin = jnp.concatenate([ckvn, (kr * kt).astype(BF16)], axis=-1)
    k_ref[...] = _dot(kin, wk_ref[...]).astype(BF16).reshape(nb, tile, QK_PAD)
    v_ref[...] = _dot(ckvn, wv_ref[...]).astype(BF16).reshape(nb, tile, MLA_WIDTH)
    sz_ref[...] = _silu(z).astype(BF16).reshape(nb, tile, MLA_WIDTH)


def _mla_proj(xs, mod, weights, tables, filled=None):
    is_ctx = filled is not None
    tile, nb = (CTX_LEN, CTX_NB) if is_ctx else (TL, 1)
    off = SEQ // tile if is_ctx else 0
    full = lambda a: pl.BlockSpec(a.shape, lambda b, i: (0,) * a.ndim)
    tok = lambda w: pl.BlockSpec((nb, tile, w), lambda b, i: (b, i + off, 0))
    pos = pl.BlockSpec((tile, HEAD_PAD), lambda b, i: (i + off, 0))
    widths = (QK_PAD, QK_PAD, MLA_WIDTH, MLA_WIDTH)
    in_specs = [pl.BlockSpec((nb, tile, D_MODEL), lambda b, i: (b, i, 0)), _mod_spec(not is_ctx)]
    in_specs += [full(w) for w in weights] + [pos, pos, pos]
    args = [xs, mod, *weights, *tables]
    aliases = {}
    if is_ctx:
        aliases = {len(args) + n: n for n in range(4)}
        in_specs += [pl.BlockSpec(memory_space=pl.ANY)] * 4
        args += list(filled)
    return pl.pallas_call(
        _mla_proj_kernel,
        grid=(BATCH // nb, xs.shape[1] // tile),
        in_specs=in_specs,
        out_specs=[tok(w) for w in widths],
        out_shape=[jax.ShapeDtypeStruct((BATCH, TOK, w), BF16) for w in widths],
        input_output_aliases=aliases,
        compiler_params=_params(("arbitrary", "arbitrary")),
        name="mla_proj_ctx" if is_ctx else "mla_proj",
    )(*args)


def _attn_kernel(q_ref, k_ref, v_ref, o_ref, s_buf, m_buf, vx_buf, cs_buf, cm_buf):
    nt = SEQ // TQ
    lane = lax.broadcasted_iota(jnp.int32, (TOK, 2 * V_HEAD_DIM), 1)
    for hp in range(HPAIRS):
        v = v_ref[0, :, hp * 2 * V_HEAD_DIM:(hp + 1) * 2 * V_HEAD_DIM]
        vx_buf[hp, 0] = jnp.where(lane < V_HEAD_DIM, v, (lane == V_HEAD_DIM).astype(BF16))
        vx_buf[hp, 1] = jnp.where(lane >= V_HEAD_DIM, v, (lane == 0).astype(BF16))

    def scores(hp, row, k0, nk, slot):
        sb, mb = (cs_buf.at[hp], cm_buf.at[hp]) if slot is None else (s_buf.at[slot], m_buf.at[slot])
        for hh in range(2):
            c0 = (2 * hp + hh) * HEAD_PAD
            s = lax.dot_general(q_ref[0, pl.ds(row, TQ), c0:c0 + HEAD_PAD], k_ref[0, k0:k0 + nk, c0:c0 + HEAD_PAD],
                                (((1,), (1,)), ((), ())), preferred_element_type=F32)
            sb[hh, :, :nk] = s
            mb[hh] = jnp.broadcast_to(jnp.max(s, axis=-1, keepdims=True), (TQ, KCH))

    def values(hp, row, k0, nk, slot):
        sb, mb = (cs_buf.at[hp], cm_buf.at[hp]) if slot is None else (s_buf.at[slot], m_buf.at[slot])
        outs = []
        for hh in range(2):
            m = mb[hh]
            ps = [jnp.exp2(sb[hh, :, n * KCH:(n + 1) * KCH] - m).astype(BF16) for n in range(nk // KCH)]
            acc = _dot(jnp.concatenate(ps, axis=-1), vx_buf[hp, hh, k0:k0 + nk, :])
            l_col = V_HEAD_DIM if hh == 0 else 0
            outs.append(acc / acc[:, l_col:l_col + 1])
        olane = lax.broadcasted_iota(jnp.int32, outs[0].shape, 1)
        o_ref[0, pl.ds(row, TQ), hp * 2 * V_HEAD_DIM:(hp + 1) * 2 * V_HEAD_DIM] = jnp.where(
            olane < V_HEAD_DIM, outs[0], outs[1]).astype(BF16)

    for hp in range(HPAIRS):
        scores(hp, SEQ, SEQ, CTX_LEN, None)
    scores(0, 0, 0, TOK, 0)
    for hp in range(HPAIRS):
        values(hp, SEQ, SEQ, CTX_LEN, None)
    for hp in range(HPAIRS):
        for t in range(1, nt):
            scores(hp, t * TQ, 0, TOK, t % 2)
            values(hp, (t - 1) * TQ, 0, TOK, (t - 1) % 2)
        if hp + 1 < HPAIRS:
            scores(hp + 1, 0, 0, TOK, 0)
        values(hp, (nt - 1) * TQ, 0, TOK, (nt - 1) % 2)


def _attention(q, k, v):
    qk = pl.BlockSpec((1, TOK, HPAIRS * 2 * HEAD_PAD), lambda b, h: (b, 0, h))
    vo = pl.BlockSpec((1, TOK, HPAIRS * 2 * V_HEAD_DIM), lambda b, h: (b, 0, h))
    return pl.pallas_call(
        _attn_kernel,
        grid=(BATCH, MLA_HEADS // (2 * HPAIRS)),
        in_specs=[qk, qk, vo],
        out_specs=vo,
        out_shape=jax.ShapeDtypeStruct((BATCH, TOK, MLA_WIDTH), BF16),
        scratch_shapes=[
            pltpu.VMEM((2, 2, TQ, TOK), F32),
            pltpu.VMEM((2, 2, TQ, KCH), F32),
            pltpu.VMEM((HPAIRS, 2, TOK, 2 * V_HEAD_DIM), BF16),
            pltpu.VMEM((HPAIRS, 2, TQ, CTX_LEN), F32),
            pltpu.VMEM((HPAIRS, 2, TQ, KCH), F32),
        ],
        compiler_params=_params(("arbitrary", "arbitrary")),
        name="attention",
    )(q, k, v)


def _swap_halves(va, vb):
    lo = lax.broadcasted_iota(jnp.int32, va.shape, 1) < UNIT_CH
    return (jnp.where(lo, va, pltpu.roll(vb, UNIT_CH, 1)),
            jnp.where(lo, pltpu.roll(va, UNIT_CH, 1), vb))


def _mla_out_kernel(o_ref, sz_ref, x_ref, mod0_ref, mod1_ref, g1_ref, wout_ref, win_ref, *refs, is_ctx):
    if is_ctx:
        xu_ref, tok_scr = refs[-2:]
    else:
        x1_ref, xu_ref, sz1_ref, tok_scr = refs
    nb, tile = x_ref.shape[0], x_ref.shape[1]
    rows = nb * tile
    a = (o_ref[...].astype(F32) * sz_ref[...].astype(F32)).astype(BF16).reshape(rows, MLA_WIDTH)
    gt = mod0_ref[0][:, 2 * D_MODEL:]
    x1 = x_ref[...].reshape(rows, D_MODEL) + gt * _dot(a, wout_ref[...])
    mod1 = mod1_ref[0]
    h = _rms(x1, g1_ref[...]) * (1.0 + mod1[:, D_MODEL:2 * D_MODEL]) + mod1[:, :D_MODEL]
    if is_ctx:
        u = _dot(h.astype(BF16), win_ref[:, :S5_WIDTH])
    else:
        x1_ref[0] = x1
        p = _dot(h.astype(BF16), win_ref[...])
        sz1_ref[0] = _silu(p[:, S5_WIDTH:]).astype(BF16)
        u = p[:, :S5_WIDTH]
    for m in range(S5_WIDTH // LANE):
        tok_scr[m] = u[:, m * LANE:(m + 1) * LANE]
    for m in range(S5_WIDTH // LANE):
        v = [tok_scr[m, pl.ds(t, rows // CH_T, stride=CH_T), :] for t in range(CH_T)]
        for hf in range(CH_T // 2):
            even, odd = _swap_halves(v[2 * hf], v[2 * hf + 1])
            c0 = 2 * m * UNIT_K + hf * LANE
            xu_ref[:, :, c0:c0 + LANE] = even.astype(BF16).reshape(nb, tile // CH_T, LANE)
            xu_ref[:, :, c0 + UNIT_K:c0 + UNIT_K + LANE] = odd.astype(BF16).reshape(nb, tile // CH_T, LANE)


def _mla_out(o, sz, xs, mod0, mod1, g1, wout, win, xu_filled=None):
    is_ctx = xu_filled is not None
    tile, nb = (CTX_LEN, CTX_NB) if is_ctx else (TL, 1)
    off = SEQ // tile if is_ctx else 0
    full = lambda a: pl.BlockSpec(a.shape, lambda b, i: (0,) * a.ndim)
    shared = lambda w: pl.BlockSpec((nb, tile, w), lambda b, i: (b, i + off, 0))
    own = lambda w: pl.BlockSpec((nb, tile, w), lambda b, i: (b, i, 0))
    xu_spec = pl.BlockSpec((nb, tile // CH_T, UNITS * UNIT_K), lambda b, i: (b, i + off, 0))
    xu_shape = jax.ShapeDtypeStruct((BATCH, NCH, UNITS * UNIT_K), BF16)
    in_specs = [shared(MLA_WIDTH), shared(MLA_WIDTH), own(D_MODEL), _mod_spec(not is_ctx), _mod_spec(not is_ctx),
                full(g1), full(wout), full(win)]
    args = [o, sz, xs, mod0, mod1, g1, wout, win]
    if is_ctx:
        in_specs.append(pl.BlockSpec(memory_space=pl.ANY))
        args.append(xu_filled)
        out_specs, out_shape, aliases = [xu_spec], [xu_shape], {len(args) - 1: 0}
    else:
        out_specs = [own(D_MODEL), xu_spec, own(S5_WIDTH)]
        out_shape = [jax.ShapeDtypeStruct((BATCH, SEQ, D_MODEL), F32), xu_shape,
                     jax.ShapeDtypeStruct((BATCH, SEQ, S5_WIDTH), BF16)]
        aliases = {}
    return pl.pallas_call(
        functools.partial(_mla_out_kernel, is_ctx=is_ctx),
        grid=(BATCH // nb, xs.shape[1] // tile),
        in_specs=in_specs,
        out_specs=out_specs,
        out_shape=out_shape,
        input_output_aliases=aliases,
        scratch_shapes=[pltpu.VMEM((S5_WIDTH // LANE, nb * tile, LANE), F32)],
        compiler_params=_params(("arbitrary", "arbitrary")),
        name="mla_out_s5_in_ctx" if is_ctx else "mla_out_s5_in",
    )(*args)


def _cmul(ar, ai, br, bi):
    return ar * br - ai * bi, ar * bi + ai * br


def _group_dot(a, b, precision=lax.Precision.HIGHEST):
    return lax.dot_general(a, b, (((2,), (2,)), ((0,), (0,))), precision=precision, preferred_element_type=F32)


def _group_transpose(eye, a):
    return _group_dot(eye, a.astype(BF16), precision=None)


def _s5_prep_kernel(are_ref, aim_ref, ls_ref, bre_ref, bim_ref, cre_ref, cim_ref, lam_ref, pb_ref, cp_ref, kk_ref):
    n = are_ref.shape[0]
    eye = (lax.broadcasted_iota(jnp.int32, (n, S5_STATE, S5_STATE), 1)
           == lax.broadcasted_iota(jnp.int32, (n, S5_STATE, S5_STATE), 2)).astype(BF16)
    ar = are_ref[...]
    ai = aim_ref[...]
    dt = jnp.exp(ls_ref[...])
    mag = jnp.exp(ar * dt)
    lb_re = mag * jnp.cos(ai * dt)
    lb_im = mag * jnp.sin(ai * dt)
    den = ar * ar + ai * ai
    nr = lb_re - 1.0
    f_re = ((nr * ar + lb_im * ai) / den)[:, None, :]
    f_im = ((lb_im * ar - nr * ai) / den)[:, None, :]
    bb_re, bb_im = _cmul(f_re, f_im, bre_ref[...], bim_ref[...])
    c_re = cre_ref[...]
    c_im = cim_ref[...]
    pw_re = jnp.ones_like(lb_re)
    pw_im = jnp.zeros_like(lb_re)
    for r in range(CH_T + 1):
        pr = pw_re[:, None, :]
        pi = pw_im[:, None, :]
        cl_re, cl_im = _cmul(c_re, c_im, pr, pi)
        if r < CH_T:
            q_re, q_im = _cmul(pr, pi, bb_re, bb_im)
            pb_ref[0, r] = q_re
            pb_ref[1, r] = q_im
            kk_ref[r] = _group_dot(bb_re, cl_re) - _group_dot(bb_im, cl_im)
        if r > 0:
            cp_ref[0, r - 1] = _group_transpose(eye, cl_re)
            cp_ref[1, r - 1] = _group_transpose(eye, -cl_im)
        if r == CH_T:
            lam_ref[0] = pw_re
            lam_ref[1] = pw_im
        else:
            pw_re, pw_im = _cmul(pw_re, pw_im, lb_re, lb_im)


def _s5_prep(a_re, a_im, log_step, b_re_t, b_im_t, c_re, c_im):
    n = a_re.shape[0]
    nb = 32
    row2 = pl.BlockSpec((nb, S5_STATE), lambda i: (i, 0))
    row3 = pl.BlockSpec((nb, S5_GROUP, S5_STATE), lambda i: (i, 0, 0))
    return pl.pallas_call(
        _s5_prep_kernel,
        grid=(n // nb,),
        in_specs=[row2, row2, pl.BlockSpec((nb, 1), lambda i: (i, 0)), row3, row3, row3, row3],
        out_specs=[
            pl.BlockSpec((2, nb, S5_STATE), lambda i: (0, i, 0)),
            pl.BlockSpec((2, CH_T, nb, S5_GROUP, S5_STATE), lambda i: (0, 0, i, 0, 0)),
            pl.BlockSpec((2, CH_T, nb, S5_STATE, S5_GROUP), lambda i: (0, 0, i, 0, 0)),
            pl.BlockSpec((CH_T, nb, S5_GROUP, S5_GROUP), lambda i: (0, i, 0, 0)),
        ],
        out_shape=[
            jax.ShapeDtypeStruct((2, n, S5_STATE), F32),
            jax.ShapeDtypeStruct((2, CH_T, n, S5_GROUP, S5_STATE), F32),
            jax.ShapeDtypeStruct((2, CH_T, n, S5_STATE, S5_GROUP), F32),
            jax.ShapeDtypeStruct((CH_T, n, S5_GROUP, S5_GROUP), F32),
        ],
        compiler_params=_params(("arbitrary",)),
        name="s5_prep",
    )(a_re, a_im, log_step, b_re_t, b_im_t, c_re, c_im)


STATE_TILES = 2 * 2 * UNIT_ST // LANE


def _hdot(a, rep):
    return _dot(a.astype(BF16), rep)


def _unit_operators(kk_ref, pb_ref, cp_ref, d_ref):
    def iota(shape, dim):
        return lax.broadcasted_iota(jnp.int32, shape, dim)

    rep16 = (iota((S5_GROUP, UNIT_K), 1) % S5_GROUP == iota((S5_GROUP, UNIT_K), 0)).astype(BF16)
    rep64 = (iota((S5_STATE, UNIT_ST), 1) % S5_STATE == iota((S5_STATE, UNIT_ST), 0)).astype(BF16)
    row = iota((UNIT_CH, UNIT_K), 0)
    col = iota((UNIT_CH, UNIT_K), 1)
    same_group_out = row // S5_GROUP == (col // S5_GROUP) % UNIT_G
    same_group_st = row // S5_GROUP == col // S5_STATE
    on_diag = row == col % UNIT_CH
    col_t = col // UNIT_CH
    srow = iota((UNIT_ST, UNIT_K), 0)
    scol = iota((UNIT_ST, UNIT_K), 1)
    st_same_group = srow // S5_STATE == (scol // S5_GROUP) % UNIT_G
    st_col_t = scol // UNIT_CH
    kexp = _hdot(jnp.concatenate([kk_ref[k, d, 0].reshape(UNIT_CH, S5_GROUP)
                                  for d in range(2) for k in range(CH_T)], axis=0), rep16)
    pexp = _hdot(jnp.concatenate([pb_ref[ri, r, d, 0].reshape(UNIT_CH, S5_STATE)
                                  for d in range(2) for ri in range(2) for r in range(CH_T)], axis=0), rep64)
    cexp = _hdot(jnp.concatenate([cp_ref[ri, rr, d, 0].reshape(UNIT_ST, S5_GROUP)
                                  for d in range(2) for ri in range(2) for rr in range(CH_T)], axis=0), rep16)

    def blk(a, idx, nrows):
        return a[idx * nrows:(idx + 1) * nrows]

    rows = []
    for j in range(CH_T):
        acc = jnp.where((col_t == j) & on_diag, d_ref[0], 0.0)
        for d in range(2):
            for k in range(CH_T):
                lag_ok = (col_t - j == k) if d == 0 else (j - col_t == k)
                acc = acc + jnp.where(lag_ok & same_group_out, blk(kexp, d * CH_T + k, UNIT_CH), 0.0)
        rows.append(acc)
    parts = [jnp.concatenate(rows, axis=0)]
    cos = []
    for d in range(2):
        rows = []
        for j in range(CH_T):
            r = CH_T - 1 - j if d == 0 else j
            rows.append(jnp.concatenate(
                [jnp.where(same_group_st, blk(pexp, (d * 2 + ri) * CH_T + r, UNIT_CH), 0.0) for ri in range(2)],
                axis=-1))
        parts.append(jnp.concatenate(rows, axis=0))
        for ri in range(2):
            acc = jnp.zeros((UNIT_ST, UNIT_K), F32)
            for rr in range(CH_T):
                t = rr if d == 0 else CH_T - 1 - rr
                acc = acc + jnp.where((st_col_t == t) & st_same_group,
                                      blk(cexp, (d * 2 + ri) * CH_T + rr, UNIT_ST), 0.0)
            cos.append(acc)
    return jnp.concatenate(parts, axis=-1).astype(BF16), jnp.concatenate(cos, axis=0).astype(BF16)


def _s5_core_kernel(x_ref, kk_ref, pb_ref, cp_ref, d_ref, lam_ref, y_ref, st_scr, sp_scr, y0_scr):
    w1, co = _unit_operators(kk_ref, pb_ref, cp_ref, d_ref)
    nlat = NCH - NCH_CTX
    for b in range(BATCH):
        r = _dot(x_ref[b], w1)
        y0_scr[b] = r[:nlat, :UNIT_K]
        for lt in range(STATE_TILES):
            c0 = UNIT_K + lt * LANE
            st_scr[lt, pl.ds(b, NCH, stride=BATCH), :] = r[:, c0:c0 + LANE]
    lam = [lam_ref[0, lt] for lt in range(STATE_TILES)]

    def rows(chunk):
        return pl.ds(pl.multiple_of(chunk * BATCH, BATCH), BATCH)

    def load_z(row, base):
        return [st_scr[base + k, rows(row), :] for k in range(4)]

    def advance(state, z, base):
        ar0, ar1, ai0, ai1 = lam[base:base + 4]
        return [ar0 * state[0] - ai0 * state[2] + z[0], ar1 * state[1] - ai1 * state[3] + z[1],
                ar0 * state[2] + ai0 * state[0] + z[2], ar1 * state[3] + ai1 * state[1] + z[3]]

    def keep(lo, hi, chunk_lo, base):
        r16 = pl.ds(pl.multiple_of(chunk_lo * BATCH, 2 * BATCH), 2 * BATCH)
        for k in range(4):
            sp_scr[base + k, r16, :] = jnp.concatenate([lo[k], hi[k]], axis=0).astype(BF16)

    def fwd_row(i):
        return jnp.where(i < NCH_CTX, nlat + i, i - NCH_CTX)

    def two_steps(m, carry):
        s_f, z_f, s_b, z_b = carry
        i = 2 * m
        rf = fwd_row(i)
        rb = NCH - 1 - i
        z_f1 = load_z(rf + 1, 0)
        z_b1 = load_z(rb - 1, 4)
        nxt = jnp.minimum(i + 2, NCH - 2)
        z_f2 = load_z(fwd_row(nxt), 0)
        z_b2 = load_z(NCH - 1 - nxt, 4)
        s_f1 = advance(s_f, z_f, 0)
        s_b1 = advance(s_b, z_b, 4)
        keep(s_f, s_f1, rf, 0)
        keep(s_b1, s_b, rb - 1, 4)
        return advance(s_f1, z_f1, 0), z_f2, advance(s_b1, z_b1, 4), z_b2

    zero = [jnp.zeros((BATCH, LANE), F32)] * 4
    lax.fori_loop(0, NCH // 2, two_steps, (zero, load_z(nlat, 0), zero, load_z(NCH - 1, 4)))
    for rb in range(BATCH):
        sl = slice(rb * nlat, (rb + 1) * nlat)
        lhs = jnp.concatenate([sp_scr[lt, sl, :] for lt in range(STATE_TILES)], axis=-1)
        yi = _dot(lhs, co)
        for t in range(UNIT_K // LANE):
            st_scr[t, sl, :] = yi[:, t * LANE:(t + 1) * LANE]
    for b in range(BATCH):
        y_ref[b] = (y0_scr[b] + jnp.concatenate(
            [st_scr[t, pl.ds(b, nlat, stride=BATCH), :] for t in range(UNIT_K // LANE)], axis=-1)).astype(BF16)


def _s5_core(xu, kk, pb, cp, d, lam):
    return pl.pallas_call(
        _s5_core_kernel,
        grid=(UNITS,),
        in_specs=[
            pl.BlockSpec((BATCH, NCH, UNIT_K), lambda q: (0, 0, q)),
            pl.BlockSpec((CH_T, 2, 1, UNIT_G, S5_GROUP, S5_GROUP), lambda q: (0, 0, q, 0, 0, 0)),
            pl.BlockSpec((2, CH_T, 2, 1, UNIT_G, S5_GROUP, S5_STATE), lambda q: (0, 0, 0, q, 0, 0, 0)),
            pl.BlockSpec((2, CH_T, 2, 1, UNIT_G, S5_STATE, S5_GROUP), lambda q: (0, 0, 0, q, 0, 0, 0)),
            pl.BlockSpec((1, UNIT_CH, 1), lambda q: (q, 0, 0)),
            pl.BlockSpec((1, STATE_TILES, SUB, LANE), lambda q: (q, 0, 0, 0)),
        ],
        out_specs=pl.BlockSpec((BATCH, NCH - NCH_CTX, UNIT_K), lambda q: (0, 0, q)),
        out_shape=jax.ShapeDtypeStruct((BATCH, NCH - NCH_CTX, UNITS * UNIT_K), BF16),
        scratch_shapes=[pltpu.VMEM((STATE_TILES, BATCH * NCH, LANE), F32),
                        pltpu.VMEM((STATE_TILES, BATCH * NCH, LANE), BF16),
                        pltpu.VMEM((BATCH, NCH - NCH_CTX, UNIT_K), F32)],
        compiler_params=_params(("arbitrary",)),
        name="s5_core",
    )(xu, kk, pb, cp, d, lam)


def _fin_kernel(y_ref, sz_ref, x_ref, mod_ref, wglu_ref, bglu_ref, wout_ref, fg_ref, o_ref, tok_scr):
    for m in range(S5_WIDTH // LANE):
        for hf in range(CH_T // 2):
            c0 = 2 * m * UNIT_K + hf * LANE
            va, vb = _swap_halves(y_ref[0, :, c0:c0 + LANE].astype(F32),
                                  y_ref[0, :, c0 + UNIT_K:c0 + UNIT_K + LANE].astype(F32))
            tok_scr[m, pl.ds(2 * hf, TF // CH_T, stride=CH_T), :] = va
            tok_scr[m, pl.ds(2 * hf + 1, TF // CH_T, stride=CH_T), :] = vb
    y = jnp.concatenate([tok_scr[m] for m in range(S5_WIDTH // LANE)], axis=-1)
    y = jax.nn.gelu(y)
    y = y * jax.nn.sigmoid(_dot(y.astype(BF16), wglu_ref[...]) + bglu_ref[...])
    a = (y * sz_ref[0].astype(F32)).astype(BF16)
    gt = mod_ref[0][:, 2 * D_MODEL:]
    x2 = x_ref[0] + gt * _dot(a, wout_ref[...])
    o_ref[0] = _rms(x2, fg_ref[...])


def _finish(y, sz1, x1, mod1, wglu, bglu, wout, fg):
    full = lambda shape: pl.BlockSpec(shape, lambda b, i: (0,) * len(shape))
    tok = pl.BlockSpec((1, TF, D_MODEL), lambda b, i: (b, i, 0))
    return pl.pallas_call(
        _fin_kernel,
        grid=(BATCH, SEQ // TF),
        in_specs=[pl.BlockSpec((1, TF // CH_T, UNITS * UNIT_K), lambda b, i: (b, i, 0)), tok, tok,
                  _mod_spec(True),
                  full((S5_WIDTH, S5_WIDTH)), full((1, S5_WIDTH)),
                  full((S5_WIDTH, D_MODEL)), full((1, D_MODEL))],
        out_specs=tok,
        out_shape=jax.ShapeDtypeStruct((BATCH, SEQ, D_MODEL), F32),
        scratch_shapes=[pltpu.VMEM((S5_WIDTH // LANE, TF, LANE), F32)],
        compiler_params=_params(("arbitrary", "arbitrary")),
        name="s5_finish",
    )(y, sz1, x1, mod1, wglu, bglu, wout, fg)


def _rope_tables():
    h = QK_ROPE_DIM // 2
    inv = 1.0 / (ROPE_THETA ** (np.arange(0, h, 2, dtype=np.float64) / h))
    pos = np.arange(SEQ)
    ang_r = (pos // GRID_W)[:, None] * inv[None, :]
    ang_c = (pos % GRID_W)[:, None] * inv[None, :]
    cos32 = np.concatenate([np.cos(ang_r)] * 2 + [np.cos(ang_c)] * 2, axis=-1)
    sin32 = np.concatenate([np.sin(ang_r)] * 2 + [np.sin(ang_c)] * 2, axis=-1)
    cos = np.zeros((TOK, HEAD_PAD), np.float32)
    sin = np.zeros((TOK, HEAD_PAD), np.float32)
    kt = np.zeros((TOK, HEAD_PAD), np.float32)
    cos[:, :QK_NOPE_DIM] = 1.0
    cos[SEQ:, QK_NOPE_DIM:QK_DIM] = 1.0
    kt[SEQ:, :QK_ROPE_DIM] = 1.0
    cos[:SEQ, QK_NOPE_DIM:QK_DIM] = cos32
    sin[:SEQ, QK_NOPE_DIM:QK_DIM] = sin32
    kt[:SEQ, :QK_ROPE_DIM] = cos32
    kt[:SEQ, QK_ROPE_DIM:2 * QK_ROPE_DIM] = sin32
    return jnp.asarray(cos), jnp.asarray(sin), jnp.asarray(kt)


def _mla_selectors():
    def partner(d):
        return (d + 8, -1.0) if d % 16 < 8 else (d - 8, 1.0)

    o2 = Q_LORA_RANK + KV_LORA_RANK
    o3 = o2 + QK_ROPE_DIM
    pin = np.zeros((o3 + MLA_WIDTH, PROJ_W), np.float32)
    pin[np.arange(o3), np.arange(o3)] = 1.0
    pin[o3 + np.arange(MLA_WIDTH), 512 + np.arange(MLA_WIDTH)] = 1.0
    pa = np.zeros((MLA_HEADS * QK_DIM, QK_PAD), np.float32)
    pb = np.zeros((MLA_HEADS * QK_DIM, QK_PAD), np.float32)
    pk = np.zeros((MLA_HEADS * 128, QK_PAD), np.float32)
    pv = np.zeros((MLA_HEADS * 128, MLA_WIDTH), np.float32)
    kb = np.zeros((128, QK_PAD), np.float32)
    for d in range(QK_ROPE_DIM):
        src, sign = partner(d)
        pin[o2 + src, o3 + d] = sign
        for hd in range(MLA_HEADS):
            pb[hd * QK_DIM + QK_NOPE_DIM + src, hd * HEAD_PAD + QK_NOPE_DIM + d] = sign
            kb[d, hd * HEAD_PAD + QK_NOPE_DIM + d] = 1.0
            kb[QK_ROPE_DIM + d, hd * HEAD_PAD + QK_NOPE_DIM + d] = 1.0
    for hd in range(MLA_HEADS):
        pa[hd * QK_DIM + np.arange(QK_DIM), hd * HEAD_PAD + np.arange(QK_DIM)] = 1.0
        pk[hd * 128 + np.arange(QK_NOPE_DIM), hd * HEAD_PAD + np.arange(QK_NOPE_DIM)] = 1.0
        pv[hd * 128 + QK_NOPE_DIM + np.arange(V_HEAD_DIM), hd * V_HEAD_DIM + np.arange(V_HEAD_DIM)] = 1.0
    return [jnp.asarray(a, dtype=BF16) for a in (pin, pa, pb, pk, pv, kb)]


def _mla_wprep_kernel(win_ref, wuq_ref, wukv_ref, pin_ref, pa_ref, pb_ref, pk_ref, pv_ref, kb_ref,
                      o_in, o_qa, o_qb, o_k, o_v):
    o_in[...] = _dot(win_ref[...].astype(BF16), pin_ref[...]).astype(BF16)
    wq = (wuq_ref[...] * (SOFTMAX_SCALE * math.log2(math.e))).astype(BF16)
    o_qa[...] = _dot(wq, pa_ref[...]).astype(BF16)
    o_qb[...] = _dot(wq, pb_ref[...]).astype(BF16)
    wkv = wukv_ref[...].astype(BF16)
    o_k[:KV_LORA_RANK] = _dot(wkv, pk_ref[...]).astype(BF16)
    o_k[KV_LORA_RANK:] = kb_ref[...]
    o_v[...] = _dot(wkv, pv_ref[...]).astype(BF16)


def _mla_weights(w_in, w_uq, w_ukv):
    nj = 4
    full = lambda a: pl.BlockSpec(a.shape, lambda j: (0, 0))
    cols = lambda rows, width: pl.BlockSpec((rows, width // nj), lambda j: (0, j))
    sel = _mla_selectors()
    widths = (PROJ_W, QK_PAD, QK_PAD, QK_PAD, MLA_WIDTH, QK_PAD)
    out_rows = (D_MODEL, Q_LORA_RANK, Q_LORA_RANK, 256, KV_LORA_RANK)
    return pl.pallas_call(
        _mla_wprep_kernel,
        grid=(nj,),
        in_specs=[full(w_in), full(w_uq), full(w_ukv)] + [cols(a.shape[0], w) for a, w in zip(sel, widths)],
        out_specs=[cols(r, w) for r, w in zip(out_rows, widths)],
        out_shape=[jax.ShapeDtypeStruct((r, w), BF16) for r, w in zip(out_rows, widths)],
        compiler_params=_params(("arbitrary",)),
        name="mla_weight_prep",
    )(w_in, w_uq, w_ukv, *sel)


def _lam_tiles(lam):
    lam = lam.reshape(2, 2, UNITS, UNIT_ST // LANE, LANE)
    lam = jnp.concatenate([lam[0, 0], lam[1, 0], lam[0, 1], lam[1, 1]], axis=1)
    return jnp.broadcast_to(lam[:, :, None, :], (UNITS, STATE_TILES, SUB, LANE))


def kernel(x, c, ctx, c_ctx, ada_w, ada_b, norm_g, mla_w_in, mla_q_norm, mla_w_uq, mla_kv_norm, mla_w_ukv, mla_w_out, s5_w_in, s5_a_re, s5_a_im, s5_log_step, s5_b_re, s5_b_im, s5_c_re, s5_c_im, s5_d, s5_w_glu, s5_b_glu, s5_w_out, final_g):
    cc = jnp.concatenate([c, c_ctx[None, :], jnp.zeros((7, D_MODEL), F32)], axis=0)
    mods = _modulation(cc, ada_w, ada_b)

    mod_lat = [mods[i, :BATCH, None, :] for i in range(2)]
    mod_ctx = [mods[i, BATCH:BATCH + 1, None, :] for i in range(2)]

    win, wqa, wqb, wk, wv = _mla_weights(mla_w_in[0], mla_w_uq[0], mla_w_ukv[0])
    weights = (norm_g[0][None, :], win, mla_q_norm[0][None, :], mla_kv_norm[0][None, :], wqa, wqb, wk, wv)
    tables = _rope_tables()
    qkvz = _mla_proj(x, mod_lat[0], weights, tables)
    q, k, v, sz = _mla_proj(ctx, mod_ctx[0], weights, tables, filled=qkvz)
    o = _attention(q, k, v)
    out_w = (norm_g[1][None, :], mla_w_out[0].astype(BF16), s5_w_in[0].astype(BF16))
    x1, xu, sz1 = _mla_out(o, sz, x, mod_lat[0], mod_lat[1], *out_w)
    xu, = _mla_out(o, sz, ctx, mod_ctx[0], mod_ctx[1], *out_w, xu_filled=xu)
    mod1 = mod_lat[1]

    n = 2 * S5_GROUPS
    lam, pb, cp, kk = _s5_prep(
        s5_a_re[0].reshape(n, S5_STATE), s5_a_im[0].reshape(n, S5_STATE), s5_log_step[0].reshape(n, 1),
        jnp.swapaxes(s5_b_re[0], -1, -2).reshape(n, S5_GROUP, S5_STATE),
        jnp.swapaxes(s5_b_im[0], -1, -2).reshape(n, S5_GROUP, S5_STATE),
        s5_c_re[0].reshape(n, S5_GROUP, S5_STATE), s5_c_im[0].reshape(n, S5_GROUP, S5_STATE))
    y = _s5_core(xu, kk.reshape(CH_T, 2, UNITS, UNIT_G, S5_GROUP, S5_GROUP),
                 pb.reshape(2, CH_T, 2, UNITS, UNIT_G, S5_GROUP, S5_STATE),
                 cp.reshape(2, CH_T, 2, UNITS, UNIT_G, S5_STATE, S5_GROUP),
                 s5_d[0].reshape(UNITS, UNIT_CH, 1), _lam_tiles(lam))
    return _finish(y, sz1, x1, mod1, s5_w_glu[0].astype(BF16), s5_b_glu[0][None, :], s5_w_out[0].astype(BF16),
                   final_g[None, :])
```

```python
import functools
import math

import jax
import jax.numpy as jnp
import numpy as np
from jax import lax
from jax.experimental import pallas as pl
from jax.experimental.pallas import tpu as pltpu

D_MODEL = 1024
BATCH = 8
SEQ = 2048
GRID_W = 64
CTX_LEN = 256
TOK = CTX_LEN + SEQ
EPS = 1e-6

MLA_HEADS = 16
QK_NOPE_DIM = 64
QK_ROPE_DIM = 32
V_HEAD_DIM = 64
Q_LORA_RANK = 256
KV_LORA_RANK = 128
MLA_WIDTH = MLA_HEADS * V_HEAD_DIM
QK_DIM = QK_NOPE_DIM + QK_ROPE_DIM
SOFTMAX_SCALE = QK_DIM ** -0.5
ROPE_THETA = 10000.0
HEAD_PAD = 128
QK_PAD = MLA_HEADS * HEAD_PAD
PROJ_W = 1536

S5_WIDTH = D_MODEL
S5_GROUP = 16
S5_GROUPS = 64
S5_STATE = 64
CH_T = 4
UNIT_G = 4
UNIT_CH = UNIT_G * S5_GROUP
UNITS = S5_GROUPS // UNIT_G
UNIT_K = CH_T * UNIT_CH
UNIT_ST = UNIT_G * S5_STATE
NCH = TOK // CH_T
NCH_CTX = CTX_LEN // CH_T
LANE = 128
SUB = 8

TL = 512
CTX_NB = 2
TF = 1024
TQ = 256
KCH = 256
HPAIRS = 2
assert TQ == CTX_LEN
VMEM_LIMIT = 56 * 1024 * 1024

F32 = jnp.float32
BF16 = jnp.bfloat16


def _params(sem, flags=None):
    return pltpu.CompilerParams(dimension_semantics=sem, vmem_limit_bytes=VMEM_LIMIT, flags=flags)


def _silu(v):
    return v * jax.nn.sigmoid(v)


def _rms(v, g):
    return v * lax.rsqrt(jnp.mean(v * v, axis=-1, keepdims=True) + EPS) * g


def _dot(a, b):
    return jnp.dot(a, b, preferred_element_type=F32)


def _mod_kernel(cc_ref, w_ref, b_ref, o_ref):
    a = _silu(cc_ref[...]).astype(BF16)
    o_ref[0] = _dot(a, w_ref[0].astype(BF16)) + b_ref[0]


def _modulation(cc, ada_w, ada_b):
    depth = ada_w.shape[0]
    tn = 768
    return pl.pallas_call(
        _mod_kernel,
        grid=(depth, 3 * D_MODEL // tn),
        in_specs=[
            pl.BlockSpec((16, D_MODEL), lambda i, j: (0, 0)),
            pl.BlockSpec((1, D_MODEL, tn), lambda i, j: (i, 0, j)),
            pl.BlockSpec((1, 1, tn), lambda i, j: (i, 0, j)),
        ],
        out_specs=pl.BlockSpec((1, 16, tn), lambda i, j: (i, 0, j)),
        out_shape=jax.ShapeDtypeStruct((depth, 16, 3 * D_MODEL), F32),
        compiler_params=_params(("arbitrary", "arbitrary")),
        name="modulation",
    )(cc, ada_w, ada_b.reshape(depth, 1, 3 * D_MODEL))


def _mod_spec(per_batch):
    return pl.BlockSpec((1, 1, 3 * D_MODEL), (lambda b, i: (b, 0, 0)) if per_batch else (lambda b, i: (0, 0, 0)))


def _mla_proj_kernel(x_ref, mod_ref, g_ref, win_ref, qg_ref, kvg_ref, wqa_ref, wqb_ref, wk_ref, wv_ref,
                     cos_ref, sin_ref, kt_ref, *refs):
    q_ref, k_ref, v_ref, sz_ref = refs[-4:]
    nb, tile = x_ref.shape[0], x_ref.shape[1]
    x = x_ref[...].reshape(nb * tile, D_MODEL)
    mod = mod_ref[0]
    sh = mod[:, :D_MODEL]
    sc = mod[:, D_MODEL:2 * D_MODEL]
    h = _rms(x, g_ref[...]) * (1.0 + sc) + sh
    p = _dot(h.astype(BF16), win_ref[...])
    cqn = _rms(p[:, :Q_LORA_RANK], qg_ref[...]).astype(BF16)
    ckvn = _rms(p[:, Q_LORA_RANK:Q_LORA_RANK + KV_LORA_RANK], kvg_ref[...]).astype(BF16)
    kr = p[:, 384:512]
    z = p[:, 512:]
    qa = _dot(cqn, wqa_ref[...])
    qb = _dot(cqn, wqb_ref[...])
    cos, sin, kt = (jnp.concatenate([t[...]] * nb, axis=0) for t in (cos_ref, sin_ref, kt_ref))
    for hd in range(MLA_HEADS):
        sl = slice(hd * HEAD_PAD, (hd + 1) * HEAD_PAD)
        q_ref[:, :, sl] = (qa[:, sl] * cos + qb[:, sl] * sin).astype(BF16).reshape(nb, tile, HEAD_PAD)
    kin = jnp.concatenate([ckvn, (kr * kt).astype(BF16)], axis=-1)
    k_ref[...] = _dot(kin, wk_ref[...]).astype(BF16).reshape(nb, tile, QK_PAD)
    v_ref[...] = _dot(ckvn, wv_ref[...]).astype(BF16).reshape(nb, tile, MLA_WIDTH)
    sz_ref[...] = _silu(z).astype(BF16).reshape(nb, tile, MLA_WIDTH)


def _mla_proj(xs, mod, weights, tables, filled=None):
    is_ctx = filled is not None
    tile, nb = (CTX_LEN, CTX_NB) if is_ctx else (TL, 1)
    off = SEQ // tile if is_ctx else 0
    full = lambda a: pl.BlockSpec(a.shape, lambda b, i: (0,) * a.ndim)
    tok = lambda w: pl.BlockSpec((nb, tile, w), lambda b, i: (b, i + off, 0))
    pos = pl.BlockSpec((tile, HEAD_PAD), lambda b, i: (i + off, 0))
    widths = (QK_PAD, QK_PAD, MLA_WIDTH, MLA_WIDTH)
    in_specs = [pl.BlockSpec((nb, tile, D_MODEL), lambda b, i: (b, i, 0)), _mod_spec(not is_ctx)]
    in_specs += [full(w) for w in weights] + [pos, pos, pos]
    args = [xs, mod, *weights, *tables]
    aliases = {}
    if is_ctx:
        aliases = {len(args) + n: n for n in range(4)}
        in_specs += [pl.BlockSpec(memory_space=pl.ANY)] * 4
        args += list(filled)
    return pl.pallas_call(
        _mla_proj_kernel,
        grid=(BATCH // nb, xs.shape[1] // tile),
        in_specs=in_specs,
        out_specs=[tok(w) for w in widths],
        out_shape=[jax.ShapeDtypeStruct((BATCH, TOK, w), BF16) for w in widths],
        input_output_aliases=aliases,
        compiler_params=_params(("arbitrary", "arbitrary")),
        name="mla_proj_ctx" if is_ctx else "mla_proj",
    )(*args)


def _attn_kernel(q_ref, k_ref, v_ref, o_ref, s_buf, m_buf, vx_buf, cs_buf, cm_buf):
    nt = SEQ // TQ
    lane = lax.broadcasted_iota(jnp.int32, (TOK, 2 * V_HEAD_DIM), 1)
    for hp in range(HPAIRS):
        v = v_ref[0, :, hp * 2 * V_HEAD_DIM:(hp + 1) * 2 * V_HEAD_DIM]
        vx_buf[hp, 0] = jnp.where(lane < V_HEAD_DIM, v, (lane == V_HEAD_DIM).astype(BF16))
        vx_buf[hp, 1] = jnp.where(lane >= V_HEAD_DIM, v, (lane == 0).astype(BF16))

    def scores(hp, row, k0, nk, slot):
        sb, mb = (cs_buf.at[hp], cm_buf.at[hp]) if slot is None else (s_buf.at[slot], m_buf.at[slot])
        for hh in range(2):
            c0 = (2 * hp + hh) * HEAD_PAD
            s = lax.dot_general(q_ref[0, pl.ds(row, TQ), c0:c0 + HEAD_PAD], k_ref[0, k0:k0 + nk, c0:c0 + HEAD_PAD],
                                (((1,), (1,)), ((), ())), preferred_element_type=F32)
            sb[hh, :, :nk] = s
            mb[hh] = jnp.broadcast_to(jnp.max(s, axis=-1, keepdims=True), (TQ, KCH))

    def values(hp, row, k0, nk, slot):
        sb, mb = (cs_buf.at[hp], cm_buf.at[hp]) if slot is None else (s_buf.at[slot], m_buf.at[slot])
        outs = []
        for hh in range(2):
            m = mb[hh]
            ps = [jnp.exp2((sb[hh, :, n * KCH:(n + 1) * KCH] - m).astype(BF16)) for n in range(nk // KCH)]
            acc = _dot(jnp.concatenate(ps, axis=-1), vx_buf[hp, hh, k0:k0 + nk, :])
            l_col = V_HEAD_DIM if hh == 0 else 0
            outs.append(acc / acc[:, l_col:l_col + 1])
        olane = lax.broadcasted_iota(jnp.int32, outs[0].shape, 1)
        o_ref[0, pl.ds(row, TQ), hp * 2 * V_HEAD_DIM:(hp + 1) * 2 * V_HEAD_DIM] = jnp.where(
            olane < V_HEAD_DIM, outs[0], outs[1]).astype(BF16)

    for hp in range(HPAIRS):
        scores(hp, SEQ, SEQ, CTX_LEN, None)
    scores(0, 0, 0, TOK, 0)
    for hp in range(HPAIRS):
        values(hp, SEQ, SEQ, CTX_LEN, None)
    for hp in range(HPAIRS):
        for t in range(1, nt):
            scores(hp, t * TQ, 0, TOK, t % 2)
            values(hp, (t - 1) * TQ, 0, TOK, (t - 1) % 2)
        if hp + 1 < HPAIRS:
            scores(hp + 1, 0, 0, TOK, 0)
        values(hp, (nt - 1) * TQ, 0, TOK, (nt - 1) % 2)


def _attention(q, k, v):
    qk = pl.BlockSpec((1, TOK, HPAIRS * 2 * HEAD_PAD), lambda b, h: (b, 0, h))
    vo = pl.BlockSpec((1, TOK, HPAIRS * 2 * V_HEAD_DIM), lambda b, h: (b, 0, h))
    return pl.pallas_call(
        _attn_kernel,
        grid=(BATCH, MLA_HEADS // (2 * HPAIRS)),
        in_specs=[qk, qk, vo],
        out_specs=vo,
        out_shape=jax.ShapeDtypeStruct((BATCH, TOK, MLA_WIDTH), BF16),
        scratch_shapes=[
            pltpu.VMEM((2, 2, TQ, TOK), F32),
            pltpu.VMEM((2, 2, TQ, KCH), F32),
            pltpu.VMEM((HPAIRS, 2, TOK, 2 * V_HEAD_DIM), BF16),
            pltpu.VMEM((HPAIRS, 2, TQ, CTX_LEN), F32),
            pltpu.VMEM((HPAIRS, 2, TQ, KCH), F32),
        ],
        compiler_params=_params(("arbitrary", "arbitrary")),
        name="attention",
    )(q, k, v)


def _swap_halves(va, vb):
    lo = lax.broadcasted_iota(jnp.int32, va.shape, 1) < UNIT_CH
    return (jnp.where(lo, va, pltpu.roll(vb, UNIT_CH, 1)),
            jnp.where(lo, pltpu.roll(va, UNIT_CH, 1), vb))


def _mla_out_kernel(o_ref, sz_ref, x_ref, mod0_ref, mod1_ref, g1_ref, wout_ref, win_ref, *refs, is_ctx):
    if is_ctx:
        xu_ref, tok_scr = refs[-2:]
    else:
        x1_ref, xu_ref, sz1_ref, tok_scr = refs
    nb, tile = x_ref.shape[0], x_ref.shape[1]
    rows = nb * tile
    a = (o_ref[...].astype(F32) * sz_ref[...].astype(F32)).astype(BF16).reshape(rows, MLA_WIDTH)
    gt = mod0_ref[0][:, 2 * D_MODEL:]
    x1 = x_ref[...].reshape(rows, D_MODEL) + gt * _dot(a, wout_ref[...])
    mod1 = mod1_ref[0]
    h = _rms(x1, g1_ref[...]) * (1.0 + mod1[:, D_MODEL:2 * D_MODEL]) + mod1[:, :D_MODEL]
    if is_ctx:
        u = _dot(h.astype(BF16), win_ref[:, :S5_WIDTH])
    else:
        x1_ref[0] = x1
        p = _dot(h.astype(BF16), win_ref[...])
        sz1_ref[0] = _silu(p[:, S5_WIDTH:]).astype(BF16)
        u = p[:, :S5_WIDTH]
    for m in range(S5_WIDTH // LANE):
        tok_scr[m] = u[:, m * LANE:(m + 1) * LANE]
    for m in range(S5_WIDTH // LANE):
        v = [tok_scr[m, pl.ds(t, rows // CH_T, stride=CH_T), :] for t in range(CH_T)]
        for hf in range(CH_T // 2):
            even, odd = _swap_halves(v[2 * hf], v[2 * hf + 1])
            c0 = 2 * m * UNIT_K + hf * LANE
            xu_ref[:, :, c0:c0 + LANE] = even.astype(BF16).reshape(nb, tile // CH_T, LANE)
            xu_ref[:, :, c0 + UNIT_K:c0 + UNIT_K + LANE] = odd.astype(BF16).reshape(nb, tile // CH_T, LANE)


def _mla_out(o, sz, xs, mod0, mod1, g1, wout, win, xu_filled=None):
    is_ctx = xu_filled is not None
    tile, nb = (CTX_LEN, CTX_NB) if is_ctx else (TL, 1)
    off = SEQ // tile if is_ctx else 0
    full = lambda a: pl.BlockSpec(a.shape, lambda b, i: (0,) * a.ndim)
    shared = lambda w: pl.BlockSpec((nb, tile, w), lambda b, i: (b, i + off, 0))
    own = lambda w: pl.BlockSpec((nb, tile, w), lambda b, i: (b, i, 0))
    xu_spec = pl.BlockSpec((nb, tile // CH_T, UNITS * UNIT_K), lambda b, i: (b, i + off, 0))
    xu_shape = jax.ShapeDtypeStruct((BATCH, NCH, UNITS * UNIT_K), BF16)
    in_specs = [shared(MLA_WIDTH), shared(MLA_WIDTH), own(D_MODEL), _mod_spec(not is_ctx), _mod_spec(not is_ctx),
                full(g1), full(wout), full(win)]
    args = [o, sz, xs, mod0, mod1, g1, wout, win]
    if is_ctx:
        in_specs.append(pl.BlockSpec(memory_space=pl.ANY))
        args.append(xu_filled)
        out_specs, out_shape, aliases = [xu_spec], [xu_shape], {len(args) - 1: 0}
    else:
        out_specs = [own(D_MODEL), xu_spec, own(S5_WIDTH)]
        out_shape = [jax.ShapeDtypeStruct((BATCH, SEQ, D_MODEL), F32), xu_shape,
                     jax.ShapeDtypeStruct((BATCH, SEQ, S5_WIDTH), BF16)]
        aliases = {}
    return pl.pallas_call(
        functools.partial(_mla_out_kernel, is_ctx=is_ctx),
        grid=(BATCH // nb, xs.shape[1] // tile),
        in_specs=in_specs,
        out_specs=out_specs,
        out_shape=out_shape,
        input_output_aliases=aliases,
        scratch_shapes=[pltpu.VMEM((S5_WIDTH // LANE, nb * tile, LANE), F32)],
        compiler_params=_params(("arbitrary", "arbitrary")),
        name="mla_out_s5_in_ctx" if is_ctx else "mla_out_s5_in",
    )(*args)


def _cmul(ar, ai, br, bi):
    return ar * br - ai * bi, ar * bi + ai * br


def _group_dot(a, b, precision=lax.Precision.HIGHEST):
    return lax.dot_general(a, b, (((2,), (2,)), ((0,), (0,))), precision=precision, preferred_element_type=F32)


def _group_transpose(eye, a):
    return _group_dot(eye, a.astype(BF16), precision=None)


def _s5_prep_kernel(are_ref, aim_ref, ls_ref, bre_ref, bim_ref, cre_ref, cim_ref, lam_ref, pb_ref, cp_ref, kk_ref):
    n = are_ref.shape[0]
    eye = (lax.broadcasted_iota(jnp.int32, (n, S5_STATE, S5_STATE), 1)
           == lax.broadcasted_iota(jnp.int32, (n, S5_STATE, S5_STATE), 2)).astype(BF16)
    ar = are_ref[...]
    ai = aim_ref[...]
    dt = jnp.exp(ls_ref[...])
    mag = jnp.exp(ar * dt)
    lb_re = mag * jnp.cos(ai * dt)
    lb_im = mag * jnp.sin(ai * dt)
    den = ar * ar + ai * ai
    nr = lb_re - 1.0
    f_re = ((nr * ar + lb_im * ai) / den)[:, None, :]
    f_im = ((lb_im * ar - nr * ai) / den)[:, None, :]
    bb_re, bb_im = _cmul(f_re, f_im, bre_ref[...], bim_ref[...])
    c_re = cre_ref[...]
    c_im = cim_ref[...]
    pw_re = jnp.ones_like(lb_re)
    pw_im = jnp.zeros_like(lb_re)
    for r in range(CH_T + 1):
        pr = pw_re[:, None, :]
        pi = pw_im[:, None, :]
        cl_re, cl_im = _cmul(c_re, c_im, pr, pi)
        if r < CH_T:
            q_re, q_im = _cmul(pr, pi, bb_re, bb_im)
            pb_ref[0, r] = q_re
            pb_ref[1, r] = q_im
            kk_ref[r] = _group_dot(bb_re, cl_re) - _group_dot(bb_im, cl_im)
        if r > 0:
            cp_ref[0, r - 1] = _group_transpose(eye, cl_re)
            cp_ref[1, r - 1] = _group_transpose(eye, -cl_im)
        if r == CH_T:
            lam_ref[0] = pw_re
            lam_ref[1] = pw_im
        else:
            pw_re, pw_im = _cmul(pw_re, pw_im, lb_re, lb_im)


def _s5_prep(a_re, a_im, log_step, b_re_t, b_im_t, c_re, c_im):
    n = a_re.shape[0]
    nb = 32
    row2 = pl.BlockSpec((nb, S5_STATE), lambda i: (i, 0))
    row3 = pl.BlockSpec((nb, S5_GROUP, S5_STATE), lambda i: (i, 0, 0))
    return pl.pallas_call(
        _s5_prep_kernel,
        grid=(n // nb,),
        in_specs=[row2, row2, pl.BlockSpec((nb, 1), lambda i: (i, 0)), row3, row3, row3, row3],
        out_specs=[
            pl.BlockSpec((2, nb, S5_STATE), lambda i: (0, i, 0)),
            pl.BlockSpec((2, CH_T, nb, S5_GROUP, S5_STATE), lambda i: (0, 0, i, 0, 0)),
            pl.BlockSpec((2, CH_T, nb, S5_STATE, S5_GROUP), lambda i: (0, 0, i, 0, 0)),
            pl.BlockSpec((CH_T, nb, S5_GROUP, S5_GROUP), lambda i: (0, i, 0, 0)),
        ],
        out_shape=[
            jax.ShapeDtypeStruct((2, n, S5_STATE), F32),
            jax.ShapeDtypeStruct((2, CH_T, n, S5_GROUP, S5_STATE), F32),
            jax.ShapeDtypeStruct((2, CH_T, n, S5_STATE, S5_GROUP), F32),
            jax.ShapeDtypeStruct((CH_T, n, S5_GROUP, S5_GROUP), F32),
        ],
        compiler_params=_params(("arbitrary",)),
        name="s5_prep",
    )(a_re, a_im, log_step, b_re_t, b_im_t, c_re, c_im)


STATE_TILES = 2 * 2 * UNIT_ST // LANE


def _hdot(a, rep):
    return _dot(a.astype(BF16), rep)


def _unit_operators(kk_ref, pb_ref, cp_ref, d_ref):
    def iota(shape, dim):
        return lax.broadcasted_iota(jnp.int32, shape, dim)

    rep16 = (iota((S5_GROUP, UNIT_K), 1) % S5_GROUP == iota((S5_GROUP, UNIT_K), 0)).astype(BF16)
    rep64 = (iota((S5_STATE, UNIT_ST), 1) % S5_STATE == iota((S5_STATE, UNIT_ST), 0)).astype(BF16)
    row = iota((UNIT_CH, UNIT_K), 0)
    col = iota((UNIT_CH, UNIT_K), 1)
    same_group_out = row // S5_GROUP == (col // S5_GROUP) % UNIT_G
    same_group_st = row // S5_GROUP == col // S5_STATE
    on_diag = row == col % UNIT_CH
    col_t = col // UNIT_CH
    srow = iota((UNIT_ST, UNIT_K), 0)
    scol = iota((UNIT_ST, UNIT_K), 1)
    st_same_group = srow // S5_STATE == (scol // S5_GROUP) % UNIT_G
    st_col_t = scol // UNIT_CH
    kexp = _hdot(jnp.concatenate([kk_ref[k, d, 0].reshape(UNIT_CH, S5_GROUP)
                                  for d in range(2) for k in range(CH_T)], axis=0), rep16)
    pexp = _hdot(jnp.concatenate([pb_ref[ri, r, d, 0].reshape(UNIT_CH, S5_STATE)
                                  for d in range(2) for ri in range(2) for r in range(CH_T)], axis=0), rep64)
    cexp = _hdot(jnp.concatenate([cp_ref[ri, rr, d, 0].reshape(UNIT_ST, S5_GROUP)
                                  for d in range(2) for ri in range(2) for rr in range(CH_T)], axis=0), rep16)

    def blk(a, idx, nrows):
        return a[idx * nrows:(idx + 1) * nrows]

    rows = []
    for j in range(CH_T):
        acc = jnp.where((col_t == j) & on_diag, d_ref[0], 0.0)
        for d in range(2):
            for k in range(CH_T):
                lag_ok = (col_t - j == k) if d == 0 else (j - col_t == k)
                acc = acc + jnp.where(lag_ok & same_group_out, blk(kexp, d * CH_T + k, UNIT_CH), 0.0)
        rows.append(acc)
    parts = [jnp.concatenate(rows, axis=0)]
    cos = []
    for d in range(2):
        rows = []
        for j in range(CH_T):
            r = CH_T - 1 - j if d == 0 else j
            rows.append(jnp.concatenate(
                [jnp.where(same_group_st, blk(pexp, (d * 2 + ri) * CH_T + r, UNIT_CH), 0.0) for ri in range(2)],
                axis=-1))
        parts.append(jnp.concatenate(rows, axis=0))
        for ri in range(2):
            acc = jnp.zeros((UNIT_ST, UNIT_K), F32)
            for rr in range(CH_T):
                t = rr if d == 0 else CH_T - 1 - rr
                acc = acc + jnp.where((st_col_t == t) & st_same_group,
                                      blk(cexp, (d * 2 + ri) * CH_T + rr, UNIT_ST), 0.0)
            cos.append(acc)
    return jnp.concatenate(parts, axis=-1).astype(BF16), jnp.concatenate(cos, axis=0).astype(BF16)


def _s5_core_kernel(x_ref, kk_ref, pb_ref, cp_ref, d_ref, lam_ref, y_ref, st_scr, sp_scr, y0_scr):
    w1, co = _unit_operators(kk_ref, pb_ref, cp_ref, d_ref)
    nlat = NCH - NCH_CTX
    for b in range(BATCH):
        r = _dot(x_ref[b], w1)
        y0_scr[b] = r[:nlat, :UNIT_K]
        for lt in range(STATE_TILES):
            c0 = UNIT_K + lt * LANE
            st_scr[lt, pl.ds(b, NCH, stride=BATCH), :] = r[:, c0:c0 + LANE]
    lam = [lam_ref[0, lt] for lt in range(STATE_TILES)]

    def rows(chunk):
        return pl.ds(pl.multiple_of(chunk * BATCH, BATCH), BATCH)

    def load_z(row, base):
        return [st_scr[base + k, rows(row), :] for k in range(4)]

    def advance(state, z, base):
        ar0, ar1, ai0, ai1 = lam[base:base + 4]
        return [ar0 * state[0] - ai0 * state[2] + z[0], ar1 * state[1] - ai1 * state[3] + z[1],
                ar0 * state[2] + ai0 * state[0] + z[2], ar1 * state[3] + ai1 * state[1] + z[3]]

    def keep(lo, hi, chunk_lo, base):
        r16 = pl.ds(pl.multiple_of(chunk_lo * BATCH, 2 * BATCH), 2 * BATCH)
        for k in range(4):
            sp_scr[base + k, r16, :] = jnp.concatenate([lo[k], hi[k]], axis=0).astype(BF16)

    def fwd_row(i):
        return jnp.where(i < NCH_CTX, nlat + i, i - NCH_CTX)

    def two_steps(m, carry):
        s_f, z_f, s_b, z_b = carry
        i = 2 * m
        rf = fwd_row(i)
        rb = NCH - 1 - i
        z_f1 = load_z(rf + 1, 0)
        z_b1 = load_z(rb - 1, 4)
        nxt = jnp.minimum(i + 2, NCH - 2)
        z_f2 = load_z(fwd_row(nxt), 0)
        z_b2 = load_z(NCH - 1 - nxt, 4)
        s_f1 = advance(s_f, z_f, 0)
        s_b1 = advance(s_b, z_b, 4)
        keep(s_f, s_f1, rf, 0)
        keep(s_b1, s_b, rb - 1, 4)
        return advance(s_f1, z_f1, 0), z_f2, advance(s_b1, z_b1, 4), z_b2

    zero = [jnp.zeros((BATCH, LANE), F32)] * 4
    lax.fori_loop(0, NCH // 2, two_steps, (zero, load_z(nlat, 0), zero, load_z(NCH - 1, 4)))
    for rb in range(BATCH):
        sl = slice(rb * nlat, (rb + 1) * nlat)
        lhs = jnp.concatenate([sp_scr[lt, sl, :] for lt in range(STATE_TILES)], axis=-1)
        yi = _dot(lhs, co)
        for t in range(UNIT_K // LANE):
            st_scr[t, sl, :] = yi[:, t * LANE:(t + 1) * LANE]
    for b in range(BATCH):
        y_ref[b] = (y0_scr[b] + jnp.concatenate(
            [st_scr[t, pl.ds(b, nlat, stride=BATCH), :] for t in range(UNIT_K // LANE)], axis=-1)).astype(BF16)


def _s5_core(xu, kk, pb, cp, d, lam):
    return pl.pallas_call(
        _s5_core_kernel,
        grid=(UNITS,),
        in_specs=[
            pl.BlockSpec((BATCH, NCH, UNIT_K), lambda q: (0, 0, q)),
            pl.BlockSpec((CH_T, 2, 1, UNIT_G, S5_GROUP, S5_GROUP), lambda q: (0, 0, q, 0, 0, 0)),
            pl.BlockSpec((2, CH_T, 2, 1, UNIT_G, S5_GROUP, S5_STATE), lambda q: (0, 0, 0, q, 0, 0, 0)),
            pl.BlockSpec((2, CH_T, 2, 1, UNIT_G, S5_STATE, S5_GROUP), lambda q: (0, 0, 0, q, 0, 0, 0)),
            pl.BlockSpec((1, UNIT_CH, 1), lambda q: (q, 0, 0)),
            pl.BlockSpec((1, STATE_TILES, SUB, LANE), lambda q: (q, 0, 0, 0)),
        ],
        out_specs=pl.BlockSpec((BATCH, NCH - NCH_CTX, UNIT_K), lambda q: (0, 0, q)),
        out_shape=jax.ShapeDtypeStruct((BATCH, NCH - NCH_CTX, UNITS * UNIT_K), BF16),
        scratch_shapes=[pltpu.VMEM((STATE_TILES, BATCH * NCH, LANE), F32),
                        pltpu.VMEM((STATE_TILES, BATCH * NCH, LANE), BF16),
                        pltpu.VMEM((BATCH, NCH - NCH_CTX, UNIT_K), F32)],
        compiler_params=_params(("arbitrary",)),
        name="s5_core",
    )(xu, kk, pb, cp, d, lam)


def _fin_kernel(y_ref, sz_ref, x_ref, mod_ref, wglu_ref, bglu_ref, wout_ref, fg_ref, o_ref, tok_scr):
    for m in range(S5_WIDTH // LANE):
        for hf in range(CH_T // 2):
            c0 = 2 * m * UNIT_K + hf * LANE
            va, vb = _swap_halves(y_ref[0, :, c0:c0 + LANE].astype(F32),
                                  y_ref[0, :, c0 + UNIT_K:c0 + UNIT_K + LANE].astype(F32))
            tok_scr[m, pl.ds(2 * hf, TF // CH_T, stride=CH_T), :] = va
            tok_scr[m, pl.ds(2 * hf + 1, TF // CH_T, stride=CH_T), :] = vb
    y = jnp.concatenate([tok_scr[m] for m in range(S5_WIDTH // LANE)], axis=-1)
    y = jax.nn.gelu(y)
    y = y * jax.nn.sigmoid(_dot(y.astype(BF16), wglu_ref[...]) + bglu_ref[...])
    a = (y * sz_ref[0].astype(F32)).astype(BF16)
    gt = mod_ref[0][:, 2 * D_MODEL:]
    x2 = x_ref[0] + gt * _dot(a, wout_ref[...])
    o_ref[0] = _rms(x2, fg_ref[...])


def _finish(y, sz1, x1, mod1, wglu, bglu, wout, fg):
    full = lambda shape: pl.BlockSpec(shape, lambda b, i: (0,) * len(shape))
    tok = pl.BlockSpec((1, TF, D_MODEL), lambda b, i: (b, i, 0))
    return pl.pallas_call(
        _fin_kernel,
        grid=(BATCH, SEQ // TF),
        in_specs=[pl.BlockSpec((1, TF // CH_T, UNITS * UNIT_K), lambda b, i: (b, i, 0)), tok, tok,
                  _mod_spec(True),
                  full((S5_WIDTH, S5_WIDTH)), full((1, S5_WIDTH)),
                  full((S5_WIDTH, D_MODEL)), full((1, D_MODEL))],
        out_specs=tok,
        out_shape=jax.ShapeDtypeStruct((BATCH, SEQ, D_MODEL), F32),
        scratch_shapes=[pltpu.VMEM((S5_WIDTH // LANE, TF, LANE), F32)],
        compiler_params=_params(("arbitrary", "arbitrary")),
        name="s5_finish",
    )(y, sz1, x1, mod1, wglu, bglu, wout, fg)


def _rope_tables():
    h = QK_ROPE_DIM // 2
    inv = 1.0 / (ROPE_THETA ** (np.arange(0, h, 2, dtype=np.float64) / h))
    pos = np.arange(SEQ)
    ang_r = (pos // GRID_W)[:, None] * inv[None, :]
    ang_c = (pos % GRID_W)[:, None] * inv[None, :]
    cos32 = np.concatenate([np.cos(ang_r)] * 2 + [np.cos(ang_c)] * 2, axis=-1)
    sin32 = np.concatenate([np.sin(ang_r)] * 2 + [np.sin(ang_c)] * 2, axis=-1)
    cos = np.zeros((TOK, HEAD_PAD), np.float32)
    sin = np.zeros((TOK, HEAD_PAD), np.float32)
    kt = np.zeros((TOK, HEAD_PAD), np.float32)
    cos[:, :QK_NOPE_DIM] = 1.0
    cos[SEQ:, QK_NOPE_DIM:QK_DIM] = 1.0
    kt[SEQ:, :QK_ROPE_DIM] = 1.0
    cos[:SEQ, QK_NOPE_DIM:QK_DIM] = cos32
    sin[:SEQ, QK_NOPE_DIM:QK_DIM] = sin32
    kt[:SEQ, :QK_ROPE_DIM] = cos32
    kt[:SEQ, QK_ROPE_DIM:2 * QK_ROPE_DIM] = sin32
    return jnp.asarray(cos), jnp.asarray(sin), jnp.asarray(kt)


def _mla_selectors():
    def partner(d):
        return (d + 8, -1.0) if d % 16 < 8 else (d - 8, 1.0)

    o2 = Q_LORA_RANK + KV_LORA_RANK
    o3 = o2 + QK_ROPE_DIM
    pin = np.zeros((o3 + MLA_WIDTH, PROJ_W), np.float32)
    pin[np.arange(o3), np.arange(o3)] = 1.0
    pin[o3 + np.arange(MLA_WIDTH), 512 + np.arange(MLA_WIDTH)] = 1.0
    pa = np.zeros((MLA_HEADS * QK_DIM, QK_PAD), np.float32)
    pb = np.zeros((MLA_HEADS * QK_DIM, QK_PAD), np.float32)
    pk = np.zeros((MLA_HEADS * 128, QK_PAD), np.float32)
    pv = np.zeros((MLA_HEADS * 128, MLA_WIDTH), np.float32)
    kb = np.zeros((128, QK_PAD), np.float32)
    for d in range(QK_ROPE_DIM):
        src, sign = partner(d)
        pin[o2 + src, o3 + d] = sign
        for hd in range(MLA_HEADS):
            pb[hd * QK_DIM + QK_NOPE_DIM + src, hd * HEAD_PAD + QK_NOPE_DIM + d] = sign
            kb[d, hd * HEAD_PAD + QK_NOPE_DIM + d] = 1.0
            kb[QK_ROPE_DIM + d, hd * HEAD_PAD + QK_NOPE_DIM + d] = 1.0
    for hd in range(MLA_HEADS):
        pa[hd * QK_DIM + np.arange(QK_DIM), hd * HEAD_PAD + np.arange(QK_DIM)] = 1.0
        pk[hd * 128 + np.arange(QK_NOPE_DIM), hd * HEAD_PAD + np.arange(QK_NOPE_DIM)] = 1.0
        pv[hd * 128 + QK_NOPE_DIM + np.arange(V_HEAD_DIM), hd * V_HEAD_DIM + np.arange(V_HEAD_DIM)] = 1.0
    return [jnp.asarray(a, dtype=BF16) for a in (pin, pa, pb, pk, pv, kb)]


def _mla_wprep_kernel(win_ref, wuq_ref, wukv_ref, pin_ref, pa_ref, pb_ref, pk_ref, pv_ref, kb_ref,
                      o_in, o_qa, o_qb, o_k, o_v):
    o_in[...] = _dot(win_ref[...].astype(BF16), pin_ref[...]).astype(BF16)
    wq = (wuq_ref[...] * (SOFTMAX_SCALE * math.log2(math.e))).astype(BF16)
    o_qa[...] = _dot(wq, pa_ref[...]).astype(BF16)
    o_qb[...] = _dot(wq, pb_ref[...]).astype(BF16)
    wkv = wukv_ref[...].astype(BF16)
    o_k[:KV_LORA_RANK] = _dot(wkv, pk_ref[...]).astype(BF16)
    o_k[KV_LORA_RANK:] = kb_ref[...]
    o_v[...] = _dot(wkv, pv_ref[...]).astype(BF16)


def _mla_weights(w_in, w_uq, w_ukv):
    nj = 4
    full = lambda a: pl.BlockSpec(a.shape, lambda j: (0, 0))
    cols = lambda rows, width: pl.BlockSpec((rows, width // nj), lambda j: (0, j))
    sel = _mla_selectors()
    widths = (PROJ_W, QK_PAD, QK_PAD, QK_PAD, MLA_WIDTH, QK_PAD)
    out_rows = (D_MODEL, Q_LORA_RANK, Q_LORA_RANK, 256, KV_LORA_RANK)
    return pl.pallas_call(
        _mla_wprep_kernel,
        grid=(nj,),
        in_specs=[full(w_in), full(w_uq), full(w_ukv)] + [cols(a.shape[0], w) for a, w in zip(sel, widths)],
        out_specs=[cols(r, w) for r, w in zip(out_rows, widths)],
        out_shape=[jax.ShapeDtypeStruct((r, w), BF16) for r, w in zip(out_rows, widths)],
        compiler_params=_params(("arbitrary",)),
        name="mla_weight_prep",
    )(w_in, w_uq, w_ukv, *sel)


def _lam_tiles(lam):
    lam = lam.reshape(2, 2, UNITS, UNIT_ST // LANE, LANE)
    lam = jnp.concatenate([lam[0, 0], lam[1, 0], lam[0, 1], lam[1, 1]], axis=1)
    return jnp.broadcast_to(lam[:, :, None, :], (UNITS, STATE_TILES, SUB, LANE))


def kernel(x, c, ctx, c_ctx, ada_w, ada_b, norm_g, mla_w_in, mla_q_norm, mla_w_uq, mla_kv_norm, mla_w_ukv, mla_w_out, s5_w_in, s5_a_re, s5_a_im, s5_log_step, s5_b_re, s5_b_im, s5_c_re, s5_c_im, s5_d, s5_w_glu, s5_b_glu, s5_w_out, final_g):
    cc = jnp.concatenate([c, c_ctx[None, :], jnp.zeros((7, D_MODEL), F32)], axis=0)
    mods = _modulation(cc, ada_w, ada_b)

    mod_lat = [mods[i, :BATCH, None, :] for i in range(2)]
    mod_ctx = [mods[i, BATCH:BATCH + 1, None, :] for i in range(2)]

    win, wqa, wqb, wk, wv = _mla_weights(mla_w_in[0], mla_w_uq[0], mla_w_ukv[0])
    weights = (norm_g[0][None, :], win, mla_q_norm[0][None, :], mla_kv_norm[0][None, :], wqa, wqb, wk, wv)
    tables = _rope_tables()
    qkvz = _mla_proj(x, mod_lat[0], weights, tables)
    q, k, v, sz = _mla_proj(ctx, mod_ctx[0], weights, tables, filled=qkvz)
    o = _attention(q, k, v)
    out_w = (norm_g[1][None, :], mla_w_out[0].astype(BF16), s5_w_in[0].astype(BF16))
    x1, xu, sz1 = _mla_out(o, sz, x, mod_lat[0], mod_lat[1], *out_w)
    xu, = _mla_out(o, sz, ctx, mod_ctx[0], mod_ctx[1], *out_w, xu_filled=xu)
    mod1 = mod_lat[1]

    n = 2 * S5_GROUPS
    lam, pb, cp, kk = _s5_prep(
        s5_a_re[0].reshape(n, S5_STATE), s5_a_im[0].reshape(n, S5_STATE), s5_log_step[0].reshape(n, 1),
        jnp.swapaxes(s5_b_re[0], -1, -2).reshape(n, S5_GROUP, S5_STATE),
        jnp.swapaxes(s5_b_im[0], -1, -2).reshape(n, S5_GROUP, S5_STATE),
        s5_c_re[0].reshape(n, S5_GROUP, S5_STATE), s5_c_im[0].reshape(n, S5_GROUP, S5_STATE))
    y = _s5_core(xu, kk.reshape(CH_T, 2, UNITS, UNIT_G, S5_GROUP, S5_GROUP),
                 pb.reshape(2, CH_T, 2, UNITS, UNIT_G, S5_GROUP, S5_STATE),
                 cp.reshape(2, CH_T, 2, UNITS, UNIT_G, S5_STATE, S5_GROUP),
                 s5_d[0].reshape(UNITS, UNIT_CH, 1), _lam_tiles(lam))
    return _finish(y, sz1, x1, mod1, s5_w_glu[0].astype(BF16), s5_b_glu[0][None, :], s5_w_out[0].astype(BF16),
                   final_g[None, :])
```

```python
import functools
import math

import jax
import jax.numpy as jnp
import numpy as np
from jax import lax
from jax.experimental import pallas as pl
from jax.experimental.pallas import tpu as pltpu

D_MODEL = 1024
BATCH = 8
SEQ = 2048
GRID_W = 64
CTX_LEN = 256
TOK = CTX_LEN + SEQ
EPS = 1e-6

MLA_HEADS = 16
QK_NOPE_DIM = 64
QK_ROPE_DIM = 32
V_HEAD_DIM = 64
Q_LORA_RANK = 256
KV_LORA_RANK = 128
MLA_WIDTH = MLA_HEADS * V_HEAD_DIM
QK_DIM = QK_NOPE_DIM + QK_ROPE_DIM
SOFTMAX_SCALE = QK_DIM ** -0.5
ROPE_THETA = 10000.0
HEAD_PAD = 128
QK_PAD = MLA_HEADS * HEAD_PAD
PROJ_W = 1536

S5_WIDTH = D_MODEL
S5_GROUP = 16
S5_GROUPS = 64
S5_STATE = 64
CH_T = 8
UNIT_G = 2
UNIT_CH = UNIT_G * S5_GROUP
UNITS = S5_GROUPS // UNIT_G
UNIT_K = CH_T * UNIT_CH
UNIT_ST = UNIT_G * S5_STATE
NCH = TOK // CH_T
NCH_CTX = CTX_LEN // CH_T
LANE = 128
SUB = 8
UNITS_PER_TILE = LANE // UNIT_CH

TL = 512
CTX_NB = 2
TF = 1024
TQ = 256
KCH = 256
HPAIRS = 2
assert TQ == CTX_LEN
VMEM_LIMIT = 56 * 1024 * 1024

F32 = jnp.float32
BF16 = jnp.bfloat16


def _params(sem, flags=None):
    return pltpu.CompilerParams(dimension_semantics=sem, vmem_limit_bytes=VMEM_LIMIT, flags=flags)


def _silu(v):
    return v * jax.nn.sigmoid(v)


def _rms(v, g):
    return v * lax.rsqrt(jnp.mean(v * v, axis=-1, keepdims=True) + EPS) * g


def _dot(a, b):
    return jnp.dot(a, b, preferred_element_type=F32)


def _mod_kernel(cc_ref, w_ref, b_ref, o_ref):
    a = _silu(cc_ref[...]).astype(BF16)
    o_ref[0] = _dot(a, w_ref[0].astype(BF16)) + b_ref[0]


def _modulation(cc, ada_w, ada_b):
    depth = ada_w.shape[0]
    tn = 768
    return pl.pallas_call(
        _mod_kernel,
        grid=(depth, 3 * D_MODEL // tn),
        in_specs=[
            pl.BlockSpec((16, D_MODEL), lambda i, j: (0, 0)),
            pl.BlockSpec((1, D_MODEL, tn), lambda i, j: (i, 0, j)),
            pl.BlockSpec((1, 1, tn), lambda i, j: (i, 0, j)),
        ],
        out_specs=pl.BlockSpec((1, 16, tn), lambda i, j: (i, 0, j)),
        out_shape=jax.ShapeDtypeStruct((depth, 16, 3 * D_MODEL), F32),
        compiler_params=_params(("arbitrary", "arbitrary")),
        name="modulation",
    )(cc, ada_w, ada_b.reshape(depth, 1, 3 * D_MODEL))


def _mod_spec(per_batch):
    return pl.BlockSpec((1, 1, 3 * D_MODEL), (lambda b, i: (b, 0, 0)) if per_batch else (lambda b, i: (0, 0, 0)))


def _mla_proj_kernel(x_ref, mod_ref, g_ref, win_ref, qg_ref, kvg_ref, wqa_ref, wqb_ref, wk_ref, wv_ref,
                     cos_ref, sin_ref, kt_ref, *refs):
    q_ref, k_ref, v_ref, sz_ref = refs[-4:]
    nb, tile = x_ref.shape[0], x_ref.shape[1]
    x = x_ref[...].reshape(nb * tile, D_MODEL)
    mod = mod_ref[0]
    sh = mod[:, :D_MODEL]
    sc = mod[:, D_MODEL:2 * D_MODEL]
    h = _rms(x, g_ref[...]) * (1.0 + sc) + sh
    p = _dot(h.astype(BF16), win_ref[...])
    cqn = _rms(p[:, :Q_LORA_RANK], qg_ref[...]).astype(BF16)
    ckvn = _rms(p[:, Q_LORA_RANK:Q_LORA_RANK + KV_LORA_RANK], kvg_ref[...]).astype(BF16)
    kr = p[:, 384:512]
    z = p[:, 512:]
    qa = _dot(cqn, wqa_ref[...])
    qb = _dot(cqn, wqb_ref[...])
    cos, sin, kt = (jnp.concatenate([t[...]] * nb, axis=0) for t in (cos_ref, sin_ref, kt_ref))
    for hd in range(MLA_HEADS):
        sl = slice(hd * HEAD_PAD, (hd + 1) * HEAD_PAD)
        q_ref[:, :, sl] = (qa[:, sl] * cos + qb[:, sl] * sin).astype(BF16).reshape(nb, tile, HEAD_PAD)
    kin = jnp.concatenate([ckvn, (kr * kt).astype(BF16)], axis=-1)
    k_ref[...] = _dot(kin, wk_ref[...]).astype(BF16).reshape(nb, tile, QK_PAD)
    v_ref[...] = _dot(ckvn, wv_ref[...]).astype(BF16).reshape(nb, tile, MLA_WIDTH)
    sz_ref[...] = _silu(z).astype(BF16).reshape(nb, tile, MLA_WIDTH)


def _mla_proj(xs, mod, weights, tables, filled=None):
    is_ctx = filled is not None
    tile, nb = (CTX_LEN, CTX_NB) if is_ctx else (TL, 1)
    off = SEQ // tile if is_ctx else 0
    full = lambda a: pl.BlockSpec(a.shape, lambda b, i: (0,) * a.ndim)
    tok = lambda w: pl.BlockSpec((nb, tile, w), lambda b, i: (b, i + off, 0))
    pos = pl.BlockSpec((tile, HEAD_PAD), lambda b, i: (i + off, 0))
    widths = (QK_PAD, QK_PAD, MLA_WIDTH, MLA_WIDTH)
    in_specs = [pl.BlockSpec((nb, tile, D_MODEL), lambda b, i: (b, i, 0)), _mod_spec(not is_ctx)]
    in_specs += [full(w) for w in weights] + [pos, pos, pos]
    args = [xs, mod, *weights, *tables]
    aliases = {}
    if is_ctx:
        aliases = {len(args) + n: n for n in range(4)}
        in_specs += [pl.BlockSpec(memory_space=pl.ANY)] * 4
        args += list(filled)
    return pl.pallas_call(
        _mla_proj_kernel,
        grid=(BATCH // nb, xs.shape[1] // tile),
        in_specs=in_specs,
        out_specs=[tok(w) for w in widths],
        out_shape=[jax.ShapeDtypeStruct((BATCH, TOK, w), BF16) for w in widths],
        input_output_aliases=aliases,
        compiler_params=_params(("arbitrary", "arbitrary")),
        name="mla_proj_ctx" if is_ctx else "mla_proj",
    )(*args)


def _attn_kernel(q_ref, k_ref, v_ref, o_ref, s_buf, m_buf, vx_buf, cs_buf, cm_buf):
    nt = SEQ // TQ
    lane = lax.broadcasted_iota(jnp.int32, (TOK, 2 * V_HEAD_DIM), 1)
    for hp in range(HPAIRS):
        v = v_ref[0, :, hp * 2 * V_HEAD_DIM:(hp + 1) * 2 * V_HEAD_DIM]
        vx_buf[hp, 0] = jnp.where(lane < V_HEAD_DIM, v, (lane == V_HEAD_DIM).astype(BF16))
        vx_buf[hp, 1] = jnp.where(lane >= V_HEAD_DIM, v, (lane == 0).astype(BF16))

    def scores(hp, row, k0, nk, slot):
        sb, mb = (cs_buf.at[hp], cm_buf.at[hp]) if slot is None else (s_buf.at[slot], m_buf.at[slot])
        for hh in range(2):
            c0 = (2 * hp + hh) * HEAD_PAD
            s = lax.dot_general(q_ref[0, pl.ds(row, TQ), c0:c0 + HEAD_PAD], k_ref[0, k0:k0 + nk, c0:c0 + HEAD_PAD],
                                (((1,), (1,)), ((), ())), preferred_element_type=F32)
            sb[hh, :, :nk] = s
            mb[hh] = jnp.broadcast_to(jnp.max(s, axis=-1, keepdims=True), (TQ, KCH))

    def values(hp, row, k0, nk, slot):
        sb, mb = (cs_buf.at[hp], cm_buf.at[hp]) if slot is None else (s_buf.at[slot], m_buf.at[slot])
        outs = []
        for hh in range(2):
            m = mb[hh]
            ps = [jnp.exp2(sb[hh, :, n * KCH:(n + 1) * KCH] - m).astype(BF16) for n in range(nk // KCH)]
            acc = _dot(jnp.concatenate(ps, axis=-1), vx_buf[hp, hh, k0:k0 + nk, :])
            l_col = V_HEAD_DIM if hh == 0 else 0
            outs.append(acc / acc[:, l_col:l_col + 1])
        olane = lax.broadcasted_iota(jnp.int32, outs[0].shape, 1)
        o_ref[0, pl.ds(row, TQ), hp * 2 * V_HEAD_DIM:(hp + 1) * 2 * V_HEAD_DIM] = jnp.where(
            olane < V_HEAD_DIM, outs[0], outs[1]).astype(BF16)

    for hp in range(HPAIRS):
        scores(hp, SEQ, SEQ, CTX_LEN, None)
    scores(0, 0, 0, TOK, 0)
    for hp in range(HPAIRS):
        values(hp, SEQ, SEQ, CTX_LEN, None)
    for hp in range(HPAIRS):
        for t in range(1, nt):
            scores(hp, t * TQ, 0, TOK, t % 2)
            values(hp, (t - 1) * TQ, 0, TOK, (t - 1) % 2)
        if hp + 1 < HPAIRS:
            scores(hp + 1, 0, 0, TOK, 0)
        values(hp, (nt - 1) * TQ, 0, TOK, (nt - 1) % 2)


def _attention(q, k, v):
    qk = pl.BlockSpec((1, TOK, HPAIRS * 2 * HEAD_PAD), lambda b, h: (b, 0, h))
    vo = pl.BlockSpec((1, TOK, HPAIRS * 2 * V_HEAD_DIM), lambda b, h: (b, 0, h))
    return pl.pallas_call(
        _attn_kernel,
        grid=(BATCH, MLA_HEADS // (2 * HPAIRS)),
        in_specs=[qk, qk, vo],
        out_specs=vo,
        out_shape=jax.ShapeDtypeStruct((BATCH, TOK, MLA_WIDTH), BF16),
        scratch_shapes=[
            pltpu.VMEM((2, 2, TQ, TOK), F32),
            pltpu.VMEM((2, 2, TQ, KCH), F32),
            pltpu.VMEM((HPAIRS, 2, TOK, 2 * V_HEAD_DIM), BF16),
            pltpu.VMEM((HPAIRS, 2, TQ, CTX_LEN), F32),
            pltpu.VMEM((HPAIRS, 2, TQ, KCH), F32),
        ],
        compiler_params=_params(("arbitrary", "arbitrary")),
        name="attention",
    )(q, k, v)


def _block_transpose(vs):
    vs = list(vs)
    n = len(vs)
    lane = lax.broadcasted_iota(jnp.int32, vs[0].shape, 1)
    step, width = n // 2, LANE // 2
    while step >= 1:
        lo = lane % (2 * width) < width
        for base in range(0, n, 2 * step):
            for i in range(base, base + step):
                va, vb = vs[i], vs[i + step]
                vs[i] = jnp.where(lo, va, pltpu.roll(vb, width, 1))
                vs[i + step] = jnp.where(lo, pltpu.roll(va, LANE - width, 1), vb)
        step, width = step // 2, width // 2
    return vs


def _mla_out_kernel(o_ref, sz_ref, x_ref, mod0_ref, mod1_ref, g1_ref, wout_ref, win_ref, *refs, is_ctx):
    if is_ctx:
        xu_ref, tok_scr = refs[-2:]
    else:
        x1_ref, xu_ref, sz1_ref, tok_scr = refs
    nb, tile = x_ref.shape[0], x_ref.shape[1]
    rows = nb * tile
    a = (o_ref[...].astype(F32) * sz_ref[...].astype(F32)).astype(BF16).reshape(rows, MLA_WIDTH)
    gt = mod0_ref[0][:, 2 * D_MODEL:]
    x1 = x_ref[...].reshape(rows, D_MODEL) + gt * _dot(a, wout_ref[...])
    mod1 = mod1_ref[0]
    h = _rms(x1, g1_ref[...]) * (1.0 + mod1[:, D_MODEL:2 * D_MODEL]) + mod1[:, :D_MODEL]
    if is_ctx:
        u = _dot(h.astype(BF16), win_ref[:, :S5_WIDTH])
    else:
        x1_ref[0] = x1
        p = _dot(h.astype(BF16), win_ref[...])
        sz1_ref[0] = _silu(p[:, S5_WIDTH:]).astype(BF16)
        u = p[:, :S5_WIDTH]
    for m in range(S5_WIDTH // LANE):
        tok_scr[m] = u[:, m * LANE:(m + 1) * LANE]
    upt = UNITS_PER_TILE
    for m in range(S5_WIDTH // LANE):
        v = [tok_scr[m, pl.ds(t, rows // CH_T, stride=CH_T), :] for t in range(CH_T)]
        for hv in range(CH_T // upt):
            for a, blk in enumerate(_block_transpose(v[hv * upt:(hv + 1) * upt])):
                c0 = (m * upt + a) * UNIT_K + hv * LANE
                xu_ref[:, :, c0:c0 + LANE] = blk.astype(BF16).reshape(nb, tile // CH_T, LANE)


def _mla_out(o, sz, xs, mod0, mod1, g1, wout, win, xu_filled=None):
    is_ctx = xu_filled is not None
    tile, nb = (CTX_LEN, CTX_NB) if is_ctx else (TL, 1)
    off = SEQ // tile if is_ctx else 0
    full = lambda a: pl.BlockSpec(a.shape, lambda b, i: (0,) * a.ndim)
    shared = lambda w: pl.BlockSpec((nb, tile, w), lambda b, i: (b, i + off, 0))
    own = lambda w: pl.BlockSpec((nb, tile, w), lambda b, i: (b, i, 0))
    xu_spec = pl.BlockSpec((nb, tile // CH_T, UNITS * UNIT_K), lambda b, i: (b, i + off, 0))
    xu_shape = jax.ShapeDtypeStruct((BATCH, NCH, UNITS * UNIT_K), BF16)
    in_specs = [shared(MLA_WIDTH), shared(MLA_WIDTH), own(D_MODEL), _mod_spec(not is_ctx), _mod_spec(not is_ctx),
                full(g1), full(wout), full(win)]
    args = [o, sz, xs, mod0, mod1, g1, wout, win]
    if is_ctx:
        in_specs.append(pl.BlockSpec(memory_space=pl.ANY))
        args.append(xu_filled)
        out_specs, out_shape, aliases = [xu_spec], [xu_shape], {len(args) - 1: 0}
    else:
        out_specs = [own(D_MODEL), xu_spec, own(S5_WIDTH)]
        out_shape = [jax.ShapeDtypeStruct((BATCH, SEQ, D_MODEL), F32), xu_shape,
                     jax.ShapeDtypeStruct((BATCH, SEQ, S5_WIDTH), BF16)]
        aliases = {}
    return pl.pallas_call(
        functools.partial(_mla_out_kernel, is_ctx=is_ctx),
        grid=(BATCH // nb, xs.shape[1] // tile),
        in_specs=in_specs,
        out_specs=out_specs,
        out_shape=out_shape,
        input_output_aliases=aliases,
        scratch_shapes=[pltpu.VMEM((S5_WIDTH // LANE, nb * tile, LANE), F32)],
        compiler_params=_params(("arbitrary", "arbitrary")),
        name="mla_out_s5_in_ctx" if is_ctx else "mla_out_s5_in",
    )(*args)


def _cmul(ar, ai, br, bi):
    return ar * br - ai * bi, ar * bi + ai * br


def _group_dot(a, b, precision=lax.Precision.HIGHEST):
    return lax.dot_general(a, b, (((2,), (2,)), ((0,), (0,))), precision=precision, preferred_element_type=F32)


def _group_transpose(eye, a):
    return _group_dot(eye, a.astype(BF16), precision=None)


def _s5_prep_kernel(are_ref, aim_ref, ls_ref, bre_ref, bim_ref, cre_ref, cim_ref, lam_ref, pb_ref, cp_ref, kk_ref):
    n = are_ref.shape[0]
    eye = (lax.broadcasted_iota(jnp.int32, (n, S5_STATE, S5_STATE), 1)
           == lax.broadcasted_iota(jnp.int32, (n, S5_STATE, S5_STATE), 2)).astype(BF16)
    ar = are_ref[...]
    ai = aim_ref[...]
    dt = jnp.exp(ls_ref[...])
    mag = jnp.exp(ar * dt)
    lb_re = mag * jnp.cos(ai * dt)
    lb_im = mag * jnp.sin(ai * dt)
    den = ar * ar + ai * ai
    nr = lb_re - 1.0
    f_re = ((nr * ar + lb_im * ai) / den)[:, None, :]
    f_im = ((lb_im * ar - nr * ai) / den)[:, None, :]
    bb_re, bb_im = _cmul(f_re, f_im, bre_ref[...], bim_ref[...])
    c_re = cre_ref[...]
    c_im = cim_ref[...]
    pw_re = jnp.ones_like(lb_re)
    pw_im = jnp.zeros_like(lb_re)
    for r in range(CH_T + 1):
        pr = pw_re[:, None, :]
        pi = pw_im[:, None, :]
        cl_re, cl_im = _cmul(c_re, c_im, pr, pi)
        if r < CH_T:
            q_re, q_im = _cmul(pr, pi, bb_re, bb_im)
            pb_ref[0, r] = q_re
            pb_ref[1, r] = q_im
            kk_ref[r] = _group_dot(bb_re, cl_re) - _group_dot(bb_im, cl_im)
        if r > 0:
            cp_ref[0, r - 1] = _group_transpose(eye, cl_re)
            cp_ref[1, r - 1] = _group_transpose(eye, -cl_im)
        if r == CH_T:
            lam_ref[0] = pw_re
            lam_ref[1] = pw_im
        else:
            pw_re, pw_im = _cmul(pw_re, pw_im, lb_re, lb_im)


def _s5_prep(a_re, a_im, log_step, b_re_t, b_im_t, c_re, c_im):
    n = a_re.shape[0]
    nb = 16
    row2 = pl.BlockSpec((nb, S5_STATE), lambda i: (i, 0))
    row3 = pl.BlockSpec((nb, S5_GROUP, S5_STATE), lambda i: (i, 0, 0))
    return pl.pallas_call(
        _s5_prep_kernel,
        grid=(n // nb,),
        in_specs=[row2, row2, pl.BlockSpec((nb, 1), lambda i: (i, 0)), row3, row3, row3, row3],
        out_specs=[
            pl.BlockSpec((2, nb, S5_STATE), lambda i: (0, i, 0)),
            pl.BlockSpec((2, CH_T, nb, S5_GROUP, S5_STATE), lambda i: (0, 0, i, 0, 0)),
            pl.BlockSpec((2, CH_T, nb, S5_STATE, S5_GROUP), lambda i: (0, 0, i, 0, 0)),
            pl.BlockSpec((CH_T, nb, S5_GROUP, S5_GROUP), lambda i: (0, i, 0, 0)),
        ],
        out_shape=[
            jax.ShapeDtypeStruct((2, n, S5_STATE), F32),
            jax.ShapeDtypeStruct((2, CH_T, n, S5_GROUP, S5_STATE), F32),
            jax.ShapeDtypeStruct((2, CH_T, n, S5_STATE, S5_GROUP), F32),
            jax.ShapeDtypeStruct((CH_T, n, S5_GROUP, S5_GROUP), F32),
        ],
        compiler_params=_params(("arbitrary",)),
        name="s5_prep",
    )(a_re, a_im, log_step, b_re_t, b_im_t, c_re, c_im)


STATE_TILES = 2 * 2 * UNIT_ST // LANE


def _hdot(a, rep):
    return _dot(a.astype(BF16), rep)


def _unit_operators(kk_ref, pb_ref, cp_ref, d_ref):
    def iota(shape, dim):
        return lax.broadcasted_iota(jnp.int32, shape, dim)

    rep16 = (iota((S5_GROUP, UNIT_K), 1) % S5_GROUP == iota((S5_GROUP, UNIT_K), 0)).astype(BF16)
    rep64 = (iota((S5_STATE, UNIT_ST), 1) % S5_STATE == iota((S5_STATE, UNIT_ST), 0)).astype(BF16)
    row = iota((UNIT_CH, UNIT_K), 0)
    col = iota((UNIT_CH, UNIT_K), 1)
    same_group_out = row // S5_GROUP == (col // S5_GROUP) % UNIT_G
    same_group_st = (iota((UNIT_CH, UNIT_ST), 0) // S5_GROUP) == (iota((UNIT_CH, UNIT_ST), 1) // S5_STATE)
    on_diag = row == col % UNIT_CH
    col_t = col // UNIT_CH
    srow = iota((UNIT_ST, UNIT_K), 0)
    scol = iota((UNIT_ST, UNIT_K), 1)
    st_same_group = srow // S5_STATE == (scol // S5_GROUP) % UNIT_G
    st_col_t = scol // UNIT_CH
    kexp = _hdot(jnp.concatenate([kk_ref[k, d, 0].reshape(UNIT_CH, S5_GROUP)
                                  for d in range(2) for k in range(CH_T)], axis=0), rep16)
    pexp = _hdot(jnp.concatenate([pb_ref[ri, r, d, 0].reshape(UNIT_CH, S5_STATE)
                                  for d in range(2) for ri in range(2) for r in range(CH_T)], axis=0), rep64)
    cexp = _hdot(jnp.concatenate([cp_ref[ri, rr, d, 0].reshape(UNIT_ST, S5_GROUP)
                                  for d in range(2) for ri in range(2) for rr in range(CH_T)], axis=0), rep16)

    def blk(a, idx, nrows):
        return a[idx * nrows:(idx + 1) * nrows]

    rows = []
    for j in range(CH_T):
        acc = jnp.where((col_t == j) & on_diag, d_ref[0], 0.0)
        for d in range(2):
            for k in range(CH_T):
                lag_ok = (col_t - j == k) if d == 0 else (j - col_t == k)
                acc = acc + jnp.where(lag_ok & same_group_out, blk(kexp, d * CH_T + k, UNIT_CH), 0.0)
        rows.append(acc)
    parts = [jnp.concatenate(rows, axis=0)]
    cos = []
    for d in range(2):
        rows = []
        for j in range(CH_T):
            r = CH_T - 1 - j if d == 0 else j
            rows.append(jnp.concatenate(
                [jnp.where(same_group_st, blk(pexp, (d * 2 + ri) * CH_T + r, UNIT_CH), 0.0) for ri in range(2)],
                axis=-1))
        parts.append(jnp.concatenate(rows, axis=0))
        for ri in range(2):
            acc = jnp.zeros((UNIT_ST, UNIT_K), F32)
            for rr in range(CH_T):
                t = rr if d == 0 else CH_T - 1 - rr
                acc = acc + jnp.where((st_col_t == t) & st_same_group,
                                      blk(cexp, (d * 2 + ri) * CH_T + rr, UNIT_ST), 0.0)
            cos.append(acc)
    return jnp.concatenate(parts, axis=-1).astype(BF16), jnp.concatenate(cos, axis=0).astype(BF16)


def _s5_core_kernel(x_ref, kk_ref, pb_ref, cp_ref, d_ref, lam_ref, y_ref, st_scr, sp_scr, y0_scr):
    w1, co = _unit_operators(kk_ref, pb_ref, cp_ref, d_ref)
    nlat = NCH - NCH_CTX
    for b in range(BATCH):
        r = _dot(x_ref[b], w1)
        y0_scr[b] = r[:nlat, :UNIT_K]
        for lt in range(STATE_TILES):
            c0 = UNIT_K + lt * LANE
            st_scr[lt, pl.ds(b, NCH, stride=BATCH), :] = r[:, c0:c0 + LANE]
    lam = [lam_ref[0, lt] for lt in range(STATE_TILES)]

    def rows(chunk):
        return pl.ds(pl.multiple_of(chunk * BATCH, BATCH), BATCH)

    nre = UNIT_ST // LANE
    tpd = 2 * nre

    def load_z(row, base):
        return [st_scr[base + k, rows(row), :] for k in range(tpd)]

    def advance(state, z, base):
        a = lam[base:base + tpd]
        re = [a[k] * state[k] - a[nre + k] * state[nre + k] + z[k] for k in range(nre)]
        im = [a[k] * state[nre + k] + a[nre + k] * state[k] + z[nre + k] for k in range(nre)]
        return re + im

    def keep(lo, hi, chunk_lo, base):
        r16 = pl.ds(pl.multiple_of(chunk_lo * BATCH, 2 * BATCH), 2 * BATCH)
        for k in range(tpd):
            sp_scr[base + k, r16, :] = jnp.concatenate([lo[k], hi[k]], axis=0).astype(BF16)

    def fwd_row(i):
        return jnp.where(i < NCH_CTX, nlat + i, i - NCH_CTX)

    def two_steps(m, carry):
        s_f, z_f, s_b, z_b = carry
        i = 2 * m
        rf = fwd_row(i)
        rb = NCH - 1 - i
        z_f1 = load_z(rf + 1, 0)
        z_b1 = load_z(rb - 1, tpd)
        nxt = jnp.minimum(i + 2, NCH - 2)
        z_f2 = load_z(fwd_row(nxt), 0)
        z_b2 = load_z(NCH - 1 - nxt, tpd)
        s_f1 = advance(s_f, z_f, 0)
        s_b1 = advance(s_b, z_b, tpd)
        keep(s_f, s_f1, rf, 0)
        keep(s_b1, s_b, rb - 1, tpd)
        return advance(s_f1, z_f1, 0), z_f2, advance(s_b1, z_b1, tpd), z_b2

    zero = [jnp.zeros((BATCH, LANE), F32)] * tpd
    lax.fori_loop(0, NCH // 2, two_steps, (zero, load_z(nlat, 0), zero, load_z(NCH - 1, tpd)))
    for rb in range(BATCH):
        sl = slice(rb * nlat, (rb + 1) * nlat)
        lhs = jnp.concatenate([sp_scr[lt, sl, :] for lt in range(STATE_TILES)], axis=-1)
        yi = _dot(lhs, co)
        for t in range(UNIT_K // LANE):
            st_scr[t, sl, :] = yi[:, t * LANE:(t + 1) * LANE]
    for b in range(BATCH):
        y_ref[b] = (y0_scr[b] + jnp.concatenate(
            [st_scr[t, pl.ds(b, nlat, stride=BATCH), :] for t in range(UNIT_K // LANE)], axis=-1)).astype(BF16)


def _s5_core(xu, kk, pb, cp, d, lam):
    return pl.pallas_call(
        _s5_core_kernel,
        grid=(UNITS,),
        in_specs=[
            pl.BlockSpec((BATCH, NCH, UNIT_K), lambda q: (0, 0, q)),
            pl.BlockSpec((CH_T, 2, 1, UNIT_G, S5_GROUP, S5_GROUP), lambda q: (0, 0, q, 0, 0, 0)),
            pl.BlockSpec((2, CH_T, 2, 1, UNIT_G, S5_GROUP, S5_STATE), lambda q: (0, 0, 0, q, 0, 0, 0)),
            pl.BlockSpec((2, CH_T, 2, 1, UNIT_G, S5_STATE, S5_GROUP), lambda q: (0, 0, 0, q, 0, 0, 0)),
            pl.BlockSpec((1, UNIT_CH, 1), lambda q: (q, 0, 0)),
            pl.BlockSpec((1, STATE_TILES, SUB, LANE), lambda q: (q, 0, 0, 0)),
        ],
        out_specs=pl.BlockSpec((BATCH, NCH - NCH_CTX, UNIT_K), lambda q: (0, 0, q)),
        out_shape=jax.ShapeDtypeStruct((BATCH, NCH - NCH_CTX, UNITS * UNIT_K), BF16),
        scratch_shapes=[pltpu.VMEM((STATE_TILES, BATCH * NCH, LANE), F32),
                        pltpu.VMEM((STATE_TILES, BATCH * NCH, LANE), BF16),
                        pltpu.VMEM((BATCH, NCH - NCH_CTX, UNIT_K), F32)],
        compiler_params=_params(("arbitrary",)),
        name="s5_core",
    )(xu, kk, pb, cp, d, lam)


def _fin_kernel(y_ref, sz_ref, x_ref, mod_ref, wglu_ref, bglu_ref, wout_ref, fg_ref, o_ref, tok_scr):
    upt = UNITS_PER_TILE
    for m in range(S5_WIDTH // LANE):
        for hv in range(CH_T // upt):
            tiles = [y_ref[0, :, (m * upt + a) * UNIT_K + hv * LANE:(m * upt + a) * UNIT_K + (hv + 1) * LANE]
                     for a in range(upt)]
            for jj, blk in enumerate(_block_transpose([t.astype(F32) for t in tiles])):
                tok_scr[m, pl.ds(hv * upt + jj, TF // CH_T, stride=CH_T), :] = blk
    y = jnp.concatenate([tok_scr[m] for m in range(S5_WIDTH // LANE)], axis=-1)
    y = jax.nn.gelu(y)
    y = y * jax.nn.sigmoid(_dot(y.astype(BF16), wglu_ref[...]) + bglu_ref[...])
    a = (y * sz_ref[0].astype(F32)).astype(BF16)
    gt = mod_ref[0][:, 2 * D_MODEL:]
    x2 = x_ref[0] + gt * _dot(a, wout_ref[...])
    o_ref[0] = _rms(x2, fg_ref[...])


def _finish(y, sz1, x1, mod1, wglu, bglu, wout, fg):
    full = lambda shape: pl.BlockSpec(shape, lambda b, i: (0,) * len(shape))
    tok = pl.BlockSpec((1, TF, D_MODEL), lambda b, i: (b, i, 0))
    return pl.pallas_call(
        _fin_kernel,
        grid=(BATCH, SEQ // TF),
        in_specs=[pl.BlockSpec((1, TF // CH_T, UNITS * UNIT_K), lambda b, i: (b, i, 0)), tok, tok,
                  _mod_spec(True),
                  full((S5_WIDTH, S5_WIDTH)), full((1, S5_WIDTH)),
                  full((S5_WIDTH, D_MODEL)), full((1, D_MODEL))],
        out_specs=tok,
        out_shape=jax.ShapeDtypeStruct((BATCH, SEQ, D_MODEL), F32),
        scratch_shapes=[pltpu.VMEM((S5_WIDTH // LANE, TF, LANE), F32)],
        compiler_params=_params(("arbitrary", "arbitrary")),
        name="s5_finish",
    )(y, sz1, x1, mod1, wglu, bglu, wout, fg)


def _rope_tables():
    h = QK_ROPE_DIM // 2
    inv = 1.0 / (ROPE_THETA ** (np.arange(0, h, 2, dtype=np.float64) / h))
    pos = np.arange(SEQ)
    ang_r = (pos // GRID_W)[:, None] * inv[None, :]
    ang_c = (pos % GRID_W)[:, None] * inv[None, :]
    cos32 = np.concatenate([np.cos(ang_r)] * 2 + [np.cos(ang_c)] * 2, axis=-1)
    sin32 = np.concatenate([np.sin(ang_r)] * 2 + [np.sin(ang_c)] * 2, axis=-1)
    cos = np.zeros((TOK, HEAD_PAD), np.float32)
    sin = np.zeros((TOK, HEAD_PAD), np.float32)
    kt = np.zeros((TOK, HEAD_PAD), np.float32)
    cos[:, :QK_NOPE_DIM] = 1.0
    cos[SEQ:, QK_NOPE_DIM:QK_DIM] = 1.0
    kt[SEQ:, :QK_ROPE_DIM] = 1.0
    cos[:SEQ, QK_NOPE_DIM:QK_DIM] = cos32
    sin[:SEQ, QK_NOPE_DIM:QK_DIM] = sin32
    kt[:SEQ, :QK_ROPE_DIM] = cos32
    kt[:SEQ, QK_ROPE_DIM:2 * QK_ROPE_DIM] = sin32
    return jnp.asarray(cos), jnp.asarray(sin), jnp.asarray(kt)


def _mla_selectors():
    def partner(d):
        return (d + 8, -1.0) if d % 16 < 8 else (d - 8, 1.0)

    o2 = Q_LORA_RANK + KV_LORA_RANK
    o3 = o2 + QK_ROPE_DIM
    pin = np.zeros((o3 + MLA_WIDTH, PROJ_W), np.float32)
    pin[np.arange(o3), np.arange(o3)] = 1.0
    pin[o3 + np.arange(MLA_WIDTH), 512 + np.arange(MLA_WIDTH)] = 1.0
    pa = np.zeros((MLA_HEADS * QK_DIM, QK_PAD), np.float32)
    pb = np.zeros((MLA_HEADS * QK_DIM, QK_PAD), np.float32)
    pk = np.zeros((MLA_HEADS * 128, QK_PAD), np.float32)
    pv = np.zeros((MLA_HEADS * 128, MLA_WIDTH), np.float32)
    kb = np.zeros((128, QK_PAD), np.float32)
    for d in range(QK_ROPE_DIM):
        src, sign = partner(d)
        pin[o2 + src, o3 + d] = sign
        for hd in range(MLA_HEADS):
            pb[hd * QK_DIM + QK_NOPE_DIM + src, hd * HEAD_PAD + QK_NOPE_DIM + d] = sign
            kb[d, hd * HEAD_PAD + QK_NOPE_DIM + d] = 1.0
            kb[QK_ROPE_DIM + d, hd * HEAD_PAD + QK_NOPE_DIM + d] = 1.0
    for hd in range(MLA_HEADS):
        pa[hd * QK_DIM + np.arange(QK_DIM), hd * HEAD_PAD + np.arange(QK_DIM)] = 1.0
        pk[hd * 128 + np.arange(QK_NOPE_DIM), hd * HEAD_PAD + np.arange(QK_NOPE_DIM)] = 1.0
        pv[hd * 128 + QK_NOPE_DIM + np.arange(V_HEAD_DIM), hd * V_HEAD_DIM + np.arange(V_HEAD_DIM)] = 1.0
    return [jnp.asarray(a, dtype=BF16) for a in (pin, pa, pb, pk, pv, kb)]


def _mla_wprep_kernel(win_ref, wuq_ref, wukv_ref, pin_ref, pa_ref, pb_ref, pk_ref, pv_ref, kb_ref,
                      o_in, o_qa, o_qb, o_k, o_v):
    o_in[...] = _dot(win_ref[...].astype(BF16), pin_ref[...]).astype(BF16)
    wq = (wuq_ref[...] * (SOFTMAX_SCALE * math.log2(math.e))).astype(BF16)
    o_qa[...] = _dot(wq, pa_ref[...]).astype(BF16)
    o_qb[...] = _dot(wq, pb_ref[...]).astype(BF16)
    wkv = wukv_ref[...].astype(BF16)
    o_k[:KV_LORA_RANK] = _dot(wkv, pk_ref[...]).astype(BF16)
    o_k[KV_LORA_RANK:] = kb_ref[...]
    o_v[...] = _dot(wkv, pv_ref[...]).astype(BF16)


def _mla_weights(w_in, w_uq, w_ukv):
    nj = 4
    full = lambda a: pl.BlockSpec(a.shape, lambda j: (0, 0))
    cols = lambda rows, width: pl.BlockSpec((rows, width // nj), lambda j: (0, j))
    sel = _mla_selectors()
    widths = (PROJ_W, QK_PAD, QK_PAD, QK_PAD, MLA_WIDTH, QK_PAD)
    out_rows = (D_MODEL, Q_LORA_RANK, Q_LORA_RANK, 256, KV_LORA_RANK)
    return pl.pallas_call(
        _mla_wprep_kernel,
        grid=(nj,),
        in_specs=[full(w_in), full(w_uq), full(w_ukv)] + [cols(a.shape[0], w) for a, w in zip(sel, widths)],
        out_specs=[cols(r, w) for r, w in zip(out_rows, widths)],
        out_shape=[jax.ShapeDtypeStruct((r, w), BF16) for r, w in zip(out_rows, widths)],
        compiler_params=_params(("arbitrary",)),
        name="mla_weight_prep",
    )(w_in, w_uq, w_ukv, *sel)


def _lam_tiles(lam):
    lam = lam.reshape(2, 2, UNITS, UNIT_ST // LANE, LANE)
    lam = jnp.concatenate([lam[0, 0], lam[1, 0], lam[0, 1], lam[1, 1]], axis=1)
    return jnp.broadcast_to(lam[:, :, None, :], (UNITS, STATE_TILES, SUB, LANE))


def kernel(x, c, ctx, c_ctx, ada_w, ada_b, norm_g, mla_w_in, mla_q_norm, mla_w_uq, mla_kv_norm, mla_w_ukv, mla_w_out, s5_w_in, s5_a_re, s5_a_im, s5_log_step, s5_b_re, s5_b_im, s5_c_re, s5_c_im, s5_d, s5_w_glu, s5_b_glu, s5_w_out, final_g):
    cc = jnp.concatenate([c, c_ctx[None, :], jnp.zeros((7, D_MODEL), F32)], axis=0)
    mods = _modulation(cc, ada_w, ada_b)

    mod_lat = [mods[i, :BATCH, None, :] for i in range(2)]
    mod_ctx = [mods[i, BATCH:BATCH + 1, None, :] for i in range(2)]

    win, wqa, wqb, wk, wv = _mla_weights(mla_w_in[0], mla_w_uq[0], mla_w_ukv[0])
    weights = (norm_g[0][None, :], win, mla_q_norm[0][None, :], mla_kv_norm[0][None, :], wqa, wqb, wk, wv)
    tables = _rope_tables()
    qkvz = _mla_proj(x, mod_lat[0], weights, tables)
    q, k, v, sz = _mla_proj(ctx, mod_ctx[0], weights, tables, filled=qkvz)
    o = _attention(q, k, v)
    out_w = (norm_g[1][None, :], mla_w_out[0].astype(BF16), s5_w_in[0].astype(BF16))
    x1, xu, sz1 = _mla_out(o, sz, x, mod_lat[0], mod_lat[1], *out_w)
    xu, = _mla_out(o, sz, ctx, mod_ctx[0], mod_ctx[1], *out_w, xu_filled=xu)
    mod1 = mod_lat[1]

    n = 2 * S5_GROUPS
    lam, pb, cp, kk = _s5_prep(
        s5_a_re[0].reshape(n, S5_STATE), s5_a_im[0].reshape(n, S5_STATE), s5_log_step[0].reshape(n, 1),
        jnp.swapaxes(s5_b_re[0], -1, -2).reshape(n, S5_GROUP, S5_STATE),
        jnp.swapaxes(s5_b_im[0], -1, -2).reshape(n, S5_GROUP, S5_STATE),
        s5_c_re[0].reshape(n, S5_GROUP, S5_STATE), s5_c_im[0].reshape(n, S5_GROUP, S5_STATE))
    y = _s5_core(xu, kk.reshape(CH_T, 2, UNITS, UNIT_G, S5_GROUP, S5_GROUP),
                 pb.reshape(2, CH_T, 2, UNITS, UNIT_G, S5_GROUP, S5_STATE),
                 cp.reshape(2, CH_T, 2, UNITS, UNIT_G, S5_STATE, S5_GROUP),
                 s5_d[0].reshape(UNITS, UNIT_CH, 1), _lam_tiles(lam))
    return _finish(y, sz1, x1, mod1, s5_w_glu[0].astype(BF16), s5_b_glu[0][None, :], s5_w_out[0].astype(BF16),
                   final_g[None, :])
```

```python
import functools
import math

import jax
import jax.numpy as jnp
import numpy as np
from jax import lax
from jax.experimental import pallas as pl
from jax.experimental.pallas import tpu as pltpu

D_MODEL = 1024
BATCH = 8
SEQ = 2048
GRID_W = 64
CTX_LEN = 256
TOK = CTX_LEN + SEQ
EPS = 1e-6

MLA_HEADS = 16
QK_NOPE_DIM = 64
QK_ROPE_DIM = 32
V_HEAD_DIM = 64
Q_LORA_RANK = 256
KV_LORA_RANK = 128
MLA_WIDTH = MLA_HEADS * V_HEAD_DIM
QK_DIM = QK_NOPE_DIM + QK_ROPE_DIM
SOFTMAX_SCALE = QK_DIM ** -0.5
ROPE_THETA = 10000.0
HEAD_PAD = 128
QK_PAD = MLA_HEADS * HEAD_PAD
PROJ_W = 1536

S5_WIDTH = D_MODEL
S5_GROUP = 16
S5_GROUPS = 64
S5_STATE = 64
CH_T = 8
UNIT_G = 2
UNIT_CH = UNIT_G * S5_GROUP
UNITS = S5_GROUPS // UNIT_G
UNIT_K = CH_T * UNIT_CH
UNIT_ST = UNIT_G * S5_STATE
NCH = TOK // CH_T
NCH_CTX = CTX_LEN // CH_T
LANE = 128
SUB = 8
UNITS_PER_TILE = LANE // UNIT_CH

TL = 512
CTX_NB = 2
TF = 1024
TQ = 256
KCH = 256
HPAIRS = 2
assert TQ == CTX_LEN
VMEM_LIMIT = 56 * 1024 * 1024

F32 = jnp.float32
BF16 = jnp.bfloat16


def _params(sem, flags=None):
    return pltpu.CompilerParams(dimension_semantics=sem, vmem_limit_bytes=VMEM_LIMIT, flags=flags)


def _silu(v):
    return v * jax.nn.sigmoid(v)


def _rms(v, g):
    return v * lax.rsqrt(jnp.mean(v * v, axis=-1, keepdims=True) + EPS) * g


def _dot(a, b):
    return jnp.dot(a, b, preferred_element_type=F32)


def _mod_kernel(cc_ref, w_ref, b_ref, o_ref):
    a = _silu(cc_ref[...]).astype(BF16)
    o_ref[0] = _dot(a, w_ref[0].astype(BF16)) + b_ref[0]


def _modulation(cc, ada_w, ada_b):
    depth = ada_w.shape[0]
    tn = 768
    return pl.pallas_call(
        _mod_kernel,
        grid=(depth, 3 * D_MODEL // tn),
        in_specs=[
            pl.BlockSpec((16, D_MODEL), lambda i, j: (0, 0)),
            pl.BlockSpec((1, D_MODEL, tn), lambda i, j: (i, 0, j)),
            pl.BlockSpec((1, 1, tn), lambda i, j: (i, 0, j)),
        ],
        out_specs=pl.BlockSpec((1, 16, tn), lambda i, j: (i, 0, j)),
        out_shape=jax.ShapeDtypeStruct((depth, 16, 3 * D_MODEL), F32),
        compiler_params=_params(("arbitrary", "arbitrary")),
        name="modulation",
    )(cc, ada_w, ada_b.reshape(depth, 1, 3 * D_MODEL))


def _mod_spec(per_batch):
    return pl.BlockSpec((1, 1, 3 * D_MODEL), (lambda b, i: (b, 0, 0)) if per_batch else (lambda b, i: (0, 0, 0)))


def _mla_proj_kernel(x_ref, mod_ref, g_ref, win_ref, qg_ref, kvg_ref, wqa_ref, wqb_ref, wk_ref, wv_ref,
                     cos_ref, sin_ref, kt_ref, *refs):
    q_ref, k_ref, v_ref, sz_ref = refs[-4:]
    nb, tile = x_ref.shape[0], x_ref.shape[1]
    x = x_ref[...].reshape(nb * tile, D_MODEL)
    mod = mod_ref[0]
    sh = mod[:, :D_MODEL]
    sc = mod[:, D_MODEL:2 * D_MODEL]
    h = _rms(x, g_ref[...]) * (1.0 + sc) + sh
    p = _dot(h.astype(BF16), win_ref[...])
    cqn = _rms(p[:, :Q_LORA_RANK], qg_ref[...]).astype(BF16)
    ckvn = _rms(p[:, Q_LORA_RANK:Q_LORA_RANK + KV_LORA_RANK], kvg_ref[...]).astype(BF16)
    kr = p[:, 384:512]
    z = p[:, 512:]
    qa = _dot(cqn, wqa_ref[...])
    qb = _dot(cqn, wqb_ref[...])
    cos, sin, kt = (jnp.concatenate([t[...]] * nb, axis=0) for t in (cos_ref, sin_ref, kt_ref))
    for hd in range(MLA_HEADS):
        sl = slice(hd * HEAD_PAD, (hd + 1) * HEAD_PAD)
        q_ref[:, :, sl] = (qa[:, sl] * cos + qb[:, sl] * sin).astype(BF16).reshape(nb, tile, HEAD_PAD)
    kin = jnp.concatenate([ckvn, (kr * kt).astype(BF16)], axis=-1)
    k_ref[...] = _dot(kin, wk_ref[...]).astype(BF16).reshape(nb, tile, QK_PAD)
    v_ref[...] = _dot(ckvn, wv_ref[...]).astype(BF16).reshape(nb, tile, MLA_WIDTH)
    sz_ref[...] = _silu(z).astype(BF16).reshape(nb, tile, MLA_WIDTH)


def _mla_proj(xs, mod, weights, tables, filled=None):
    is_ctx = filled is not None
    tile, nb = (CTX_LEN, CTX_NB) if is_ctx else (TL, 1)
    off = SEQ // tile if is_ctx else 0
    full = lambda a: pl.BlockSpec(a.shape, lambda b, i: (0,) * a.ndim)
    tok = lambda w: pl.BlockSpec((nb, tile, w), lambda b, i: (b, i + off, 0))
    pos = pl.BlockSpec((tile, HEAD_PAD), lambda b, i: (i + off, 0))
    widths = (QK_PAD, QK_PAD, MLA_WIDTH, MLA_WIDTH)
    in_specs = [pl.BlockSpec((nb, tile, D_MODEL), lambda b, i: (b, i, 0)), _mod_spec(not is_ctx)]
    in_specs += [full(w) for w in weights] + [pos, pos, pos]
    args = [xs, mod, *weights, *tables]
    aliases = {}
    if is_ctx:
        aliases = {len(args) + n: n for n in range(4)}
        in_specs += [pl.BlockSpec(memory_space=pl.ANY)] * 4
        args += list(filled)
    return pl.pallas_call(
        _mla_proj_kernel,
        grid=(BATCH // nb, xs.shape[1] // tile),
        in_specs=in_specs,
        out_specs=[tok(w) for w in widths],
        out_shape=[jax.ShapeDtypeStruct((BATCH, TOK, w), BF16) for w in widths],
        input_output_aliases=aliases,
        compiler_params=_params(("arbitrary", "arbitrary")),
        name="mla_proj_ctx" if is_ctx else "mla_proj",
    )(*args)


def _attn_kernel(q_ref, k_ref, v_ref, o_ref, s_buf, m_buf, vx_buf, cs_buf, cm_buf):
    nt = SEQ // TQ
    lane = lax.broadcasted_iota(jnp.int32, (TOK, 2 * V_HEAD_DIM), 1)
    for hp in range(HPAIRS):
        v = v_ref[0, :, hp * 2 * V_HEAD_DIM:(hp + 1) * 2 * V_HEAD_DIM]
        vx_buf[hp, 0] = jnp.where(lane < V_HEAD_DIM, v, (lane == V_HEAD_DIM).astype(BF16))
        vx_buf[hp, 1] = jnp.where(lane >= V_HEAD_DIM, v, (lane == 0).astype(BF16))

    def scores(hp, row, k0, nk, slot):
        sb, mb = (cs_buf.at[hp], cm_buf.at[hp]) if slot is None else (s_buf.at[slot], m_buf.at[slot])
        for hh in range(2):
            c0 = (2 * hp + hh) * HEAD_PAD
            s = lax.dot_general(q_ref[0, pl.ds(row, TQ), c0:c0 + HEAD_PAD], k_ref[0, k0:k0 + nk, c0:c0 + HEAD_PAD],
                                (((1,), (1,)), ((), ())), preferred_element_type=F32)
            sb[hh, :, :nk] = s
            mb[hh] = jnp.broadcast_to(jnp.max(s, axis=-1, keepdims=True), (TQ, KCH))

    def values(hp, row, k0, nk, slot):
        sb, mb = (cs_buf.at[hp], cm_buf.at[hp]) if slot is None else (s_buf.at[slot], m_buf.at[slot])
        outs = []
        for hh in range(2):
            m = mb[hh]
            ps = [jnp.exp2(sb[hh, :, n * KCH:(n + 1) * KCH] - m).astype(BF16) for n in range(nk // KCH)]
            acc = _dot(jnp.concatenate(ps, axis=-1), vx_buf[hp, hh, k0:k0 + nk, :])
            l_col = V_HEAD_DIM if hh == 0 else 0
            outs.append(acc / acc[:, l_col:l_col + 1])
        olane = lax.broadcasted_iota(jnp.int32, outs[0].shape, 1)
        o_ref[0, pl.ds(row, TQ), hp * 2 * V_HEAD_DIM:(hp + 1) * 2 * V_HEAD_DIM] = jnp.where(
            olane < V_HEAD_DIM, outs[0], outs[1]).astype(BF16)

    for hp in range(HPAIRS):
        scores(hp, SEQ, SEQ, CTX_LEN, None)
    scores(0, 0, 0, TOK, 0)
    for hp in range(HPAIRS):
        values(hp, SEQ, SEQ, CTX_LEN, None)
    for hp in range(HPAIRS):
        for t in range(1, nt):
            scores(hp, t * TQ, 0, TOK, t % 2)
            values(hp, (t - 1) * TQ, 0, TOK, (t - 1) % 2)
        if hp + 1 < HPAIRS:
            scores(hp + 1, 0, 0, TOK, 0)
        values(hp, (nt - 1) * TQ, 0, TOK, (nt - 1) % 2)


def _attention(q, k, v):
    qk = pl.BlockSpec((1, TOK, HPAIRS * 2 * HEAD_PAD), lambda b, h: (b, 0, h))
    vo = pl.BlockSpec((1, TOK, HPAIRS * 2 * V_HEAD_DIM), lambda b, h: (b, 0, h))
    return pl.pallas_call(
        _attn_kernel,
        grid=(BATCH, MLA_HEADS // (2 * HPAIRS)),
        in_specs=[qk, qk, vo],
        out_specs=vo,
        out_shape=jax.ShapeDtypeStruct((BATCH, TOK, MLA_WIDTH), BF16),
        scratch_shapes=[
            pltpu.VMEM((2, 2, TQ, TOK), F32),
            pltpu.VMEM((2, 2, TQ, KCH), F32),
            pltpu.VMEM((HPAIRS, 2, TOK, 2 * V_HEAD_DIM), BF16),
            pltpu.VMEM((HPAIRS, 2, TQ, CTX_LEN), F32),
            pltpu.VMEM((HPAIRS, 2, TQ, KCH), F32),
        ],
        compiler_params=_params(("arbitrary", "arbitrary")),
        name="attention",
    )(q, k, v)


def _block_transpose(vs):
    vs = list(vs)
    n = len(vs)
    lane = lax.broadcasted_iota(jnp.int32, vs[0].shape, 1)
    step, width = n // 2, LANE // 2
    while step >= 1:
        lo = lane % (2 * width) < width
        for base in range(0, n, 2 * step):
            for i in range(base, base + step):
                va, vb = vs[i], vs[i + step]
                vs[i] = jnp.where(lo, va, pltpu.roll(vb, width, 1))
                vs[i + step] = jnp.where(lo, pltpu.roll(va, LANE - width, 1), vb)
        step, width = step // 2, width // 2
    return vs


def _mla_out_kernel(o_ref, sz_ref, x_ref, mod0_ref, mod1_ref, g1_ref, wout_ref, win_ref, *refs, is_ctx):
    if is_ctx:
        xu_ref, tok_scr = refs[-2:]
    else:
        x1_ref, xu_ref, sz1_ref, tok_scr = refs
    nb, tile = x_ref.shape[0], x_ref.shape[1]
    rows = nb * tile
    a = (o_ref[...].astype(F32) * sz_ref[...].astype(F32)).astype(BF16).reshape(rows, MLA_WIDTH)
    gt = mod0_ref[0][:, 2 * D_MODEL:]
    x1 = x_ref[...].reshape(rows, D_MODEL) + gt * _dot(a, wout_ref[...])
    mod1 = mod1_ref[0]
    h = _rms(x1, g1_ref[...]) * (1.0 + mod1[:, D_MODEL:2 * D_MODEL]) + mod1[:, :D_MODEL]
    if is_ctx:
        u = _dot(h.astype(BF16), win_ref[:, :S5_WIDTH])
    else:
        x1_ref[0] = x1
        p = _dot(h.astype(BF16), win_ref[...])
        sz1_ref[0] = _silu(p[:, S5_WIDTH:]).astype(BF16)
        u = p[:, :S5_WIDTH]
    for m in range(S5_WIDTH // LANE):
        tok_scr[m] = u[:, m * LANE:(m + 1) * LANE]
    upt = UNITS_PER_TILE
    for m in range(S5_WIDTH // LANE):
        v = [tok_scr[m, pl.ds(t, rows // CH_T, stride=CH_T), :] for t in range(CH_T)]
        for hv in range(CH_T // upt):
            for a, blk in enumerate(_block_transpose(v[hv * upt:(hv + 1) * upt])):
                c0 = (m * upt + a) * UNIT_K + hv * LANE
                xu_ref[:, :, c0:c0 + LANE] = blk.astype(BF16).reshape(nb, tile // CH_T, LANE)


def _mla_out(o, sz, xs, mod0, mod1, g1, wout, win, xu_filled=None):
    is_ctx = xu_filled is not None
    tile, nb = (CTX_LEN, CTX_NB) if is_ctx else (TL, 1)
    off = SEQ // tile if is_ctx else 0
    full = lambda a: pl.BlockSpec(a.shape, lambda b, i: (0,) * a.ndim)
    shared = lambda w: pl.BlockSpec((nb, tile, w), lambda b, i: (b, i + off, 0))
    own = lambda w: pl.BlockSpec((nb, tile, w), lambda b, i: (b, i, 0))
    xu_spec = pl.BlockSpec((nb, tile // CH_T, UNITS * UNIT_K), lambda b, i: (b, i + off, 0))
    xu_shape = jax.ShapeDtypeStruct((BATCH, NCH, UNITS * UNIT_K), BF16)
    in_specs = [shared(MLA_WIDTH), shared(MLA_WIDTH), own(D_MODEL), _mod_spec(not is_ctx), _mod_spec(not is_ctx),
                full(g1), full(wout), full(win)]
    args = [o, sz, xs, mod0, mod1, g1, wout, win]
    if is_ctx:
        in_specs.append(pl.BlockSpec(memory_space=pl.ANY))
        args.append(xu_filled)
        out_specs, out_shape, aliases = [xu_spec], [xu_shape], {len(args) - 1: 0}
    else:
        out_specs = [own(D_MODEL), xu_spec, own(S5_WIDTH)]
        out_shape = [jax.ShapeDtypeStruct((BATCH, SEQ, D_MODEL), F32), xu_shape,
                     jax.ShapeDtypeStruct((BATCH, SEQ, S5_WIDTH), BF16)]
        aliases = {}
    return pl.pallas_call(
        functools.partial(_mla_out_kernel, is_ctx=is_ctx),
        grid=(BATCH // nb, xs.shape[1] // tile),
        in_specs=in_specs,
        out_specs=out_specs,
        out_shape=out_shape,
        input_output_aliases=aliases,
        scratch_shapes=[pltpu.VMEM((S5_WIDTH // LANE, nb * tile, LANE), F32)],
        compiler_params=_params(("arbitrary", "arbitrary")),
        name="mla_out_s5_in_ctx" if is_ctx else "mla_out_s5_in",
    )(*args)


def _cmul(ar, ai, br, bi):
    return ar * br - ai * bi, ar * bi + ai * br


def _group_dot(a, b, precision=lax.Precision.HIGHEST):
    return lax.dot_general(a, b, (((2,), (2,)), ((0,), (0,))), precision=precision, preferred_element_type=F32)


def _s5_prep_kernel(are_ref, aim_ref, ls_ref, bre_ref, bim_ref, cre_ref, cim_ref, lam_ref, pb_ref, cp_ref, kk_ref):
    ar = are_ref[...]
    ai = aim_ref[...]
    dt = jnp.exp(ls_ref[...])
    mag = jnp.exp(ar * dt)
    lb_re = mag * jnp.cos(ai * dt)
    lb_im = mag * jnp.sin(ai * dt)
    den = ar * ar + ai * ai
    nr = lb_re - 1.0
    f_re = ((nr * ar + lb_im * ai) / den)[:, None, :]
    f_im = ((lb_im * ar - nr * ai) / den)[:, None, :]
    bb_re, bb_im = _cmul(f_re, f_im, bre_ref[...], bim_ref[...])
    c_re = cre_ref[...]
    c_im = cim_ref[...]
    pw_re = jnp.ones_like(lb_re)
    pw_im = jnp.zeros_like(lb_re)
    for r in range(CH_T + 1):
        pr = pw_re[:, None, :]
        pi = pw_im[:, None, :]
        cl_re, cl_im = _cmul(c_re, c_im, pr, pi)
        if r < CH_T:
            q_re, q_im = _cmul(pr, pi, bb_re, bb_im)
            pb_ref[0, r] = q_re
            pb_ref[1, r] = q_im
            kk_ref[r] = _group_dot(jnp.concatenate([bb_re, bb_im], axis=-1),
                                   jnp.concatenate([cl_re, -cl_im], axis=-1))
        if r > 0:
            cp_ref[0, r - 1] = cl_re
            cp_ref[1, r - 1] = -cl_im
        if r == CH_T:
            lam_ref[0] = pw_re
            lam_ref[1] = pw_im
        else:
            pw_re, pw_im = _cmul(pw_re, pw_im, lb_re, lb_im)


def _s5_prep(a_re, a_im, log_step, b_re_t, b_im_t, c_re, c_im):
    n = a_re.shape[0]
    nb = 32
    row2 = pl.BlockSpec((nb, S5_STATE), lambda i: (i, 0))
    row3 = pl.BlockSpec((nb, S5_GROUP, S5_STATE), lambda i: (i, 0, 0))
    return pl.pallas_call(
        _s5_prep_kernel,
        grid=(n // nb,),
        in_specs=[row2, row2, pl.BlockSpec((nb, 1), lambda i: (i, 0)), row3, row3, row3, row3],
        out_specs=[
            pl.BlockSpec((2, nb, S5_STATE), lambda i: (0, i, 0)),
            pl.BlockSpec((2, CH_T, nb, S5_GROUP, S5_STATE), lambda i: (0, 0, i, 0, 0)),
            pl.BlockSpec((2, CH_T, nb, S5_GROUP, S5_STATE), lambda i: (0, 0, i, 0, 0)),
            pl.BlockSpec((CH_T, nb, S5_GROUP, S5_GROUP), lambda i: (0, i, 0, 0)),
        ],
        out_shape=[
            jax.ShapeDtypeStruct((2, n, S5_STATE), F32),
            jax.ShapeDtypeStruct((2, CH_T, n, S5_GROUP, S5_STATE), F32),
            jax.ShapeDtypeStruct((2, CH_T, n, S5_GROUP, S5_STATE), F32),
            jax.ShapeDtypeStruct((CH_T, n, S5_GROUP, S5_GROUP), F32),
        ],
        compiler_params=_params(("arbitrary",)),
        name="s5_prep",
    )(a_re, a_im, log_step, b_re_t, b_im_t, c_re, c_im)


STATE_TILES = 2 * 2 * UNIT_ST // LANE


def _hdot(a, rep):
    return _dot(a.astype(BF16), rep)


def _unit_operators(kk_ref, pb_ref, cp_ref, d_ref):
    def iota(shape, dim):
        return lax.broadcasted_iota(jnp.int32, shape, dim)

    rep16 = (iota((S5_GROUP, UNIT_K), 1) % S5_GROUP == iota((S5_GROUP, UNIT_K), 0)).astype(BF16)
    rep64 = (iota((S5_STATE, UNIT_ST), 1) % S5_STATE == iota((S5_STATE, UNIT_ST), 0)).astype(BF16)
    row = iota((UNIT_CH, UNIT_K), 0)
    col = iota((UNIT_CH, UNIT_K), 1)
    same_group_out = row // S5_GROUP == (col // S5_GROUP) % UNIT_G
    same_group_st = (iota((UNIT_CH, UNIT_ST), 0) // S5_GROUP) == (iota((UNIT_CH, UNIT_ST), 1) // S5_STATE)
    on_diag = row == col % UNIT_CH
    col_t = col // UNIT_CH
    kexp = _hdot(jnp.concatenate([kk_ref[k, d, 0].reshape(UNIT_CH, S5_GROUP)
                                  for d in range(2) for k in range(CH_T)], axis=0), rep16)
    pexp = _hdot(jnp.concatenate([pb_ref[ri, r, d, 0].reshape(UNIT_CH, S5_STATE)
                                  for d in range(2) for ri in range(2) for r in range(CH_T)], axis=0), rep64)
    cexp = _hdot(jnp.concatenate([cp_ref[ri, rr, d, 0].reshape(UNIT_CH, S5_STATE)
                                  for d in range(2) for ri in range(2) for rr in range(CH_T)], axis=0), rep64)

    def blk(a, idx, nrows):
        return a[idx * nrows:(idx + 1) * nrows]

    rows = []
    for j in range(CH_T):
        acc = None
        for d in range(2):
            lag = col_t - j if d == 0 else j - col_t
            sel = jnp.zeros((UNIT_CH, UNIT_K), F32)
            for k in range(CH_T):
                sel = jnp.where(lag == k, blk(kexp, d * CH_T + k, UNIT_CH), sel)
            acc = sel if acc is None else acc + sel
        acc = jnp.where(same_group_out, acc, 0.0)
        rows.append(acc + jnp.where((col_t == j) & on_diag, d_ref[0], 0.0))
    parts = [jnp.concatenate(rows, axis=0)]
    cots = []
    for d in range(2):
        rows = []
        for j in range(CH_T):
            r = CH_T - 1 - j if d == 0 else j
            rows.append(jnp.concatenate(
                [jnp.where(same_group_st, blk(pexp, (d * 2 + ri) * CH_T + r, UNIT_CH), 0.0) for ri in range(2)],
                axis=-1))
        parts.append(jnp.concatenate(rows, axis=0))
        rows = []
        for t in range(CH_T):
            rr = t if d == 0 else CH_T - 1 - t
            rows.append(jnp.concatenate(
                [jnp.where(same_group_st, blk(cexp, (d * 2 + ri) * CH_T + rr, UNIT_CH), 0.0) for ri in range(2)],
                axis=-1))
        cots.append(jnp.concatenate(rows, axis=0))
    return jnp.concatenate(parts, axis=-1).astype(BF16), jnp.concatenate(cots, axis=-1).astype(BF16)


def _s5_core_kernel(x_ref, kk_ref, pb_ref, cp_ref, d_ref, lam_ref, y_ref, st_scr, sp_scr, y0_scr):
    w1, cot = _unit_operators(kk_ref, pb_ref, cp_ref, d_ref)
    nlat = NCH - NCH_CTX
    for b in range(BATCH):
        r = _dot(x_ref[b], w1)
        y0_scr[b] = r[:nlat, :UNIT_K]
        for lt in range(STATE_TILES):
            c0 = UNIT_K + lt * LANE
            st_scr[lt, pl.ds(b, NCH, stride=BATCH), :] = r[:, c0:c0 + LANE]
    lam = [lam_ref[0, lt] for lt in range(STATE_TILES)]

    def rows(chunk):
        return pl.ds(pl.multiple_of(chunk * BATCH, BATCH), BATCH)

    nre = UNIT_ST // LANE
    tpd = 2 * nre

    def load_z(row, base):
        return [st_scr[base + k, rows(row), :] for k in range(tpd)]

    def advance(state, z, base):
        a = lam[base:base + tpd]
        re = [a[k] * state[k] - a[nre + k] * state[nre + k] + z[k] for k in range(nre)]
        im = [a[k] * state[nre + k] + a[nre + k] * state[k] + z[nre + k] for k in range(nre)]
        return re + im

    def keep(lo, hi, chunk_lo, base):
        r16 = pl.ds(pl.multiple_of(chunk_lo * BATCH, 2 * BATCH), 2 * BATCH)
        for k in range(tpd):
            sp_scr[base + k, r16, :] = jnp.concatenate([lo[k], hi[k]], axis=0).astype(BF16)

    def fwd_row(i):
        return jnp.where(i < NCH_CTX, nlat + i, i - NCH_CTX)

    def two_steps(m, carry):
        s_f, z_f, s_b, z_b = carry
        i = 2 * m
        rf = fwd_row(i)
        rb = NCH - 1 - i
        z_f1 = load_z(rf + 1, 0)
        z_b1 = load_z(rb - 1, tpd)
        nxt = jnp.minimum(i + 2, NCH - 2)
        z_f2 = load_z(fwd_row(nxt), 0)
        z_b2 = load_z(NCH - 1 - nxt, tpd)
        s_f1 = advance(s_f, z_f, 0)
        s_b1 = advance(s_b, z_b, tpd)
        keep(s_f, s_f1, rf, 0)
        keep(s_b1, s_b, rb - 1, tpd)
        return advance(s_f1, z_f1, 0), z_f2, advance(s_b1, z_b1, tpd), z_b2

    zero = [jnp.zeros((BATCH, LANE), F32)] * tpd
    lax.fori_loop(0, NCH // 2, two_steps, (zero, load_z(nlat, 0), zero, load_z(NCH - 1, tpd)))
    for rb in range(BATCH):
        sl = slice(rb * nlat, (rb + 1) * nlat)
        lhs = jnp.concatenate([sp_scr[lt, sl, :] for lt in range(STATE_TILES)], axis=-1)
        yi = lax.dot_general(lhs, cot, (((1,), (1,)), ((), ())), preferred_element_type=F32)
        for t in range(UNIT_K // LANE):
            st_scr[t, sl, :] = yi[:, t * LANE:(t + 1) * LANE]
    for b in range(BATCH):
        y_ref[b] = (y0_scr[b] + jnp.concatenate(
            [st_scr[t, pl.ds(b, nlat, stride=BATCH), :] for t in range(UNIT_K // LANE)], axis=-1)).astype(BF16)


def _s5_core(xu, kk, pb, cp, d, lam):
    return pl.pallas_call(
        _s5_core_kernel,
        grid=(UNITS,),
        in_specs=[
            pl.BlockSpec((BATCH, NCH, UNIT_K), lambda q: (0, 0, q)),
            pl.BlockSpec((CH_T, 2, 1, UNIT_G, S5_GROUP, S5_GROUP), lambda q: (0, 0, q, 0, 0, 0)),
            pl.BlockSpec((2, CH_T, 2, 1, UNIT_G, S5_GROUP, S5_STATE), lambda q: (0, 0, 0, q, 0, 0, 0)),
            pl.BlockSpec((2, CH_T, 2, 1, UNIT_G, S5_GROUP, S5_STATE), lambda q: (0, 0, 0, q, 0, 0, 0)),
            pl.BlockSpec((1, UNIT_CH, 1), lambda q: (q, 0, 0)),
            pl.BlockSpec((1, STATE_TILES, SUB, LANE), lambda q: (q, 0, 0, 0)),
        ],
        out_specs=pl.BlockSpec((BATCH, NCH - NCH_CTX, UNIT_K), lambda q: (0, 0, q)),
        out_shape=jax.ShapeDtypeStruct((BATCH, NCH - NCH_CTX, UNITS * UNIT_K), BF16),
        scratch_shapes=[pltpu.VMEM((STATE_TILES, BATCH * NCH, LANE), F32),
                        pltpu.VMEM((STATE_TILES, BATCH * NCH, LANE), BF16),
                        pltpu.VMEM((BATCH, NCH - NCH_CTX, UNIT_K), F32)],
        compiler_params=_params(("arbitrary",)),
        name="s5_core",
    )(xu, kk, pb, cp, d, lam)


def _fin_kernel(y_ref, sz_ref, x_ref, mod_ref, wglu_ref, bglu_ref, wout_ref, fg_ref, o_ref, tok_scr):
    upt = UNITS_PER_TILE
    for m in range(S5_WIDTH // LANE):
        for hv in range(CH_T // upt):
            tiles = [y_ref[0, :, (m * upt + a) * UNIT_K + hv * LANE:(m * upt + a) * UNIT_K + (hv + 1) * LANE]
                     for a in range(upt)]
            for jj, blk in enumerate(_block_transpose([t.astype(F32) for t in tiles])):
                tok_scr[m, pl.ds(hv * upt + jj, TF // CH_T, stride=CH_T), :] = blk
    y = jnp.concatenate([tok_scr[m] for m in range(S5_WIDTH // LANE)], axis=-1)
    y = jax.nn.gelu(y)
    y = y * jax.nn.sigmoid(_dot(y.astype(BF16), wglu_ref[...]) + bglu_ref[...])
    a = (y * sz_ref[0].astype(F32)).astype(BF16)
    gt = mod_ref[0][:, 2 * D_MODEL:]
    x2 = x_ref[0] + gt * _dot(a, wout_ref[...])
    o_ref[0] = _rms(x2, fg_ref[...])


def _finish(y, sz1, x1, mod1, wglu, bglu, wout, fg):
    full = lambda shape: pl.BlockSpec(shape, lambda b, i: (0,) * len(shape))
    tok = pl.BlockSpec((1, TF, D_MODEL), lambda b, i: (b, i, 0))
    return pl.pallas_call(
        _fin_kernel,
        grid=(BATCH, SEQ // TF),
        in_specs=[pl.BlockSpec((1, TF // CH_T, UNITS * UNIT_K), lambda b, i: (b, i, 0)), tok, tok,
                  _mod_spec(True),
                  full((S5_WIDTH, S5_WIDTH)), full((1, S5_WIDTH)),
                  full((S5_WIDTH, D_MODEL)), full((1, D_MODEL))],
        out_specs=tok,
        out_shape=jax.ShapeDtypeStruct((BATCH, SEQ, D_MODEL), F32),
        scratch_shapes=[pltpu.VMEM((S5_WIDTH // LANE, TF, LANE), F32)],
        compiler_params=_params(("arbitrary", "arbitrary")),
        name="s5_finish",
    )(y, sz1, x1, mod1, wglu, bglu, wout, fg)


def _rope_tables():
    h = QK_ROPE_DIM // 2
    inv = 1.0 / (ROPE_THETA ** (np.arange(0, h, 2, dtype=np.float64) / h))
    pos = np.arange(SEQ)
    ang_r = (pos // GRID_W)[:, None] * inv[None, :]
    ang_c = (pos % GRID_W)[:, None] * inv[None, :]
    cos32 = np.concatenate([np.cos(ang_r)] * 2 + [np.cos(ang_c)] * 2, axis=-1)
    sin32 = np.concatenate([np.sin(ang_r)] * 2 + [np.sin(ang_c)] * 2, axis=-1)
    cos = np.zeros((TOK, HEAD_PAD), np.float32)
    sin = np.zeros((TOK, HEAD_PAD), np.float32)
    kt = np.zeros((TOK, HEAD_PAD), np.float32)
    cos[:, :QK_NOPE_DIM] = 1.0
    cos[SEQ:, QK_NOPE_DIM:QK_DIM] = 1.0
    kt[SEQ:, :QK_ROPE_DIM] = 1.0
    cos[:SEQ, QK_NOPE_DIM:QK_DIM] = cos32
    sin[:SEQ, QK_NOPE_DIM:QK_DIM] = sin32
    kt[:SEQ, :QK_ROPE_DIM] = cos32
    kt[:SEQ, QK_ROPE_DIM:2 * QK_ROPE_DIM] = sin32
    return jnp.asarray(cos), jnp.asarray(sin), jnp.asarray(kt)


def _mla_selectors():
    def partner(d):
        return (d + 8, -1.0) if d % 16 < 8 else (d - 8, 1.0)

    o2 = Q_LORA_RANK + KV_LORA_RANK
    o3 = o2 + QK_ROPE_DIM
    pin = np.zeros((o3 + MLA_WIDTH, PROJ_W), np.float32)
    pin[np.arange(o3), np.arange(o3)] = 1.0
    pin[o3 + np.arange(MLA_WIDTH), 512 + np.arange(MLA_WIDTH)] = 1.0
    pa = np.zeros((MLA_HEADS * QK_DIM, QK_PAD), np.float32)
    pb = np.zeros((MLA_HEADS * QK_DIM, QK_PAD), np.float32)
    pk = np.zeros((MLA_HEADS * 128, QK_PAD), np.float32)
    pv = np.zeros((MLA_HEADS * 128, MLA_WIDTH), np.float32)
    kb = np.zeros((128, QK_PAD), np.float32)
    for d in range(QK_ROPE_DIM):
        src, sign = partner(d)
        pin[o2 + src, o3 + d] = sign
        for hd in range(MLA_HEADS):
            pb[hd * QK_DIM + QK_NOPE_DIM + src, hd * HEAD_PAD + QK_NOPE_DIM + d] = sign
            kb[d, hd * HEAD_PAD + QK_NOPE_DIM + d] = 1.0
            kb[QK_ROPE_DIM + d, hd * HEAD_PAD + QK_NOPE_DIM + d] = 1.0
    for hd in range(MLA_HEADS):
        pa[hd * QK_DIM + np.arange(QK_DIM), hd * HEAD_PAD + np.arange(QK_DIM)] = 1.0
        pk[hd * 128 + np.arange(QK_NOPE_DIM), hd * HEAD_PAD + np.arange(QK_NOPE_DIM)] = 1.0
        pv[hd * 128 + QK_NOPE_DIM + np.arange(V_HEAD_DIM), hd * V_HEAD_DIM + np.arange(V_HEAD_DIM)] = 1.0
    return [jnp.asarray(a, dtype=BF16) for a in (pin, pa, pb, pk, pv, kb)]


def _mla_wprep_kernel(win_ref, wuq_ref, wukv_ref, pin_ref, pa_ref, pb_ref, pk_ref, pv_ref, kb_ref,
                      o_in, o_qa, o_qb, o_k, o_v):
    o_in[...] = _dot(win_ref[...].astype(BF16), pin_ref[...]).astype(BF16)
    wq = (wuq_ref[...] * (SOFTMAX_SCALE * math.log2(math.e))).astype(BF16)
    o_qa[...] = _dot(wq, pa_ref[...]).astype(BF16)
    o_qb[...] = _dot(wq, pb_ref[...]).astype(BF16)
    wkv = wukv_ref[...].astype(BF16)
    o_k[:KV_LORA_RANK] = _dot(wkv, pk_ref[...]).astype(BF16)
    o_k[KV_LORA_RANK:] = kb_ref[...]
    o_v[...] = _dot(wkv, pv_ref[...]).astype(BF16)


def _mla_weights(w_in, w_uq, w_ukv):
    nj = 4
    full = lambda a: pl.BlockSpec(a.shape, lambda j: (0, 0))
    cols = lambda rows, width: pl.BlockSpec((rows, width // nj), lambda j: (0, j))
    sel = _mla_selectors()
    widths = (PROJ_W, QK_PAD, QK_PAD, QK_PAD, MLA_WIDTH, QK_PAD)
    out_rows = (D_MODEL, Q_LORA_RANK, Q_LORA_RANK, 256, KV_LORA_RANK)
    return pl.pallas_call(
        _mla_wprep_kernel,
        grid=(nj,),
        in_specs=[full(w_in), full(w_uq), full(w_ukv)] + [cols(a.shape[0], w) for a, w in zip(sel, widths)],
        out_specs=[cols(r, w) for r, w in zip(out_rows, widths)],
        out_shape=[jax.ShapeDtypeStruct((r, w), BF16) for r, w in zip(out_rows, widths)],
        compiler_params=_params(("arbitrary",)),
        name="mla_weight_prep",
    )(w_in, w_uq, w_ukv, *sel)


def _lam_tiles(lam):
    lam = lam.reshape(2, 2, UNITS, UNIT_ST // LANE, LANE)
    lam = jnp.concatenate([lam[0, 0], lam[1, 0], lam[0, 1], lam[1, 1]], axis=1)
    return jnp.broadcast_to(lam[:, :, None, :], (UNITS, STATE_TILES, SUB, LANE))


def kernel(x, c, ctx, c_ctx, ada_w, ada_b, norm_g, mla_w_in, mla_q_norm, mla_w_uq, mla_kv_norm, mla_w_ukv, mla_w_out, s5_w_in, s5_a_re, s5_a_im, s5_log_step, s5_b_re, s5_b_im, s5_c_re, s5_c_im, s5_d, s5_w_glu, s5_b_glu, s5_w_out, final_g):
    cc = jnp.concatenate([c, c_ctx[None, :], jnp.zeros((7, D_MODEL), F32)], axis=0)
    mods = _modulation(cc, ada_w, ada_b)

    mod_lat = [mods[i, :BATCH, None, :] for i in range(2)]
    mod_ctx = [mods[i, BATCH:BATCH + 1, None, :] for i in range(2)]

    win, wqa, wqb, wk, wv = _mla_weights(mla_w_in[0], mla_w_uq[0], mla_w_ukv[0])
    weights = (norm_g[0][None, :], win, mla_q_norm[0][None, :], mla_kv_norm[0][None, :], wqa, wqb, wk, wv)
    tables = _rope_tables()
    qkvz = _mla_proj(x, mod_lat[0], weights, tables)
    q, k, v, sz = _mla_proj(ctx, mod_ctx[0], weights, tables, filled=qkvz)
    o = _attention(q, k, v)
    out_w = (norm_g[1][None, :], mla_w_out[0].astype(BF16), s5_w_in[0].astype(BF16))
    x1, xu, sz1 = _mla_out(o, sz, x, mod_lat[0], mod_lat[1], *out_w)
    xu, = _mla_out(o, sz, ctx, mod_ctx[0], mod_ctx[1], *out_w, xu_filled=xu)
    mod1 = mod_lat[1]

    n = 2 * S5_GROUPS
    lam, pb, cp, kk = _s5_prep(
        s5_a_re[0].reshape(n, S5_STATE), s5_a_im[0].reshape(n, S5_STATE), s5_log_step[0].reshape(n, 1),
        jnp.swapaxes(s5_b_re[0], -1, -2).reshape(n, S5_GROUP, S5_STATE),
        jnp.swapaxes(s5_b_im[0], -1, -2).reshape(n, S5_GROUP, S5_STATE),
        s5_c_re[0].reshape(n, S5_GROUP, S5_STATE), s5_c_im[0].reshape(n, S5_GROUP, S5_STATE))
    y = _s5_core(xu, kk.reshape(CH_T, 2, UNITS, UNIT_G, S5_GROUP, S5_GROUP),
                 pb.reshape(2, CH_T, 2, UNITS, UNIT_G, S5_GROUP, S5_STATE),
                 cp.reshape(2, CH_T, 2, UNITS, UNIT_G, S5_GROUP, S5_STATE),
                 s5_d[0].reshape(UNITS, UNIT_CH, 1), _lam_tiles(lam))
    return _finish(y, sz1, x1, mod1, s5_w_glu[0].astype(BF16), s5_b_glu[0][None, :], s5_w_out[0].astype(BF16),
                   final_g[None, :])
```

```python
import functools
import math

import jax
import jax.numpy as jnp
import numpy as np
from jax import lax
from jax.experimental import pallas as pl
from jax.experimental.pallas import tpu as pltpu

D_MODEL = 1024
BATCH = 8
SEQ = 2048
GRID_W = 64
CTX_LEN = 256
TOK = CTX_LEN + SEQ
EPS = 1e-6

MLA_HEADS = 16
QK_NOPE_DIM = 64
QK_ROPE_DIM = 32
V_HEAD_DIM = 64
Q_LORA_RANK = 256
KV_LORA_RANK = 128
MLA_WIDTH = MLA_HEADS * V_HEAD_DIM
QK_DIM = QK_NOPE_DIM + QK_ROPE_DIM
SOFTMAX_SCALE = QK_DIM ** -0.5
ROPE_THETA = 10000.0
HEAD_PAD = 128
QK_PAD = MLA_HEADS * HEAD_PAD
PROJ_W = 1536

S5_WIDTH = D_MODEL
S5_GROUP = 16
S5_GROUPS = 64
S5_STATE = 64
CH_T = 8
UNIT_G = 2
UNIT_CH = UNIT_G * S5_GROUP
UNITS = S5_GROUPS // UNIT_G
UNIT_K = CH_T * UNIT_CH
UNIT_ST = UNIT_G * S5_STATE
NCH = TOK // CH_T
NCH_CTX = CTX_LEN // CH_T
LANE = 128
SUB = 8
UNITS_PER_TILE = LANE // UNIT_CH

TL = 512
CTX_NB = 2
TF = 512
TQ = 256
KCH = 256
HPAIRS = 2
assert TQ == CTX_LEN
VMEM_LIMIT = 56 * 1024 * 1024

F32 = jnp.float32
BF16 = jnp.bfloat16


def _params(sem, flags=None):
    return pltpu.CompilerParams(dimension_semantics=sem, vmem_limit_bytes=VMEM_LIMIT, flags=flags)


def _silu(v):
    return v * jax.nn.sigmoid(v)


def _rms(v, g):
    return v * lax.rsqrt(jnp.mean(v * v, axis=-1, keepdims=True) + EPS) * g


def _dot(a, b):
    return jnp.dot(a, b, preferred_element_type=F32)


def _mod_kernel(cc_ref, w_ref, b_ref, o_ref):
    a = _silu(cc_ref[...]).astype(BF16)
    o_ref[0] = _dot(a, w_ref[0].astype(BF16)) + b_ref[0]


def _modulation(cc, ada_w, ada_b):
    depth = ada_w.shape[0]
    tn = 768
    return pl.pallas_call(
        _mod_kernel,
        grid=(depth, 3 * D_MODEL // tn),
        in_specs=[
            pl.BlockSpec((16, D_MODEL), lambda i, j: (0, 0)),
            pl.BlockSpec((1, D_MODEL, tn), lambda i, j: (i, 0, j)),
            pl.BlockSpec((1, 1, tn), lambda i, j: (i, 0, j)),
        ],
        out_specs=pl.BlockSpec((1, 16, tn), lambda i, j: (i, 0, j)),
        out_shape=jax.ShapeDtypeStruct((depth, 16, 3 * D_MODEL), F32),
        compiler_params=_params(("arbitrary", "arbitrary")),
        name="modulation",
    )(cc, ada_w, ada_b.reshape(depth, 1, 3 * D_MODEL))


def _mod_spec(per_batch):
    return pl.BlockSpec((1, 1, 3 * D_MODEL), (lambda b, i: (b, 0, 0)) if per_batch else (lambda b, i: (0, 0, 0)))


def _mla_proj_kernel(x_ref, mod_ref, g_ref, win_ref, qg_ref, kvg_ref, wqa_ref, wqb_ref, wk_ref, wv_ref,
                     cos_ref, sin_ref, kt_ref, *refs):
    q_ref, k_ref, v_ref, sz_ref = refs[-4:]
    nb, tile = x_ref.shape[0], x_ref.shape[1]
    x = x_ref[...].reshape(nb * tile, D_MODEL)
    mod = mod_ref[0]
    sh = mod[:, :D_MODEL]
    sc = mod[:, D_MODEL:2 * D_MODEL]
    h = _rms(x, g_ref[...]) * (1.0 + sc) + sh
    p = _dot(h.astype(BF16), win_ref[...])
    cqn = _rms(p[:, :Q_LORA_RANK], qg_ref[...]).astype(BF16)
    ckvn = _rms(p[:, Q_LORA_RANK:Q_LORA_RANK + KV_LORA_RANK], kvg_ref[...]).astype(BF16)
    kr = p[:, 384:512]
    z = p[:, 512:]
    qa = _dot(cqn, wqa_ref[...])
    qb = _dot(cqn, wqb_ref[...])
    cos, sin, kt = (jnp.concatenate([t[...]] * nb, axis=0) for t in (cos_ref, sin_ref, kt_ref))
    for hd in range(MLA_HEADS):
        sl = slice(hd * HEAD_PAD, (hd + 1) * HEAD_PAD)
        q_ref[:, :, sl] = (qa[:, sl] * cos + qb[:, sl] * sin).astype(BF16).reshape(nb, tile, HEAD_PAD)
    kin = jnp.concatenate([ckvn, (kr * kt).astype(BF16)], axis=-1)
    k_ref[...] = _dot(kin, wk_ref[...]).astype(BF16).reshape(nb, tile, QK_PAD)
    v_ref[...] = _dot(ckvn, wv_ref[...]).astype(BF16).reshape(nb, tile, MLA_WIDTH)
    sz_ref[...] = _silu(z).astype(BF16).reshape(nb, tile, MLA_WIDTH)


def _mla_proj(xs, mod, weights, tables, filled=None):
    is_ctx = filled is not None
    tile, nb = (CTX_LEN, CTX_NB) if is_ctx else (TL, 1)
    off = SEQ // tile if is_ctx else 0
    full = lambda a: pl.BlockSpec(a.shape, lambda b, i: (0,) * a.ndim)
    tok = lambda w: pl.BlockSpec((nb, tile, w), lambda b, i: (b, i + off, 0))
    pos = pl.BlockSpec((tile, HEAD_PAD), lambda b, i: (i + off, 0))
    widths = (QK_PAD, QK_PAD, MLA_WIDTH, MLA_WIDTH)
    in_specs = [pl.BlockSpec((nb, tile, D_MODEL), lambda b, i: (b, i, 0)), _mod_spec(not is_ctx)]
    in_specs += [full(w) for w in weights] + [pos, pos, pos]
    args = [xs, mod, *weights, *tables]
    aliases = {}
    if is_ctx:
        aliases = {len(args) + n: n for n in range(4)}
        in_specs += [pl.BlockSpec(memory_space=pl.ANY)] * 4
        args += list(filled)
    return pl.pallas_call(
        _mla_proj_kernel,
        grid=(BATCH // nb, xs.shape[1] // tile),
        in_specs=in_specs,
        out_specs=[tok(w) for w in widths],
        out_shape=[jax.ShapeDtypeStruct((BATCH, TOK, w), BF16) for w in widths],
        input_output_aliases=aliases,
        compiler_params=_params(("arbitrary", "arbitrary")),
        name="mla_proj_ctx" if is_ctx else "mla_proj",
    )(*args)


def _attn_kernel(q_ref, k_ref, v_ref, o_ref, s_buf, m_buf, vx_buf, cs_buf, cm_buf):
    nt = SEQ // TQ
    lane = lax.broadcasted_iota(jnp.int32, (TOK, 2 * V_HEAD_DIM), 1)
    for hp in range(HPAIRS):
        v = v_ref[0, :, hp * 2 * V_HEAD_DIM:(hp + 1) * 2 * V_HEAD_DIM]
        vx_buf[hp, 0] = jnp.where(lane < V_HEAD_DIM, v, (lane == V_HEAD_DIM).astype(BF16))
        vx_buf[hp, 1] = jnp.where(lane >= V_HEAD_DIM, v, (lane == 0).astype(BF16))

    def scores(hp, row, k0, nk, slot):
        sb, mb = (cs_buf.at[hp], cm_buf.at[hp]) if slot is None else (s_buf.at[slot], m_buf.at[slot])
        for hh in range(2):
            c0 = (2 * hp + hh) * HEAD_PAD
            s = lax.dot_general(q_ref[0, pl.ds(row, TQ), c0:c0 + HEAD_PAD], k_ref[0, k0:k0 + nk, c0:c0 + HEAD_PAD],
                                (((1,), (1,)), ((), ())), preferred_element_type=F32)
            sb[hh, :, :nk] = s
            mb[hh] = jnp.broadcast_to(jnp.max(s, axis=-1, keepdims=True), (TQ, KCH))

    def values(hp, row, k0, nk, slot):
        sb, mb = (cs_buf.at[hp], cm_buf.at[hp]) if slot is None else (s_buf.at[slot], m_buf.at[slot])
        outs = []
        for hh in range(2):
            m = mb[hh]
            ps = [jnp.exp2(sb[hh, :, n * KCH:(n + 1) * KCH] - m).astype(BF16) for n in range(nk // KCH)]
            acc = _dot(jnp.concatenate(ps, axis=-1), vx_buf[hp, hh, k0:k0 + nk, :])
            l_col = V_HEAD_DIM if hh == 0 else 0
            outs.append(acc / acc[:, l_col:l_col + 1])
        olane = lax.broadcasted_iota(jnp.int32, outs[0].shape, 1)
        o_ref[0, pl.ds(row, TQ), hp * 2 * V_HEAD_DIM:(hp + 1) * 2 * V_HEAD_DIM] = jnp.where(
            olane < V_HEAD_DIM, outs[0], outs[1]).astype(BF16)

    for hp in range(HPAIRS):
        scores(hp, SEQ, SEQ, CTX_LEN, None)
    scores(0, 0, 0, TOK, 0)
    for hp in range(HPAIRS):
        values(hp, SEQ, SEQ, CTX_LEN, None)
    for hp in range(HPAIRS):
        for t in range(1, nt):
            scores(hp, t * TQ, 0, TOK, t % 2)
            values(hp, (t - 1) * TQ, 0, TOK, (t - 1) % 2)
        if hp + 1 < HPAIRS:
            scores(hp + 1, 0, 0, TOK, 0)
        values(hp, (nt - 1) * TQ, 0, TOK, (nt - 1) % 2)


def _attention(q, k, v):
    qk = pl.BlockSpec((1, TOK, HPAIRS * 2 * HEAD_PAD), lambda b, h: (b, 0, h))
    vo = pl.BlockSpec((1, TOK, HPAIRS * 2 * V_HEAD_DIM), lambda b, h: (b, 0, h))
    return pl.pallas_call(
        _attn_kernel,
        grid=(BATCH, MLA_HEADS // (2 * HPAIRS)),
        in_specs=[qk, qk, vo],
        out_specs=vo,
        out_shape=jax.ShapeDtypeStruct((BATCH, TOK, MLA_WIDTH), BF16),
        scratch_shapes=[
            pltpu.VMEM((2, 2, TQ, TOK), F32),
            pltpu.VMEM((2, 2, TQ, KCH), F32),
            pltpu.VMEM((HPAIRS, 2, TOK, 2 * V_HEAD_DIM), BF16),
            pltpu.VMEM((HPAIRS, 2, TQ, CTX_LEN), F32),
            pltpu.VMEM((HPAIRS, 2, TQ, KCH), F32),
        ],
        compiler_params=_params(("arbitrary", "arbitrary")),
        name="attention",
    )(q, k, v)


def _block_transpose(vs):
    vs = list(vs)
    n = len(vs)
    lane = lax.broadcasted_iota(jnp.int32, vs[0].shape, 1)
    step, width = n // 2, LANE // 2
    while step >= 1:
        lo = lane % (2 * width) < width
        for base in range(0, n, 2 * step):
            for i in range(base, base + step):
                va, vb = vs[i], vs[i + step]
                vs[i] = jnp.where(lo, va, pltpu.roll(vb, width, 1))
                vs[i + step] = jnp.where(lo, pltpu.roll(va, LANE - width, 1), vb)
        step, width = step // 2, width // 2
    return vs


def _mla_out_kernel(o_ref, sz_ref, x_ref, mod0_ref, mod1_ref, g1_ref, wout_ref, win_ref, *refs, is_ctx):
    if is_ctx:
        xu_ref, tok_scr = refs[-2:]
    else:
        x1_ref, xu_ref, sz1_ref, tok_scr = refs
    nb, tile = x_ref.shape[0], x_ref.shape[1]
    rows = nb * tile
    a = (o_ref[...].astype(F32) * sz_ref[...].astype(F32)).astype(BF16).reshape(rows, MLA_WIDTH)
    gt = mod0_ref[0][:, 2 * D_MODEL:]
    x1 = x_ref[...].reshape(rows, D_MODEL) + gt * _dot(a, wout_ref[...])
    mod1 = mod1_ref[0]
    h = _rms(x1, g1_ref[...]) * (1.0 + mod1[:, D_MODEL:2 * D_MODEL]) + mod1[:, :D_MODEL]
    if is_ctx:
        u = _dot(h.astype(BF16), win_ref[:, :S5_WIDTH])
    else:
        x1_ref[0] = x1
        p = _dot(h.astype(BF16), win_ref[...])
        sz1_ref[0] = _silu(p[:, S5_WIDTH:]).astype(BF16)
        u = p[:, :S5_WIDTH]
    for m in range(S5_WIDTH // LANE):
        tok_scr[m] = u[:, m * LANE:(m + 1) * LANE]
    upt = UNITS_PER_TILE
    for m in range(S5_WIDTH // LANE):
        v = [tok_scr[m, pl.ds(t, rows // CH_T, stride=CH_T), :] for t in range(CH_T)]
        for hv in range(CH_T // upt):
            for a, blk in enumerate(_block_transpose(v[hv * upt:(hv + 1) * upt])):
                c0 = (m * upt + a) * UNIT_K + hv * LANE
                xu_ref[:, :, c0:c0 + LANE] = blk.astype(BF16).reshape(nb, tile // CH_T, LANE)


def _mla_out(o, sz, xs, mod0, mod1, g1, wout, win, xu_filled=None):
    is_ctx = xu_filled is not None
    tile, nb = (CTX_LEN, CTX_NB) if is_ctx else (TL, 1)
    off = SEQ // tile if is_ctx else 0
    full = lambda a: pl.BlockSpec(a.shape, lambda b, i: (0,) * a.ndim)
    shared = lambda w: pl.BlockSpec((nb, tile, w), lambda b, i: (b, i + off, 0))
    own = lambda w: pl.BlockSpec((nb, tile, w), lambda b, i: (b, i, 0))
    xu_spec = pl.BlockSpec((nb, tile // CH_T, UNITS * UNIT_K), lambda b, i: (b, i + off, 0))
    xu_shape = jax.ShapeDtypeStruct((BATCH, NCH, UNITS * UNIT_K), BF16)
    in_specs = [shared(MLA_WIDTH), shared(MLA_WIDTH), own(D_MODEL), _mod_spec(not is_ctx), _mod_spec(not is_ctx),
                full(g1), full(wout), full(win)]
    args = [o, sz, xs, mod0, mod1, g1, wout, win]
    if is_ctx:
        in_specs.append(pl.BlockSpec(memory_space=pl.ANY))
        args.append(xu_filled)
        out_specs, out_shape, aliases = [xu_spec], [xu_shape], {len(args) - 1: 0}
    else:
        out_specs = [own(D_MODEL), xu_spec, own(S5_WIDTH)]
        out_shape = [jax.ShapeDtypeStruct((BATCH, SEQ, D_MODEL), F32), xu_shape,
                     jax.ShapeDtypeStruct((BATCH, SEQ, S5_WIDTH), BF16)]
        aliases = {}
    return pl.pallas_call(
        functools.partial(_mla_out_kernel, is_ctx=is_ctx),
        grid=(BATCH // nb, xs.shape[1] // tile),
        in_specs=in_specs,
        out_specs=out_specs,
        out_shape=out_shape,
        input_output_aliases=aliases,
        scratch_shapes=[pltpu.VMEM((S5_WIDTH // LANE, nb * tile, LANE), F32)],
        compiler_params=_params(("arbitrary", "arbitrary")),
        name="mla_out_s5_in_ctx" if is_ctx else "mla_out_s5_in",
    )(*args)


def _cmul(ar, ai, br, bi):
    return ar * br - ai * bi, ar * bi + ai * br


def _group_dot(a, b, precision=lax.Precision.HIGHEST):
    return lax.dot_general(a, b, (((2,), (2,)), ((0,), (0,))), precision=precision, preferred_element_type=F32)


def _s5_prep_kernel(are_ref, aim_ref, ls_ref, bre_ref, bim_ref, cre_ref, cim_ref, lam_ref, pb_ref, cp_ref, kk_ref):
    ar = are_ref[...]
    ai = aim_ref[...]
    dt = jnp.exp(ls_ref[...])
    mag = jnp.exp(ar * dt)
    lb_re = mag * jnp.cos(ai * dt)
    lb_im = mag * jnp.sin(ai * dt)
    den = ar * ar + ai * ai
    nr = lb_re - 1.0
    f_re = ((nr * ar + lb_im * ai) / den)[:, None, :]
    f_im = ((lb_im * ar - nr * ai) / den)[:, None, :]
    bb_re, bb_im = _cmul(f_re, f_im, bre_ref[...], bim_ref[...])
    c_re = cre_ref[...]
    c_im = cim_ref[...]
    pw_re = jnp.ones_like(lb_re)
    pw_im = jnp.zeros_like(lb_re)
    for r in range(CH_T + 1):
        pr = pw_re[:, None, :]
        pi = pw_im[:, None, :]
        cl_re, cl_im = _cmul(c_re, c_im, pr, pi)
        if r < CH_T:
            q_re, q_im = _cmul(pr, pi, bb_re, bb_im)
            pb_ref[0, r] = q_re
            pb_ref[1, r] = q_im
            kk_ref[r] = _group_dot(jnp.concatenate([bb_re, bb_im], axis=-1),
                                   jnp.concatenate([cl_re, -cl_im], axis=-1))
        if r > 0:
            cp_ref[0, r - 1] = cl_re
            cp_ref[1, r - 1] = -cl_im
        if r == CH_T:
            lam_ref[0] = pw_re
            lam_ref[1] = pw_im
        else:
            pw_re, pw_im = _cmul(pw_re, pw_im, lb_re, lb_im)


def _s5_prep(a_re, a_im, log_step, b_re_t, b_im_t, c_re, c_im):
    n = a_re.shape[0]
    nb = 32
    row2 = pl.BlockSpec((nb, S5_STATE), lambda i: (i, 0))
    row3 = pl.BlockSpec((nb, S5_GROUP, S5_STATE), lambda i: (i, 0, 0))
    return pl.pallas_call(
        _s5_prep_kernel,
        grid=(n // nb,),
        in_specs=[row2, row2, pl.BlockSpec((nb, 1), lambda i: (i, 0)), row3, row3, row3, row3],
        out_specs=[
            pl.BlockSpec((2, nb, S5_STATE), lambda i: (0, i, 0)),
            pl.BlockSpec((2, CH_T, nb, S5_GROUP, S5_STATE), lambda i: (0, 0, i, 0, 0)),
            pl.BlockSpec((2, CH_T, nb, S5_GROUP, S5_STATE), lambda i: (0, 0, i, 0, 0)),
            pl.BlockSpec((CH_T, nb, S5_GROUP, S5_GROUP), lambda i: (0, i, 0, 0)),
        ],
        out_shape=[
            jax.ShapeDtypeStruct((2, n, S5_STATE), F32),
            jax.ShapeDtypeStruct((2, CH_T, n, S5_GROUP, S5_STATE), F32),
            jax.ShapeDtypeStruct((2, CH_T, n, S5_GROUP, S5_STATE), F32),
            jax.ShapeDtypeStruct((CH_T, n, S5_GROUP, S5_GROUP), F32),
        ],
        compiler_params=_params(("arbitrary",)),
        name="s5_prep",
    )(a_re, a_im, log_step, b_re_t, b_im_t, c_re, c_im)


STATE_TILES = 2 * 2 * UNIT_ST // LANE


def _hdot(a, rep):
    return _dot(a.astype(BF16), rep)


def _unit_operators(kk_ref, pb_ref, cp_ref, d_ref):
    def iota(shape, dim):
        return lax.broadcasted_iota(jnp.int32, shape, dim)

    rep16 = (iota((S5_GROUP, UNIT_K), 1) % S5_GROUP == iota((S5_GROUP, UNIT_K), 0)).astype(BF16)
    rep64 = (iota((S5_STATE, UNIT_ST), 1) % S5_STATE == iota((S5_STATE, UNIT_ST), 0)).astype(BF16)
    row = iota((UNIT_CH, UNIT_K), 0)
    col = iota((UNIT_CH, UNIT_K), 1)
    same_group_out = row // S5_GROUP == (col // S5_GROUP) % UNIT_G
    same_group_st = (iota((UNIT_CH, UNIT_ST), 0) // S5_GROUP) == (iota((UNIT_CH, UNIT_ST), 1) // S5_STATE)
    on_diag = row == col % UNIT_CH
    col_t = col // UNIT_CH
    kexp = _hdot(jnp.concatenate([kk_ref[k, d, 0].reshape(UNIT_CH, S5_GROUP)
                                  for d in range(2) for k in range(CH_T)], axis=0), rep16)
    pexp = _hdot(jnp.concatenate([pb_ref[ri, r, d, 0].reshape(UNIT_CH, S5_STATE)
                                  for d in range(2) for ri in range(2) for r in range(CH_T)], axis=0), rep64)
    cexp = _hdot(jnp.concatenate([cp_ref[ri, rr, d, 0].reshape(UNIT_CH, S5_STATE)
                                  for d in range(2) for ri in range(2) for rr in range(CH_T)], axis=0), rep64)

    def blk(a, idx, nrows):
        return a[idx * nrows:(idx + 1) * nrows]

    rows = []
    for j in range(CH_T):
        acc = None
        for d in range(2):
            lag = col_t - j if d == 0 else j - col_t
            sel = jnp.zeros((UNIT_CH, UNIT_K), F32)
            for k in range(CH_T):
                sel = jnp.where(lag == k, blk(kexp, d * CH_T + k, UNIT_CH), sel)
            acc = sel if acc is None else acc + sel
        acc = jnp.where(same_group_out, acc, 0.0)
        rows.append(acc + jnp.where((col_t == j) & on_diag, d_ref[0], 0.0))
    parts = [jnp.concatenate(rows, axis=0)]
    cots = []
    for d in range(2):
        rows = []
        for j in range(CH_T):
            r = CH_T - 1 - j if d == 0 else j
            rows.append(jnp.concatenate(
                [jnp.where(same_group_st, blk(pexp, (d * 2 + ri) * CH_T + r, UNIT_CH), 0.0) for ri in range(2)],
                axis=-1))
        parts.append(jnp.concatenate(rows, axis=0))
        rows = []
        for t in range(CH_T):
            rr = t if d == 0 else CH_T - 1 - t
            rows.append(jnp.concatenate(
                [jnp.where(same_group_st, blk(cexp, (d * 2 + ri) * CH_T + rr, UNIT_CH), 0.0) for ri in range(2)],
                axis=-1))
        cots.append(jnp.concatenate(rows, axis=0))
    return jnp.concatenate(parts, axis=-1).astype(BF16), jnp.concatenate(cots, axis=-1).astype(BF16)


def _s5_core_kernel(x_ref, kk_ref, pb_ref, cp_ref, d_ref, lam_ref, y_ref, st_scr, sp_scr, y0_scr):
    w1, cot = _unit_operators(kk_ref, pb_ref, cp_ref, d_ref)
    nlat = NCH - NCH_CTX
    for b in range(BATCH):
        r = _dot(x_ref[b], w1)
        y0_scr[b] = r[:nlat, :UNIT_K]
        for lt in range(STATE_TILES):
            c0 = UNIT_K + lt * LANE
            st_scr[lt, pl.ds(b, NCH, stride=BATCH), :] = r[:, c0:c0 + LANE]
    lam = [lam_ref[0, lt] for lt in range(STATE_TILES)]

    def rows(chunk):
        return pl.ds(pl.multiple_of(chunk * BATCH, BATCH), BATCH)

    nre = UNIT_ST // LANE
    tpd = 2 * nre

    def load_z(row, base):
        return [st_scr[base + k, rows(row), :] for k in range(tpd)]

    def advance(state, z, base):
        a = lam[base:base + tpd]
        re = [a[k] * state[k] - a[nre + k] * state[nre + k] + z[k] for k in range(nre)]
        im = [a[k] * state[nre + k] + a[nre + k] * state[k] + z[nre + k] for k in range(nre)]
        return re + im

    def keep(lo, hi, chunk_lo, base):
        r16 = pl.ds(pl.multiple_of(chunk_lo * BATCH, 2 * BATCH), 2 * BATCH)
        for k in range(tpd):
            sp_scr[base + k, r16, :] = jnp.concatenate([lo[k], hi[k]], axis=0).astype(BF16)

    def fwd_row(i):
        return jnp.where(i < NCH_CTX, nlat + i, i - NCH_CTX)

    def two_steps(m, carry):
        s_f, z_f, s_b, z_b = carry
        i = 2 * m
        rf = fwd_row(i)
        rb = NCH - 1 - i
        z_f1 = load_z(rf + 1, 0)
        z_b1 = load_z(rb - 1, tpd)
        nxt = jnp.minimum(i + 2, NCH - 2)
        z_f2 = load_z(fwd_row(nxt), 0)
        z_b2 = load_z(NCH - 1 - nxt, tpd)
        s_f1 = advance(s_f, z_f, 0)
        s_b1 = advance(s_b, z_b, tpd)
        keep(s_f, s_f1, rf, 0)
        keep(s_b1, s_b, rb - 1, tpd)
        return advance(s_f1, z_f1, 0), z_f2, advance(s_b1, z_b1, tpd), z_b2

    zero = [jnp.zeros((BATCH, LANE), F32)] * tpd
    lax.fori_loop(0, NCH // 2, two_steps, (zero, load_z(nlat, 0), zero, load_z(NCH - 1, tpd)))
    for rb in range(BATCH):
        sl = slice(rb * nlat, (rb + 1) * nlat)
        lhs = jnp.concatenate([sp_scr[lt, sl, :] for lt in range(STATE_TILES)], axis=-1)
        yi = lax.dot_general(lhs, cot, (((1,), (1,)), ((), ())), preferred_element_type=F32)
        for t in range(UNIT_K // LANE):
            st_scr[t, sl, :] = yi[:, t * LANE:(t + 1) * LANE]
    for b in range(BATCH):
        y_ref[b] = (y0_scr[b] + jnp.concatenate(
            [st_scr[t, pl.ds(b, nlat, stride=BATCH), :] for t in range(UNIT_K // LANE)], axis=-1)).astype(BF16)


def _s5_core(xu, kk, pb, cp, d, lam):
    return pl.pallas_call(
        _s5_core_kernel,
        grid=(UNITS,),
        in_specs=[
            pl.BlockSpec((BATCH, NCH, UNIT_K), lambda q: (0, 0, q)),
            pl.BlockSpec((CH_T, 2, 1, UNIT_G, S5_GROUP, S5_GROUP), lambda q: (0, 0, q, 0, 0, 0)),
            pl.BlockSpec((2, CH_T, 2, 1, UNIT_G, S5_GROUP, S5_STATE), lambda q: (0, 0, 0, q, 0, 0, 0)),
            pl.BlockSpec((2, CH_T, 2, 1, UNIT_G, S5_GROUP, S5_STATE), lambda q: (0, 0, 0, q, 0, 0, 0)),
            pl.BlockSpec((1, UNIT_CH, 1), lambda q: (q, 0, 0)),
            pl.BlockSpec((1, STATE_TILES, SUB, LANE), lambda q: (q, 0, 0, 0)),
        ],
        out_specs=pl.BlockSpec((BATCH, NCH - NCH_CTX, UNIT_K), lambda q: (0, 0, q)),
        out_shape=jax.ShapeDtypeStruct((BATCH, NCH - NCH_CTX, UNITS * UNIT_K), BF16),
        scratch_shapes=[pltpu.VMEM((STATE_TILES, BATCH * NCH, LANE), F32),
                        pltpu.VMEM((STATE_TILES, BATCH * NCH, LANE), BF16),
                        pltpu.VMEM((BATCH, NCH - NCH_CTX, UNIT_K), F32)],
        compiler_params=_params(("arbitrary",)),
        name="s5_core",
    )(xu, kk, pb, cp, d, lam)


def _fin_kernel(y_ref, sz_ref, x_ref, mod_ref, wglu_ref, bglu_ref, wout_ref, fg_ref, o_ref, tok_scr):
    upt = UNITS_PER_TILE
    for m in range(S5_WIDTH // LANE):
        for hv in range(CH_T // upt):
            tiles = [y_ref[0, :, (m * upt + a) * UNIT_K + hv * LANE:(m * upt + a) * UNIT_K + (hv + 1) * LANE]
                     for a in range(upt)]
            for jj, blk in enumerate(_block_transpose([t.astype(F32) for t in tiles])):
                tok_scr[m, pl.ds(hv * upt + jj, TF // CH_T, stride=CH_T), :] = blk
    y = jnp.concatenate([tok_scr[m] for m in range(S5_WIDTH // LANE)], axis=-1)
    y = jax.nn.gelu(y)
    y = y * jax.nn.sigmoid(_dot(y.astype(BF16), wglu_ref[...]) + bglu_ref[...])
    a = (y * sz_ref[0].astype(F32)).astype(BF16)
    gt = mod_ref[0][:, 2 * D_MODEL:]
    x2 = x_ref[0] + gt * _dot(a, wout_ref[...])
    o_ref[0] = _rms(x2, fg_ref[...])


def _finish(y, sz1, x1, mod1, wglu, bglu, wout, fg):
    full = lambda shape: pl.BlockSpec(shape, lambda b, i: (0,) * len(shape))
    tok = pl.BlockSpec((1, TF, D_MODEL), lambda b, i: (b, i, 0))
    return pl.pallas_call(
        _fin_kernel,
        grid=(BATCH, SEQ // TF),
        in_specs=[pl.BlockSpec((1, TF // CH_T, UNITS * UNIT_K), lambda b, i: (b, i, 0)), tok, tok,
                  _mod_spec(True),
                  full((S5_WIDTH, S5_WIDTH)), full((1, S5_WIDTH)),
                  full((S5_WIDTH, D_MODEL)), full((1, D_MODEL))],
        out_specs=tok,
        out_shape=jax.ShapeDtypeStruct((BATCH, SEQ, D_MODEL), F32),
        scratch_shapes=[pltpu.VMEM((S5_WIDTH // LANE, TF, LANE), F32)],
        compiler_params=_params(("arbitrary", "arbitrary")),
        name="s5_finish",
    )(y, sz1, x1, mod1, wglu, bglu, wout, fg)


def _rope_tables():
    h = QK_ROPE_DIM // 2
    inv = 1.0 / (ROPE_THETA ** (np.arange(0, h, 2, dtype=np.float64) / h))
    pos = np.arange(SEQ)
    ang_r = (pos // GRID_W)[:, None] * inv[None, :]
    ang_c = (pos % GRID_W)[:, None] * inv[None, :]
    cos32 = np.concatenate([np.cos(ang_r)] * 2 + [np.cos(ang_c)] * 2, axis=-1)
    sin32 = np.concatenate([np.sin(ang_r)] * 2 + [np.sin(ang_c)] * 2, axis=-1)
    cos = np.zeros((TOK, HEAD_PAD), np.float32)
    sin = np.zeros((TOK, HEAD_PAD), np.float32)
    kt = np.zeros((TOK, HEAD_PAD), np.float32)
    cos[:, :QK_NOPE_DIM] = 1.0
    cos[SEQ:, QK_NOPE_DIM:QK_DIM] = 1.0
    kt[SEQ:, :QK_ROPE_DIM] = 1.0
    cos[:SEQ, QK_NOPE_DIM:QK_DIM] = cos32
    sin[:SEQ, QK_NOPE_DIM:QK_DIM] = sin32
    kt[:SEQ, :QK_ROPE_DIM] = cos32
    kt[:SEQ, QK_ROPE_DIM:2 * QK_ROPE_DIM] = sin32
    return jnp.asarray(cos), jnp.asarray(sin), jnp.asarray(kt)


def _mla_selectors():
    def partner(d):
        return (d + 8, -1.0) if d % 16 < 8 else (d - 8, 1.0)

    o2 = Q_LORA_RANK + KV_LORA_RANK
    o3 = o2 + QK_ROPE_DIM
    pin = np.zeros((o3 + MLA_WIDTH, PROJ_W), np.float32)
    pin[np.arange(o3), np.arange(o3)] = 1.0
    pin[o3 + np.arange(MLA_WIDTH), 512 + np.arange(MLA_WIDTH)] = 1.0
    pa = np.zeros((MLA_HEADS * QK_DIM, QK_PAD), np.float32)
    pb = np.zeros((MLA_HEADS * QK_DIM, QK_PAD), np.float32)
    pk = np.zeros((MLA_HEADS * 128, QK_PAD), np.float32)
    pv = np.zeros((MLA_HEADS * 128, MLA_WIDTH), np.float32)
    kb = np.zeros((128, QK_PAD), np.float32)
    for d in range(QK_ROPE_DIM):
        src, sign = partner(d)
        pin[o2 + src, o3 + d] = sign
        for hd in range(MLA_HEADS):
            pb[hd * QK_DIM + QK_NOPE_DIM + src, hd * HEAD_PAD + QK_NOPE_DIM + d] = sign
            kb[d, hd * HEAD_PAD + QK_NOPE_DIM + d] = 1.0
            kb[QK_ROPE_DIM + d, hd * HEAD_PAD + QK_NOPE_DIM + d] = 1.0
    for hd in range(MLA_HEADS):
        pa[hd * QK_DIM + np.arange(QK_DIM), hd * HEAD_PAD + np.arange(QK_DIM)] = 1.0
        pk[hd * 128 + np.arange(QK_NOPE_DIM), hd * HEAD_PAD + np.arange(QK_NOPE_DIM)] = 1.0
        pv[hd * 128 + QK_NOPE_DIM + np.arange(V_HEAD_DIM), hd * V_HEAD_DIM + np.arange(V_HEAD_DIM)] = 1.0
    return [jnp.asarray(a, dtype=BF16) for a in (pin, pa, pb, pk, pv, kb)]


def _mla_wprep_kernel(win_ref, wuq_ref, wukv_ref, pin_ref, pa_ref, pb_ref, pk_ref, pv_ref, kb_ref,
                      o_in, o_qa, o_qb, o_k, o_v):
    o_in[...] = _dot(win_ref[...].astype(BF16), pin_ref[...]).astype(BF16)
    wq = (wuq_ref[...] * (SOFTMAX_SCALE * math.log2(math.e))).astype(BF16)
    o_qa[...] = _dot(wq, pa_ref[...]).astype(BF16)
    o_qb[...] = _dot(wq, pb_ref[...]).astype(BF16)
    wkv = wukv_ref[...].astype(BF16)
    o_k[:KV_LORA_RANK] = _dot(wkv, pk_ref[...]).astype(BF16)
    o_k[KV_LORA_RANK:] = kb_ref[...]
    o_v[...] = _dot(wkv, pv_ref[...]).astype(BF16)


def _mla_weights(w_in, w_uq, w_ukv):
    nj = 4
    full = lambda a: pl.BlockSpec(a.shape, lambda j: (0, 0))
    cols = lambda rows, width: pl.BlockSpec((rows, width // nj), lambda j: (0, j))
    sel = _mla_selectors()
    widths = (PROJ_W, QK_PAD, QK_PAD, QK_PAD, MLA_WIDTH, QK_PAD)
    out_rows = (D_MODEL, Q_LORA_RANK, Q_LORA_RANK, 256, KV_LORA_RANK)
    return pl.pallas_call(
        _mla_wprep_kernel,
        grid=(nj,),
        in_specs=[full(w_in), full(w_uq), full(w_ukv)] + [cols(a.shape[0], w) for a, w in zip(sel, widths)],
        out_specs=[cols(r, w) for r, w in zip(out_rows, widths)],
        out_shape=[jax.ShapeDtypeStruct((r, w), BF16) for r, w in zip(out_rows, widths)],
        compiler_params=_params(("arbitrary",)),
        name="mla_weight_prep",
    )(w_in, w_uq, w_ukv, *sel)


def _lam_tiles(lam):
    lam = lam.reshape(2, 2, UNITS, UNIT_ST // LANE, LANE)
    lam = jnp.concatenate([lam[0, 0], lam[1, 0], lam[0, 1], lam[1, 1]], axis=1)
    return jnp.broadcast_to(lam[:, :, None, :], (UNITS, STATE_TILES, SUB, LANE))


def kernel(x, c, ctx, c_ctx, ada_w, ada_b, norm_g, mla_w_in, mla_q_norm, mla_w_uq, mla_kv_norm, mla_w_ukv, mla_w_out, s5_w_in, s5_a_re, s5_a_im, s5_log_step, s5_b_re, s5_b_im, s5_c_re, s5_c_im, s5_d, s5_w_glu, s5_b_glu, s5_w_out, final_g):
    cc = jnp.concatenate([c, c_ctx[None, :], jnp.zeros((7, D_MODEL), F32)], axis=0)
    mods = _modulation(cc, ada_w, ada_b)

    mod_lat = [mods[i, :BATCH, None, :] for i in range(2)]
    mod_ctx = [mods[i, BATCH:BATCH + 1, None, :] for i in range(2)]

    win, wqa, wqb, wk, wv = _mla_weights(mla_w_in[0], mla_w_uq[0], mla_w_ukv[0])
    weights = (norm_g[0][None, :], win, mla_q_norm[0][None, :], mla_kv_norm[0][None, :], wqa, wqb, wk, wv)
    tables = _rope_tables()
    qkvz = _mla_proj(x, mod_lat[0], weights, tables)
    q, k, v, sz = _mla_proj(ctx, mod_ctx[0], weights, tables, filled=qkvz)
    o = _attention(q, k, v)
    out_w = (norm_g[1][None, :], mla_w_out[0].astype(BF16), s5_w_in[0].astype(BF16))
    x1, xu, sz1 = _mla_out(o, sz, x, mod_lat[0], mod_lat[1], *out_w)
    xu, = _mla_out(o, sz, ctx, mod_ctx[0], mod_ctx[1], *out_w, xu_filled=xu)
    mod1 = mod_lat[1]

    n = 2 * S5_GROUPS
    lam, pb, cp, kk = _s5_prep(
        s5_a_re[0].reshape(n, S5_STATE), s5_a_im[0].reshape(n, S5_STATE), s5_log_step[0].reshape(n, 1),
        jnp.swapaxes(s5_b_re[0], -1, -2).reshape(n, S5_GROUP, S5_STATE),
        jnp.swapaxes(s5_b_im[0], -1, -2).reshape(n, S5_GROUP, S5_STATE),
        s5_c_re[0].reshape(n, S5_GROUP, S5_STATE), s5_c_im[0].reshape(n, S5_GROUP, S5_STATE))
    y = _s5_core(xu, kk.reshape(CH_T, 2, UNITS, UNIT_G, S5_GROUP, S5_GROUP),
                 pb.reshape(2, CH_T, 2, UNITS, UNIT_G, S5_GROUP, S5_STATE),
                 cp.reshape(2, CH_T, 2, UNITS, UNIT_G, S5_GROUP, S5_STATE),
                 s5_d[0].reshape(UNITS, UNIT_CH, 1), _lam_tiles(lam))
    return _finish(y, sz1, x1, mod1, s5_w_glu[0].astype(BF16), s5_b_glu[0][None, :], s5_w_out[0].astype(BF16),
                   final_g[None, :])
```
